```python
import jax, jax.numpy as jnp
from jax import lax
import numpy as np

D_MODEL = 2048
BATCH = 4
SEQ = 2048
DEPTH = 1
DEC_BATCH = 4
DEC_SEQ = 4096
PAST_LEN = 128

N_MEM = 256
HEAD_DIM = 128
DIL_PAIRS = ((128, 1), (512, 4), (2048, 16))
HEADS_PER_GROUP = 4
N_DIL_HEADS = HEADS_PER_GROUP * len(DIL_PAIRS)
D_ATTN = N_DIL_HEADS * HEAD_DIM
D_ATTN_OUT = HEADS_PER_GROUP * HEAD_DIM
POOL_WINDOWS = (2, 4, 8, 16)
POOL_GROUP = 256
D_POOL = POOL_GROUP * len(POOL_WINDOWS)
N_XHEADS = 4
XHEAD_DIM = 256
D_XATTN = N_XHEADS * XHEAD_DIM
D_IN = 3 * D_ATTN + D_POOL + D_XATTN
N_BRANCH = 3
D_FF = 5504
EPS = 1e-6
NEG = -1e30

kernel_name = "hybrid_dilated_pool_memory_encoder"


def rms_norm(x, gain):
    x32 = x.astype(jnp.float32)
    y = x32 * lax.rsqrt(jnp.mean(x32 * x32, axis=-1, keepdims=True) + EPS)
    return (y * gain.astype(jnp.float32)).astype(x.dtype)


def swiglu(h, w_up, w_down):
    g, u = jnp.split(h @ w_up, 2, axis=-1)
    return (jax.nn.silu(g) * u) @ w_down


def alibi_slopes():
    s = 2.0 ** (-8.0 * np.arange(1, N_DIL_HEADS + 1) / N_DIL_HEADS)
    return s.reshape(HEADS_PER_GROUP, len(DIL_PAIRS)).T.astype(np.float32)


def dilated_window_attention(q, k, v, slopes, window, dilation):
    B, T, H, hd = q.shape
    d = dilation
    r = window // (2 * d)
    L = T // d
    nb = -(-L // r)
    Lp = nb * r

    def to_sub(a):
        a = a.reshape(B, L, d, H, hd).transpose(0, 2, 3, 1, 4)
        return jnp.pad(a, ((0, 0), (0, 0), (0, 0), (0, Lp - L), (0, 0)))

    def neighbours(a):
        a = jnp.pad(a, ((0, 0), (0, 0), (0, 0), (r, r), (0, 0))).reshape(B, d, H, nb + 2, r, hd)
        return jnp.concatenate([a[:, :, :, :-2], a[:, :, :, 1:-1], a[:, :, :, 2:]], axis=4)

    qb = to_sub(q).reshape(B, d, H, nb, r, hd)
    kb = neighbours(to_sub(k))
    vb = neighbours(to_sub(v))
    s = jnp.einsum('bdhnqc,bdhnkc->bdhnqk', qb, kb).astype(jnp.float32) * (hd ** -0.5)
    a_idx = jnp.arange(r)[:, None]
    c_idx = jnp.arange(3 * r)[None, :]
    rel = c_idx - r - a_idx
    kj = (jnp.arange(nb)[:, None, None] - 1) * r + c_idx[None]
    valid = (jnp.abs(rel) <= r)[None] & (kj >= 0) & (kj < L)
    dist = (d * jnp.abs(rel)).astype(jnp.float32)
    bias = -slopes[:, None, None] * dist[None]
    s = s + bias[None, None, :, None]
    s = jnp.where(valid[None, None, None], s, NEG)
    lse = jax.nn.logsumexp(s, axis=-1)
    p = jnp.exp(s - lse[..., None])
    o = jnp.einsum('bdhnqk,bdhnkc->bdhnqc', p.astype(v.dtype), vb)
    o = o.reshape(B, d, H, Lp, hd)[:, :, :, :L].transpose(0, 3, 1, 2, 4).reshape(B, T, H, hd)
    lse = lse.reshape(B, d, H, Lp)[..., :L].transpose(0, 3, 1, 2).reshape(B, T, H)
    return o, lse


def multiscale_pool(u, w_pool, pool_scale):
    B, T, _ = u.shape
    G = len(POOL_WINDOWS)
    ug = u.reshape(B, T, G, POOL_GROUP).astype(jnp.float32)
    cs = jnp.concatenate([jnp.zeros((B, 1, G, POOL_GROUP), jnp.float32), jnp.cumsum(ug, axis=1)], axis=1)
    t = jnp.arange(T)
    outs = []
    for g, w in enumerate(POOL_WINDOWS):
        lo = jnp.clip(t - w // 2, 0, T)
        hi = jnp.clip(t - w // 2 + w, 0, T)
        cs_g = cs[:, :, g]
        mean = (cs_g[:, hi] - cs_g[:, lo]) / (hi - lo).astype(jnp.float32)[None, :, None]
        outs.append(mean - ug[:, :, g])
    pooled = jnp.stack(outs, axis=2).astype(u.dtype)
    mixed = jnp.einsum('btgc,gce->btge', pooled, w_pool).reshape(B, T, D_POOL)
    return mixed * pool_scale


def memory_attention(q, mem_h, w_mem_kv):
    B, T, _ = q.shape
    M = mem_h.shape[1]
    k, v = jnp.split(mem_h @ w_mem_kv, 2, axis=-1)
    k = k.reshape(B, M, N_XHEADS, XHEAD_DIM)
    v = v.reshape(B, M, N_XHEADS, XHEAD_DIM)
    qh = q.reshape(B, T, N_XHEADS, XHEAD_DIM)
    s = jnp.einsum('bthc,bmhc->bhtm', qh, k).astype(jnp.float32) * (XHEAD_DIM ** -0.5)
    p = jax.nn.softmax(s, axis=-1).astype(v.dtype)
    return jnp.einsum('bhtm,bmhc->bthc', p, v).reshape(B, T, D_XATTN)


def encoder_layer(x, mem, ffn1_norm_pre, ffn1_w_up, ffn1_w_down, ffn1_norm_post,
                  mix_norm_pre, mem_norm, w_in, w_mem_kv, w_pool, pool_scale,
                  w_br_attn, w_br_pool, w_br_mem, w_gate, b_gate, w_out, mix_norm_post,
                  ffn2_norm_pre, ffn2_w_up, ffn2_w_down, ffn2_norm_post, final_norm):
    B, T, _ = x.shape
    x = x + 0.5 * rms_norm(swiglu(rms_norm(x, ffn1_norm_pre), ffn1_w_up, ffn1_w_down), ffn1_norm_post)
    h = rms_norm(x, mix_norm_pre)
    proj = h @ w_in
    qkv = proj[..., :3 * D_ATTN].reshape(B, T, 3, len(DIL_PAIRS), HEADS_PER_GROUP, HEAD_DIM)
    u_pool = proj[..., 3 * D_ATTN:3 * D_ATTN + D_POOL]
    q_mem = proj[..., 3 * D_ATTN + D_POOL:]
    slopes = alibi_slopes()
    outs, lses = [], []
    for g, (window, dilation) in enumerate(DIL_PAIRS):
        o, l = dilated_window_attention(qkv[:, :, 0, g], qkv[:, :, 1, g], qkv[:, :, 2, g],
                                        jnp.asarray(slopes[g]), window, dilation)
        outs.append(o)
        lses.append(l)
    alpha = jax.nn.softmax(jnp.stack(lses, axis=0), axis=0)
    y_attn = jnp.sum(alpha[..., None] * jnp.stack(outs, axis=0).astype(jnp.float32), axis=0)
    y_attn = y_attn.astype(x.dtype).reshape(B, T, D_ATTN_OUT)
    y_pool = multiscale_pool(u_pool, w_pool, pool_scale)
    y_mem = memory_attention(q_mem, rms_norm(mem, mem_norm), w_mem_kv)
    gates = jax.nn.sigmoid((h @ w_gate + b_gate).astype(jnp.float32)).astype(x.dtype)
    gates = gates.reshape(B, T, N_BRANCH, D_MODEL)
    merged = (gates[:, :, 0] * (y_attn @ w_br_attn)
              + gates[:, :, 1] * (y_pool @ w_br_pool)
              + gates[:, :, 2] * (y_mem @ w_br_mem))
    x = x + rms_norm(merged @ w_out, mix_norm_post)
    x = x + 0.5 * rms_norm(swiglu(rms_norm(x, ffn2_norm_pre), ffn2_w_up, ffn2_w_down), ffn2_norm_post)
    return rms_norm(x, final_norm)


def run_trunk(x, mem, params):
    for l in range(DEPTH):
        layer_params = [p[l] for p in params]
        x = encoder_layer(x, mem, *layer_params)
    return x


def setup_inputs(seed: int = 0) -> dict:
    key = jax.random.key(seed)
    ks = jax.random.split(key, 32)

    def w(k, shape, fan_in):
        return jax.random.normal(k, shape, jnp.float32) * fan_in ** -0.5

    def gain(k, shape):
        return 1.0 + 0.02 * jax.random.normal(k, shape, jnp.float32)

    L = DEPTH
    return {
        "x_prompt": jax.random.normal(ks[0], (BATCH, SEQ, D_MODEL), jnp.float32),
        "x_sample": jax.random.normal(ks[1], (DEC_BATCH, DEC_SEQ, D_MODEL), jnp.float32),
        "mem_prompt": jax.random.normal(ks[2], (BATCH, N_MEM, D_MODEL), jnp.float32),
        "mem_sample": jax.random.normal(ks[3], (DEC_BATCH, N_MEM, D_MODEL), jnp.float32),
        "ffn1_norm_pre": gain(ks[4], (L, D_MODEL)),
        "ffn1_w_up": w(ks[5], (L, D_MODEL, 2 * D_FF), D_MODEL),
        "ffn1_w_down": w(ks[6], (L, D_FF, D_MODEL), D_FF),
        "ffn1_norm_post": gain(ks[7], (L, D_MODEL)),
        "mix_norm_pre": gain(ks[8], (L, D_MODEL)),
        "mem_norm": gain(ks[9], (L, D_MODEL)),
        "w_in": w(ks[10], (L, D_MODEL, D_IN), D_MODEL),
        "w_mem_kv": w(ks[11], (L, D_MODEL, 2 * D_XATTN), D_MODEL),
        "w_pool": w(ks[12], (L, len(POOL_WINDOWS), POOL_GROUP, POOL_GROUP), POOL_GROUP),
        "pool_scale": gain(ks[13], (L, D_POOL)),
        "w_br_attn": w(ks[14], (L, D_ATTN_OUT, D_MODEL), D_ATTN_OUT),
        "w_br_pool": w(ks[15], (L, D_POOL, D_MODEL), D_POOL),
        "w_br_mem": w(ks[16], (L, D_XATTN, D_MODEL), D_XATTN),
        "w_gate": w(ks[17], (L, D_MODEL, N_BRANCH * D_MODEL), D_MODEL),
        "b_gate": 0.02 * jax.random.normal(ks[18], (L, N_BRANCH * D_MODEL), jnp.float32),
        "w_out": w(ks[19], (L, D_MODEL, D_MODEL), D_MODEL),
        "mix_norm_post": gain(ks[20], (L, D_MODEL)),
        "ffn2_norm_pre": gain(ks[21], (L, D_MODEL)),
        "ffn2_w_up": w(ks[22], (L, D_MODEL, 2 * D_FF), D_MODEL),
        "ffn2_w_down": w(ks[23], (L, D_FF, D_MODEL), D_FF),
        "ffn2_norm_post": gain(ks[24], (L, D_MODEL)),
        "final_norm": gain(ks[25], (L, D_MODEL)),
    }


def reference(x_prompt, x_sample, mem_prompt, mem_sample,
              ffn1_norm_pre, ffn1_w_up, ffn1_w_down, ffn1_norm_post,
              mix_norm_pre, mem_norm, w_in, w_mem_kv, w_pool, pool_scale,
              w_br_attn, w_br_pool, w_br_mem, w_gate, b_gate, w_out, mix_norm_post,
              ffn2_norm_pre, ffn2_w_up, ffn2_w_down, ffn2_norm_post, final_norm):
    params = [ffn1_norm_pre, ffn1_w_up, ffn1_w_down, ffn1_norm_post,
              mix_norm_pre, mem_norm, w_in, w_mem_kv, w_pool, pool_scale,
              w_br_attn, w_br_pool, w_br_mem, w_gate, b_gate, w_out, mix_norm_post,
              ffn2_norm_pre, ffn2_w_up, ffn2_w_down, ffn2_norm_post, final_norm]
    y_prompt = run_trunk(x_prompt, mem_prompt, params)
    y_sample = run_trunk(x_sample, mem_sample, params)
    return (y_prompt, y_sample)
```

```python
import functools

import numpy as np
import jax
import jax.numpy as jnp
from jax import lax
from jax.experimental import pallas as pl
from jax.experimental.pallas import tpu as pltpu

F32 = jnp.float32
BF16 = jnp.bfloat16

N_MEM = 256
HEAD_DIM = 128
DIL_PAIRS = ((128, 1), (512, 4), (2048, 16))
HEADS_PER_GROUP = 4
N_GROUPS = len(DIL_PAIRS)
N_DIL_HEADS = HEADS_PER_GROUP * N_GROUPS
D_ATTN = N_DIL_HEADS * HEAD_DIM
POOL_WINDOWS = (2, 4, 8, 16)
POOL_GROUP = 256
D_POOL = POOL_GROUP * len(POOL_WINDOWS)
N_XHEADS = 4
XHEAD_DIM = 256
D_XATTN = N_XHEADS * XHEAD_DIM
N_BRANCH = 3
EPS = 1e-6
NEG = -1e30
BAND_RADIUS = 64

V7X_VMEM_BYTES = 64 * 1024 * 1024
V7X_LANES = 128
MIB = 1024 * 1024

FFN_TM = 512
FFN_TF = 512
PROJ_TM = 1024
PROJ_TN = 512
ATT_QB = 128
ATT_KW = 256
POOL_ROWS = 256
POOL_HALO = 16
MEM_TM = 512
MERGE_TM = 256


def _vmem_limit(nbytes):
    return int(min(nbytes * 5 // 4 + 8 * MIB, V7X_VMEM_BYTES - 6 * MIB))


def _rms(x, gain):
    ms = jnp.mean(x * x, axis=-1, keepdims=True)
    return x * lax.rsqrt(ms + EPS) * gain


def _alibi_slopes():
    s = 2.0 ** (-8.0 * np.arange(1, N_DIL_HEADS + 1) / N_DIL_HEADS)
    return s.reshape(HEADS_PER_GROUP, N_GROUPS).T.astype(np.float32)


def _ffn_kernel(x_ref, gpre_ref, wgu_ref, wd_ref, gpost_ref, gfin_ref, o_ref, hs_ref, *, tf, final):
    f = pl.program_id(1)

    @pl.when(f == 0)
    def _():
        hs_ref[...] = _rms(x_ref[...], gpre_ref[...]).astype(BF16)
        o_ref[...] = jnp.zeros_like(o_ref)

    gu = jnp.dot(hs_ref[...], wgu_ref[...], preferred_element_type=F32)
    g = gu[:, :tf]
    u = gu[:, tf:]
    a = (g * jax.nn.sigmoid(g) * u).astype(BF16)
    o_ref[...] += jnp.dot(a, wd_ref[...], preferred_element_type=F32)

    @pl.when(f == pl.num_programs(1) - 1)
    def _():
        y = x_ref[...] + 0.5 * _rms(o_ref[...], gpost_ref[...])
        if final:
            y = _rms(y, gfin_ref[...])
        o_ref[...] = y


def _ffn(x, gpre, wgu, wd, gpost, gfin, *, final):
    n, d = x.shape
    tm, tf = FFN_TM, FFN_TF
    nf = wd.shape[0] // tf
    est = 2 * tm * d * 4 * 2 + tm * d * 2 + 2 * (d * 2 * tf + tf * d) * 2 + 3 * tm * 2 * tf * 4
    vec = pl.BlockSpec((1, d), lambda i, f: (0, 0))
    return pl.pallas_call(
        functools.partial(_ffn_kernel, tf=tf, final=final),
        grid=(n // tm, nf),
        in_specs=[
            pl.BlockSpec((tm, d), lambda i, f: (i, 0)),
            vec,
            pl.BlockSpec((d, 2 * tf), lambda i, f: (0, f)),
            pl.BlockSpec((tf, d), lambda i, f: (f, 0)),
            vec,
            vec,
        ],
        out_specs=pl.BlockSpec((tm, d), lambda i, f: (i, 0)),
        out_shape=jax.ShapeDtypeStruct((n, d), F32),
        scratch_shapes=[pltpu.VMEM((tm, d), BF16)],
        compiler_params=pltpu.CompilerParams(
            dimension_semantics=("arbitrary", "arbitrary"), vmem_limit_bytes=_vmem_limit(est)),
        name="ffn_final" if final else "ffn",
    )(x, gpre, wgu, wd, gpost, gfin)


def _proj_kernel(x_ref, g_ref, w_ref, b_ref, o_ref, hs_ref, *, n_gate):
    j = pl.program_id(1)

    @pl.when(j == 0)
    def _():
        hs_ref[...] = _rms(x_ref[...], g_ref[...]).astype(BF16)

    acc = jnp.dot(hs_ref[...], w_ref[...], preferred_element_type=F32)
    if n_gate == 0:
        o_ref[...] = acc.astype(o_ref.dtype)
    else:
        @pl.when(j < n_gate)
        def _():
            o_ref[...] = jax.nn.sigmoid(acc + b_ref[...]).astype(o_ref.dtype)

        @pl.when(j >= n_gate)
        def _():
            o_ref[...] = acc.astype(o_ref.dtype)


def _proj(x, gain, w, bias, *, n_gate_cols):
    n, d = x.shape
    ncol = w.shape[1]
    tm, tn = min(PROJ_TM, n), PROJ_TN
    n_gate = n_gate_cols // tn
    n_bias_blocks = max(bias.shape[1] // tn, 1)
    est = 2 * tm * d * 4 + tm * d * 2 + 2 * d * tn * 2 + 2 * tm * tn * 2 + 2 * tm * tn * 4
    return pl.pallas_call(
        functools.partial(_proj_kernel, n_gate=n_gate),
        grid=(n // tm, ncol // tn),
        in_specs=[
            pl.BlockSpec((tm, d), lambda i, j: (i, 0)),
            pl.BlockSpec((1, d), lambda i, j: (0, 0)),
            pl.BlockSpec((d, tn), lambda i, j: (0, j)),
            pl.BlockSpec((1, tn), lambda i, j: (0, jnp.minimum(j, n_bias_blocks - 1))),
        ],
        out_specs=pl.BlockSpec((tm, tn), lambda i, j: (i, j)),
        out_shape=jax.ShapeDtypeStruct((n, ncol), BF16),
        scratch_shapes=[pltpu.VMEM((tm, d), BF16)],
        compiler_params=pltpu.CompilerParams(
            dimension_semantics=("arbitrary", "arbitrary"), vmem_limit_bytes=_vmem_limit(est)),
        name="proj",
    )(x, gain, w, bias)


def _attn_kernel(slopes_ref, q_ref, k_ref, v_ref, o_ref, m_ref, l_ref, n_ref, stage_ref, qd_ref, kd_ref, vd_ref,
                 *, seq):
    h = pl.program_id(1)
    g = pl.program_id(2)
    scale = HEAD_DIM ** -0.5
    qb = ATT_QB

    def run_group(gi, d):
        sub_len = seq // d
        kw = min(ATT_KW, sub_len)
        blocks_per_sub = sub_len // qb
        slope_d = slopes_ref[gi, h] * float(d)

        if d == 1:
            qs, ks, vs = q_ref, k_ref, v_ref
        else:
            for src, dst in ((q_ref, qd_ref), (k_ref, kd_ref), (v_ref, vd_ref)):
                stage_ref[...] = src[...].astype(F32)
                for res in range(d):
                    dst[pl.ds(res * sub_len, sub_len), :] = (
                        stage_ref[pl.ds(res, sub_len, stride=d), :].astype(BF16))
            qs, ks, vs = qd_ref, kd_ref, vd_ref

        row = lax.broadcasted_iota(jnp.int32, (qb, kw), 0)
        col = lax.broadcasted_iota(jnp.int32, (qb, kw), 1)
        col_minus_row = col - row

        def block(idx, carry):
            res = idx // blocks_per_sub
            i = idx - res * blocks_per_sub
            q0 = i * qb
            k0 = jnp.clip(q0 - BAND_RADIUS, 0, sub_len - kw)
            qstart = pl.multiple_of(idx * qb, qb)
            kstart = pl.multiple_of(res * sub_len + k0, BAND_RADIUS)
            qblk = qs[pl.ds(qstart, qb), :]
            kwin = ks[pl.ds(kstart, kw), :]
            vwin = vs[pl.ds(kstart, kw), :]
            s = lax.dot_general(qblk, kwin, (((1,), (1,)), ((), ())), preferred_element_type=F32)
            arel = jnp.abs(col_minus_row + (k0 - q0))
            s = s * scale - slope_d * arel.astype(F32)
            s = jnp.where(arel <= BAND_RADIUS, s, NEG)
            m = jnp.max(s, axis=-1, keepdims=True)
            p = jnp.exp(s - m)
            l = jnp.sum(p, axis=-1, keepdims=True)
            n = jnp.dot(p.astype(BF16), vwin, preferred_element_type=F32)
            m = jnp.broadcast_to(m, (qb, HEAD_DIM))
            l = jnp.broadcast_to(l, (qb, HEAD_DIM))
            if d == 1:
                rows = pl.ds(qstart, qb)
            else:
                rows = pl.ds(q0 * d + res, qb, stride=d)
            if gi == 0:
                m_ref[rows, :] = m
                l_ref[rows, :] = l
                n_ref[rows, :] = n
            else:
                m_old = m_ref[rows, :]
                m_new = jnp.maximum(m_old, m)
                a_old = jnp.exp(m_old - m_new)
                a_new = jnp.exp(m - m_new)
                m_ref[rows, :] = m_new
                l_ref[rows, :] = a_old * l_ref[rows, :] + a_new * l
                n_ref[rows, :] = a_old * n_ref[rows, :] + a_new * n
            return carry

        lax.fori_loop(0, d * blocks_per_sub, block, 0)

    for gi, (_, d) in enumerate(DIL_PAIRS):
        pl.when(g == gi)(functools.partial(run_group, gi, d))

    @pl.when(g == N_GROUPS - 1)
    def _():
        o_ref[...] = (n_ref[...] / l_ref[...]).astype(o_ref.dtype)


def _attention(pg, slopes, *, batch, seq, col0):
    blk0 = col0 // HEAD_DIM

    def qkv_spec(which):
        return pl.BlockSpec(
            (seq, HEAD_DIM), lambda b, h, g: (b, blk0 + (which * N_GROUPS + g) * HEADS_PER_GROUP + h))

    est = 3 * 2 * seq * HEAD_DIM * 2 + 2 * seq * HEAD_DIM * 2 + 4 * seq * HEAD_DIM * 4 + 3 * seq * HEAD_DIM * 2
    return pl.pallas_call(
        functools.partial(_attn_kernel, seq=seq),
        grid=(batch, HEADS_PER_GROUP, N_GROUPS),
        in_specs=[pl.BlockSpec(memory_space=pltpu.SMEM), qkv_spec(0), qkv_spec(1), qkv_spec(2)],
        out_specs=pl.BlockSpec((seq, HEAD_DIM), lambda b, h, g: (b, h)),
        out_shape=jax.ShapeDtypeStruct((batch * seq, HEADS_PER_GROUP * HEAD_DIM), BF16),
        scratch_shapes=[
            pltpu.VMEM((seq, HEAD_DIM), F32),
            pltpu.VMEM((seq, HEAD_DIM), F32),
            pltpu.VMEM((seq, HEAD_DIM), F32),
            pltpu.VMEM((seq, HEAD_DIM), F32),
            pltpu.VMEM((seq, HEAD_DIM), BF16),
            pltpu.VMEM((seq, HEAD_DIM), BF16),
            pltpu.VMEM((seq, HEAD_DIM), BF16),
        ],
        compiler_params=pltpu.CompilerParams(
            dimension_semantics=("arbitrary", "arbitrary", "arbitrary"), vmem_limit_bytes=_vmem_limit(est)),
        name="attention",
    )(slopes, pg, pg, pg)


def _pool_kernel(u_ref, w_ref, sc_ref, o_ref, pad_ref, *, seq):
    grp = pl.program_id(1)
    halo, rows = POOL_HALO, POOL_ROWS
    ext = rows + 2 * halo
    zeros = jnp.zeros((halo, POOL_GROUP), F32)
    pad_ref[pl.ds(0, halo), :] = zeros
    pad_ref[pl.ds(halo + seq, halo), :] = zeros
    pad_ref[pl.ds(halo, seq), :] = u_ref[...].astype(F32)

    def run(window):
        def chunk(c, carry):
            base = pl.multiple_of(c * rows, rows)
            p = pad_ref[pl.ds(base, ext), :]
            s = p + pltpu.roll(p, 1, 0)
            half = 1
            while 2 * half < window:
                s = pltpu.roll(s, ext - half, 0) + pltpu.roll(s, half, 0)
                half *= 2
            s = s[halo:halo + rows, :]
            tok = p[halo:halo + rows, :]
            t = base + lax.broadcasted_iota(jnp.int32, (rows, 1), 0)
            lo = jnp.maximum(t - window // 2, 0)
            hi = jnp.minimum(t - window // 2 + window, seq)
            pooled = s / (hi - lo).astype(F32) - tok
            y = jnp.dot(pooled.astype(BF16), w_ref[...], preferred_element_type=F32) * sc_ref[...]
            o_ref[pl.ds(base, rows), :] = y.astype(o_ref.dtype)
            return carry

        lax.fori_loop(0, seq // rows, chunk, 0)

    for gi, window in enumerate(POOL_WINDOWS):
        pl.when(grp == gi)(functools.partial(run, window))


def _pool(pg, w_pool, pool_scale, *, batch, seq, col0):
    blk0 = col0 // POOL_GROUP
    ngrp = len(POOL_WINDOWS)
    est = 2 * seq * POOL_GROUP * 2 * 2 + (seq + 2 * POOL_HALO) * POOL_GROUP * 4 + 2 * POOL_GROUP * POOL_GROUP * 2
    return pl.pallas_call(
        functools.partial(_pool_kernel, seq=seq),
        grid=(batch, ngrp),
        in_specs=[
            pl.BlockSpec((seq, POOL_GROUP), lambda b, g: (b, blk0 + g)),
            pl.BlockSpec((None, POOL_GROUP, POOL_GROUP), lambda b, g: (g, 0, 0)),
            pl.BlockSpec((1, POOL_GROUP), lambda b, g: (0, g)),
        ],
        out_specs=pl.BlockSpec((seq, POOL_GROUP), lambda b, g: (b, g)),
        out_shape=jax.ShapeDtypeStruct((batch * seq, D_POOL), BF16),
        scratch_shapes=[pltpu.VMEM((seq + 2 * POOL_HALO, POOL_GROUP), F32)],
        compiler_params=pltpu.CompilerParams(
            dimension_semantics=("arbitrary", "arbitrary"), vmem_limit_bytes=_vmem_limit(est)),
        name="pool",
    )(pg, w_pool, pool_scale)


def _memattn_kernel(q_ref, kv_ref, o_ref):
    scale = XHEAD_DIM ** -0.5
    for hh in range(N_XHEADS):
        lo = hh * XHEAD_DIM
        q = q_ref[:, lo:lo + XHEAD_DIM]
        k = kv_ref[:, lo:lo + XHEAD_DIM]
        v = kv_ref[:, D_XATTN + lo:D_XATTN + lo + XHEAD_DIM]
        s = lax.dot_general(q, k, (((1,), (1,)), ((), ())), preferred_element_type=F32) * scale
        m = jnp.max(s, axis=-1, keepdims=True)
        p = jnp.exp(s - m)
        l = jnp.sum(p, axis=-1, keepdims=True)
        y = jnp.dot(p.astype(BF16), v, preferred_element_type=F32) / l
        o_ref[:, lo:lo + XHEAD_DIM] = y.astype(o_ref.dtype)


def _memattn(pg, kv, *, seq, col0):
    n = pg.shape[0]
    tm = MEM_TM
    blk0 = col0 // D_XATTN
    tiles_per_seq = seq // tm
    est = 2 * tm * D_XATTN * 2 * 2 + 2 * N_MEM * 2 * D_XATTN * 2 + 4 * tm * N_MEM * 4
    return pl.pallas_call(
        _memattn_kernel,
        grid=(n // tm,),
        in_specs=[
            pl.BlockSpec((tm, D_XATTN), lambda i: (i, blk0)),
            pl.BlockSpec((N_MEM, 2 * D_XATTN), lambda i: (i // tiles_per_seq, 0)),
        ],
        out_specs=pl.BlockSpec((tm, D_XATTN), lambda i: (i, 0)),
        out_shape=jax.ShapeDtypeStruct((n, D_XATTN), BF16),
        compiler_params=pltpu.CompilerParams(
            dimension_semantics=("arbitrary",), vmem_limit_bytes=_vmem_limit(est)),
        name="memattn",
    )(pg, kv)


def _merge_kernel(gates_ref, ya_ref, yp_ref, ym_ref, x_ref, wa_ref, wp_ref, wm_ref, wo_ref, gain_ref, o_ref):
    d = x_ref.shape[1]
    merged = gates_ref[:, 0:d].astype(F32) * jnp.dot(ya_ref[...], wa_ref[...], preferred_element_type=F32)
    merged += gates_ref[:, d:2 * d].astype(F32) * jnp.dot(yp_ref[...], wp_ref[...], preferred_element_type=F32)
    merged += gates_ref[:, 2 * d:3 * d].astype(F32) * jnp.dot(ym_ref[...], wm_ref[...], preferred_element_type=F32)
    z = jnp.dot(merged.astype(BF16), wo_ref[...], preferred_element_type=F32)
    o_ref[...] = x_ref[...] + _rms(z, gain_ref[...])


def _merge(pg, y_attn, y_pool, y_mem, x, wa, wp, wm, wo, gain):
    n, d = x.shape
    tm = MERGE_TM
    wbytes = (wa.size + wp.size + wm.size + wo.size) * 2

    def const(shape):
        return pl.BlockSpec(shape, lambda i: (0, 0), pipeline_mode=pl.Buffered(1))

    est = (2 * tm * (N_BRANCH * d + y_attn.shape[1] + y_pool.shape[1] + y_mem.shape[1]) * 2
           + 4 * tm * d * 4 + wbytes + 4 * tm * d * 4)
    return pl.pallas_call(
        _merge_kernel,
        grid=(n // tm,),
        in_specs=[
            pl.BlockSpec((tm, N_BRANCH * d), lambda i: (i, 0)),
            pl.BlockSpec((tm, y_attn.shape[1]), lambda i: (i, 0)),
            pl.BlockSpec((tm, y_pool.shape[1]), lambda i: (i, 0)),
            pl.BlockSpec((tm, y_mem.shape[1]), lambda i: (i, 0)),
            pl.BlockSpec((tm, d), lambda i: (i, 0)),
            const(wa.shape), const(wp.shape), const(wm.shape), const(wo.shape),
            pl.BlockSpec((1, d), lambda i: (0, 0)),
        ],
        out_specs=pl.BlockSpec((tm, d), lambda i: (i, 0)),
        out_shape=jax.ShapeDtypeStruct((n, d), F32),
        compiler_params=pltpu.CompilerParams(
            dimension_semantics=("arbitrary",), vmem_limit_bytes=_vmem_limit(est)),
        name="merge",
    )(pg, y_attn, y_pool, y_mem, x, wa, wp, wm, wo, gain)


def _prep_ffn_weights(w_up, w_down, tf):
    d, two_f = w_up.shape
    dff = two_f // 2
    fpad = -(-dff // tf) * tf
    nf = fpad // tf
    wg = jnp.pad(w_up[:, :dff].astype(BF16), ((0, 0), (0, fpad - dff))).reshape(d, nf, tf)
    wu = jnp.pad(w_up[:, dff:].astype(BF16), ((0, 0), (0, fpad - dff))).reshape(d, nf, tf)
    wgu = jnp.concatenate([wg, wu], axis=2).reshape(d, nf * 2 * tf)
    wd = jnp.pad(w_down.astype(BF16), ((0, fpad - dff), (0, 0)))
    return wgu, wd


def _row(v):
    return v.reshape(1, -1).astype(F32)


def _trunk(x, mem, p, slopes):
    batch, seq, d = x.shape
    xf = x.reshape(batch * seq, d)
    x1 = _ffn(xf, p["ffn1_norm_pre"], p["ffn1_wgu"], p["ffn1_wd"], p["ffn1_norm_post"], p["ffn1_norm_post"],
              final=False)
    pg = _proj(x1, p["mix_norm_pre"], p["w_cat"], p["b_gate"], n_gate_cols=N_BRANCH * d)
    col_qmem = N_BRANCH * d
    col_pool = col_qmem + D_XATTN
    col_qkv = col_pool + D_POOL
    kv = _proj(mem.reshape(batch * N_MEM, d), p["mem_norm"], p["w_mem_kv"], p["b_none"], n_gate_cols=0)
    y_attn = _attention(pg, slopes, batch=batch, seq=seq, col0=col_qkv)
    y_pool = _pool(pg, p["w_pool"], p["pool_scale"], batch=batch, seq=seq, col0=col_pool)
    y_mem = _memattn(pg, kv, seq=seq, col0=col_qmem)
    x2 = _merge(pg, y_attn, y_pool, y_mem, x1, p["w_br_attn"], p["w_br_pool"], p["w_br_mem"], p["w_out"],
                p["mix_norm_post"])
    y = _ffn(x2, p["ffn2_norm_pre"], p["ffn2_wgu"], p["ffn2_wd"], p["ffn2_norm_post"], p["final_norm"], final=True)
    return y.reshape(batch, seq, d)


def _prep_params(ffn1_norm_pre, ffn1_w_up, ffn1_w_down, ffn1_norm_post, mix_norm_pre, mem_norm, w_in, w_mem_kv,
                 w_pool, pool_scale, w_br_attn, w_br_pool, w_br_mem, w_gate, b_gate, w_out, mix_norm_post,
                 ffn2_norm_pre, ffn2_w_up, ffn2_w_down, ffn2_norm_post, final_norm):
    p = {}
    p["ffn1_wgu"], p["ffn1_wd"] = _prep_ffn_weights(ffn1_w_up, ffn1_w_down, FFN_TF)
    p["ffn2_wgu"], p["ffn2_wd"] = _prep_ffn_weights(ffn2_w_up, ffn2_w_down, FFN_TF)
    n_qkv = 3 * D_ATTN
    p["w_cat"] = jnp.concatenate(
        [w_gate, w_in[:, n_qkv + D_POOL:], w_in[:, n_qkv:n_qkv + D_POOL], w_in[:, :n_qkv]], axis=1).astype(BF16)
    p["b_gate"] = _row(b_gate)
    p["b_none"] = jnp.zeros((1, PROJ_TN), F32)
    p["w_mem_kv"] = w_mem_kv.astype(BF16)
    p["w_pool"] = w_pool.astype(BF16)
    p["pool_scale"] = _row(pool_scale)
    for name, w in (("w_br_attn", w_br_attn), ("w_br_pool", w_br_pool), ("w_br_mem", w_br_mem), ("w_out", w_out)):
        p[name] = w.astype(BF16)
    for name, v in (("ffn1_norm_pre", ffn1_norm_pre), ("ffn1_norm_post", ffn1_norm_post),
                    ("mix_norm_pre", mix_norm_pre), ("mem_norm", mem_norm), ("mix_norm_post", mix_norm_post),
                    ("ffn2_norm_pre", ffn2_norm_pre), ("ffn2_norm_post", ffn2_norm_post),
                    ("final_norm", final_norm)):
        p[name] = _row(v)
    return p


def kernel(x_prompt, x_sample, mem_prompt, mem_sample, ffn1_norm_pre, ffn1_w_up, ffn1_w_down, ffn1_norm_post,
           mix_norm_pre, mem_norm, w_in, w_mem_kv, w_pool, pool_scale, w_br_attn, w_br_pool, w_br_mem, w_gate, b_gate,
           w_out, mix_norm_post, ffn2_norm_pre, ffn2_w_up, ffn2_w_down, ffn2_norm_post, final_norm):
    layer = [ffn1_norm_pre, ffn1_w_up, ffn1_w_down, ffn1_norm_post, mix_norm_pre, mem_norm, w_in, w_mem_kv, w_pool,
             pool_scale, w_br_attn, w_br_pool, w_br_mem, w_gate, b_gate, w_out, mix_norm_post, ffn2_norm_pre,
             ffn2_w_up, ffn2_w_down, ffn2_norm_post, final_norm]
    depth = ffn1_norm_pre.shape[0]
    slopes = jnp.asarray(_alibi_slopes())
    y_prompt, y_sample = x_prompt, x_sample
    for layer_idx in range(depth):
        p = _prep_params(*[w[layer_idx] for w in layer])
        y_prompt = _trunk(y_prompt, mem_prompt, p, slopes)
        y_sample = _trunk(y_sample, mem_sample, p, slopes)
    return (y_prompt, y_sample)
```

```python
import functools

import numpy as np
import jax
import jax.numpy as jnp
from jax import lax
from jax.experimental import pallas as pl
from jax.experimental.pallas import tpu as pltpu

F32 = jnp.float32
BF16 = jnp.bfloat16

N_MEM = 256
HEAD_DIM = 128
DIL_PAIRS = ((128, 1), (512, 4), (2048, 16))
HEADS_PER_GROUP = 4
N_GROUPS = len(DIL_PAIRS)
N_DIL_HEADS = HEADS_PER_GROUP * N_GROUPS
D_ATTN = N_DIL_HEADS * HEAD_DIM
POOL_WINDOWS = (2, 4, 8, 16)
POOL_GROUP = 256
D_POOL = POOL_GROUP * len(POOL_WINDOWS)
N_XHEADS = 4
XHEAD_DIM = 256
D_XATTN = N_XHEADS * XHEAD_DIM
N_BRANCH = 3
EPS = 1e-6
NEG = -1e30
BAND_RADIUS = 64

V7X_VMEM_BYTES = 64 * 1024 * 1024
MIB = 1024 * 1024

FFN_TM = 512
FFN_TF = 512
PROJ_TM = 1024
PROJ_TN = 512
ATT_QB = 128
ATT_KW = 256
ATT_UNROLL = 4
ATT_FIN_ROWS = 256
POOL_ROWS = 256
POOL_HALO = 16
MEM_TM = 512
MERGE_TM = 256


def _vmem_limit(nbytes):
    return int(min(nbytes * 5 // 4 + 8 * MIB, V7X_VMEM_BYTES - 6 * MIB))


def _rms(x, gain):
    ms = jnp.mean(x * x, axis=-1, keepdims=True)
    return x * lax.rsqrt(ms + EPS) * gain


def _alibi_slopes():
    s = 2.0 ** (-8.0 * np.arange(1, N_DIL_HEADS + 1) / N_DIL_HEADS)
    return s.reshape(HEADS_PER_GROUP, N_GROUPS).T.astype(np.float32)


def _ffn_kernel(x_ref, gpre_ref, wg_ref, wu_ref, wd_ref, gpost_ref, gfin_ref, o_ref, hs_ref, *, tf, rem, final):
    f = pl.program_id(1)
    last = pl.num_programs(1) - 1

    @pl.when(f == 0)
    def _():
        hs_ref[...] = _rms(x_ref[...], gpre_ref[...]).astype(BF16)
        o_ref[...] = jnp.zeros_like(o_ref)

    def step(valid):
        hs = hs_ref[...]
        g = jnp.dot(hs, wg_ref[:, :valid], preferred_element_type=F32)
        u = jnp.dot(hs, wu_ref[:, :valid], preferred_element_type=F32)
        a = (g * jax.nn.sigmoid(g) * u).astype(BF16)
        o_ref[...] += jnp.dot(a, wd_ref[:valid, :], preferred_element_type=F32)

    if rem == tf:
        step(tf)
    else:
        pl.when(f < last)(functools.partial(step, tf))
        pl.when(f == last)(functools.partial(step, rem))

    @pl.when(f == last)
    def _():
        y = x_ref[...] + 0.5 * _rms(o_ref[...], gpost_ref[...])
        if final:
            y = _rms(y, gfin_ref[...])
        o_ref[...] = y


def _ffn(x, gpre, wg, wu, wd, gpost, gfin, *, final):
    n, d = x.shape
    tm, tf = FFN_TM, FFN_TF
    dff = wd.shape[0]
    nf = pl.cdiv(dff, tf)
    rem = dff - (nf - 1) * tf
    est = 2 * tm * d * 4 * 2 + tm * d * 2 + 2 * 3 * d * tf * 2 + 3 * tm * 2 * tf * 4
    vec = pl.BlockSpec((1, d), lambda i, f: (0, 0))
    return pl.pallas_call(
        functools.partial(_ffn_kernel, tf=tf, rem=rem, final=final),
        grid=(n // tm, nf),
        in_specs=[
            pl.BlockSpec((tm, d), lambda i, f: (i, 0)),
            vec,
            pl.BlockSpec((d, tf), lambda i, f: (0, f)),
            pl.BlockSpec((d, tf), lambda i, f: (0, f)),
            pl.BlockSpec((tf, d), lambda i, f: (f, 0)),
            vec,
            vec,
        ],
        out_specs=pl.BlockSpec((tm, d), lambda i, f: (i, 0)),
        out_shape=jax.ShapeDtypeStruct((n, d), F32),
        scratch_shapes=[pltpu.VMEM((tm, d), BF16)],
        compiler_params=pltpu.CompilerParams(
            dimension_semantics=("arbitrary", "arbitrary"), vmem_limit_bytes=_vmem_limit(est)),
        name="ffn_final" if final else "ffn",
    )(x, gpre, wg, wu, wd, gpost, gfin)


def _proj_kernel(x_ref, g_ref, w_ref, b_ref, o_ref, hs_ref, *, n_gate):
    j = pl.program_id(1)

    @pl.when(j == 0)
    def _():
        hs_ref[...] = _rms(x_ref[...], g_ref[...]).astype(BF16)

    acc = jnp.dot(hs_ref[...], w_ref[...], preferred_element_type=F32)
    if n_gate > 0:
        acc = jnp.where(j < n_gate, jax.nn.sigmoid(acc + b_ref[...]), acc)
    o_ref[...] = acc.astype(o_ref.dtype)


def _proj(x, gain, w, bias, *, n_gate_cols):
    n, d = x.shape
    ncol = w.shape[1]
    tm, tn = min(PROJ_TM, n), PROJ_TN
    n_gate = n_gate_cols // tn
    n_bias_blocks = max(bias.shape[1] // tn, 1)
    est = 2 * tm * d * 4 + tm * d * 2 + 2 * d * tn * 2 + 2 * tm * tn * 2 + 2 * tm * tn * 4
    return pl.pallas_call(
        functools.partial(_proj_kernel, n_gate=n_gate),
        grid=(n // tm, ncol // tn),
        in_specs=[
            pl.BlockSpec((tm, d), lambda i, j: (i, 0)),
            pl.BlockSpec((1, d), lambda i, j: (0, 0)),
            pl.BlockSpec((d, tn), lambda i, j: (0, j)),
            pl.BlockSpec((1, tn), lambda i, j: (0, jnp.minimum(j, n_bias_blocks - 1))),
        ],
        out_specs=pl.BlockSpec((tm, tn), lambda i, j: (i, j)),
        out_shape=jax.ShapeDtypeStruct((n, ncol), BF16),
        scratch_shapes=[pltpu.VMEM((tm, d), BF16)],
        compiler_params=pltpu.CompilerParams(
            dimension_semantics=("arbitrary", "arbitrary"), vmem_limit_bytes=_vmem_limit(est)),
        name="proj",
    )(x, gain, w, bias)


def _attn_kernel(slopes_ref, q_ref, k_ref, v_ref, o_ref,
                 m0_ref, m1_ref, m2_ref, l0_ref, l1_ref, l2_ref, n0_ref, n1_ref, n2_ref,
                 stage_ref, qd_ref, kd_ref, va_ref, bias_ref, *, seq):
    h = pl.program_id(1)
    g = pl.program_id(2)
    scale = HEAD_DIM ** -0.5
    qb = ATT_QB
    m_refs = (m0_ref, m1_ref, m2_ref)
    l_refs = (l0_ref, l1_ref, l2_ref)
    n_refs = (n0_ref, n1_ref, n2_ref)

    va_ref[:, HEAD_DIM:] = jnp.ones((seq, HEAD_DIM), BF16)

    def run_group(gi, d):
        sub_len = seq // d
        kw = min(ATT_KW, sub_len)
        nb = sub_len // qb
        slope_d = slopes_ref[gi, h] * float(d)

        row = lax.broadcasted_iota(jnp.int32, (qb, kw), 0)
        col = lax.broadcasted_iota(jnp.int32, (qb, kw), 1)
        for kind, off in enumerate((0, -BAND_RADIUS, qb - kw)):
            arel = jnp.abs(col - row + off)
            bias_ref[kind, :, :kw] = jnp.where(arel <= BAND_RADIUS, -slope_d * arel.astype(F32), NEG)

        if d == 1:
            qs, ks = q_ref, k_ref
            va_ref[:, :HEAD_DIM] = v_ref[...]
        else:
            for src, dst, lanes in ((q_ref, qd_ref, None), (k_ref, kd_ref, None), (v_ref, va_ref, HEAD_DIM)):
                stage_ref[...] = src[...].astype(F32)
                for res in range(d):
                    val = stage_ref[pl.ds(res, sub_len, stride=d), :].astype(BF16)
                    if lanes is None:
                        dst[pl.ds(res * sub_len, sub_len), :] = val
                    else:
                        dst[pl.ds(res * sub_len, sub_len), :lanes] = val
            qs, ks = qd_ref, kd_ref

        def block(idx):
            res = idx // nb
            i = idx - res * nb
            q0 = i * qb
            k0 = jnp.clip(q0 - BAND_RADIUS, 0, sub_len - kw)
            qstart = pl.multiple_of(idx * qb, qb)
            kstart = pl.multiple_of(res * sub_len + k0, BAND_RADIUS)
            s = lax.dot_general(qs[pl.ds(qstart, qb), :], ks[pl.ds(kstart, kw), :],
                                (((1,), (1,)), ((), ())), preferred_element_type=F32)
            if nb == 1:
                bias = bias_ref[0, :, :kw]
            else:
                kind = jnp.where(i == 0, 0, jnp.where(i == nb - 1, 2, 1))
                bias = bias_ref[kind]
            s = s * scale + bias
            m = jnp.max(s, axis=-1, keepdims=True)
            p = jnp.exp(s - m).astype(BF16)
            na = jnp.dot(p, va_ref[pl.ds(kstart, kw), :], preferred_element_type=F32)
            if d == 1:
                rows = pl.ds(qstart, qb)
            else:
                rows = pl.ds(q0 * d + res, qb, stride=d)
            m_refs[gi][rows, :] = jnp.broadcast_to(m, (qb, HEAD_DIM))
            l_refs[gi][rows, :] = na[:, HEAD_DIM:]
            n_refs[gi][rows, :] = na[:, :HEAD_DIM]

        def blocks(it, carry):
            for u in range(ATT_UNROLL):
                block(it * ATT_UNROLL + u)
            return carry

        lax.fori_loop(0, d * nb // ATT_UNROLL, blocks, 0)

    for gi, (_, d) in enumerate(DIL_PAIRS):
        pl.when(g == gi)(functools.partial(run_group, gi, d))

    @pl.when(g == N_GROUPS - 1)
    def _():
        def fin(c, carry):
            r = pl.ds(pl.multiple_of(c * ATT_FIN_ROWS, ATT_FIN_ROWS), ATT_FIN_ROWS)
            ms = [m_ref[r, :] for m_ref in m_refs]
            mx = jnp.maximum(jnp.maximum(ms[0], ms[1]), ms[2])
            num = jnp.zeros((ATT_FIN_ROWS, HEAD_DIM), F32)
            den = jnp.zeros((ATT_FIN_ROWS, HEAD_DIM), F32)
            for gi in range(N_GROUPS):
                e = jnp.exp(ms[gi] - mx)
                num += e * n_refs[gi][r, :]
                den += e * l_refs[gi][r, :]
            o_ref[r, :] = (num / den).astype(o_ref.dtype)
            return carry

        lax.fori_loop(0, seq // ATT_FIN_ROWS, fin, 0)


def _attention(pg, slopes, *, batch, seq, col0):
    blk0 = col0 // HEAD_DIM

    def qkv_spec(which):
        return pl.BlockSpec(
            (seq, HEAD_DIM), lambda b, h, g: (b, blk0 + (which * N_GROUPS + g) * HEADS_PER_GROUP + h))

    tile = seq * HEAD_DIM
    est = 4 * 2 * tile * 2 + 10 * tile * 4 + 4 * tile * 2 + 3 * ATT_QB * ATT_KW * 4
    state = [pltpu.VMEM((seq, HEAD_DIM), F32) for _ in range(3 * N_GROUPS)]
    return pl.pallas_call(
        functools.partial(_attn_kernel, seq=seq),
        grid=(batch, HEADS_PER_GROUP, N_GROUPS),
        in_specs=[pl.BlockSpec(memory_space=pltpu.SMEM), qkv_spec(0), qkv_spec(1), qkv_spec(2)],
        out_specs=pl.BlockSpec((seq, HEAD_DIM), lambda b, h, g: (b, h)),
        out_shape=jax.ShapeDtypeStruct((batch * seq, HEADS_PER_GROUP * HEAD_DIM), BF16),
        scratch_shapes=state + [
            pltpu.VMEM((seq, HEAD_DIM), F32),
            pltpu.VMEM((seq, HEAD_DIM), BF16),
            pltpu.VMEM((seq, HEAD_DIM), BF16),
            pltpu.VMEM((seq, 2 * HEAD_DIM), BF16),
            pltpu.VMEM((3, ATT_QB, ATT_KW), F32),
        ],
        compiler_params=pltpu.CompilerParams(
            dimension_semantics=("arbitrary", "arbitrary", "arbitrary"), vmem_limit_bytes=_vmem_limit(est)),
        name="attention",
    )(slopes, pg, pg, pg)


def _pool_kernel(u_ref, w_ref, sc_ref, o_ref, pad_ref, *, seq):
    grp = pl.program_id(1)
    halo, rows = POOL_HALO, POOL_ROWS
    ext = rows + 2 * halo
    zeros = jnp.zeros((halo, POOL_GROUP), F32)
    pad_ref[pl.ds(0, halo), :] = zeros
    pad_ref[pl.ds(halo + seq, halo), :] = zeros
    pad_ref[pl.ds(halo, seq), :] = u_ref[...].astype(F32)

    def run(window):
        def chunk(c, carry):
            base = pl.multiple_of(c * rows, rows)
            p = pad_ref[pl.ds(base, ext), :]
            s = p + pltpu.roll(p, 1, 0)
            half = 1
            while 2 * half < window:
                s = pltpu.roll(s, ext - half, 0) + pltpu.roll(s, half, 0)
                half *= 2
            s = s[halo:halo + rows, :]
            tok = p[halo:halo + rows, :]
            t = base + lax.broadcasted_iota(jnp.int32, (rows, 1), 0)
            lo = jnp.maximum(t - window // 2, 0)
            hi = jnp.minimum(t - window // 2 + window, seq)
            pooled = s / (hi - lo).astype(F32) - tok
            y = jnp.dot(pooled.astype(BF16), w_ref[...], preferred_element_type=F32) * sc_ref[...]
            o_ref[pl.ds(base, rows), :] = y.astype(o_ref.dtype)
            return carry

        lax.fori_loop(0, seq // rows, chunk, 0)

    for gi, window in enumerate(POOL_WINDOWS):
        pl.when(grp == gi)(functools.partial(run, window))


def _pool(pg, w_pool, pool_scale, *, batch, seq, col0):
    blk0 = col0 // POOL_GROUP
    ngrp = len(POOL_WINDOWS)
    est = 2 * seq * POOL_GROUP * 2 * 2 + (seq + 2 * POOL_HALO) * POOL_GROUP * 4 + 2 * POOL_GROUP * POOL_GROUP * 2
    return pl.pallas_call(
        functools.partial(_pool_kernel, seq=seq),
        grid=(batch, ngrp),
        in_specs=[
            pl.BlockSpec((seq, POOL_GROUP), lambda b, g: (b, blk0 + g)),
            pl.BlockSpec((None, POOL_GROUP, POOL_GROUP), lambda b, g: (g, 0, 0)),
            pl.BlockSpec((1, POOL_GROUP), lambda b, g: (0, g)),
        ],
        out_specs=pl.BlockSpec((seq, POOL_GROUP), lambda b, g: (b, g)),
        out_shape=jax.ShapeDtypeStruct((batch * seq, D_POOL), BF16),
        scratch_shapes=[pltpu.VMEM((seq + 2 * POOL_HALO, POOL_GROUP), F32)],
        compiler_params=pltpu.CompilerParams(
            dimension_semantics=("arbitrary", "arbitrary"), vmem_limit_bytes=_vmem_limit(est)),
        name="pool",
    )(pg, w_pool, pool_scale)


def _memattn_kernel(q_ref, kv_ref, o_ref):
    scale = XHEAD_DIM ** -0.5
    for hh in range(N_XHEADS):
        lo = hh * XHEAD_DIM
        q = q_ref[:, lo:lo + XHEAD_DIM]
        k = kv_ref[:, lo:lo + XHEAD_DIM]
        v = kv_ref[:, D_XATTN + lo:D_XATTN + lo + XHEAD_DIM]
        s = lax.dot_general(q, k, (((1,), (1,)), ((), ())), preferred_element_type=F32) * scale
        m = jnp.max(s, axis=-1, keepdims=True)
        p = jnp.exp(s - m)
        l = jnp.sum(p, axis=-1, keepdims=True)
        y = jnp.dot(p.astype(BF16), v, preferred_element_type=F32) / l
        o_ref[:, lo:lo + XHEAD_DIM] = y.astype(o_ref.dtype)


def _memattn(pg, kv, *, seq, col0):
    n = pg.shape[0]
    tm = MEM_TM
    blk0 = col0 // D_XATTN
    tiles_per_seq = seq // tm
    est = 2 * tm * D_XATTN * 2 * 2 + 2 * N_MEM * 2 * D_XATTN * 2 + 4 * tm * N_MEM * 4
    return pl.pallas_call(
        _memattn_kernel,
        grid=(n // tm,),
        in_specs=[
            pl.BlockSpec((tm, D_XATTN), lambda i: (i, blk0)),
            pl.BlockSpec((N_MEM, 2 * D_XATTN), lambda i: (i // tiles_per_seq, 0)),
        ],
        out_specs=pl.BlockSpec((tm, D_XATTN), lambda i: (i, 0)),
        out_shape=jax.ShapeDtypeStruct((n, D_XATTN), BF16),
        compiler_params=pltpu.CompilerParams(
            dimension_semantics=("arbitrary",), vmem_limit_bytes=_vmem_limit(est)),
        name="memattn",
    )(pg, kv)


def _merge_kernel(gates_ref, ya_ref, yp_ref, ym_ref, x_ref, wa_ref, wp_ref, wm_ref, wo_ref, gain_ref, o_ref):
    d = x_ref.shape[1]
    merged = gates_ref[:, 0:d].astype(F32) * jnp.dot(ya_ref[...], wa_ref[...], preferred_element_type=F32)
    merged += gates_ref[:, d:2 * d].astype(F32) * jnp.dot(yp_ref[...], wp_ref[...], preferred_element_type=F32)
    merged += gates_ref[:, 2 * d:3 * d].astype(F32) * jnp.dot(ym_ref[...], wm_ref[...], preferred_element_type=F32)
    z = jnp.dot(merged.astype(BF16), wo_ref[...], preferred_element_type=F32)
    o_ref[...] = x_ref[...] + _rms(z, gain_ref[...])


def _merge(pg, y_attn, y_pool, y_mem, x, wa, wp, wm, wo, gain):
    n, d = x.shape
    tm = MERGE_TM
    wbytes = (wa.size + wp.size + wm.size + wo.size) * 2

    def const(shape):
        return pl.BlockSpec(shape, lambda i: (0, 0), pipeline_mode=pl.Buffered(1))

    est = (2 * tm * (N_BRANCH * d + y_attn.shape[1] + y_pool.shape[1] + y_mem.shape[1]) * 2
           + 4 * tm * d * 4 + wbytes + 4 * tm * d * 4)
    return pl.pallas_call(
        _merge_kernel,
        grid=(n // tm,),
        in_specs=[
            pl.BlockSpec((tm, N_BRANCH * d), lambda i: (i, 0)),
            pl.BlockSpec((tm, y_attn.shape[1]), lambda i: (i, 0)),
            pl.BlockSpec((tm, y_pool.shape[1]), lambda i: (i, 0)),
            pl.BlockSpec((tm, y_mem.shape[1]), lambda i: (i, 0)),
            pl.BlockSpec((tm, d), lambda i: (i, 0)),
            const(wa.shape), const(wp.shape), const(wm.shape), const(wo.shape),
            pl.BlockSpec((1, d), lambda i: (0, 0)),
        ],
        out_specs=pl.BlockSpec((tm, d), lambda i: (i, 0)),
        out_shape=jax.ShapeDtypeStruct((n, d), F32),
        compiler_params=pltpu.CompilerParams(
            dimension_semantics=("arbitrary",), vmem_limit_bytes=_vmem_limit(est)),
        name="merge",
    )(pg, y_attn, y_pool, y_mem, x, wa, wp, wm, wo, gain)


def _row(v):
    return v.reshape(1, -1).astype(F32)


def _trunk(x, mem, p, slopes):
    batch, seq, d = x.shape
    xf = x.reshape(batch * seq, d)
    x1 = _ffn(xf, p["ffn1_norm_pre"], p["ffn1_wg"], p["ffn1_wu"], p["ffn1_wd"], p["ffn1_norm_post"],
              p["ffn1_norm_post"], final=False)
    pg = _proj(x1, p["mix_norm_pre"], p["w_cat"], p["b_gate"], n_gate_cols=N_BRANCH * d)
    col_qmem = N_BRANCH * d
    col_pool = col_qmem + D_XATTN
    col_qkv = col_pool + D_POOL
    kv = _proj(mem.reshape(batch * N_MEM, d), p["mem_norm"], p["w_mem_kv"], p["b_none"], n_gate_cols=0)
    y_attn = _attention(pg, slopes, batch=batch, seq=seq, col0=col_qkv)
    y_pool = _pool(pg, p["w_pool"], p["pool_scale"], batch=batch, seq=seq, col0=col_pool)
    y_mem = _memattn(pg, kv, seq=seq, col0=col_qmem)
    x2 = _merge(pg, y_attn, y_pool, y_mem, x1, p["w_br_attn"], p["w_br_pool"], p["w_br_mem"], p["w_out"],
                p["mix_norm_post"])
    y = _ffn(x2, p["ffn2_norm_pre"], p["ffn2_wg"], p["ffn2_wu"], p["ffn2_wd"], p["ffn2_norm_post"],
             p["final_norm"], final=True)
    return y.reshape(batch, seq, d)


def _prep_params(ffn1_norm_pre, ffn1_w_up, ffn1_w_down, ffn1_norm_post, mix_norm_pre, mem_norm, w_in, w_mem_kv,
                 w_pool, pool_scale, w_br_attn, w_br_pool, w_br_mem, w_gate, b_gate, w_out, mix_norm_post,
                 ffn2_norm_pre, ffn2_w_up, ffn2_w_down, ffn2_norm_post, final_norm):
    p = {}
    for name, w_up, w_down in (("ffn1", ffn1_w_up, ffn1_w_down), ("ffn2", ffn2_w_up, ffn2_w_down)):
        dff = w_down.shape[0]
        p[name + "_wg"] = w_up[:, :dff].astype(BF16)
        p[name + "_wu"] = w_up[:, dff:].astype(BF16)
        p[name + "_wd"] = w_down.astype(BF16)
    n_qkv = 3 * D_ATTN
    p["w_cat"] = jnp.concatenate(
        [w_gate.astype(BF16), w_in[:, n_qkv + D_POOL:].astype(BF16), w_in[:, n_qkv:n_qkv + D_POOL].astype(BF16),
         w_in[:, :n_qkv].astype(BF16)], axis=1)
    p["b_gate"] = _row(b_gate)
    p["b_none"] = jnp.zeros((1, PROJ_TN), F32)
    p["w_mem_kv"] = w_mem_kv.astype(BF16)
    p["w_pool"] = w_pool.astype(BF16)
    p["pool_scale"] = _row(pool_scale)
    for name, w in (("w_br_attn", w_br_attn), ("w_br_pool", w_br_pool), ("w_br_mem", w_br_mem), ("w_out", w_out)):
        p[name] = w.astype(BF16)
    for name, v in (("ffn1_norm_pre", ffn1_norm_pre), ("ffn1_norm_post", ffn1_norm_post),
                    ("mix_norm_pre", mix_norm_pre), ("mem_norm", mem_norm), ("mix_norm_post", mix_norm_post),
                    ("ffn2_norm_pre", ffn2_norm_pre), ("ffn2_norm_post", ffn2_norm_post),
                    ("final_norm", final_norm)):
        p[name] = _row(v)
    return p


def kernel(x_prompt, x_sample, mem_prompt, mem_sample, ffn1_norm_pre, ffn1_w_up, ffn1_w_down, ffn1_norm_post,
           mix_norm_pre, mem_norm, w_in, w_mem_kv, w_pool, pool_scale, w_br_attn, w_br_pool, w_br_mem, w_gate, b_gate,
           w_out, mix_norm_post, ffn2_norm_pre, ffn2_w_up, ffn2_w_down, ffn2_norm_post, final_norm):
    layer = [ffn1_norm_pre, ffn1_w_up, ffn1_w_down, ffn1_norm_post, mix_norm_pre, mem_norm, w_in, w_mem_kv, w_pool,
             pool_scale, w_br_attn, w_br_pool, w_br_mem, w_gate, b_gate, w_out, mix_norm_post, ffn2_norm_pre,
             ffn2_w_up, ffn2_w_down, ffn2_norm_post, final_norm]
    depth = ffn1_norm_pre.shape[0]
    slopes = jnp.asarray(_alibi_slopes())
    y_prompt, y_sample = x_prompt, x_sample
    for layer_idx in range(depth):
        p = _prep_params(*[w[layer_idx] for w in layer])
        y_prompt = _trunk(y_prompt, mem_prompt, p, slopes)
        y_sample = _trunk(y_sample, mem_sample, p, slopes)
    return (y_prompt, y_sample)
```

```python
import functools

import numpy as np
import jax
import jax.numpy as jnp
from jax import lax
from jax.experimental import pallas as pl
from jax.experimental.pallas import tpu as pltpu

F32 = jnp.float32
BF16 = jnp.bfloat16

N_MEM = 256
HEAD_DIM = 128
DIL_PAIRS = ((128, 1), (512, 4), (2048, 16))
HEADS_PER_GROUP = 4
N_GROUPS = len(DIL_PAIRS)
N_DIL_HEADS = HEADS_PER_GROUP * N_GROUPS
D_ATTN = N_DIL_HEADS * HEAD_DIM
POOL_WINDOWS = (2, 4, 8, 16)
POOL_GROUP = 256
D_POOL = POOL_GROUP * len(POOL_WINDOWS)
N_XHEADS = 4
XHEAD_DIM = 256
D_XATTN = N_XHEADS * XHEAD_DIM
N_BRANCH = 3
EPS = 1e-6
NEG = -1e30
BAND_RADIUS = 64

V7X_VMEM_BYTES = 64 * 1024 * 1024
MIB = 1024 * 1024

FFN_TM = 1024
FFN_TF = 512
PROJ_TM = 1024
PROJ_TN = 512
PROJ_ROWS = 256
ATT_QB = 128
ATT_KW = 256
ATT_UNROLL = 4
ATT_FIN_ROWS = 256
POOL_ROWS = 256
POOL_HALO = 16
MEM_TM = 512
MERGE_TM = 256


def _vmem_limit(nbytes):
    return int(min(nbytes * 5 // 4 + 8 * MIB, V7X_VMEM_BYTES - 6 * MIB))


def _rms(x, gain):
    ms = jnp.mean(x * x, axis=-1, keepdims=True)
    return x * lax.rsqrt(ms + EPS) * gain


def _alibi_slopes():
    s = 2.0 ** (-8.0 * np.arange(1, N_DIL_HEADS + 1) / N_DIL_HEADS)
    return s.reshape(HEADS_PER_GROUP, N_GROUPS).T.astype(np.float32)


def _ffn_kernel(x_ref, gpre_ref, wg_ref, wu_ref, wd_ref, gpost_ref, gfin_ref, o_ref, hs_ref, *, tf, rem, final):
    f = pl.program_id(1)
    last = pl.num_programs(1) - 1

    @pl.when(f == 0)
    def _():
        hs_ref[...] = _rms(x_ref[...], gpre_ref[...]).astype(BF16)
        o_ref[...] = jnp.zeros_like(o_ref)

    def step(valid):
        hs = hs_ref[...]
        g = jnp.dot(hs, wg_ref[:, :valid], preferred_element_type=F32)
        u = jnp.dot(hs, wu_ref[:, :valid], preferred_element_type=F32)
        a = (g * jax.nn.sigmoid(g) * u).astype(BF16)
        o_ref[...] += jnp.dot(a, wd_ref[:valid, :], preferred_element_type=F32)

    if rem == tf:
        step(tf)
    else:
        pl.when(f < last)(functools.partial(step, tf))
        pl.when(f == last)(functools.partial(step, rem))

    @pl.when(f == last)
    def _():
        y = x_ref[...] + 0.5 * _rms(o_ref[...], gpost_ref[...])
        if final:
            y = _rms(y, gfin_ref[...])
        o_ref[...] = y


def _ffn(x, gpre, wg, wu, wd, gpost, gfin, *, final):
    n, d = x.shape
    tm, tf = FFN_TM, FFN_TF
    dff = wd.shape[0]
    nf = pl.cdiv(dff, tf)
    rem = dff - (nf - 1) * tf
    est = 2 * tm * d * 4 * 2 + tm * d * 2 + 2 * 3 * d * tf * 2 + 3 * tm * 2 * tf * 4
    vec = pl.BlockSpec((1, d), lambda i, f: (0, 0))
    return pl.pallas_call(
        functools.partial(_ffn_kernel, tf=tf, rem=rem, final=final),
        grid=(n // tm, nf),
        in_specs=[
            pl.BlockSpec((tm, d), lambda i, f: (i, 0)),
            vec,
            pl.BlockSpec((d, tf), lambda i, f: (0, f)),
            pl.BlockSpec((d, tf), lambda i, f: (0, f)),
            pl.BlockSpec((tf, d), lambda i, f: (f, 0)),
            vec,
            vec,
        ],
        out_specs=pl.BlockSpec((tm, d), lambda i, f: (i, 0)),
        out_shape=jax.ShapeDtypeStruct((n, d), F32),
        scratch_shapes=[pltpu.VMEM((tm, d), BF16)],
        compiler_params=pltpu.CompilerParams(
            dimension_semantics=("arbitrary", "arbitrary"), vmem_limit_bytes=_vmem_limit(est)),
        name="ffn_final" if final else "ffn",
    )(x, gpre, wg, wu, wd, gpost, gfin)


def _proj_kernel(x_ref, g_ref, w_ref, b_ref, o_ref, hs_ref, *, n_gate):
    j = pl.program_id(1)

    @pl.when(j == 0)
    def _():
        hs_ref[...] = _rms(x_ref[...], g_ref[...]).astype(BF16)

    for r in range(0, x_ref.shape[0], PROJ_ROWS):
        rows = pl.ds(r, min(PROJ_ROWS, x_ref.shape[0] - r))
        acc = jnp.dot(hs_ref[rows, :], w_ref[...], preferred_element_type=F32)
        if n_gate > 0:
            acc = jnp.where(j < n_gate, jax.nn.sigmoid(acc + b_ref[...]), acc)
        o_ref[rows, :] = acc.astype(o_ref.dtype)


def _proj(x, gain, w, bias, *, n_gate_cols):
    n, d = x.shape
    ncol = w.shape[1]
    tm, tn = min(PROJ_TM, n), PROJ_TN
    n_gate = n_gate_cols // tn
    n_bias_blocks = max(bias.shape[1] // tn, 1)
    est = 2 * tm * d * 4 + tm * d * 2 + 2 * d * tn * 2 + 2 * tm * tn * 2 + 2 * tm * tn * 4
    return pl.pallas_call(
        functools.partial(_proj_kernel, n_gate=n_gate),
        grid=(n // tm, ncol // tn),
        in_specs=[
            pl.BlockSpec((tm, d), lambda i, j: (i, 0)),
            pl.BlockSpec((1, d), lambda i, j: (0, 0)),
            pl.BlockSpec((d, tn), lambda i, j: (0, j)),
            pl.BlockSpec((1, tn), lambda i, j: (0, jnp.minimum(j, n_bias_blocks - 1))),
        ],
        out_specs=pl.BlockSpec((tm, tn), lambda i, j: (i, j)),
        out_shape=jax.ShapeDtypeStruct((n, ncol), BF16),
        scratch_shapes=[pltpu.VMEM((tm, d), BF16)],
        compiler_params=pltpu.CompilerParams(
            dimension_semantics=("arbitrary", "arbitrary"), vmem_limit_bytes=_vmem_limit(est)),
        name="proj",
    )(x, gain, w, bias)


def _attn_kernel(slopes_ref, q_ref, k_ref, v_ref, o_ref,
                 m0_ref, m1_ref, m2_ref, l0_ref, l1_ref, l2_ref, n0_ref, n1_ref, n2_ref,
                 stage_ref, qd_ref, kd_ref, va_ref, bias_ref, *, seq):
    h = pl.program_id(1)
    g = pl.program_id(2)
    scale = HEAD_DIM ** -0.5
    qb = ATT_QB
    m_refs = (m0_ref, m1_ref, m2_ref)
    l_refs = (l0_ref, l1_ref, l2_ref)
    n_refs = (n0_ref, n1_ref, n2_ref)

    va_ref[:, HEAD_DIM:] = jnp.ones((seq, HEAD_DIM), BF16)

    def run_group(gi, d):
        sub_len = seq // d
        kw = min(ATT_KW, sub_len)
        nb = sub_len // qb
        slope_d = slopes_ref[gi, h] * float(d)

        row = lax.broadcasted_iota(jnp.int32, (qb, kw), 0)
        col = lax.broadcasted_iota(jnp.int32, (qb, kw), 1)
        for kind, off in enumerate((0, -BAND_RADIUS, qb - kw)):
            arel = jnp.abs(col - row + off)
            bias_ref[kind, :, :kw] = jnp.where(arel <= BAND_RADIUS, -slope_d * arel.astype(F32), NEG)

        if d == 1:
            qs, ks = q_ref, k_ref
            va_ref[:, :HEAD_DIM] = v_ref[...]
        else:
            for src, dst, lanes in ((q_ref, qd_ref, None), (k_ref, kd_ref, None), (v_ref, va_ref, HEAD_DIM)):
                stage_ref[...] = src[...].astype(F32)
                for res in range(d):
                    val = stage_ref[pl.ds(res, sub_len, stride=d), :].astype(BF16)
                    if lanes is None:
                        dst[pl.ds(res * sub_len, sub_len), :] = val
                    else:
                        dst[pl.ds(res * sub_len, sub_len), :lanes] = val
            qs, ks = qd_ref, kd_ref

        def block(idx):
            res = idx // nb
            i = idx - res * nb
            q0 = i * qb
            k0 = jnp.clip(q0 - BAND_RADIUS, 0, sub_len - kw)
            qstart = pl.multiple_of(idx * qb, qb)
            kstart = pl.multiple_of(res * sub_len + k0, BAND_RADIUS)
            s = lax.dot_general(qs[pl.ds(qstart, qb), :], ks[pl.ds(kstart, kw), :],
                                (((1,), (1,)), ((), ())), preferred_element_type=F32)
            if nb == 1:
                bias = bias_ref[0, :, :kw]
            else:
                kind = jnp.where(i == 0, 0, jnp.where(i == nb - 1, 2, 1))
                bias = bias_ref[kind]
            s = s * scale + bias
            m = jnp.max(s, axis=-1, keepdims=True)
            p = jnp.exp(s - m).astype(BF16)
            na = jnp.dot(p, va_ref[pl.ds(kstart, kw), :], preferred_element_type=F32)
            if d == 1:
                rows = pl.ds(qstart, qb)
            else:
                rows = pl.ds(q0 * d + res, qb, stride=d)
            m_refs[gi][rows, :] = jnp.broadcast_to(m, (qb, HEAD_DIM))
            l_refs[gi][rows, :] = na[:, HEAD_DIM:]
            n_refs[gi][rows, :] = na[:, :HEAD_DIM]

        def blocks(it, carry):
            for u in range(ATT_UNROLL):
                block(it * ATT_UNROLL + u)
            return carry

        lax.fori_loop(0, d * nb // ATT_UNROLL, blocks, 0)

    for gi, (_, d) in enumerate(DIL_PAIRS):
        pl.when(g == gi)(functools.partial(run_group, gi, d))

    @pl.when(g == N_GROUPS - 1)
    def _():
        def fin(c, carry):
            r = pl.ds(pl.multiple_of(c * ATT_FIN_ROWS, ATT_FIN_ROWS), ATT_FIN_ROWS)
            ms = [m_ref[r, :] for m_ref in m_refs]
            mx = jnp.maximum(jnp.maximum(ms[0], ms[1]), ms[2])
            num = jnp.zeros((ATT_FIN_ROWS, HEAD_DIM), F32)
            den = jnp.zeros((ATT_FIN_ROWS, HEAD_DIM), F32)
            for gi in range(N_GROUPS):
                e = jnp.exp(ms[gi] - mx)
                num += e * n_refs[gi][r, :]
                den += e * l_refs[gi][r, :]
            o_ref[r, :] = (num / den).astype(o_ref.dtype)
            return carry

        lax.fori_loop(0, seq // ATT_FIN_ROWS, fin, 0)


def _attention(pg, slopes, *, batch, seq, col0):
    blk0 = col0 // HEAD_DIM

    def qkv_spec(which):
        return pl.BlockSpec(
            (seq, HEAD_DIM), lambda b, h, g: (b, blk0 + (which * N_GROUPS + g) * HEADS_PER_GROUP + h))

    tile = seq * HEAD_DIM
    est = 4 * 2 * tile * 2 + 10 * tile * 4 + 4 * tile * 2 + 3 * ATT_QB * ATT_KW * 4
    state = [pltpu.VMEM((seq, HEAD_DIM), F32) for _ in range(3 * N_GROUPS)]
    return pl.pallas_call(
        functools.partial(_attn_kernel, seq=seq),
        grid=(batch, HEADS_PER_GROUP, N_GROUPS),
        in_specs=[pl.BlockSpec(memory_space=pltpu.SMEM), qkv_spec(0), qkv_spec(1), qkv_spec(2)],
        out_specs=pl.BlockSpec((seq, HEAD_DIM), lambda b, h, g: (b, h)),
        out_shape=jax.ShapeDtypeStruct((batch * seq, HEADS_PER_GROUP * HEAD_DIM), BF16),
        scratch_shapes=state + [
            pltpu.VMEM((seq, HEAD_DIM), F32),
            pltpu.VMEM((seq, HEAD_DIM), BF16),
            pltpu.VMEM((seq, HEAD_DIM), BF16),
            pltpu.VMEM((seq, 2 * HEAD_DIM), BF16),
            pltpu.VMEM((3, ATT_QB, ATT_KW), F32),
        ],
        compiler_params=pltpu.CompilerParams(
            dimension_semantics=("arbitrary", "arbitrary", "arbitrary"), vmem_limit_bytes=_vmem_limit(est)),
        name="attention",
    )(slopes, pg, pg, pg)


def _pool_kernel(u_ref, w_ref, sc_ref, o_ref, pad_ref, *, seq):
    grp = pl.program_id(1)
    halo, rows = POOL_HALO, POOL_ROWS
    ext = rows + 2 * halo
    zeros = jnp.zeros((halo, POOL_GROUP), F32)
    pad_ref[pl.ds(0, halo), :] = zeros
    pad_ref[pl.ds(halo + seq, halo), :] = zeros
    pad_ref[pl.ds(halo, seq), :] = u_ref[...].astype(F32)

    def run(window):
        def chunk(c, carry):
            base = pl.multiple_of(c * rows, rows)
            p = pad_ref[pl.ds(base, ext), :]
            s = p + pltpu.roll(p, 1, 0)
            half = 1
            while 2 * half < window:
                s = pltpu.roll(s, ext - half, 0) + pltpu.roll(s, half, 0)
                half *= 2
            s = s[halo:halo + rows, :]
            tok = p[halo:halo + rows, :]
            t = base + lax.broadcasted_iota(jnp.int32, (rows, 1), 0)
            lo = jnp.maximum(t - window // 2, 0)
            hi = jnp.minimum(t - window // 2 + window, seq)
            pooled = s / (hi - lo).astype(F32) - tok
            y = jnp.dot(pooled.astype(BF16), w_ref[...], preferred_element_type=F32) * sc_ref[...]
            o_ref[pl.ds(base, rows), :] = y.astype(o_ref.dtype)
            return carry

        lax.fori_loop(0, seq // rows, chunk, 0)

    for gi, window in enumerate(POOL_WINDOWS):
        pl.when(grp == gi)(functools.partial(run, window))


def _pool(pg, w_pool, pool_scale, *, batch, seq, col0):
    blk0 = col0 // POOL_GROUP
    ngrp = len(POOL_WINDOWS)
    est = 2 * seq * POOL_GROUP * 2 * 2 + (seq + 2 * POOL_HALO) * POOL_GROUP * 4 + 2 * POOL_GROUP * POOL_GROUP * 2
    return pl.pallas_call(
        functools.partial(_pool_kernel, seq=seq),
        grid=(batch, ngrp),
        in_specs=[
            pl.BlockSpec((seq, POOL_GROUP), lambda b, g: (b, blk0 + g)),
            pl.BlockSpec((None, POOL_GROUP, POOL_GROUP), lambda b, g: (g, 0, 0)),
            pl.BlockSpec((1, POOL_GROUP), lambda b, g: (0, g)),
        ],
        out_specs=pl.BlockSpec((seq, POOL_GROUP), lambda b, g: (b, g)),
        out_shape=jax.ShapeDtypeStruct((batch * seq, D_POOL), BF16),
        scratch_shapes=[pltpu.VMEM((seq + 2 * POOL_HALO, POOL_GROUP), F32)],
        compiler_params=pltpu.CompilerParams(
            dimension_semantics=("arbitrary", "arbitrary"), vmem_limit_bytes=_vmem_limit(est)),
        name="pool",
    )(pg, w_pool, pool_scale)


def _memattn_kernel(q_ref, kv_ref, o_ref):
    scale = XHEAD_DIM ** -0.5
    for hh in range(N_XHEADS):
        lo = hh * XHEAD_DIM
        q = q_ref[:, lo:lo + XHEAD_DIM]
        k = kv_ref[:, lo:lo + XHEAD_DIM]
        v = kv_ref[:, D_XATTN + lo:D_XATTN + lo + XHEAD_DIM]
        s = lax.dot_general(q, k, (((1,), (1,)), ((), ())), preferred_element_type=F32) * scale
        m = jnp.max(s, axis=-1, keepdims=True)
        p = jnp.exp(s - m)
        l = jnp.sum(p, axis=-1, keepdims=True)
        y = jnp.dot(p.astype(BF16), v, preferred_element_type=F32) / l
        o_ref[:, lo:lo + XHEAD_DIM] = y.astype(o_ref.dtype)


def _memattn(pg, kv, *, seq, col0):
    n = pg.shape[0]
    tm = MEM_TM
    blk0 = col0 // D_XATTN
    tiles_per_seq = seq // tm
    est = 2 * tm * D_XATTN * 2 * 2 + 2 * N_MEM * 2 * D_XATTN * 2 + 4 * tm * N_MEM * 4
    return pl.pallas_call(
        _memattn_kernel,
        grid=(n // tm,),
        in_specs=[
            pl.BlockSpec((tm, D_XATTN), lambda i: (i, blk0)),
            pl.BlockSpec((N_MEM, 2 * D_XATTN), lambda i: (i // tiles_per_seq, 0)),
        ],
        out_specs=pl.BlockSpec((tm, D_XATTN), lambda i: (i, 0)),
        out_shape=jax.ShapeDtypeStruct((n, D_XATTN), BF16),
        compiler_params=pltpu.CompilerParams(
            dimension_semantics=("arbitrary",), vmem_limit_bytes=_vmem_limit(est)),
        name="memattn",
    )(pg, kv)


def _merge_kernel(gates_ref, ya_ref, yp_ref, ym_ref, x_ref, wa_ref, wp_ref, wm_ref, wo_ref, gain_ref, o_ref):
    d = x_ref.shape[1]
    merged = gates_ref[:, 0:d].astype(F32) * jnp.dot(ya_ref[...], wa_ref[...], preferred_element_type=F32)
    merged += gates_ref[:, d:2 * d].astype(F32) * jnp.dot(yp_ref[...], wp_ref[...], preferred_element_type=F32)
    merged += gates_ref[:, 2 * d:3 * d].astype(F32) * jnp.dot(ym_ref[...], wm_ref[...], preferred_element_type=F32)
    z = jnp.dot(merged.astype(BF16), wo_ref[...], preferred_element_type=F32)
    o_ref[...] = x_ref[...] + _rms(z, gain_ref[...])


def _merge(pg, y_attn, y_pool, y_mem, x, wa, wp, wm, wo, gain):
    n, d = x.shape
    tm = MERGE_TM
    wbytes = (wa.size + wp.size + wm.size + wo.size) * 2

    def const(shape):
        return pl.BlockSpec(shape, lambda i: (0, 0), pipeline_mode=pl.Buffered(1))

    est = (2 * tm * (N_BRANCH * d + y_attn.shape[1] + y_pool.shape[1] + y_mem.shape[1]) * 2
           + 4 * tm * d * 4 + wbytes + 4 * tm * d * 4)
    return pl.pallas_call(
        _merge_kernel,
        grid=(n // tm,),
        in_specs=[
            pl.BlockSpec((tm, N_BRANCH * d), lambda i: (i, 0)),
            pl.BlockSpec((tm, y_attn.shape[1]), lambda i: (i, 0)),
            pl.BlockSpec((tm, y_pool.shape[1]), lambda i: (i, 0)),
            pl.BlockSpec((tm, y_mem.shape[1]), lambda i: (i, 0)),
            pl.BlockSpec((tm, d), lambda i: (i, 0)),
            const(wa.shape), const(wp.shape), const(wm.shape), const(wo.shape),
            pl.BlockSpec((1, d), lambda i: (0, 0)),
        ],
        out_specs=pl.BlockSpec((tm, d), lambda i: (i, 0)),
        out_shape=jax.ShapeDtypeStruct((n, d), F32),
        compiler_params=pltpu.CompilerParams(
            dimension_semantics=("arbitrary",), vmem_limit_bytes=_vmem_limit(est)),
        name="merge",
    )(pg, y_attn, y_pool, y_mem, x, wa, wp, wm, wo, gain)


def _row(v):
    return v.reshape(1, -1).astype(F32)


def _trunk(x, mem, p, slopes):
    batch, seq, d = x.shape
    xf = x.reshape(batch * seq, d)
    x1 = _ffn(xf, p["ffn1_norm_pre"], p["ffn1_wg"], p["ffn1_wu"], p["ffn1_wd"], p["ffn1_norm_post"],
              p["ffn1_norm_post"], final=False)
    pg = _proj(x1, p["mix_norm_pre"], p["w_cat"], p["b_gate"], n_gate_cols=N_BRANCH * d)
    col_qmem = N_BRANCH * d
    col_pool = col_qmem + D_XATTN
    col_qkv = col_pool + D_POOL
    kv = _proj(mem.reshape(batch * N_MEM, d), p["mem_norm"], p["w_mem_kv"], p["b_none"], n_gate_cols=0)
    y_attn = _attention(pg, slopes, batch=batch, seq=seq, col0=col_qkv)
    y_pool = _pool(pg, p["w_pool"], p["pool_scale"], batch=batch, seq=seq, col0=col_pool)
    y_mem = _memattn(pg, kv, seq=seq, col0=col_qmem)
    x2 = _merge(pg, y_attn, y_pool, y_mem, x1, p["w_br_attn"], p["w_br_pool"], p["w_br_mem"], p["w_out"],
                p["mix_norm_post"])
    y = _ffn(x2, p["ffn2_norm_pre"], p["ffn2_wg"], p["ffn2_wu"], p["ffn2_wd"], p["ffn2_norm_post"],
             p["final_norm"], final=True)
    return y.reshape(batch, seq, d)


def _prep_params(ffn1_norm_pre, ffn1_w_up, ffn1_w_down, ffn1_norm_post, mix_norm_pre, mem_norm, w_in, w_mem_kv,
                 w_pool, pool_scale, w_br_attn, w_br_pool, w_br_mem, w_gate, b_gate, w_out, mix_norm_post,
                 ffn2_norm_pre, ffn2_w_up, ffn2_w_down, ffn2_norm_post, final_norm):
    p = {}
    for name, w_up, w_down in (("ffn1", ffn1_w_up, ffn1_w_down), ("ffn2", ffn2_w_up, ffn2_w_down)):
        dff = w_down.shape[0]
        p[name + "_wg"] = w_up[:, :dff].astype(BF16)
        p[name + "_wu"] = w_up[:, dff:].astype(BF16)
        p[name + "_wd"] = w_down.astype(BF16)
    n_qkv = 3 * D_ATTN
    p["w_cat"] = jnp.concatenate(
        [w_gate.astype(BF16), w_in[:, n_qkv + D_POOL:].astype(BF16), w_in[:, n_qkv:n_qkv + D_POOL].astype(BF16),
         w_in[:, :n_qkv].astype(BF16)], axis=1)
    p["b_gate"] = _row(b_gate)
    p["b_none"] = jnp.zeros((1, PROJ_TN), F32)
    p["w_mem_kv"] = w_mem_kv.astype(BF16)
    p["w_pool"] = w_pool.astype(BF16)
    p["pool_scale"] = _row(pool_scale)
    for name, w in (("w_br_attn", w_br_attn), ("w_br_pool", w_br_pool), ("w_br_mem", w_br_mem), ("w_out", w_out)):
        p[name] = w.astype(BF16)
    for name, v in (("ffn1_norm_pre", ffn1_norm_pre), ("ffn1_norm_post", ffn1_norm_post),
                    ("mix_norm_pre", mix_norm_pre), ("mem_norm", mem_norm), ("mix_norm_post", mix_norm_post),
                    ("ffn2_norm_pre", ffn2_norm_pre), ("ffn2_norm_post", ffn2_norm_post),
                    ("final_norm", final_norm)):
        p[name] = _row(v)
    return p


def kernel(x_prompt, x_sample, mem_prompt, mem_sample, ffn1_norm_pre, ffn1_w_up, ffn1_w_down, ffn1_norm_post,
           mix_norm_pre, mem_norm, w_in, w_mem_kv, w_pool, pool_scale, w_br_attn, w_br_pool, w_br_mem, w_gate, b_gate,
           w_out, mix_norm_post, ffn2_norm_pre, ffn2_w_up, ffn2_w_down, ffn2_norm_post, final_norm):
    layer = [ffn1_norm_pre, ffn1_w_up, ffn1_w_down, ffn1_norm_post, mix_norm_pre, mem_norm, w_in, w_mem_kv, w_pool,
             pool_scale, w_br_attn, w_br_pool, w_br_mem, w_gate, b_gate, w_out, mix_norm_post, ffn2_norm_pre,
             ffn2_w_up, ffn2_w_down, ffn2_norm_post, final_norm]
    depth = ffn1_norm_pre.shape[0]
    slopes = jnp.asarray(_alibi_slopes())
    y_prompt, y_sample = x_prompt, x_sample
    for layer_idx in range(depth):
        p = _prep_params(*[w[layer_idx] for w in layer])
        y_prompt = _trunk(y_prompt, mem_prompt, p, slopes)
        y_sample = _trunk(y_sample, mem_sample, p, slopes)
    return (y_prompt, y_sample)
```

```python
import functools

import numpy as np
import jax
import jax.numpy as jnp
from jax import lax
from jax.experimental import pallas as pl
from jax.experimental.pallas import tpu as pltpu

F32 = jnp.float32
BF16 = jnp.bfloat16

D_MODEL = 2048
N_MEM = 256
HEAD_DIM = 128
DIL_PAIRS = ((128, 1), (512, 4), (2048, 16))
HEADS_PER_GROUP = 4
N_GROUPS = len(DIL_PAIRS)
N_DIL_HEADS = HEADS_PER_GROUP * N_GROUPS
D_ATTN = N_DIL_HEADS * HEAD_DIM
POOL_WINDOWS = (2, 4, 8, 16)
POOL_GROUP = 256
D_POOL = POOL_GROUP * len(POOL_WINDOWS)
N_XHEADS = 4
XHEAD_DIM = 256
D_XATTN = N_XHEADS * XHEAD_DIM
N_BRANCH = 3
EPS = 1e-6
NEG = -1e30
BAND_RADIUS = 64

V7X_VMEM_BYTES = 64 * 1024 * 1024
MIB = 1024 * 1024

FFN_TM = 1024
FFN_TF = 512
PROJ_TM = 1024
PROJ_TN = 512
PROJ_TG = 1024
PROJ_ROWS = 256
ATT_QB = 128
ATT_KW = 256
ATT_UNROLL = 4
ATT_FIN_ROWS = 256
POOL_ROWS = 256
POOL_HALO = 16
MEM_TM = 512
MERGE_TM = 256
CAST_BLOCK_BYTES = 8 * MIB


def _vmem_limit(nbytes):
    return int(min(nbytes * 5 // 4 + 8 * MIB, V7X_VMEM_BYTES - 6 * MIB))


def _rms(x, gain):
    ms = jnp.mean(x * x, axis=-1, keepdims=True)
    return x * lax.rsqrt(ms + EPS) * gain


def _alibi_slopes():
    s = 2.0 ** (-8.0 * np.arange(1, N_DIL_HEADS + 1) / N_DIL_HEADS)
    return s.reshape(HEADS_PER_GROUP, N_GROUPS).T.astype(np.float32)


def _ffn_kernel(x_ref, gpre_ref, wg_ref, wu_ref, wd_ref, gpost_ref, gfin_ref, o_ref, hs_ref, *, tf, rem, final):
    f = pl.program_id(1)
    last = pl.num_programs(1) - 1

    @pl.when(f == 0)
    def _():
        hs_ref[...] = _rms(x_ref[...], gpre_ref[...]).astype(BF16)
        o_ref[...] = jnp.zeros_like(o_ref)

    def step(valid):
        hs = hs_ref[...]
        g = jnp.dot(hs, wg_ref[:, :valid], preferred_element_type=F32)
        u = jnp.dot(hs, wu_ref[:, :valid], preferred_element_type=F32)
        a = (g * jax.nn.sigmoid(g) * u).astype(BF16)
        o_ref[...] += jnp.dot(a, wd_ref[:valid, :], preferred_element_type=F32)

    if rem == tf:
        step(tf)
    else:
        pl.when(f < last)(functools.partial(step, tf))
        pl.when(f == last)(functools.partial(step, rem))

    @pl.when(f == last)
    def _():
        y = x_ref[...] + 0.5 * _rms(o_ref[...], gpost_ref[...])
        if final:
            y = _rms(y, gfin_ref[...])
        o_ref[...] = y


def _ffn(x, gpre, wg, wu, wd, gpost, gfin, *, final):
    n, d = x.shape
    tm, tf = FFN_TM, FFN_TF
    dff = wd.shape[0]
    nf = pl.cdiv(dff, tf)
    rem = dff - (nf - 1) * tf
    est = 2 * tm * d * 4 * 2 + tm * d * 2 + 2 * 3 * d * tf * 2 + 3 * tm * 2 * tf * 4
    vec = pl.BlockSpec((1, d), lambda i, f: (0, 0))
    return pl.pallas_call(
        functools.partial(_ffn_kernel, tf=tf, rem=rem, final=final),
        grid=(n // tm, nf),
        in_specs=[
            pl.BlockSpec((tm, d), lambda i, f: (i, 0)),
            vec,
            pl.BlockSpec((d, tf), lambda i, f: (0, f)),
            pl.BlockSpec((d, tf), lambda i, f: (0, f)),
            pl.BlockSpec((tf, d), lambda i, f: (f, 0)),
            vec,
            vec,
        ],
        out_specs=pl.BlockSpec((tm, d), lambda i, f: (i, 0)),
        out_shape=jax.ShapeDtypeStruct((n, d), F32),
        scratch_shapes=[pltpu.VMEM((tm, d), BF16)],
        compiler_params=pltpu.CompilerParams(
            dimension_semantics=("arbitrary", "arbitrary"), vmem_limit_bytes=_vmem_limit(est)),
        name="ffn_final" if final else "ffn",
    )(x, gpre, wg, wu, wd, gpost, gfin)


def _kvproj_kernel(x_ref, g_ref, w_ref, o_ref, hs_ref):
    @pl.when(pl.program_id(1) == 0)
    def _():
        hs_ref[...] = _rms(x_ref[...], g_ref[...]).astype(BF16)

    o_ref[...] = jnp.dot(hs_ref[...], w_ref[...], preferred_element_type=F32).astype(o_ref.dtype)


def _kvproj(x, gain, w):
    n, d = x.shape
    ncol = w.shape[1]
    tm, tn = n, PROJ_TN
    est = 2 * tm * d * 4 + tm * d * 2 + 2 * d * tn * 2 + 2 * tm * tn * 2 + 2 * tm * tn * 4
    return pl.pallas_call(
        _kvproj_kernel,
        grid=(n // tm, ncol // tn),
        in_specs=[
            pl.BlockSpec((tm, d), lambda i, j: (i, 0)),
            pl.BlockSpec((1, d), lambda i, j: (0, 0)),
            pl.BlockSpec((d, tn), lambda i, j: (0, j)),
        ],
        out_specs=pl.BlockSpec((tm, tn), lambda i, j: (i, j)),
        out_shape=jax.ShapeDtypeStruct((n, ncol), BF16),
        scratch_shapes=[pltpu.VMEM((tm, d), BF16)],
        compiler_params=pltpu.CompilerParams(
            dimension_semantics=("arbitrary", "arbitrary"), vmem_limit_bytes=_vmem_limit(est)),
        name="kvproj",
    )(x, gain, w)


N_GATE_STEPS = N_BRANCH * D_MODEL // PROJ_TG
NAT_CHUNKS = (11, 12, 9, 10, 0, 3, 6)
N_NAT_STEPS = len(NAT_CHUNKS)
N_DIL_STEPS = 3
MIX_STEPS = N_GATE_STEPS + N_NAT_STEPS + 2 * N_DIL_STEPS


def _lin_chunk(s):
    t = jnp.maximum(s - N_GATE_STEPS, 0)
    nat = jnp.where(t < 2, 11 + t, jnp.where(t < 4, 7 + t, 3 * (t - 4)))
    d4 = 3 * (t - N_NAT_STEPS) + 1
    d16 = 3 * (t - N_NAT_STEPS - N_DIL_STEPS) + 2
    return jnp.where(t < N_NAT_STEPS, nat, jnp.where(t < N_NAT_STEPS + N_DIL_STEPS, d4, d16))


def _mixproj_kernel(x_ref, g_ref, wg_ref, b_ref, wi_ref, og_ref, on_ref, o4_ref, o16_ref, hs_ref, stage_ref):
    s = pl.program_id(1)
    tm = x_ref.shape[0]
    row_chunks = [pl.ds(r, PROJ_ROWS) for r in range(0, tm, PROJ_ROWS)]

    @pl.when(s == 0)
    def _():
        hs_ref[...] = _rms(x_ref[...], g_ref[...]).astype(BF16)

    @pl.when(s < N_GATE_STEPS)
    def _():
        for rows in row_chunks:
            acc = jnp.dot(hs_ref[rows, :], wg_ref[...], preferred_element_type=F32)
            og_ref[rows, :] = jax.nn.sigmoid(acc + b_ref[...]).astype(og_ref.dtype)

    @pl.when((s >= N_GATE_STEPS) & (s < N_GATE_STEPS + N_NAT_STEPS))
    def _():
        for rows in row_chunks:
            on_ref[rows, :] = jnp.dot(hs_ref[rows, :], wi_ref[...], preferred_element_type=F32).astype(on_ref.dtype)

    def dilated(o_ref, d):
        per_res = PROJ_ROWS // d
        for rc, rows in enumerate(row_chunks):
            acc = jnp.dot(hs_ref[rows, :], wi_ref[...], preferred_element_type=F32)
            for c in range(PROJ_TN // HEAD_DIM):
                stage_ref[c] = acc[:, c * HEAD_DIM:(c + 1) * HEAD_DIM]
            for res in range(d):
                for c in range(PROJ_TN // HEAD_DIM):
                    o_ref[res, pl.ds(rc * per_res, per_res), c * HEAD_DIM:(c + 1) * HEAD_DIM] = (
                        stage_ref[c, pl.ds(res, per_res, stride=d), :].astype(o_ref.dtype))

    first_d4 = N_GATE_STEPS + N_NAT_STEPS
    pl.when((s >= first_d4) & (s < first_d4 + N_DIL_STEPS))(functools.partial(dilated, o4_ref, DIL_PAIRS[1][1]))
    pl.when(s >= first_d4 + N_DIL_STEPS)(functools.partial(dilated, o16_ref, DIL_PAIRS[2][1]))


def _mixproj(x, gain, w_gate, b_gate, w_in, *, batch, seq):
    n, d = x.shape
    tm, tn, tg = PROJ_TM, PROJ_TN, PROJ_TG
    tiles_per_seq = seq // tm
    d4, d16 = DIL_PAIRS[1][1], DIL_PAIRS[2][1]
    first_nat = N_GATE_STEPS
    first_d4 = first_nat + N_NAT_STEPS
    first_d16 = first_d4 + N_DIL_STEPS

    def gate_idx(s):
        return jnp.minimum(s, N_GATE_STEPS - 1)

    def dil_spec(dil, first):
        return pl.BlockSpec(
            (None, dil, tm // dil, tn),
            lambda i, s: (i // tiles_per_seq, 0, i % tiles_per_seq, jnp.clip(s - first, 0, N_DIL_STEPS - 1)))

    est = (2 * tm * d * 4 + tm * d * 2 + 2 * d * (tg + tn) * 2 + 2 * tm * (tg + 3 * tn) * 2
           + PROJ_ROWS * tn * 4 + 2 * PROJ_ROWS * tg * 4)
    return pl.pallas_call(
        _mixproj_kernel,
        grid=(n // tm, MIX_STEPS),
        in_specs=[
            pl.BlockSpec((tm, d), lambda i, s: (i, 0)),
            pl.BlockSpec((1, d), lambda i, s: (0, 0)),
            pl.BlockSpec((d, tg), lambda i, s: (0, gate_idx(s))),
            pl.BlockSpec((1, tg), lambda i, s: (0, gate_idx(s))),
            pl.BlockSpec((d, tn), lambda i, s: (0, _lin_chunk(s))),
        ],
        out_specs=[
            pl.BlockSpec((tm, tg), lambda i, s: (i, gate_idx(s))),
            pl.BlockSpec((tm, tn), lambda i, s: (i, jnp.clip(s - first_nat, 0, N_NAT_STEPS - 1))),
            dil_spec(d4, first_d4),
            dil_spec(d16, first_d16),
        ],
        out_shape=[
            jax.ShapeDtypeStruct((n, N_BRANCH * d), BF16),
            jax.ShapeDtypeStruct((n, N_NAT_STEPS * tn), BF16),
            jax.ShapeDtypeStruct((batch, d4, seq // d4, N_DIL_STEPS * tn), BF16),
            jax.ShapeDtypeStruct((batch, d16, seq // d16, N_DIL_STEPS * tn), BF16),
        ],
        scratch_shapes=[pltpu.VMEM((tm, d), BF16), pltpu.VMEM((tn // HEAD_DIM, PROJ_ROWS, HEAD_DIM), F32)],
        compiler_params=pltpu.CompilerParams(
            dimension_semantics=("arbitrary", "arbitrary"), vmem_limit_bytes=_vmem_limit(est)),
        name="mixproj",
    )(x, gain, w_gate, b_gate, w_in)


def _attn_kernel(slopes_ref, q0_ref, k0_ref, v0_ref, q1_ref, k1_ref, v1_ref, q2_ref, k2_ref, v2_ref, o_ref,
                 m0_ref, m1_ref, m2_ref, l0_ref, l1_ref, l2_ref, n0_ref, n1_ref, n2_ref, va_ref, bias_ref, *, seq):
    h = pl.program_id(1)
    scale = HEAD_DIM ** -0.5
    qb = ATT_QB
    qkv_refs = ((q0_ref, k0_ref, v0_ref), (q1_ref, k1_ref, v1_ref), (q2_ref, k2_ref, v2_ref))
    m_refs = (m0_ref, m1_ref, m2_ref)
    l_refs = (l0_ref, l1_ref, l2_ref)
    n_refs = (n0_ref, n1_ref, n2_ref)

    va_ref[:, HEAD_DIM:] = jnp.ones((seq, HEAD_DIM), BF16)

    for gi, (_, d) in enumerate(DIL_PAIRS):
        q_ref, k_ref, v_ref = qkv_refs[gi]
        sub_len = seq // d
        kw = min(ATT_KW, sub_len)
        nb = sub_len // qb
        slope_d = slopes_ref[gi, h] * float(d)

        row = lax.broadcasted_iota(jnp.int32, (qb, kw), 0)
        col = lax.broadcasted_iota(jnp.int32, (qb, kw), 1)
        for kind, off in enumerate((0, -BAND_RADIUS, qb - kw)):
            arel = jnp.abs(col - row + off)
            bias_ref[kind, :, :kw] = jnp.where(arel <= BAND_RADIUS, -slope_d * arel.astype(F32), NEG)

        if d == 1:
            va_ref[:, :HEAD_DIM] = v_ref[...]
        else:
            for res in range(d):
                va_ref[pl.ds(res * sub_len, sub_len), :HEAD_DIM] = v_ref[res]

        def block(idx, gi=gi, d=d, q_ref=q_ref, k_ref=k_ref, sub_len=sub_len, kw=kw, nb=nb):
            res = idx // nb
            i = idx - res * nb
            q0 = pl.multiple_of(i * qb, qb)
            k0 = pl.multiple_of(jnp.clip(q0 - BAND_RADIUS, 0, sub_len - kw), BAND_RADIUS)
            if d == 1:
                qblk = q_ref[pl.ds(q0, qb), :]
                kwin = k_ref[pl.ds(k0, kw), :]
            else:
                qblk = q_ref[res, pl.ds(q0, qb), :]
                kwin = k_ref[res, pl.ds(k0, kw), :]
            s = lax.dot_general(qblk, kwin, (((1,), (1,)), ((), ())), preferred_element_type=F32)
            if nb == 1:
                bias = bias_ref[0, :, :kw]
            else:
                kind = jnp.where(i == 0, 0, jnp.where(i == nb - 1, 2, 1))
                bias = bias_ref[kind]
            s = s * scale + bias
            m = jnp.max(s, axis=-1, keepdims=True)
            p = jnp.exp(s - m).astype(BF16)
            kstart = pl.multiple_of(res * sub_len + k0, BAND_RADIUS)
            na = jnp.dot(p, va_ref[pl.ds(kstart, kw), :], preferred_element_type=F32)
            if d == 1:
                rows = pl.ds(q0, qb)
            else:
                rows = pl.ds(q0 * d + res, qb, stride=d)
            m_refs[gi][rows, :] = jnp.broadcast_to(m, (qb, HEAD_DIM))
            l_refs[gi][rows, :] = na[:, HEAD_DIM:]
            n_refs[gi][rows, :] = na[:, :HEAD_DIM]

        def blocks(it, carry, block=block):
            for u in range(ATT_UNROLL):
                block(it * ATT_UNROLL + u)
            return carry

        lax.fori_loop(0, d * nb // ATT_UNROLL, blocks, 0)

    def fin(c, carry):
        r = pl.ds(pl.multiple_of(c * ATT_FIN_ROWS, ATT_FIN_ROWS), ATT_FIN_ROWS)
        ms = [m_ref[r, :] for m_ref in m_refs]
        mx = jnp.maximum(jnp.maximum(ms[0], ms[1]), ms[2])
        num = jnp.zeros((ATT_FIN_ROWS, HEAD_DIM), F32)
        den = jnp.zeros((ATT_FIN_ROWS, HEAD_DIM), F32)
        for gi in range(N_GROUPS):
            e = jnp.exp(ms[gi] - mx)
            num += e * n_refs[gi][r, :]
            den += e * l_refs[gi][r, :]
        o_ref[r, :] = (num / den).astype(o_ref.dtype)
        return carry

    lax.fori_loop(0, seq // ATT_FIN_ROWS, fin, 0)


def _attention(nat, qkv4, qkv16, slopes, *, batch, seq, col0):
    blk0 = col0 // HEAD_DIM
    d4, d16 = DIL_PAIRS[1][1], DIL_PAIRS[2][1]

    def nat_map(b, h, *, which):
        return (b, blk0 + which * HEADS_PER_GROUP + h)

    def dil_map(b, h, *, which):
        return (b, 0, 0, which * HEADS_PER_GROUP + h)

    in_specs = [pl.BlockSpec(memory_space=pltpu.SMEM)]
    in_specs += [pl.BlockSpec((seq, HEAD_DIM), functools.partial(nat_map, which=w)) for w in range(3)]
    for dil in (d4, d16):
        in_specs += [pl.BlockSpec((None, dil, seq // dil, HEAD_DIM), functools.partial(dil_map, which=w))
                     for w in range(3)]
    tile = seq * HEAD_DIM
    est = 10 * 2 * tile * 2 + 9 * tile * 4 + 2 * tile * 2 + 3 * ATT_QB * ATT_KW * 4
    state = [pltpu.VMEM((seq, HEAD_DIM), F32) for _ in range(3 * N_GROUPS)]
    return pl.pallas_call(
        functools.partial(_attn_kernel, seq=seq),
        grid=(batch, HEADS_PER_GROUP),
        in_specs=in_specs,
        out_specs=pl.BlockSpec((seq, HEAD_DIM), lambda b, h: (b, h)),
        out_shape=jax.ShapeDtypeStruct((batch * seq, HEADS_PER_GROUP * HEAD_DIM), BF16),
        scratch_shapes=state + [
            pltpu.VMEM((seq, 2 * HEAD_DIM), BF16),
            pltpu.VMEM((3, ATT_QB, ATT_KW), F32),
        ],
        compiler_params=pltpu.CompilerParams(
            dimension_semantics=("arbitrary", "arbitrary"), vmem_limit_bytes=_vmem_limit(est)),
        name="attention",
    )(slopes, nat, nat, nat, qkv4, qkv4, qkv4, qkv16, qkv16, qkv16)


def _pool_kernel(u_ref, w_ref, sc_ref, o_ref, pad_ref, pooled_ref, *, seq):
    grp = pl.program_id(1)
    halo, rows = POOL_HALO, POOL_ROWS
    ext = rows + 2 * halo
    zeros = jnp.zeros((halo, POOL_GROUP), F32)
    pad_ref[pl.ds(0, halo), :] = zeros
    pad_ref[pl.ds(halo + seq, halo), :] = zeros
    pad_ref[pl.ds(halo, seq), :] = u_ref[...].astype(F32)

    def run(window):
        def chunk(c, carry):
            base = pl.multiple_of(c * rows, rows)
            p = pad_ref[pl.ds(base, ext), :]
            s = p + pltpu.roll(p, 1, 0)
            half = 1
            while 2 * half < window:
                s = pltpu.roll(s, ext - half, 0) + pltpu.roll(s, half, 0)
                half *= 2
            s = s[halo:halo + rows, :]
            tok = p[halo:halo + rows, :]
            t = base + lax.broadcasted_iota(jnp.int32, (rows, 1), 0)
            lo = jnp.maximum(t - window // 2, 0)
            hi = jnp.minimum(t - window // 2 + window, seq)
            pooled_ref[pl.ds(base, rows), :] = (s / (hi - lo).astype(F32) - tok).astype(BF16)
            return carry

        lax.fori_loop(0, seq // rows, chunk, 0)

    for gi, window in enumerate(POOL_WINDOWS):
        pl.when(grp == gi)(functools.partial(run, window))

    y = jnp.dot(pooled_ref[...], w_ref[...], preferred_element_type=F32) * sc_ref[...]
    o_ref[...] = y.astype(o_ref.dtype)


def _pool(pg, w_pool, pool_scale, *, batch, seq, col0):
    blk0 = col0 // POOL_GROUP
    ngrp = len(POOL_WINDOWS)
    est = (2 * seq * POOL_GROUP * 2 * 2 + (seq + 2 * POOL_HALO) * POOL_GROUP * 4 + 2 * POOL_GROUP * POOL_GROUP * 2
           + seq * POOL_GROUP * (2 + 4))
    return pl.pallas_call(
        functools.partial(_pool_kernel, seq=seq),
        grid=(batch, ngrp),
        in_specs=[
            pl.BlockSpec((seq, POOL_GROUP), lambda b, g: (b, blk0 + g)),
            pl.BlockSpec((None, POOL_GROUP, POOL_GROUP), lambda b, g: (g, 0, 0)),
            pl.BlockSpec((1, POOL_GROUP), lambda b, g: (0, g)),
        ],
        out_specs=pl.BlockSpec((seq, POOL_GROUP), lambda b, g: (b, g)),
        out_shape=jax.ShapeDtypeStruct((batch * seq, D_POOL), BF16),
        scratch_shapes=[pltpu.VMEM((seq + 2 * POOL_HALO, POOL_GROUP), F32), pltpu.VMEM((seq, POOL_GROUP), BF16)],
        compiler_params=pltpu.CompilerParams(
            dimension_semantics=("arbitrary", "arbitrary"), vmem_limit_bytes=_vmem_limit(est)),
        name="pool",
    )(pg, w_pool, pool_scale)


def _memattn_kernel(q_ref, kv_ref, o_ref):
    scale = XHEAD_DIM ** -0.5
    for hh in range(N_XHEADS):
        lo = hh * XHEAD_DIM
        q = q_ref[:, lo:lo + XHEAD_DIM]
        k = kv_ref[:, lo:lo + XHEAD_DIM]
        v = kv_ref[:, D_XATTN + lo:D_XATTN + lo + XHEAD_DIM]
        s = lax.dot_general(q, k, (((1,), (1,)), ((), ())), preferred_element_type=F32) * scale
        m = jnp.max(s, axis=-1, keepdims=True)
        p = jnp.exp(s - m)
        l = jnp.sum(p, axis=-1, keepdims=True)
        y = jnp.dot(p.astype(BF16), v, preferred_element_type=F32) / l
        o_ref[:, lo:lo + XHEAD_DIM] = y.astype(o_ref.dtype)


def _memattn(pg, kv, *, seq, col0):
    n = pg.shape[0]
    tm = MEM_TM
    blk0 = col0 // D_XATTN
    tiles_per_seq = seq // tm
    est = 2 * tm * D_XATTN * 2 * 2 + 2 * N_MEM * 2 * D_XATTN * 2 + 4 * tm * N_MEM * 4
    return pl.pallas_call(
        _memattn_kernel,
        grid=(n // tm,),
        in_specs=[
            pl.BlockSpec((tm, D_XATTN), lambda i: (i, blk0)),
            pl.BlockSpec((N_MEM, 2 * D_XATTN), lambda i: (i // tiles_per_seq, 0)),
        ],
        out_specs=pl.BlockSpec((tm, D_XATTN), lambda i: (i, 0)),
        out_shape=jax.ShapeDtypeStruct((n, D_XATTN), BF16),
        compiler_params=pltpu.CompilerParams(
            dimension_semantics=("arbitrary",), vmem_limit_bytes=_vmem_limit(est)),
        name="memattn",
    )(pg, kv)


def _merge_kernel(gates_ref, ya_ref, yp_ref, ym_ref, x_ref, wa_ref, wp_ref, wm_ref, wo_ref, gain_ref, o_ref):
    d = x_ref.shape[1]
    merged = gates_ref[:, 0:d].astype(F32) * jnp.dot(ya_ref[...], wa_ref[...], preferred_element_type=F32)
    merged += gates_ref[:, d:2 * d].astype(F32) * jnp.dot(yp_ref[...], wp_ref[...], preferred_element_type=F32)
    merged += gates_ref[:, 2 * d:3 * d].astype(F32) * jnp.dot(ym_ref[...], wm_ref[...], preferred_element_type=F32)
    z = jnp.dot(merged.astype(BF16), wo_ref[...], preferred_element_type=F32)
    o_ref[...] = x_ref[...] + _rms(z, gain_ref[...])


def _merge(gates, y_attn, y_pool, y_mem, x, wa, wp, wm, wo, gain):
    n, d = x.shape
    tm = MERGE_TM
    wbytes = (wa.size + wp.size + wm.size + wo.size) * 2

    def const(shape):
        return pl.BlockSpec(shape, lambda i: (0, 0), pipeline_mode=pl.Buffered(1))

    est = (2 * tm * (N_BRANCH * d + y_attn.shape[1] + y_pool.shape[1] + y_mem.shape[1]) * 2
           + 4 * tm * d * 4 + wbytes + 4 * tm * d * 4)
    return pl.pallas_call(
        _merge_kernel,
        grid=(n // tm,),
        in_specs=[
            pl.BlockSpec((tm, N_BRANCH * d), lambda i: (i, 0)),
            pl.BlockSpec((tm, y_attn.shape[1]), lambda i: (i, 0)),
            pl.BlockSpec((tm, y_pool.shape[1]), lambda i: (i, 0)),
            pl.BlockSpec((tm, y_mem.shape[1]), lambda i: (i, 0)),
            pl.BlockSpec((tm, d), lambda i: (i, 0)),
            const(wa.shape), const(wp.shape), const(wm.shape), const(wo.shape),
            pl.BlockSpec((1, d), lambda i: (0, 0)),
        ],
        out_specs=pl.BlockSpec((tm, d), lambda i: (i, 0)),
        out_shape=jax.ShapeDtypeStruct((n, d), F32),
        compiler_params=pltpu.CompilerParams(
            dimension_semantics=("arbitrary",), vmem_limit_bytes=_vmem_limit(est)),
        name="merge",
    )(gates, y_attn, y_pool, y_mem, x, wa, wp, wm, wo, gain)


def _cast_kernel(w_ref, *o_refs):
    lo = 0
    for o_ref in o_refs:
        width = o_ref.shape[1]
        o_ref[...] = w_ref[:, lo:lo + width].astype(o_ref.dtype)
        lo += width


def _cast_bf16(w, widths=None, target_bytes=CAST_BLOCK_BYTES):
    rows, cols = w.shape
    widths = (cols,) if widths is None else tuple(widths)
    br = rows
    while br * cols * 4 > target_bytes and br % 32 == 0:
        br //= 2
    est = 2 * br * cols * 4 + 2 * br * cols * 2
    outs = pl.pallas_call(
        _cast_kernel,
        grid=(rows // br,),
        in_specs=[pl.BlockSpec((br, cols), lambda i: (i, 0))],
        out_specs=[pl.BlockSpec((br, wd), lambda i: (i, 0)) for wd in widths],
        out_shape=[jax.ShapeDtypeStruct((rows, wd), BF16) for wd in widths],
        compiler_params=pltpu.CompilerParams(
            dimension_semantics=("arbitrary",), vmem_limit_bytes=_vmem_limit(est)),
        name="cast",
    )(w)
    return outs if len(widths) > 1 else outs[0]


def _row(v):
    return v.reshape(1, -1).astype(F32)


def _trunk(x, mem, p, slopes):
    batch, seq, d = x.shape
    xf = x.reshape(batch * seq, d)
    x1 = _ffn(xf, p["ffn1_norm_pre"], p["ffn1_wg"], p["ffn1_wu"], p["ffn1_wd"], p["ffn1_norm_post"],
              p["ffn1_norm_post"], final=False)
    gates, nat, qkv4, qkv16 = _mixproj(x1, p["mix_norm_pre"], p["w_gate"], p["b_gate"], p["w_in"],
                                       batch=batch, seq=seq)
    col_pool = D_XATTN
    col_qkv0 = col_pool + D_POOL
    kv = _kvproj(mem.reshape(batch * N_MEM, d), p["mem_norm"], p["w_mem_kv"])
    y_attn = _attention(nat, qkv4, qkv16, slopes, batch=batch, seq=seq, col0=col_qkv0)
    y_pool = _pool(nat, p["w_pool"], p["pool_scale"], batch=batch, seq=seq, col0=col_pool)
    y_mem = _memattn(nat, kv, seq=seq, col0=0)
    x2 = _merge(gates, y_attn, y_pool, y_mem, x1, p["w_br_attn"], p["w_br_pool"], p["w_br_mem"], p["w_out"],
                p["mix_norm_post"])
    y = _ffn(x2, p["ffn2_norm_pre"], p["ffn2_wg"], p["ffn2_wu"], p["ffn2_wd"], p["ffn2_norm_post"],
             p["final_norm"], final=True)
    return y.reshape(batch, seq, d)


def _prep_params(ffn1_norm_pre, ffn1_w_up, ffn1_w_down, ffn1_norm_post, mix_norm_pre, mem_norm, w_in, w_mem_kv,
                 w_pool, pool_scale, w_br_attn, w_br_pool, w_br_mem, w_gate, b_gate, w_out, mix_norm_post,
                 ffn2_norm_pre, ffn2_w_up, ffn2_w_down, ffn2_norm_post, final_norm):
    p = {}
    for name, w_up, w_down in (("ffn1", ffn1_w_up, ffn1_w_down), ("ffn2", ffn2_w_up, ffn2_w_down)):
        dff = w_down.shape[0]
        p[name + "_wg"], p[name + "_wu"] = _cast_bf16(w_up, (dff, dff))
        p[name + "_wd"] = _cast_bf16(w_down)
    p["w_gate"] = _cast_bf16(w_gate)
    p["w_in"] = _cast_bf16(w_in)
    p["b_gate"] = _row(b_gate)
    p["w_mem_kv"] = _cast_bf16(w_mem_kv)
    p["w_pool"] = w_pool.astype(BF16)
    p["pool_scale"] = _row(pool_scale)
    for name, w in (("w_br_attn", w_br_attn), ("w_br_pool", w_br_pool), ("w_br_mem", w_br_mem), ("w_out", w_out)):
        p[name] = _cast_bf16(w)
    for name, v in (("ffn1_norm_pre", ffn1_norm_pre), ("ffn1_norm_post", ffn1_norm_post),
                    ("mix_norm_pre", mix_norm_pre), ("mem_norm", mem_norm), ("mix_norm_post", mix_norm_post),
                    ("ffn2_norm_pre", ffn2_norm_pre), ("ffn2_norm_post", ffn2_norm_post),
                    ("final_norm", final_norm)):
        p[name] = _row(v)
    return p


def kernel(x_prompt, x_sample, mem_prompt, mem_sample, ffn1_norm_pre, ffn1_w_up, ffn1_w_down, ffn1_norm_post,
           mix_norm_pre, mem_norm, w_in, w_mem_kv, w_pool, pool_scale, w_br_attn, w_br_pool, w_br_mem, w_gate, b_gate,
           w_out, mix_norm_post, ffn2_norm_pre, ffn2_w_up, ffn2_w_down, ffn2_norm_post, final_norm):
    layer = [ffn1_norm_pre, ffn1_w_up, ffn1_w_down, ffn1_norm_post, mix_norm_pre, mem_norm, w_in, w_mem_kv, w_pool,
             pool_scale, w_br_attn, w_br_pool, w_br_mem, w_gate, b_gate, w_out, mix_norm_post, ffn2_norm_pre,
             ffn2_w_up, ffn2_w_down, ffn2_norm_post, final_norm]
    depth = ffn1_norm_pre.shape[0]
    slopes = jnp.asarray(_alibi_slopes())
    y_prompt, y_sample = x_prompt, x_sample
    for layer_idx in range(depth):
        p = _prep_params(*[w[layer_idx] for w in layer])
        y_prompt = _trunk(y_prompt, mem_prompt, p, slopes)
        y_sample = _trunk(y_sample, mem_sample, p, slopes)
    return (y_prompt, y_sample)
```

```python
import functools

import numpy as np
import jax
import jax.numpy as jnp
from jax import lax
from jax.experimental import pallas as pl
from jax.experimental.pallas import tpu as pltpu

F32 = jnp.float32
BF16 = jnp.bfloat16

D_MODEL = 2048
N_MEM = 256
HEAD_DIM = 128
DIL_PAIRS = ((128, 1), (512, 4), (2048, 16))
HEADS_PER_GROUP = 4
N_GROUPS = len(DIL_PAIRS)
N_DIL_HEADS = HEADS_PER_GROUP * N_GROUPS
D_ATTN = N_DIL_HEADS * HEAD_DIM
POOL_WINDOWS = (2, 4, 8, 16)
POOL_GROUP = 256
D_POOL = POOL_GROUP * len(POOL_WINDOWS)
N_XHEADS = 4
XHEAD_DIM = 256
D_XATTN = N_XHEADS * XHEAD_DIM
N_BRANCH = 3
EPS = 1e-6
NEG = -1e30
BAND_RADIUS = 64

V7X_VMEM_BYTES = 64 * 1024 * 1024
MIB = 1024 * 1024

FFN_TM = 1024
FFN_TF = 512
PROJ_TM = 1024
PROJ_TN = 512
PROJ_TG = 1536
PROJ_ROWS = 256
ATT_QB = 128
ATT_KW = 256
ATT_UNROLL = 8
ATT_FIN_ROWS = 256
POOL_ROWS = 256
POOL_HALO = 16
MEM_TM = 512
MERGE_TM = 512
MERGE_ROWS = 256
CAST_BLOCK_BYTES = 8 * MIB


def _vmem_limit(nbytes):
    return int(min(nbytes * 5 // 4 + 8 * MIB, V7X_VMEM_BYTES - 6 * MIB))


def _rms(x, gain):
    ms = jnp.mean(x * x, axis=-1, keepdims=True)
    return x * lax.rsqrt(ms + EPS) * gain


def _alibi_slopes():
    s = 2.0 ** (-8.0 * np.arange(1, N_DIL_HEADS + 1) / N_DIL_HEADS)
    return s.reshape(HEADS_PER_GROUP, N_GROUPS).T.astype(np.float32)


def _ffn_kernel(x_ref, gpre_ref, wg_ref, wu_ref, wd_ref, gpost_ref, gfin_ref, o_ref, hs_ref, *, tf, rem, final):
    f = pl.program_id(1)
    last = pl.num_programs(1) - 1

    @pl.when(f == 0)
    def _():
        hs_ref[...] = _rms(x_ref[...], gpre_ref[...]).astype(BF16)
        o_ref[...] = jnp.zeros_like(o_ref)

    def step(valid):
        hs = hs_ref[...]
        g = jnp.dot(hs, wg_ref[:, :valid], preferred_element_type=F32)
        u = jnp.dot(hs, wu_ref[:, :valid], preferred_element_type=F32)
        a = (g * jax.nn.sigmoid(g) * u).astype(BF16)
        o_ref[...] += jnp.dot(a, wd_ref[:valid, :], preferred_element_type=F32)

    if rem == tf:
        step(tf)
    else:
        pl.when(f < last)(functools.partial(step, tf))
        pl.when(f == last)(functools.partial(step, rem))

    @pl.when(f == last)
    def _():
        y = x_ref[...] + 0.5 * _rms(o_ref[...], gpost_ref[...])
        if final:
            y = _rms(y, gfin_ref[...])
        o_ref[...] = y


def _ffn(x, gpre, wg, wu, wd, gpost, gfin, *, final):
    n, d = x.shape
    tm, tf = FFN_TM, FFN_TF
    dff = wd.shape[0]
    nf = pl.cdiv(dff, tf)
    rem = dff - (nf - 1) * tf
    est = 2 * tm * d * 4 * 2 + tm * d * 2 + 2 * 3 * d * tf * 2 + 3 * tm * 2 * tf * 4
    vec = pl.BlockSpec((1, d), lambda i, f: (0, 0))
    return pl.pallas_call(
        functools.partial(_ffn_kernel, tf=tf, rem=rem, final=final),
        grid=(n // tm, nf),
        in_specs=[
            pl.BlockSpec((tm, d), lambda i, f: (i, 0)),
            vec,
            pl.BlockSpec((None, d, tf), lambda i, f: (f, 0, 0)),
            pl.BlockSpec((None, d, tf), lambda i, f: (f, 0, 0)),
            pl.BlockSpec((tf, d), lambda i, f: (f, 0)),
            vec,
            vec,
        ],
        out_specs=pl.BlockSpec((tm, d), lambda i, f: (i, 0)),
        out_shape=jax.ShapeDtypeStruct((n, d), F32),
        scratch_shapes=[pltpu.VMEM((tm, d), BF16)],
        compiler_params=pltpu.CompilerParams(
            dimension_semantics=("arbitrary", "arbitrary"), vmem_limit_bytes=_vmem_limit(est)),
        name="ffn_final" if final else "ffn",
    )(x, gpre, wg, wu, wd, gpost, gfin)


def _kvproj_kernel(x_ref, g_ref, w_ref, o_ref, hs_ref):
    @pl.when(pl.program_id(1) == 0)
    def _():
        hs_ref[...] = _rms(x_ref[...], g_ref[...]).astype(BF16)

    o_ref[...] = jnp.dot(hs_ref[...], w_ref[...], preferred_element_type=F32).astype(o_ref.dtype)


def _kvproj(x, gain, w):
    n, d = x.shape
    ncol = w.shape[1]
    tm, tn = n, PROJ_TN
    est = 2 * tm * d * 4 + tm * d * 2 + 2 * d * tn * 2 + 2 * tm * tn * 2 + 2 * tm * tn * 4
    return pl.pallas_call(
        _kvproj_kernel,
        grid=(n // tm, ncol // tn),
        in_specs=[
            pl.BlockSpec((tm, d), lambda i, j: (i, 0)),
            pl.BlockSpec((1, d), lambda i, j: (0, 0)),
            pl.BlockSpec((d, tn), lambda i, j: (0, j)),
        ],
        out_specs=pl.BlockSpec((tm, tn), lambda i, j: (i, j)),
        out_shape=jax.ShapeDtypeStruct((n, ncol), BF16),
        scratch_shapes=[pltpu.VMEM((tm, d), BF16)],
        compiler_params=pltpu.CompilerParams(
            dimension_semantics=("arbitrary", "arbitrary"), vmem_limit_bytes=_vmem_limit(est)),
        name="kvproj",
    )(x, gain, w)


N_GATE_STEPS = N_BRANCH * D_MODEL // PROJ_TG
NAT_CHUNKS = (11, 12, 9, 10, 0, 3, 6)
N_NAT_STEPS = len(NAT_CHUNKS)
N_DIL_STEPS = 3
MIX_STEPS = N_GATE_STEPS + N_NAT_STEPS + 2 * N_DIL_STEPS


def _lin_chunk(s):
    t = jnp.maximum(s - N_GATE_STEPS, 0)
    nat = jnp.where(t < 2, 11 + t, jnp.where(t < 4, 7 + t, 3 * (t - 4)))
    d4 = 3 * (t - N_NAT_STEPS) + 1
    d16 = 3 * (t - N_NAT_STEPS - N_DIL_STEPS) + 2
    return jnp.where(t < N_NAT_STEPS, nat, jnp.where(t < N_NAT_STEPS + N_DIL_STEPS, d4, d16))


def _mixproj_kernel(x_ref, g_ref, wg_ref, b_ref, wi_ref, og_ref, on_ref, o4_ref, o16_ref, hs_ref, stage_ref):
    s = pl.program_id(1)
    tm = x_ref.shape[0]
    row_chunks = [pl.ds(r, PROJ_ROWS) for r in range(0, tm, PROJ_ROWS)]

    @pl.when(s == 0)
    def _():
        hs_ref[...] = _rms(x_ref[...], g_ref[...]).astype(BF16)

    @pl.when(s < N_GATE_STEPS)
    def _():
        for rows in row_chunks:
            acc = jnp.dot(hs_ref[rows, :], wg_ref[...], preferred_element_type=F32)
            og_ref[rows, :] = jax.nn.sigmoid(acc + b_ref[...]).astype(og_ref.dtype)

    @pl.when((s >= N_GATE_STEPS) & (s < N_GATE_STEPS + N_NAT_STEPS))
    def _():
        for rows in row_chunks:
            on_ref[rows, :] = jnp.dot(hs_ref[rows, :], wi_ref[...], preferred_element_type=F32).astype(on_ref.dtype)

    def dilated(o_ref, d):
        per_res = PROJ_ROWS // d
        for rc, rows in enumerate(row_chunks):
            acc = jnp.dot(hs_ref[rows, :], wi_ref[...], preferred_element_type=F32)
            for c in range(PROJ_TN // HEAD_DIM):
                stage_ref[c] = acc[:, c * HEAD_DIM:(c + 1) * HEAD_DIM]
            for res in range(d):
                for c in range(PROJ_TN // HEAD_DIM):
                    o_ref[res, pl.ds(rc * per_res, per_res), c * HEAD_DIM:(c + 1) * HEAD_DIM] = (
                        stage_ref[c, pl.ds(res, per_res, stride=d), :].astype(o_ref.dtype))

    first_d4 = N_GATE_STEPS + N_NAT_STEPS
    pl.when((s >= first_d4) & (s < first_d4 + N_DIL_STEPS))(functools.partial(dilated, o4_ref, DIL_PAIRS[1][1]))
    pl.when(s >= first_d4 + N_DIL_STEPS)(functools.partial(dilated, o16_ref, DIL_PAIRS[2][1]))


def _mixproj(x, gain, w_gate, b_gate, w_in, *, batch, seq):
    n, d = x.shape
    tm, tn, tg = PROJ_TM, PROJ_TN, PROJ_TG
    tiles_per_seq = seq // tm
    d4, d16 = DIL_PAIRS[1][1], DIL_PAIRS[2][1]
    first_nat = N_GATE_STEPS
    first_d4 = first_nat + N_NAT_STEPS
    first_d16 = first_d4 + N_DIL_STEPS

    def gate_idx(s):
        return jnp.minimum(s, N_GATE_STEPS - 1)

    def dil_spec(dil, first):
        return pl.BlockSpec(
            (None, dil, tm // dil, tn),
            lambda i, s: (i // tiles_per_seq, 0, i % tiles_per_seq, jnp.clip(s - first, 0, N_DIL_STEPS - 1)))

    est = (2 * tm * d * 4 + tm * d * 2 + 2 * d * (tg + tn) * 2 + 2 * tm * (tg + 3 * tn) * 2
           + PROJ_ROWS * tn * 4 + 2 * PROJ_ROWS * tg * 4)
    return pl.pallas_call(
        _mixproj_kernel,
        grid=(n // tm, MIX_STEPS),
        in_specs=[
            pl.BlockSpec((tm, d), lambda i, s: (i, 0)),
            pl.BlockSpec((1, d), lambda i, s: (0, 0)),
            pl.BlockSpec((d, tg), lambda i, s: (0, gate_idx(s))),
            pl.BlockSpec((1, tg), lambda i, s: (0, gate_idx(s))),
            pl.BlockSpec((d, tn), lambda i, s: (0, _lin_chunk(s))),
        ],
        out_specs=[
            pl.BlockSpec((tm, tg), lambda i, s: (i, gate_idx(s))),
            pl.BlockSpec((tm, tn), lambda i, s: (i, jnp.clip(s - first_nat, 0, N_NAT_STEPS - 1))),
            dil_spec(d4, first_d4),
            dil_spec(d16, first_d16),
        ],
        out_shape=[
            jax.ShapeDtypeStruct((n, N_BRANCH * d), BF16),
            jax.ShapeDtypeStruct((n, N_NAT_STEPS * tn), BF16),
            jax.ShapeDtypeStruct((batch, d4, seq // d4, N_DIL_STEPS * tn), BF16),
            jax.ShapeDtypeStruct((batch, d16, seq // d16, N_DIL_STEPS * tn), BF16),
        ],
        scratch_shapes=[pltpu.VMEM((tm, d), BF16), pltpu.VMEM((tn // HEAD_DIM, PROJ_ROWS, HEAD_DIM), F32)],
        compiler_params=pltpu.CompilerParams(
            dimension_semantics=("arbitrary", "arbitrary"), vmem_limit_bytes=_vmem_limit(est)),
        name="mixproj",
    )(x, gain, w_gate, b_gate, w_in)


def _attn_kernel(slopes_ref, q0_ref, k0_ref, v0_ref, q1_ref, k1_ref, v1_ref, q2_ref, k2_ref, v2_ref, o_ref,
                 m0_ref, m1_ref, m2_ref, l0_ref, l1_ref, l2_ref, n0_ref, n1_ref, n2_ref, va_ref, bias_ref, *, seq):
    h = pl.program_id(1)
    scale = HEAD_DIM ** -0.5
    qb = ATT_QB
    qkv_refs = ((q0_ref, k0_ref, v0_ref), (q1_ref, k1_ref, v1_ref), (q2_ref, k2_ref, v2_ref))
    m_refs = (m0_ref, m1_ref, m2_ref)
    l_refs = (l0_ref, l1_ref, l2_ref)
    n_refs = (n0_ref, n1_ref, n2_ref)

    va_ref[:, HEAD_DIM:] = jnp.ones((seq, HEAD_DIM), BF16)

    for gi, (_, d) in enumerate(DIL_PAIRS):
        q_ref, k_ref, v_ref = qkv_refs[gi]
        sub_len = seq // d
        kw = min(ATT_KW, sub_len)
        nb = sub_len // qb
        slope_d = slopes_ref[gi, h] * float(d)

        row = lax.broadcasted_iota(jnp.int32, (qb, kw), 0)
        col = lax.broadcasted_iota(jnp.int32, (qb, kw), 1)
        for kind, off in enumerate((0, -BAND_RADIUS, qb - kw)):
            arel = jnp.abs(col - row + off)
            bias_ref[kind, :, :kw] = jnp.where(arel <= BAND_RADIUS, -slope_d * arel.astype(F32), NEG)

        if d == 1:
            va_ref[:, :HEAD_DIM] = v_ref[...]
        else:
            for res in range(d):
                va_ref[pl.ds(res * sub_len, sub_len), :HEAD_DIM] = v_ref[res]

        def block(idx, gi=gi, d=d, q_ref=q_ref, k_ref=k_ref, sub_len=sub_len, kw=kw, nb=nb):
            res = idx // nb
            i = idx - res * nb
            q0 = pl.multiple_of(i * qb, qb)
            k0 = pl.multiple_of(jnp.clip(q0 - BAND_RADIUS, 0, sub_len - kw), BAND_RADIUS)
            if d == 1:
                qblk = q_ref[pl.ds(q0, qb), :]
                kwin = k_ref[pl.ds(k0, kw), :]
            else:
                qblk = q_ref[res, pl.ds(q0, qb), :]
                kwin = k_ref[res, pl.ds(k0, kw), :]
            s = lax.dot_general(qblk, kwin, (((1,), (1,)), ((), ())), preferred_element_type=F32)
            if nb == 1:
                bias = bias_ref[0, :, :kw]
            else:
                kind = jnp.where(i == 0, 0, jnp.where(i == nb - 1, 2, 1))
                bias = bias_ref[kind]
            s = s * scale + bias
            m = jnp.max(s, axis=-1, keepdims=True)
            p = jnp.exp(s - m).astype(BF16)
            kstart = pl.multiple_of(res * sub_len + k0, BAND_RADIUS)
            na = jnp.dot(p, va_ref[pl.ds(kstart, kw), :], preferred_element_type=F32)
            if d == 1:
                rows = pl.ds(q0, qb)
            else:
                rows = pl.ds(q0 * d + res, qb, stride=d)
            m_refs[gi][rows, :] = jnp.broadcast_to(m, (qb, HEAD_DIM))
            l_refs[gi][rows, :] = na[:, HEAD_DIM:]
            n_refs[gi][rows, :] = na[:, :HEAD_DIM]

        def blocks(it, carry, block=block):
            for u in range(ATT_UNROLL):
                block(it * ATT_UNROLL + u)
            return carry

        lax.fori_loop(0, d * nb // ATT_UNROLL, blocks, 0)

    def fin(c, carry):
        r = pl.ds(pl.multiple_of(c * ATT_FIN_ROWS, ATT_FIN_ROWS), ATT_FIN_ROWS)
        ms = [m_ref[r, :] for m_ref in m_refs]
        mx = jnp.maximum(jnp.maximum(ms[0], ms[1]), ms[2])
        num = jnp.zeros((ATT_FIN_ROWS, HEAD_DIM), F32)
        den = jnp.zeros((ATT_FIN_ROWS, HEAD_DIM), F32)
        for gi in range(N_GROUPS):
            e = jnp.exp(ms[gi] - mx)
            num += e * n_refs[gi][r, :]
            den += e * l_refs[gi][r, :]
        o_ref[r, :] = (num / den).astype(o_ref.dtype)
        return carry

    lax.fori_loop(0, seq // ATT_FIN_ROWS, fin, 0)


def _attention(nat, qkv4, qkv16, slopes, *, batch, seq, col0):
    blk0 = col0 // HEAD_DIM
    d4, d16 = DIL_PAIRS[1][1], DIL_PAIRS[2][1]

    def nat_map(b, h, *, which):
        return (b, blk0 + which * HEADS_PER_GROUP + h)

    def dil_map(b, h, *, which):
        return (b, 0, 0, which * HEADS_PER_GROUP + h)

    in_specs = [pl.BlockSpec(memory_space=pltpu.SMEM)]
    in_specs += [pl.BlockSpec((seq, HEAD_DIM), functools.partial(nat_map, which=w)) for w in range(3)]
    for dil in (d4, d16):
        in_specs += [pl.BlockSpec((None, dil, seq // dil, HEAD_DIM), functools.partial(dil_map, which=w))
                     for w in range(3)]
    tile = seq * HEAD_DIM
    est = 10 * 2 * tile * 2 + 9 * tile * 4 + 2 * tile * 2 + 3 * ATT_QB * ATT_KW * 4
    state = [pltpu.VMEM((seq, HEAD_DIM), F32) for _ in range(3 * N_GROUPS)]
    return pl.pallas_call(
        functools.partial(_attn_kernel, seq=seq),
        grid=(batch, HEADS_PER_GROUP),
        in_specs=in_specs,
        out_specs=pl.BlockSpec((seq, HEAD_DIM), lambda b, h: (b, h)),
        out_shape=jax.ShapeDtypeStruct((batch * seq, HEADS_PER_GROUP * HEAD_DIM), BF16),
        scratch_shapes=state + [
            pltpu.VMEM((seq, 2 * HEAD_DIM), BF16),
            pltpu.VMEM((3, ATT_QB, ATT_KW), F32),
        ],
        compiler_params=pltpu.CompilerParams(
            dimension_semantics=("arbitrary", "arbitrary"), vmem_limit_bytes=_vmem_limit(est)),
        name="attention",
    )(slopes, nat, nat, nat, qkv4, qkv4, qkv4, qkv16, qkv16, qkv16)


def _pool_kernel(u_ref, w_ref, sc_ref, o_ref, pad_ref, pooled_ref, *, seq):
    grp = pl.program_id(1)
    halo, rows = POOL_HALO, POOL_ROWS
    ext = rows + 2 * halo
    zeros = jnp.zeros((halo, POOL_GROUP), F32)
    pad_ref[pl.ds(0, halo), :] = zeros
    pad_ref[pl.ds(halo + seq, halo), :] = zeros
    pad_ref[pl.ds(halo, seq), :] = u_ref[...].astype(F32)

    def run(window):
        def chunk(c, carry):
            base = pl.multiple_of(c * rows, rows)
            p = pad_ref[pl.ds(base, ext), :]
            s = p + pltpu.roll(p, 1, 0)
            half = 1
            while 2 * half < window:
                s = pltpu.roll(s, ext - half, 0) + pltpu.roll(s, half, 0)
                half *= 2
            s = s[halo:halo + rows, :]
            tok = p[halo:halo + rows, :]
            t = base + lax.broadcasted_iota(jnp.int32, (rows, 1), 0)
            lo = jnp.maximum(t - window // 2, 0)
            hi = jnp.minimum(t - window // 2 + window, seq)
            pooled_ref[pl.ds(base, rows), :] = (s / (hi - lo).astype(F32) - tok).astype(BF16)
            return carry

        lax.fori_loop(0, seq // rows, chunk, 0)

    for gi, window in enumerate(POOL_WINDOWS):
        pl.when(grp == gi)(functools.partial(run, window))

    y = jnp.dot(pooled_ref[...], w_ref[...], preferred_element_type=F32) * sc_ref[...]
    o_ref[...] = y.astype(o_ref.dtype)


def _pool(pg, w_pool, pool_scale, *, batch, seq, col0):
    blk0 = col0 // POOL_GROUP
    ngrp = len(POOL_WINDOWS)
    est = (2 * seq * POOL_GROUP * 2 * 2 + (seq + 2 * POOL_HALO) * POOL_GROUP * 4 + 2 * POOL_GROUP * POOL_GROUP * 2
           + seq * POOL_GROUP * (2 + 4))
    return pl.pallas_call(
        functools.partial(_pool_kernel, seq=seq),
        grid=(batch, ngrp),
        in_specs=[
            pl.BlockSpec((seq, POOL_GROUP), lambda b, g: (b, blk0 + g)),
            pl.BlockSpec((None, POOL_GROUP, POOL_GROUP), lambda b, g: (g, 0, 0)),
            pl.BlockSpec((1, POOL_GROUP), lambda b, g: (0, g)),
        ],
        out_specs=pl.BlockSpec((seq, POOL_GROUP), lambda b, g: (b, g)),
        out_shape=jax.ShapeDtypeStruct((batch * seq, D_POOL), BF16),
        scratch_shapes=[pltpu.VMEM((seq + 2 * POOL_HALO, POOL_GROUP), F32), pltpu.VMEM((seq, POOL_GROUP), BF16)],
        compiler_params=pltpu.CompilerParams(
            dimension_semantics=("arbitrary", "arbitrary"), vmem_limit_bytes=_vmem_limit(est)),
        name="pool",
    )(pg, w_pool, pool_scale)


def _memattn_kernel(q_ref, kv_ref, o_ref):
    scale = XHEAD_DIM ** -0.5
    for hh in range(N_XHEADS):
        lo = hh * XHEAD_DIM
        q = q_ref[:, lo:lo + XHEAD_DIM]
        k = kv_ref[:, lo:lo + XHEAD_DIM]
        v = kv_ref[:, D_XATTN + lo:D_XATTN + lo + XHEAD_DIM]
        s = lax.dot_general(q, k, (((1,), (1,)), ((), ())), preferred_element_type=F32) * scale
        m = jnp.max(s, axis=-1, keepdims=True)
        p = jnp.exp(s - m)
        l = jnp.sum(p, axis=-1, keepdims=True)
        y = jnp.dot(p.astype(BF16), v, preferred_element_type=F32) / l
        o_ref[:, lo:lo + XHEAD_DIM] = y.astype(o_ref.dtype)


def _memattn(pg, kv, *, seq, col0):
    n = pg.shape[0]
    tm = MEM_TM
    blk0 = col0 // D_XATTN
    tiles_per_seq = seq // tm
    est = 2 * tm * D_XATTN * 2 * 2 + 2 * N_MEM * 2 * D_XATTN * 2 + 4 * tm * N_MEM * 4
    return pl.pallas_call(
        _memattn_kernel,
        grid=(n // tm,),
        in_specs=[
            pl.BlockSpec((tm, D_XATTN), lambda i: (i, blk0)),
            pl.BlockSpec((N_MEM, 2 * D_XATTN), lambda i: (i // tiles_per_seq, 0)),
        ],
        out_specs=pl.BlockSpec((tm, D_XATTN), lambda i: (i, 0)),
        out_shape=jax.ShapeDtypeStruct((n, D_XATTN), BF16),
        compiler_params=pltpu.CompilerParams(
            dimension_semantics=("arbitrary",), vmem_limit_bytes=_vmem_limit(est)),
        name="memattn",
    )(pg, kv)


def _merge_kernel(gates_ref, ya_ref, yp_ref, ym_ref, x_ref, wa_ref, wp_ref, wm_ref, wo_ref, gain_ref, o_ref):
    d = x_ref.shape[1]
    for r in range(0, x_ref.shape[0], MERGE_ROWS):
        rows = pl.ds(r, MERGE_ROWS)
        merged = gates_ref[rows, 0:d].astype(F32) * jnp.dot(ya_ref[rows, :], wa_ref[...], preferred_element_type=F32)
        merged += gates_ref[rows, d:2 * d].astype(F32) * jnp.dot(yp_ref[rows, :], wp_ref[...],
                                                                   preferred_element_type=F32)
        merged += gates_ref[rows, 2 * d:3 * d].astype(F32) * jnp.dot(ym_ref[rows, :], wm_ref[...],
                                                                       preferred_element_type=F32)
        z = jnp.dot(merged.astype(BF16), wo_ref[...], preferred_element_type=F32)
        o_ref[rows, :] = x_ref[rows, :] + _rms(z, gain_ref[...])


def _merge(gates, y_attn, y_pool, y_mem, x, wa, wp, wm, wo, gain):
    n, d = x.shape
    tm = MERGE_TM
    wbytes = (wa.size + wp.size + wm.size + wo.size) * 2

    def const(shape):
        return pl.BlockSpec(shape, lambda i: (0, 0), pipeline_mode=pl.Buffered(1))

    est = (2 * tm * (N_BRANCH * d + y_attn.shape[1] + y_pool.shape[1] + y_mem.shape[1]) * 2
           + 4 * tm * d * 4 + wbytes + 4 * tm * d * 4)
    return pl.pallas_call(
        _merge_kernel,
        grid=(n // tm,),
        in_specs=[
            pl.BlockSpec((tm, N_BRANCH * d), lambda i: (i, 0)),
            pl.BlockSpec((tm, y_attn.shape[1]), lambda i: (i, 0)),
            pl.BlockSpec((tm, y_pool.shape[1]), lambda i: (i, 0)),
            pl.BlockSpec((tm, y_mem.shape[1]), lambda i: (i, 0)),
            pl.BlockSpec((tm, d), lambda i: (i, 0)),
            const(wa.shape), const(wp.shape), const(wm.shape), const(wo.shape),
            pl.BlockSpec((1, d), lambda i: (0, 0)),
        ],
        out_specs=pl.BlockSpec((tm, d), lambda i: (i, 0)),
        out_shape=jax.ShapeDtypeStruct((n, d), F32),
        compiler_params=pltpu.CompilerParams(
            dimension_semantics=("arbitrary",), vmem_limit_bytes=_vmem_limit(est)),
        name="merge",
    )(gates, y_attn, y_pool, y_mem, x, wa, wp, wm, wo, gain)


def _cast_kernel(w_ref, *o_refs):
    lo = 0
    for o_ref in o_refs:
        width = o_ref.shape[1]
        o_ref[...] = w_ref[:, lo:lo + width].astype(o_ref.dtype)
        lo += width


def _cast_bf16(w, widths=None, target_bytes=CAST_BLOCK_BYTES):
    rows, cols = w.shape
    widths = (cols,) if widths is None else tuple(widths)
    br = rows
    while br * cols * 4 > target_bytes and br % 32 == 0:
        br //= 2
    est = 2 * br * cols * 4 + 2 * br * cols * 2
    outs = pl.pallas_call(
        _cast_kernel,
        grid=(rows // br,),
        in_specs=[pl.BlockSpec((br, cols), lambda i: (i, 0))],
        out_specs=[pl.BlockSpec((br, wd), lambda i: (i, 0)) for wd in widths],
        out_shape=[jax.ShapeDtypeStruct((rows, wd), BF16) for wd in widths],
        compiler_params=pltpu.CompilerParams(
            dimension_semantics=("arbitrary",), vmem_limit_bytes=_vmem_limit(est)),
        name="cast",
    )(w)
    return outs if len(widths) > 1 else outs[0]


def _cast_chunked_kernel(w_ref, *o_refs, width):
    for k, o_ref in enumerate(o_refs):
        nchunk, _, tf = o_ref.shape
        for c in range(nchunk):
            valid = min(tf, width - c * tf)
            lo = k * width + c * tf
            o_ref[c, :, :valid] = w_ref[:, lo:lo + valid].astype(o_ref.dtype)
            if valid < tf:
                o_ref[c, :, valid:] = jnp.zeros((o_ref.shape[1], tf - valid), o_ref.dtype)


def _cast_chunked(w, n_out, tf, target_bytes=CAST_BLOCK_BYTES):
    rows, cols = w.shape
    width = cols // n_out
    nchunk = pl.cdiv(width, tf)
    br = rows
    while br * cols * 4 > target_bytes and br % 32 == 0:
        br //= 2
    est = 2 * br * cols * 4 + 2 * n_out * nchunk * br * tf * 2
    return pl.pallas_call(
        functools.partial(_cast_chunked_kernel, width=width),
        grid=(rows // br,),
        in_specs=[pl.BlockSpec((br, cols), lambda i: (i, 0))],
        out_specs=[pl.BlockSpec((nchunk, br, tf), lambda i: (0, i, 0)) for _ in range(n_out)],
        out_shape=[jax.ShapeDtypeStruct((nchunk, rows, tf), BF16) for _ in range(n_out)],
        compiler_params=pltpu.CompilerParams(
            dimension_semantics=("arbitrary",), vmem_limit_bytes=_vmem_limit(est)),
        name="cast_chunked",
    )(w)


def _row(v):
    return v.reshape(1, -1).astype(F32)


def _trunk(x, mem, p, slopes):
    batch, seq, d = x.shape
    xf = x.reshape(batch * seq, d)
    x1 = _ffn(xf, p["ffn1_norm_pre"], p["ffn1_wg"], p["ffn1_wu"], p["ffn1_wd"], p["ffn1_norm_post"],
              p["ffn1_norm_post"], final=False)
    gates, nat, qkv4, qkv16 = _mixproj(x1, p["mix_norm_pre"], p["w_gate"], p["b_gate"], p["w_in"],
                                       batch=batch, seq=seq)
    col_pool = D_XATTN
    col_qkv0 = col_pool + D_POOL
    kv = _kvproj(mem.reshape(batch * N_MEM, d), p["mem_norm"], p["w_mem_kv"])
    y_attn = _attention(nat, qkv4, qkv16, slopes, batch=batch, seq=seq, col0=col_qkv0)
    y_pool = _pool(nat, p["w_pool"], p["pool_scale"], batch=batch, seq=seq, col0=col_pool)
    y_mem = _memattn(nat, kv, seq=seq, col0=0)
    x2 = _merge(gates, y_attn, y_pool, y_mem, x1, p["w_br_attn"], p["w_br_pool"], p["w_br_mem"], p["w_out"],
                p["mix_norm_post"])
    y = _ffn(x2, p["ffn2_norm_pre"], p["ffn2_wg"], p["ffn2_wu"], p["ffn2_wd"], p["ffn2_norm_post"],
             p["final_norm"], final=True)
    return y.reshape(batch, seq, d)


def _prep_params(ffn1_norm_pre, ffn1_w_up, ffn1_w_down, ffn1_norm_post, mix_norm_pre, mem_norm, w_in, w_mem_kv,
                 w_pool, pool_scale, w_br_attn, w_br_pool, w_br_mem, w_gate, b_gate, w_out, mix_norm_post,
                 ffn2_norm_pre, ffn2_w_up, ffn2_w_down, ffn2_norm_post, final_norm):
    p = {}
    for name, w_up, w_down in (("ffn1", ffn1_w_up, ffn1_w_down), ("ffn2", ffn2_w_up, ffn2_w_down)):
        dff = w_down.shape[0]
        p[name + "_wg"], p[name + "_wu"] = _cast_chunked(w_up, 2, FFN_TF)
        p[name + "_wd"] = _cast_bf16(w_down)
    p["w_gate"] = _cast_bf16(w_gate)
    p["w_in"] = _cast_bf16(w_in)
    p["b_gate"] = _row(b_gate)
    p["w_mem_kv"] = _cast_bf16(w_mem_kv)
    p["w_pool"] = w_pool.astype(BF16)
    p["pool_scale"] = _row(pool_scale)
    for name, w in (("w_br_attn", w_br_attn), ("w_br_pool", w_br_pool), ("w_br_mem", w_br_mem), ("w_out", w_out)):
        p[name] = _cast_bf16(w)
    for name, v in (("ffn1_norm_pre", ffn1_norm_pre), ("ffn1_norm_post", ffn1_norm_post),
                    ("mix_norm_pre", mix_norm_pre), ("mem_norm", mem_norm), ("mix_norm_post", mix_norm_post),
                    ("ffn2_norm_pre", ffn2_norm_pre), ("ffn2_norm_post", ffn2_norm_post),
                    ("final_norm", final_norm)):
        p[name] = _row(v)
    return p


def kernel(x_prompt, x_sample, mem_prompt, mem_sample, ffn1_norm_pre, ffn1_w_up, ffn1_w_down, ffn1_norm_post,
           mix_norm_pre, mem_norm, w_in, w_mem_kv, w_pool, pool_scale, w_br_attn, w_br_pool, w_br_mem, w_gate, b_gate,
           w_out, mix_norm_post, ffn2_norm_pre, ffn2_w_up, ffn2_w_down, ffn2_norm_post, final_norm):
    layer = [ffn1_norm_pre, ffn1_w_up, ffn1_w_down, ffn1_norm_post, mix_norm_pre, mem_norm, w_in, w_mem_kv, w_pool,
             pool_scale, w_br_attn, w_br_pool, w_br_mem, w_gate, b_gate, w_out, mix_norm_post, ffn2_norm_pre,
             ffn2_w_up, ffn2_w_down, ffn2_norm_post, final_norm]
    depth = ffn1_norm_pre.shape[0]
    slopes = jnp.asarray(_alibi_slopes())
    y_prompt, y_sample = x_prompt, x_sample
    for layer_idx in range(depth):
        p = _prep_params(*[w[layer_idx] for w in layer])
        y_prompt = _trunk(y_prompt, mem_prompt, p, slopes)
        y_sample = _trunk(y_sample, mem_sample, p, slopes)
    return (y_prompt, y_sample)
```

```python
import functools

import numpy as np
import jax
import jax.numpy as jnp
from jax import lax
from jax.experimental import pallas as pl
from jax.experimental.pallas import tpu as pltpu

F32 = jnp.float32
BF16 = jnp.bfloat16

D_MODEL = 2048
N_MEM = 256
HEAD_DIM = 128
DIL_PAIRS = ((128, 1), (512, 4), (2048, 16))
HEADS_PER_GROUP = 4
N_GROUPS = len(DIL_PAIRS)
N_DIL_HEADS = HEADS_PER_GROUP * N_GROUPS
D_ATTN = N_DIL_HEADS * HEAD_DIM
POOL_WINDOWS = (2, 4, 8, 16)
POOL_GROUP = 256
D_POOL = POOL_GROUP * len(POOL_WINDOWS)
N_XHEADS = 4
XHEAD_DIM = 256
D_XATTN = N_XHEADS * XHEAD_DIM
N_BRANCH = 3
EPS = 1e-6
NEG = -1e30
BAND_RADIUS = 64

V7X_VMEM_BYTES = 64 * 1024 * 1024
MIB = 1024 * 1024

FFN_TM = 1024
FFN_TF = 512
PROJ_TM = 1024
PROJ_TN = 512
PROJ_TG = 1536
PROJ_ROWS = 512
ATT_QB = 128
ATT_KW = 256
ATT_UNROLL = 8
ATT_FIN_ROWS = 256
POOL_ROWS = 256
POOL_HALO = 16
MEM_TM = 512
MERGE_TM = 512
MERGE_ROWS = 512
CAST_BLOCK_BYTES = 8 * MIB


def _vmem_limit(nbytes):
    return int(min(nbytes * 5 // 4 + 8 * MIB, V7X_VMEM_BYTES - 6 * MIB))


def _rms(x, gain):
    ms = jnp.mean(x * x, axis=-1, keepdims=True)
    return x * lax.rsqrt(ms + EPS) * gain


def _alibi_slopes():
    s = 2.0 ** (-8.0 * np.arange(1, N_DIL_HEADS + 1) / N_DIL_HEADS)
    return s.reshape(HEADS_PER_GROUP, N_GROUPS).T.astype(np.float32)


def _ffn_kernel(x_ref, gpre_ref, wg_ref, wu_ref, wd_ref, gpost_ref, gfin_ref, o_ref, hs_ref, *, tf, rem, final):
    f = pl.program_id(1)
    last = pl.num_programs(1) - 1

    @pl.when(f == 0)
    def _():
        hs_ref[...] = _rms(x_ref[...], gpre_ref[...]).astype(BF16)
        o_ref[...] = jnp.zeros_like(o_ref)

    def step(valid):
        hs = hs_ref[...]
        g = jnp.dot(hs, wg_ref[:, :valid], preferred_element_type=F32)
        u = jnp.dot(hs, wu_ref[:, :valid], preferred_element_type=F32)
        a = (g * jax.nn.sigmoid(g) * u).astype(BF16)
        o_ref[...] += jnp.dot(a, wd_ref[:valid, :], preferred_element_type=F32)

    if rem == tf:
        step(tf)
    else:
        pl.when(f < last)(functools.partial(step, tf))
        pl.when(f == last)(functools.partial(step, rem))

    @pl.when(f == last)
    def _():
        y = x_ref[...] + 0.5 * _rms(o_ref[...], gpost_ref[...])
        if final:
            y = _rms(y, gfin_ref[...])
        o_ref[...] = y


def _ffn(x, gpre, wg, wu, wd, gpost, gfin, *, final):
    n, d = x.shape
    tm, tf = FFN_TM, FFN_TF
    dff = wd.shape[0]
    nf = pl.cdiv(dff, tf)
    rem = dff - (nf - 1) * tf
    est = 2 * tm * d * 4 * 2 + tm * d * 2 + 2 * 3 * d * tf * 2 + 3 * tm * 2 * tf * 4
    vec = pl.BlockSpec((1, d), lambda i, f: (0, 0))
    return pl.pallas_call(
        functools.partial(_ffn_kernel, tf=tf, rem=rem, final=final),
        grid=(n // tm, nf),
        in_specs=[
            pl.BlockSpec((tm, d), lambda i, f: (i, 0)),
            vec,
            pl.BlockSpec((None, d, tf), lambda i, f: (f, 0, 0)),
            pl.BlockSpec((None, d, tf), lambda i, f: (f, 0, 0)),
            pl.BlockSpec((tf, d), lambda i, f: (f, 0)),
            vec,
            vec,
        ],
        out_specs=pl.BlockSpec((tm, d), lambda i, f: (i, 0)),
        out_shape=jax.ShapeDtypeStruct((n, d), F32),
        scratch_shapes=[pltpu.VMEM((tm, d), BF16)],
        compiler_params=pltpu.CompilerParams(
            dimension_semantics=("arbitrary", "arbitrary"), vmem_limit_bytes=_vmem_limit(est)),
        name="ffn_final" if final else "ffn",
    )(x, gpre, wg, wu, wd, gpost, gfin)


def _kvproj_kernel(x_ref, g_ref, w_ref, o_ref, hs_ref):
    @pl.when(pl.program_id(1) == 0)
    def _():
        hs_ref[...] = _rms(x_ref[...], g_ref[...]).astype(BF16)

    o_ref[...] = jnp.dot(hs_ref[...], w_ref[...], preferred_element_type=F32).astype(o_ref.dtype)


def _kvproj(x, gain, w):
    n, d = x.shape
    ncol = w.shape[1]
    tm, tn = n, PROJ_TN
    est = 2 * tm * d * 4 + tm * d * 2 + 2 * d * tn * 2 + 2 * tm * tn * 2 + 2 * tm * tn * 4
    return pl.pallas_call(
        _kvproj_kernel,
        grid=(n // tm, ncol // tn),
        in_specs=[
            pl.BlockSpec((tm, d), lambda i, j: (i, 0)),
            pl.BlockSpec((1, d), lambda i, j: (0, 0)),
            pl.BlockSpec((d, tn), lambda i, j: (0, j)),
        ],
        out_specs=pl.BlockSpec((tm, tn), lambda i, j: (i, j)),
        out_shape=jax.ShapeDtypeStruct((n, ncol), BF16),
        scratch_shapes=[pltpu.VMEM((tm, d), BF16)],
        compiler_params=pltpu.CompilerParams(
            dimension_semantics=("arbitrary", "arbitrary"), vmem_limit_bytes=_vmem_limit(est)),
        name="kvproj",
    )(x, gain, w)


N_GATE_STEPS = N_BRANCH * D_MODEL // PROJ_TG
NAT_CHUNKS = (11, 12, 9, 10, 0, 3, 6)
N_NAT_STEPS = len(NAT_CHUNKS)
N_DIL_STEPS = 3
MIX_STEPS = N_GATE_STEPS + N_NAT_STEPS + 2 * N_DIL_STEPS


def _lin_chunk(s):
    t = jnp.maximum(s - N_GATE_STEPS, 0)
    nat = jnp.where(t < 2, 11 + t, jnp.where(t < 4, 7 + t, 3 * (t - 4)))
    d4 = 3 * (t - N_NAT_STEPS) + 1
    d16 = 3 * (t - N_NAT_STEPS - N_DIL_STEPS) + 2
    return jnp.where(t < N_NAT_STEPS, nat, jnp.where(t < N_NAT_STEPS + N_DIL_STEPS, d4, d16))


def _mixproj_kernel(x_ref, g_ref, wg_ref, b_ref, wi_ref, og_ref, on_ref, o4_ref, o16_ref, hs_ref, stage_ref):
    s = pl.program_id(1)
    tm = x_ref.shape[0]
    row_chunks = [pl.ds(r, PROJ_ROWS) for r in range(0, tm, PROJ_ROWS)]

    @pl.when(s == 0)
    def _():
        hs_ref[...] = _rms(x_ref[...], g_ref[...]).astype(BF16)

    @pl.when(s < N_GATE_STEPS)
    def _():
        for rows in row_chunks:
            acc = jnp.dot(hs_ref[rows, :], wg_ref[...], preferred_element_type=F32)
            og_ref[rows, :] = jax.nn.sigmoid(acc + b_ref[...]).astype(og_ref.dtype)

    @pl.when((s >= N_GATE_STEPS) & (s < N_GATE_STEPS + N_NAT_STEPS))
    def _():
        for rows in row_chunks:
            on_ref[rows, :] = jnp.dot(hs_ref[rows, :], wi_ref[...], preferred_element_type=F32).astype(on_ref.dtype)

    def dilated(o_ref, d):
        per_res = PROJ_ROWS // d
        for rc, rows in enumerate(row_chunks):
            acc = jnp.dot(hs_ref[rows, :], wi_ref[...], preferred_element_type=F32)
            for c in range(PROJ_TN // HEAD_DIM):
                stage_ref[c] = acc[:, c * HEAD_DIM:(c + 1) * HEAD_DIM]
            for res in range(d):
                for c in range(PROJ_TN // HEAD_DIM):
                    o_ref[res, pl.ds(rc * per_res, per_res), c * HEAD_DIM:(c + 1) * HEAD_DIM] = (
                        stage_ref[c, pl.ds(res, per_res, stride=d), :].astype(o_ref.dtype))

    first_d4 = N_GATE_STEPS + N_NAT_STEPS
    pl.when((s >= first_d4) & (s < first_d4 + N_DIL_STEPS))(functools.partial(dilated, o4_ref, DIL_PAIRS[1][1]))
    pl.when(s >= first_d4 + N_DIL_STEPS)(functools.partial(dilated, o16_ref, DIL_PAIRS[2][1]))


def _mixproj(x, gain, w_gate, b_gate, w_in, *, batch, seq):
    n, d = x.shape
    tm, tn, tg = PROJ_TM, PROJ_TN, PROJ_TG
    tiles_per_seq = seq // tm
    d4, d16 = DIL_PAIRS[1][1], DIL_PAIRS[2][1]
    first_nat = N_GATE_STEPS
    first_d4 = first_nat + N_NAT_STEPS
    first_d16 = first_d4 + N_DIL_STEPS

    def gate_idx(s):
        return jnp.minimum(s, N_GATE_STEPS - 1)

    def dil_spec(dil, first):
        return pl.BlockSpec(
            (None, dil, tm // dil, tn),
            lambda i, s: (i // tiles_per_seq, 0, i % tiles_per_seq, jnp.clip(s - first, 0, N_DIL_STEPS - 1)))

    est = (2 * tm * d * 4 + tm * d * 2 + 2 * d * (tg + tn) * 2 + 2 * tm * (tg + 3 * tn) * 2
           + PROJ_ROWS * tn * 4 + 2 * PROJ_ROWS * tg * 4)
    return pl.pallas_call(
        _mixproj_kernel,
        grid=(n // tm, MIX_STEPS),
        in_specs=[
            pl.BlockSpec((tm, d), lambda i, s: (i, 0)),
            pl.BlockSpec((1, d), lambda i, s: (0, 0)),
            pl.BlockSpec((d, tg), lambda i, s: (0, gate_idx(s))),
            pl.BlockSpec((1, tg), lambda i, s: (0, gate_idx(s))),
            pl.BlockSpec((d, tn), lambda i, s: (0, _lin_chunk(s))),
        ],
        out_specs=[
            pl.BlockSpec((tm, tg), lambda i, s: (i, gate_idx(s))),
            pl.BlockSpec((tm, tn), lambda i, s: (i, jnp.clip(s - first_nat, 0, N_NAT_STEPS - 1))),
            dil_spec(d4, first_d4),
            dil_spec(d16, first_d16),
        ],
        out_shape=[
            jax.ShapeDtypeStruct((n, N_BRANCH * d), BF16),
            jax.ShapeDtypeStruct((n, N_NAT_STEPS * tn), BF16),
            jax.ShapeDtypeStruct((batch, d4, seq // d4, N_DIL_STEPS * tn), BF16),
            jax.ShapeDtypeStruct((batch, d16, seq // d16, N_DIL_STEPS * tn), BF16),
        ],
        scratch_shapes=[pltpu.VMEM((tm, d), BF16), pltpu.VMEM((tn // HEAD_DIM, PROJ_ROWS, HEAD_DIM), F32)],
        compiler_params=pltpu.CompilerParams(
            dimension_semantics=("arbitrary", "arbitrary"), vmem_limit_bytes=_vmem_limit(est)),
        name="mixproj",
    )(x, gain, w_gate, b_gate, w_in)


def _attn_kernel(slopes_ref, q0_ref, k0_ref, v0_ref, q1_ref, k1_ref, v1_ref, q2_ref, k2_ref, v2_ref, o_ref,
                 m0_ref, m1_ref, m2_ref, l0_ref, l1_ref, l2_ref, n0_ref, n1_ref, n2_ref, va_ref, bias_ref, *, seq):
    h = pl.program_id(1)
    scale = HEAD_DIM ** -0.5
    qb = ATT_QB
    qkv_refs = ((q0_ref, k0_ref, v0_ref), (q1_ref, k1_ref, v1_ref), (q2_ref, k2_ref, v2_ref))
    m_refs = (m0_ref, m1_ref, m2_ref)
    l_refs = (l0_ref, l1_ref, l2_ref)
    n_refs = (n0_ref, n1_ref, n2_ref)

    va_ref[:, HEAD_DIM:] = jnp.ones((seq, HEAD_DIM), BF16)

    for gi, (_, d) in enumerate(DIL_PAIRS):
        q_ref, k_ref, v_ref = qkv_refs[gi]
        sub_len = seq // d
        kw = min(ATT_KW, sub_len)
        nb = sub_len // qb
        slope_d = slopes_ref[gi, h] * float(d)

        row = lax.broadcasted_iota(jnp.int32, (qb, kw), 0)
        col = lax.broadcasted_iota(jnp.int32, (qb, kw), 1)
        for kind, off in enumerate((0, -BAND_RADIUS, qb - kw)):
            arel = jnp.abs(col - row + off)
            bias_ref[kind, :, :kw] = jnp.where(arel <= BAND_RADIUS, -slope_d * arel.astype(F32), NEG)

        if d == 1:
            va_ref[:, :HEAD_DIM] = v_ref[...]
        else:
            for res in range(d):
                va_ref[pl.ds(res * sub_len, sub_len), :HEAD_DIM] = v_ref[res]

        def block(idx, gi=gi, d=d, q_ref=q_ref, k_ref=k_ref, sub_len=sub_len, kw=kw, nb=nb):
            res = idx // nb
            i = idx - res * nb
            q0 = pl.multiple_of(i * qb, qb)
            k0 = pl.multiple_of(jnp.clip(q0 - BAND_RADIUS, 0, sub_len - kw), BAND_RADIUS)
            if d == 1:
                qblk = q_ref[pl.ds(q0, qb), :]
                kwin = k_ref[pl.ds(k0, kw), :]
            else:
                qblk = q_ref[res, pl.ds(q0, qb), :]
                kwin = k_ref[res, pl.ds(k0, kw), :]
            s = lax.dot_general(qblk, kwin, (((1,), (1,)), ((), ())), preferred_element_type=F32)
            if nb == 1:
                bias = bias_ref[0, :, :kw]
            else:
                kind = jnp.where(i == 0, 0, jnp.where(i == nb - 1, 2, 1))
                bias = bias_ref[kind]
            s = s * scale + bias
            m = jnp.max(s, axis=-1, keepdims=True)
            p = jnp.exp(s - m).astype(BF16)
            kstart = pl.multiple_of(res * sub_len + k0, BAND_RADIUS)
            na = jnp.dot(p, va_ref[pl.ds(kstart, kw), :], preferred_element_type=F32)
            if d == 1:
                rows = pl.ds(q0, qb)
            else:
                rows = pl.ds(q0 * d + res, qb, stride=d)
            m_refs[gi][rows, :] = jnp.broadcast_to(m, (qb, HEAD_DIM))
            l_refs[gi][rows, :] = na[:, HEAD_DIM:]
            n_refs[gi][rows, :] = na[:, :HEAD_DIM]

        def blocks(it, carry, block=block):
            for u in range(ATT_UNROLL):
                block(it * ATT_UNROLL + u)
            return carry

        lax.fori_loop(0, d * nb // ATT_UNROLL, blocks, 0)

    def fin(c, carry):
        r = pl.ds(pl.multiple_of(c * ATT_FIN_ROWS, ATT_FIN_ROWS), ATT_FIN_ROWS)
        ms = [m_ref[r, :] for m_ref in m_refs]
        mx = jnp.maximum(jnp.maximum(ms[0], ms[1]), ms[2])
        num = jnp.zeros((ATT_FIN_ROWS, HEAD_DIM), F32)
        den = jnp.zeros((ATT_FIN_ROWS, HEAD_DIM), F32)
        for gi in range(N_GROUPS):
            e = jnp.exp(ms[gi] - mx)
            num += e * n_refs[gi][r, :]
            den += e * l_refs[gi][r, :]
        o_ref[r, :] = (num / den).astype(o_ref.dtype)
        return carry

    lax.fori_loop(0, seq // ATT_FIN_ROWS, fin, 0)


def _attention(nat, qkv4, qkv16, slopes, *, batch, seq, col0):
    blk0 = col0 // HEAD_DIM
    d4, d16 = DIL_PAIRS[1][1], DIL_PAIRS[2][1]

    def nat_map(b, h, *, which):
        return (b, blk0 + which * HEADS_PER_GROUP + h)

    def dil_map(b, h, *, which):
        return (b, 0, 0, which * HEADS_PER_GROUP + h)

    in_specs = [pl.BlockSpec(memory_space=pltpu.SMEM)]
    in_specs += [pl.BlockSpec((seq, HEAD_DIM), functools.partial(nat_map, which=w)) for w in range(3)]
    for dil in (d4, d16):
        in_specs += [pl.BlockSpec((None, dil, seq // dil, HEAD_DIM), functools.partial(dil_map, which=w))
                     for w in range(3)]
    tile = seq * HEAD_DIM
    est = 10 * 2 * tile * 2 + 9 * tile * 4 + 2 * tile * 2 + 3 * ATT_QB * ATT_KW * 4
    state = [pltpu.VMEM((seq, HEAD_DIM), F32) for _ in range(3 * N_GROUPS)]
    return pl.pallas_call(
        functools.partial(_attn_kernel, seq=seq),
        grid=(batch, HEADS_PER_GROUP),
        in_specs=in_specs,
        out_specs=pl.BlockSpec((seq, HEAD_DIM), lambda b, h: (b, h)),
        out_shape=jax.ShapeDtypeStruct((batch * seq, HEADS_PER_GROUP * HEAD_DIM), BF16),
        scratch_shapes=state + [
            pltpu.VMEM((seq, 2 * HEAD_DIM), BF16),
            pltpu.VMEM((3, ATT_QB, ATT_KW), F32),
        ],
        compiler_params=pltpu.CompilerParams(
            dimension_semantics=("arbitrary", "arbitrary"), vmem_limit_bytes=_vmem_limit(est)),
        name="attention",
    )(slopes, nat, nat, nat, qkv4, qkv4, qkv4, qkv16, qkv16, qkv16)


def _pool_kernel(u_ref, w_ref, sc_ref, o_ref, pad_ref, pooled_ref, *, seq):
    grp = pl.program_id(1)
    halo, rows = POOL_HALO, POOL_ROWS
    ext = rows + 2 * halo
    zeros = jnp.zeros((halo, POOL_GROUP), F32)
    pad_ref[pl.ds(0, halo), :] = zeros
    pad_ref[pl.ds(halo + seq, halo), :] = zeros
    pad_ref[pl.ds(halo, seq), :] = u_ref[...].astype(F32)

    def run(window):
        def chunk(c, carry):
            base = pl.multiple_of(c * rows, rows)
            p = pad_ref[pl.ds(base, ext), :]
            s = p + pltpu.roll(p, 1, 0)
            half = 1
            while 2 * half < window:
                s = pltpu.roll(s, ext - half, 0) + pltpu.roll(s, half, 0)
                half *= 2
            s = s[halo:halo + rows, :]
            tok = p[halo:halo + rows, :]
            t = base + lax.broadcasted_iota(jnp.int32, (rows, 1), 0)
            lo = jnp.maximum(t - window // 2, 0)
            hi = jnp.minimum(t - window // 2 + window, seq)
            pooled_ref[pl.ds(base, rows), :] = (s / (hi - lo).astype(F32) - tok).astype(BF16)
            return carry

        lax.fori_loop(0, seq // rows, chunk, 0)

    for gi, window in enumerate(POOL_WINDOWS):
        pl.when(grp == gi)(functools.partial(run, window))

    y = jnp.dot(pooled_ref[...], w_ref[...], preferred_element_type=F32) * sc_ref[...]
    o_ref[...] = y.astype(o_ref.dtype)


def _pool(pg, w_pool, pool_scale, *, batch, seq, col0):
    blk0 = col0 // POOL_GROUP
    ngrp = len(POOL_WINDOWS)
    est = (2 * seq * POOL_GROUP * 2 * 2 + (seq + 2 * POOL_HALO) * POOL_GROUP * 4 + 2 * POOL_GROUP * POOL_GROUP * 2
           + seq * POOL_GROUP * (2 + 4))
    return pl.pallas_call(
        functools.partial(_pool_kernel, seq=seq),
        grid=(batch, ngrp),
        in_specs=[
            pl.BlockSpec((seq, POOL_GROUP), lambda b, g: (b, blk0 + g)),
            pl.BlockSpec((None, POOL_GROUP, POOL_GROUP), lambda b, g: (g, 0, 0)),
            pl.BlockSpec((1, POOL_GROUP), lambda b, g: (0, g)),
        ],
        out_specs=pl.BlockSpec((seq, POOL_GROUP), lambda b, g: (b, g)),
        out_shape=jax.ShapeDtypeStruct((batch * seq, D_POOL), BF16),
        scratch_shapes=[pltpu.VMEM((seq + 2 * POOL_HALO, POOL_GROUP), F32), pltpu.VMEM((seq, POOL_GROUP), BF16)],
        compiler_params=pltpu.CompilerParams(
            dimension_semantics=("arbitrary", "arbitrary"), vmem_limit_bytes=_vmem_limit(est)),
        name="pool",
    )(pg, w_pool, pool_scale)


def _memattn_kernel(q_ref, kv_ref, o_ref):
    scale = XHEAD_DIM ** -0.5
    for hh in range(N_XHEADS):
        lo = hh * XHEAD_DIM
        q = q_ref[:, lo:lo + XHEAD_DIM]
        k = kv_ref[:, lo:lo + XHEAD_DIM]
        v = kv_ref[:, D_XATTN + lo:D_XATTN + lo + XHEAD_DIM]
        s = lax.dot_general(q, k, (((1,), (1,)), ((), ())), preferred_element_type=F32) * scale
        m = jnp.max(s, axis=-1, keepdims=True)
        p = jnp.exp(s - m)
        l = jnp.sum(p, axis=-1, keepdims=True)
        y = jnp.dot(p.astype(BF16), v, preferred_element_type=F32) / l
        o_ref[:, lo:lo + XHEAD_DIM] = y.astype(o_ref.dtype)


def _memattn(pg, kv, *, seq, col0):
    n = pg.shape[0]
    tm = MEM_TM
    blk0 = col0 // D_XATTN
    tiles_per_seq = seq // tm
    est = 2 * tm * D_XATTN * 2 * 2 + 2 * N_MEM * 2 * D_XATTN * 2 + 4 * tm * N_MEM * 4
    return pl.pallas_call(
        _memattn_kernel,
        grid=(n // tm,),
        in_specs=[
            pl.BlockSpec((tm, D_XATTN), lambda i: (i, blk0)),
            pl.BlockSpec((N_MEM, 2 * D_XATTN), lambda i: (i // tiles_per_seq, 0)),
        ],
        out_specs=pl.BlockSpec((tm, D_XATTN), lambda i: (i, 0)),
        out_shape=jax.ShapeDtypeStruct((n, D_XATTN), BF16),
        compiler_params=pltpu.CompilerParams(
            dimension_semantics=("arbitrary",), vmem_limit_bytes=_vmem_limit(est)),
        name="memattn",
    )(pg, kv)


def _merge_kernel(gates_ref, ya_ref, yp_ref, ym_ref, x_ref, wa_ref, wp_ref, wm_ref, wo_ref, gain_ref, o_ref):
    d = x_ref.shape[1]
    for r in range(0, x_ref.shape[0], MERGE_ROWS):
        rows = pl.ds(r, MERGE_ROWS)
        merged = gates_ref[rows, 0:d].astype(F32) * jnp.dot(ya_ref[rows, :], wa_ref[...], preferred_element_type=F32)
        merged += gates_ref[rows, d:2 * d].astype(F32) * jnp.dot(yp_ref[rows, :], wp_ref[...],
                                                                   preferred_element_type=F32)
        merged += gates_ref[rows, 2 * d:3 * d].astype(F32) * jnp.dot(ym_ref[rows, :], wm_ref[...],
                                                                       preferred_element_type=F32)
        z = jnp.dot(merged.astype(BF16), wo_ref[...], preferred_element_type=F32)
        o_ref[rows, :] = x_ref[rows, :] + _rms(z, gain_ref[...])


def _merge(gates, y_attn, y_pool, y_mem, x, wa, wp, wm, wo, gain):
    n, d = x.shape
    tm = MERGE_TM
    wbytes = (wa.size + wp.size + wm.size + wo.size) * 2

    def const(shape):
        return pl.BlockSpec(shape, lambda i: (0, 0), pipeline_mode=pl.Buffered(1))

    est = (2 * tm * (N_BRANCH * d + y_attn.shape[1] + y_pool.shape[1] + y_mem.shape[1]) * 2
           + 4 * tm * d * 4 + wbytes + 4 * tm * d * 4)
    return pl.pallas_call(
        _merge_kernel,
        grid=(n // tm,),
        in_specs=[
            pl.BlockSpec((tm, N_BRANCH * d), lambda i: (i, 0)),
            pl.BlockSpec((tm, y_attn.shape[1]), lambda i: (i, 0)),
            pl.BlockSpec((tm, y_pool.shape[1]), lambda i: (i, 0)),
            pl.BlockSpec((tm, y_mem.shape[1]), lambda i: (i, 0)),
            pl.BlockSpec((tm, d), lambda i: (i, 0)),
            const(wa.shape), const(wp.shape), const(wm.shape), const(wo.shape),
            pl.BlockSpec((1, d), lambda i: (0, 0)),
        ],
        out_specs=pl.BlockSpec((tm, d), lambda i: (i, 0)),
        out_shape=jax.ShapeDtypeStruct((n, d), F32),
        compiler_params=pltpu.CompilerParams(
            dimension_semantics=("arbitrary",), vmem_limit_bytes=_vmem_limit(est)),
        name="merge",
    )(gates, y_attn, y_pool, y_mem, x, wa, wp, wm, wo, gain)


def _cast_kernel(w_ref, *o_refs):
    lo = 0
    for o_ref in o_refs:
        width = o_ref.shape[1]
        o_ref[...] = w_ref[:, lo:lo + width].astype(o_ref.dtype)
        lo += width


def _cast_bf16(w, widths=None, target_bytes=CAST_BLOCK_BYTES):
    rows, cols = w.shape
    widths = (cols,) if widths is None else tuple(widths)
    br = rows
    while br * cols * 4 > target_bytes and br % 32 == 0:
        br //= 2
    est = 2 * br * cols * 4 + 2 * br * cols * 2
    outs = pl.pallas_call(
        _cast_kernel,
        grid=(rows // br,),
        in_specs=[pl.BlockSpec((br, cols), lambda i: (i, 0))],
        out_specs=[pl.BlockSpec((br, wd), lambda i: (i, 0)) for wd in widths],
        out_shape=[jax.ShapeDtypeStruct((rows, wd), BF16) for wd in widths],
        compiler_params=pltpu.CompilerParams(
            dimension_semantics=("arbitrary",), vmem_limit_bytes=_vmem_limit(est)),
        name="cast",
    )(w)
    return outs if len(widths) > 1 else outs[0]


def _cast_chunked_kernel(w_ref, *o_refs, width):
    for k, o_ref in enumerate(o_refs):
        nchunk, _, tf = o_ref.shape
        for c in range(nchunk):
            valid = min(tf, width - c * tf)
            lo = k * width + c * tf
            o_ref[c, :, :valid] = w_ref[:, lo:lo + valid].astype(o_ref.dtype)
            if valid < tf:
                o_ref[c, :, valid:] = jnp.zeros((o_ref.shape[1], tf - valid), o_ref.dtype)


def _cast_chunked(w, n_out, tf, target_bytes=CAST_BLOCK_BYTES):
    rows, cols = w.shape
    width = cols // n_out
    nchunk = pl.cdiv(width, tf)
    br = rows
    while br * cols * 4 > target_bytes and br % 32 == 0:
        br //= 2
    est = 2 * br * cols * 4 + 2 * n_out * nchunk * br * tf * 2
    return pl.pallas_call(
        functools.partial(_cast_chunked_kernel, width=width),
        grid=(rows // br,),
        in_specs=[pl.BlockSpec((br, cols), lambda i: (i, 0))],
        out_specs=[pl.BlockSpec((nchunk, br, tf), lambda i: (0, i, 0)) for _ in range(n_out)],
        out_shape=[jax.ShapeDtypeStruct((nchunk, rows, tf), BF16) for _ in range(n_out)],
        compiler_params=pltpu.CompilerParams(
            dimension_semantics=("arbitrary",), vmem_limit_bytes=_vmem_limit(est)),
        name="cast_chunked",
    )(w)


def _row(v):
    return v.reshape(1, -1).astype(F32)


def _trunk(x, mem, p, slopes):
    batch, seq, d = x.shape
    xf = x.reshape(batch * seq, d)
    x1 = _ffn(xf, p["ffn1_norm_pre"], p["ffn1_wg"], p["ffn1_wu"], p["ffn1_wd"], p["ffn1_norm_post"],
              p["ffn1_norm_post"], final=False)
    gates, nat, qkv4, qkv16 = _mixproj(x1, p["mix_norm_pre"], p["w_gate"], p["b_gate"], p["w_in"],
                                       batch=batch, seq=seq)
    col_pool = D_XATTN
    col_qkv0 = col_pool + D_POOL
    kv = _kvproj(mem.reshape(batch * N_MEM, d), p["mem_norm"], p["w_mem_kv"])
    y_attn = _attention(nat, qkv4, qkv16, slopes, batch=batch, seq=seq, col0=col_qkv0)
    y_pool = _pool(nat, p["w_pool"], p["pool_scale"], batch=batch, seq=seq, col0=col_pool)
    y_mem = _memattn(nat, kv, seq=seq, col0=0)
    x2 = _merge(gates, y_attn, y_pool, y_mem, x1, p["w_br_attn"], p["w_br_pool"], p["w_br_mem"], p["w_out"],
                p["mix_norm_post"])
    y = _ffn(x2, p["ffn2_norm_pre"], p["ffn2_wg"], p["ffn2_wu"], p["ffn2_wd"], p["ffn2_norm_post"],
             p["final_norm"], final=True)
    return y.reshape(batch, seq, d)


def _prep_params(ffn1_norm_pre, ffn1_w_up, ffn1_w_down, ffn1_norm_post, mix_norm_pre, mem_norm, w_in, w_mem_kv,
                 w_pool, pool_scale, w_br_attn, w_br_pool, w_br_mem, w_gate, b_gate, w_out, mix_norm_post,
                 ffn2_norm_pre, ffn2_w_up, ffn2_w_down, ffn2_norm_post, final_norm):
    p = {}
    for name, w_up, w_down in (("ffn1", ffn1_w_up, ffn1_w_down), ("ffn2", ffn2_w_up, ffn2_w_down)):
        dff = w_down.shape[0]
        p[name + "_wg"], p[name + "_wu"] = _cast_chunked(w_up, 2, FFN_TF)
        p[name + "_wd"] = _cast_bf16(w_down)
    p["w_gate"] = _cast_bf16(w_gate)
    p["w_in"] = _cast_bf16(w_in)
    p["b_gate"] = _row(b_gate)
    p["w_mem_kv"] = _cast_bf16(w_mem_kv)
    p["w_pool"] = w_pool.astype(BF16)
    p["pool_scale"] = _row(pool_scale)
    for name, w in (("w_br_attn", w_br_attn), ("w_br_pool", w_br_pool), ("w_br_mem", w_br_mem), ("w_out", w_out)):
        p[name] = _cast_bf16(w)
    for name, v in (("ffn1_norm_pre", ffn1_norm_pre), ("ffn1_norm_post", ffn1_norm_post),
                    ("mix_norm_pre", mix_norm_pre), ("mem_norm", mem_norm), ("mix_norm_post", mix_norm_post),
                    ("ffn2_norm_pre", ffn2_norm_pre), ("ffn2_norm_post", ffn2_norm_post),
                    ("final_norm", final_norm)):
        p[name] = _row(v)
    return p


def kernel(x_prompt, x_sample, mem_prompt, mem_sample, ffn1_norm_pre, ffn1_w_up, ffn1_w_down, ffn1_norm_post,
           mix_norm_pre, mem_norm, w_in, w_mem_kv, w_pool, pool_scale, w_br_attn, w_br_pool, w_br_mem, w_gate, b_gate,
           w_out, mix_norm_post, ffn2_norm_pre, ffn2_w_up, ffn2_w_down, ffn2_norm_post, final_norm):
    layer = [ffn1_norm_pre, ffn1_w_up, ffn1_w_down, ffn1_norm_post, mix_norm_pre, mem_norm, w_in, w_mem_kv, w_pool,
             pool_scale, w_br_attn, w_br_pool, w_br_mem, w_gate, b_gate, w_out, mix_norm_post, ffn2_norm_pre,
             ffn2_w_up, ffn2_w_down, ffn2_norm_post, final_norm]
    depth = ffn1_norm_pre.shape[0]
    slopes = jnp.asarray(_alibi_slopes())
    y_prompt, y_sample = x_prompt, x_sample
    for layer_idx in range(depth):
        p = _prep_params(*[w[layer_idx] for w in layer])
        y_prompt = _trunk(y_prompt, mem_prompt, p, slopes)
        y_sample = _trunk(y_sample, mem_sample, p, slopes)
    return (y_prompt, y_sample)
```

```python
import functools

import numpy as np
import jax
import jax.numpy as jnp
from jax import lax
from jax.experimental import pallas as pl
from jax.experimental.pallas import tpu as pltpu

F32 = jnp.float32
BF16 = jnp.bfloat16

D_MODEL = 2048
N_MEM = 256
HEAD_DIM = 128
DIL_PAIRS = ((128, 1), (512, 4), (2048, 16))
HEADS_PER_GROUP = 4
N_GROUPS = len(DIL_PAIRS)
N_DIL_HEADS = HEADS_PER_GROUP * N_GROUPS
D_ATTN = N_DIL_HEADS * HEAD_DIM
POOL_WINDOWS = (2, 4, 8, 16)
POOL_GROUP = 256
D_POOL = POOL_GROUP * len(POOL_WINDOWS)
N_XHEADS = 4
XHEAD_DIM = 256
D_XATTN = N_XHEADS * XHEAD_DIM
N_BRANCH = 3
EPS = 1e-6
NEG = -1e30
BAND_RADIUS = 64

V7X_VMEM_BYTES = 64 * 1024 * 1024
MIB = 1024 * 1024

FFN_TM = 1024
FFN_TF = 512
PROJ_TM = 1024
PROJ_TN = 512
PROJ_TG = 1536
PROJ_ROWS = 256
ATT_QB = 128
ATT_KW = 256
ATT_UNROLL = 8
ATT_FIN_ROWS = 256
POOL_ROWS = 256
POOL_HALO = 16
MEM_TM = 512
MERGE_TM = 512
MERGE_ROWS = 512
CAST_BLOCK_BYTES = 8 * MIB


def _vmem_limit(nbytes):
    return int(min(nbytes * 5 // 4 + 8 * MIB, V7X_VMEM_BYTES - 6 * MIB))


def _rms(x, gain):
    ms = jnp.mean(x * x, axis=-1, keepdims=True)
    return x * lax.rsqrt(ms + EPS) * gain


def _alibi_slopes():
    s = 2.0 ** (-8.0 * np.arange(1, N_DIL_HEADS + 1) / N_DIL_HEADS)
    return s.reshape(HEADS_PER_GROUP, N_GROUPS).T.astype(np.float32)


def _ffn_kernel(x_ref, gpre_ref, wg_ref, wu_ref, wd_ref, gpost_ref, gfin_ref, o_ref, hs_ref, *, tf, rem, final):
    f = pl.program_id(1)
    last = pl.num_programs(1) - 1

    @pl.when(f == 0)
    def _():
        hs_ref[...] = _rms(x_ref[...], gpre_ref[...]).astype(BF16)
        o_ref[...] = jnp.zeros_like(o_ref)

    def step(valid):
        hs = hs_ref[...]
        g = jnp.dot(hs, wg_ref[:, :valid], preferred_element_type=F32)
        u = jnp.dot(hs, wu_ref[:, :valid], preferred_element_type=F32)
        a = (g * jax.nn.sigmoid(g) * u).astype(BF16)
        o_ref[...] += jnp.dot(a, wd_ref[:valid, :], preferred_element_type=F32)

    if rem == tf:
        step(tf)
    else:
        pl.when(f < last)(functools.partial(step, tf))
        pl.when(f == last)(functools.partial(step, rem))

    @pl.when(f == last)
    def _():
        y = x_ref[...] + 0.5 * _rms(o_ref[...], gpost_ref[...])
        if final:
            y = _rms(y, gfin_ref[...])
        o_ref[...] = y


def _ffn(x, gpre, wg, wu, wd, gpost, gfin, *, final):
    n, d = x.shape
    tm, tf = FFN_TM, FFN_TF
    dff = wd.shape[0]
    nf = pl.cdiv(dff, tf)
    rem = dff - (nf - 1) * tf
    est = 2 * tm * d * 4 * 2 + tm * d * 2 + 2 * 3 * d * tf * 2 + 3 * tm * 2 * tf * 4
    vec = pl.BlockSpec((1, d), lambda i, f: (0, 0))
    return pl.pallas_call(
        functools.partial(_ffn_kernel, tf=tf, rem=rem, final=final),
        grid=(n // tm, nf),
        in_specs=[
            pl.BlockSpec((tm, d), lambda i, f: (i, 0)),
            vec,
            pl.BlockSpec((None, d, tf), lambda i, f: (f, 0, 0)),
            pl.BlockSpec((None, d, tf), lambda i, f: (f, 0, 0)),
            pl.BlockSpec((tf, d), lambda i, f: (f, 0)),
            vec,
            vec,
        ],
        out_specs=pl.BlockSpec((tm, d), lambda i, f: (i, 0)),
        out_shape=jax.ShapeDtypeStruct((n, d), F32),
        scratch_shapes=[pltpu.VMEM((tm, d), BF16)],
        compiler_params=pltpu.CompilerParams(
            dimension_semantics=("arbitrary", "arbitrary"), vmem_limit_bytes=_vmem_limit(est)),
        name="ffn_final" if final else "ffn",
    )(x, gpre, wg, wu, wd, gpost, gfin)


def _kvproj_kernel(x_ref, g_ref, w_ref, o_ref, hs_ref):
    @pl.when(pl.program_id(1) == 0)
    def _():
        hs_ref[...] = _rms(x_ref[...], g_ref[...]).astype(BF16)

    o_ref[...] = jnp.dot(hs_ref[...], w_ref[...], preferred_element_type=F32).astype(o_ref.dtype)


def _kvproj(x, gain, w):
    n, d = x.shape
    ncol = w.shape[1]
    tm, tn = n, PROJ_TN
    est = 2 * tm * d * 4 + tm * d * 2 + 2 * d * tn * 2 + 2 * tm * tn * 2 + 2 * tm * tn * 4
    return pl.pallas_call(
        _kvproj_kernel,
        grid=(n // tm, ncol // tn),
        in_specs=[
            pl.BlockSpec((tm, d), lambda i, j: (i, 0)),
            pl.BlockSpec((1, d), lambda i, j: (0, 0)),
            pl.BlockSpec((d, tn), lambda i, j: (0, j)),
        ],
        out_specs=pl.BlockSpec((tm, tn), lambda i, j: (i, j)),
        out_shape=jax.ShapeDtypeStruct((n, ncol), BF16),
        scratch_shapes=[pltpu.VMEM((tm, d), BF16)],
        compiler_params=pltpu.CompilerParams(
            dimension_semantics=("arbitrary", "arbitrary"), vmem_limit_bytes=_vmem_limit(est)),
        name="kvproj",
    )(x, gain, w)


NAT_CHUNKS = (11, 12, 9, 10, 0, 3, 6)
N_NAT_STEPS = len(NAT_CHUNKS)
N_DIL_STEPS = 3
N_GATE_STEPS = N_BRANCH * D_MODEL // PROJ_TG
FIRST_D4 = N_NAT_STEPS
FIRST_D16 = FIRST_D4 + N_DIL_STEPS
FIRST_GATE = FIRST_D16 + N_DIL_STEPS
MIX_STEPS = FIRST_GATE + N_GATE_STEPS


def _lin_chunk(s):
    t = jnp.minimum(s, FIRST_GATE - 1)
    nat = jnp.where(t < 2, 11 + t, jnp.where(t < 4, 7 + t, 3 * (t - 4)))
    d4 = 3 * (t - FIRST_D4) + 1
    d16 = 3 * (t - FIRST_D16) + 2
    return jnp.where(t < FIRST_D4, nat, jnp.where(t < FIRST_D16, d4, d16))


def _mixproj_kernel(x_ref, g_ref, wg_ref, b_ref, wi_ref, og_ref, on_ref, o4_ref, o16_ref, hs_ref, stage_ref):
    s = pl.program_id(1)
    tm = x_ref.shape[0]
    row_chunks = [pl.ds(r, PROJ_ROWS) for r in range(0, tm, PROJ_ROWS)]

    @pl.when(s == 0)
    def _():
        hs_ref[...] = _rms(x_ref[...], g_ref[...]).astype(BF16)

    @pl.when(s < FIRST_D4)
    def _():
        for rows in row_chunks:
            on_ref[rows, :] = jnp.dot(hs_ref[rows, :], wi_ref[...], preferred_element_type=F32).astype(on_ref.dtype)

    def dilated(o_ref, d):
        per_res = PROJ_ROWS // d
        for rc, rows in enumerate(row_chunks):
            acc = jnp.dot(hs_ref[rows, :], wi_ref[...], preferred_element_type=F32)
            for c in range(PROJ_TN // HEAD_DIM):
                stage_ref[c] = acc[:, c * HEAD_DIM:(c + 1) * HEAD_DIM]
            for res in range(d):
                for c in range(PROJ_TN // HEAD_DIM):
                    o_ref[res, pl.ds(rc * per_res, per_res), c * HEAD_DIM:(c + 1) * HEAD_DIM] = (
                        stage_ref[c, pl.ds(res, per_res, stride=d), :].astype(o_ref.dtype))

    pl.when((s >= FIRST_D4) & (s < FIRST_D16))(functools.partial(dilated, o4_ref, DIL_PAIRS[1][1]))
    pl.when((s >= FIRST_D16) & (s < FIRST_GATE))(functools.partial(dilated, o16_ref, DIL_PAIRS[2][1]))

    @pl.when(s >= FIRST_GATE)
    def _():
        for rows in row_chunks:
            acc = jnp.dot(hs_ref[rows, :], wg_ref[...], preferred_element_type=F32)
            og_ref[rows, :] = jax.nn.sigmoid(acc + b_ref[...]).astype(og_ref.dtype)


def _mixproj(x, gain, w_gate, b_gate, w_in, *, batch, seq):
    n, d = x.shape
    tm, tn, tg = PROJ_TM, PROJ_TN, PROJ_TG
    tiles_per_seq = seq // tm
    d4, d16 = DIL_PAIRS[1][1], DIL_PAIRS[2][1]

    def gate_idx(s):
        return jnp.maximum(s - FIRST_GATE, 0)

    def dil_spec(dil, first):
        return pl.BlockSpec(
            (None, dil, tm // dil, tn),
            lambda i, s: (i // tiles_per_seq, 0, i % tiles_per_seq, jnp.clip(s - first, 0, N_DIL_STEPS - 1)))

    est = (2 * tm * d * 4 + tm * d * 2 + 2 * d * (tg + tn) * 2 + 2 * tm * (tg + 3 * tn) * 2
           + PROJ_ROWS * tn * 4 + 2 * PROJ_ROWS * tg * 4)
    return pl.pallas_call(
        _mixproj_kernel,
        grid=(n // tm, MIX_STEPS),
        in_specs=[
            pl.BlockSpec((tm, d), lambda i, s: (i, 0)),
            pl.BlockSpec((1, d), lambda i, s: (0, 0)),
            pl.BlockSpec((d, tg), lambda i, s: (0, gate_idx(s))),
            pl.BlockSpec((1, tg), lambda i, s: (0, gate_idx(s))),
            pl.BlockSpec((d, tn), lambda i, s: (0, _lin_chunk(s))),
        ],
        out_specs=[
            pl.BlockSpec((tm, tg), lambda i, s: (i, gate_idx(s))),
            pl.BlockSpec((tm, tn), lambda i, s: (i, jnp.minimum(s, N_NAT_STEPS - 1))),
            dil_spec(d4, FIRST_D4),
            dil_spec(d16, FIRST_D16),
        ],
        out_shape=[
            jax.ShapeDtypeStruct((n, N_BRANCH * d), BF16),
            jax.ShapeDtypeStruct((n, N_NAT_STEPS * tn), BF16),
            jax.ShapeDtypeStruct((batch, d4, seq // d4, N_DIL_STEPS * tn), BF16),
            jax.ShapeDtypeStruct((batch, d16, seq // d16, N_DIL_STEPS * tn), BF16),
        ],
        scratch_shapes=[pltpu.VMEM((tm, d), BF16), pltpu.VMEM((tn // HEAD_DIM, PROJ_ROWS, HEAD_DIM), F32)],
        compiler_params=pltpu.CompilerParams(
            dimension_semantics=("arbitrary", "arbitrary"), vmem_limit_bytes=_vmem_limit(est)),
        name="mixproj",
    )(x, gain, w_gate, b_gate, w_in)


def _attn_kernel(slopes_ref, q0_ref, k0_ref, v0_ref, q1_ref, k1_ref, v1_ref, q2_ref, k2_ref, v2_ref, o_ref,
                 m0_ref, m1_ref, m2_ref, l0_ref, l1_ref, l2_ref, n0_ref, n1_ref, n2_ref, va_ref, bias_ref, *, seq):
    h = pl.program_id(1)
    scale = HEAD_DIM ** -0.5
    qb = ATT_QB
    qkv_refs = ((q0_ref, k0_ref, v0_ref), (q1_ref, k1_ref, v1_ref), (q2_ref, k2_ref, v2_ref))
    m_refs = (m0_ref, m1_ref, m2_ref)
    l_refs = (l0_ref, l1_ref, l2_ref)
    n_refs = (n0_ref, n1_ref, n2_ref)

    va_ref[:, HEAD_DIM:] = jnp.ones((seq, HEAD_DIM), BF16)

    for gi, (_, d) in enumerate(DIL_PAIRS):
        q_ref, k_ref, v_ref = qkv_refs[gi]
        sub_len = seq // d
        kw = min(ATT_KW, sub_len)
        nb = sub_len // qb
        slope_d = slopes_ref[gi, h] * float(d)

        row = lax.broadcasted_iota(jnp.int32, (qb, kw), 0)
        col = lax.broadcasted_iota(jnp.int32, (qb, kw), 1)
        for kind, off in enumerate((0, -BAND_RADIUS, qb - kw)):
            arel = jnp.abs(col - row + off)
            bias_ref[kind, :, :kw] = jnp.where(arel <= BAND_RADIUS, -slope_d * arel.astype(F32), NEG)

        if d == 1:
            va_ref[:, :HEAD_DIM] = v_ref[...]
        else:
            for res in range(d):
                va_ref[pl.ds(res * sub_len, sub_len), :HEAD_DIM] = v_ref[res]

        def block(idx, gi=gi, d=d, q_ref=q_ref, k_ref=k_ref, sub_len=sub_len, kw=kw, nb=nb):
            res = idx // nb
            i = idx - res * nb
            q0 = pl.multiple_of(i * qb, qb)
            k0 = pl.multiple_of(jnp.clip(q0 - BAND_RADIUS, 0, sub_len - kw), BAND_RADIUS)
            if d == 1:
                qblk = q_ref[pl.ds(q0, qb), :]
                kwin = k_ref[pl.ds(k0, kw), :]
            else:
                qblk = q_ref[res, pl.ds(q0, qb), :]
                kwin = k_ref[res, pl.ds(k0, kw), :]
            s = lax.dot_general(qblk, kwin, (((1,), (1,)), ((), ())), preferred_element_type=F32)
            if nb == 1:
                bias = bias_ref[0, :, :kw]
            else:
                kind = jnp.where(i == 0, 0, jnp.where(i == nb - 1, 2, 1))
                bias = bias_ref[kind]
            s = s * scale + bias
            m = jnp.max(s, axis=-1, keepdims=True)
            p = jnp.exp(s - m).astype(BF16)
            kstart = pl.multiple_of(res * sub_len + k0, BAND_RADIUS)
            na = jnp.dot(p, va_ref[pl.ds(kstart, kw), :], preferred_element_type=F32)
            if d == 1:
                rows = pl.ds(q0, qb)
            else:
                rows = pl.ds(q0 * d + res, qb, stride=d)
            m_refs[gi][rows, :] = jnp.broadcast_to(m, (qb, HEAD_DIM))
            l_refs[gi][rows, :] = na[:, HEAD_DIM:]
            n_refs[gi][rows, :] = na[:, :HEAD_DIM]

        def blocks(it, carry, block=block):
            for u in range(ATT_UNROLL):
                block(it * ATT_UNROLL + u)
            return carry

        lax.fori_loop(0, d * nb // ATT_UNROLL, blocks, 0)

    def fin(c, carry):
        r = pl.ds(pl.multiple_of(c * ATT_FIN_ROWS, ATT_FIN_ROWS), ATT_FIN_ROWS)
        ms = [m_ref[r, :] for m_ref in m_refs]
        mx = jnp.maximum(jnp.maximum(ms[0], ms[1]), ms[2])
        num = jnp.zeros((ATT_FIN_ROWS, HEAD_DIM), F32)
        den = jnp.zeros((ATT_FIN_ROWS, HEAD_DIM), F32)
        for gi in range(N_GROUPS):
            e = jnp.exp(ms[gi] - mx)
            num += e * n_refs[gi][r, :]
            den += e * l_refs[gi][r, :]
        o_ref[r, :] = (num / den).astype(o_ref.dtype)
        return carry

    lax.fori_loop(0, seq // ATT_FIN_ROWS, fin, 0)


def _attention(nat, qkv4, qkv16, slopes, *, batch, seq, col0):
    blk0 = col0 // HEAD_DIM
    d4, d16 = DIL_PAIRS[1][1], DIL_PAIRS[2][1]

    def nat_map(b, h, *, which):
        return (b, blk0 + which * HEADS_PER_GROUP + h)

    def dil_map(b, h, *, which):
        return (b, 0, 0, which * HEADS_PER_GROUP + h)

    in_specs = [pl.BlockSpec(memory_space=pltpu.SMEM)]
    in_specs += [pl.BlockSpec((seq, HEAD_DIM), functools.partial(nat_map, which=w)) for w in range(3)]
    for dil in (d4, d16):
        in_specs += [pl.BlockSpec((None, dil, seq // dil, HEAD_DIM), functools.partial(dil_map, which=w))
                     for w in range(3)]
    tile = seq * HEAD_DIM
    est = 10 * 2 * tile * 2 + 9 * tile * 4 + 2 * tile * 2 + 3 * ATT_QB * ATT_KW * 4
    state = [pltpu.VMEM((seq, HEAD_DIM), F32) for _ in range(3 * N_GROUPS)]
    return pl.pallas_call(
        functools.partial(_attn_kernel, seq=seq),
        grid=(batch, HEADS_PER_GROUP),
        in_specs=in_specs,
        out_specs=pl.BlockSpec((seq, HEAD_DIM), lambda b, h: (b, h)),
        out_shape=jax.ShapeDtypeStruct((batch * seq, HEADS_PER_GROUP * HEAD_DIM), BF16),
        scratch_shapes=state + [
            pltpu.VMEM((seq, 2 * HEAD_DIM), BF16),
            pltpu.VMEM((3, ATT_QB, ATT_KW), F32),
        ],
        compiler_params=pltpu.CompilerParams(
            dimension_semantics=("arbitrary", "arbitrary"), vmem_limit_bytes=_vmem_limit(est)),
        name="attention",
    )(slopes, nat, nat, nat, qkv4, qkv4, qkv4, qkv16, qkv16, qkv16)


def _pool_kernel(u_ref, w_ref, sc_ref, o_ref, pad_ref, pooled_ref, *, seq):
    grp = pl.program_id(1)
    halo, rows = POOL_HALO, POOL_ROWS
    ext = rows + 2 * halo
    zeros = jnp.zeros((halo, POOL_GROUP), F32)
    pad_ref[pl.ds(0, halo), :] = zeros
    pad_ref[pl.ds(halo + seq, halo), :] = zeros
    pad_ref[pl.ds(halo, seq), :] = u_ref[...].astype(F32)

    def run(window):
        def chunk(c, carry):
            base = pl.multiple_of(c * rows, rows)
            p = pad_ref[pl.ds(base, ext), :]
            s = p + pltpu.roll(p, 1, 0)
            half = 1
            while 2 * half < window:
                s = pltpu.roll(s, ext - half, 0) + pltpu.roll(s, half, 0)
                half *= 2
            s = s[halo:halo + rows, :]
            tok = p[halo:halo + rows, :]
            t = base + lax.broadcasted_iota(jnp.int32, (rows, 1), 0)
            lo = jnp.maximum(t - window // 2, 0)
            hi = jnp.minimum(t - window // 2 + window, seq)
            pooled_ref[pl.ds(base, rows), :] = (s / (hi - lo).astype(F32) - tok).astype(BF16)
            return carry

        lax.fori_loop(0, seq // rows, chunk, 0)

    for gi, window in enumerate(POOL_WINDOWS):
        pl.when(grp == gi)(functools.partial(run, window))

    y = jnp.dot(pooled_ref[...], w_ref[...], preferred_element_type=F32) * sc_ref[...]
    o_ref[...] = y.astype(o_ref.dtype)


def _pool(pg, w_pool, pool_scale, *, batch, seq, col0):
    blk0 = col0 // POOL_GROUP
    ngrp = len(POOL_WINDOWS)
    est = (2 * seq * POOL_GROUP * 2 * 2 + (seq + 2 * POOL_HALO) * POOL_GROUP * 4 + 2 * POOL_GROUP * POOL_GROUP * 2
           + seq * POOL_GROUP * (2 + 4))
    return pl.pallas_call(
        functools.partial(_pool_kernel, seq=seq),
        grid=(batch, ngrp),
        in_specs=[
            pl.BlockSpec((seq, POOL_GROUP), lambda b, g: (b, blk0 + g)),
            pl.BlockSpec((None, POOL_GROUP, POOL_GROUP), lambda b, g: (g, 0, 0)),
            pl.BlockSpec((1, POOL_GROUP), lambda b, g: (0, g)),
        ],
        out_specs=pl.BlockSpec((seq, POOL_GROUP), lambda b, g: (b, g)),
        out_shape=jax.ShapeDtypeStruct((batch * seq, D_POOL), BF16),
        scratch_shapes=[pltpu.VMEM((seq + 2 * POOL_HALO, POOL_GROUP), F32), pltpu.VMEM((seq, POOL_GROUP), BF16)],
        compiler_params=pltpu.CompilerParams(
            dimension_semantics=("arbitrary", "arbitrary"), vmem_limit_bytes=_vmem_limit(est)),
        name="pool",
    )(pg, w_pool, pool_scale)


def _memattn_kernel(q_ref, kv_ref, o_ref):
    scale = XHEAD_DIM ** -0.5
    for hh in range(N_XHEADS):
        lo = hh * XHEAD_DIM
        q = q_ref[:, lo:lo + XHEAD_DIM]
        k = kv_ref[:, lo:lo + XHEAD_DIM]
        v = kv_ref[:, D_XATTN + lo:D_XATTN + lo + XHEAD_DIM]
        s = lax.dot_general(q, k, (((1,), (1,)), ((), ())), preferred_element_type=F32) * scale
        m = jnp.max(s, axis=-1, keepdims=True)
        p = jnp.exp(s - m)
        l = jnp.sum(p, axis=-1, keepdims=True)
        y = jnp.dot(p.astype(BF16), v, preferred_element_type=F32) / l
        o_ref[:, lo:lo + XHEAD_DIM] = y.astype(o_ref.dtype)


def _memattn(pg, kv, *, seq, col0):
    n = pg.shape[0]
    tm = MEM_TM
    blk0 = col0 // D_XATTN
    tiles_per_seq = seq // tm
    est = 2 * tm * D_XATTN * 2 * 2 + 2 * N_MEM * 2 * D_XATTN * 2 + 4 * tm * N_MEM * 4
    return pl.pallas_call(
        _memattn_kernel,
        grid=(n // tm,),
        in_specs=[
            pl.BlockSpec((tm, D_XATTN), lambda i: (i, blk0)),
            pl.BlockSpec((N_MEM, 2 * D_XATTN), lambda i: (i // tiles_per_seq, 0)),
        ],
        out_specs=pl.BlockSpec((tm, D_XATTN), lambda i: (i, 0)),
        out_shape=jax.ShapeDtypeStruct((n, D_XATTN), BF16),
        compiler_params=pltpu.CompilerParams(
            dimension_semantics=("arbitrary",), vmem_limit_bytes=_vmem_limit(est)),
        name="memattn",
    )(pg, kv)


def _merge_kernel(gates_ref, ya_ref, yp_ref, ym_ref, x_ref, wa_ref, wp_ref, wm_ref, wo_ref, gain_ref, o_ref):
    d = x_ref.shape[1]
    for r in range(0, x_ref.shape[0], MERGE_ROWS):
        rows = pl.ds(r, MERGE_ROWS)
        merged = gates_ref[rows, 0:d].astype(F32) * jnp.dot(ya_ref[rows, :], wa_ref[...], preferred_element_type=F32)
        merged += gates_ref[rows, d:2 * d].astype(F32) * jnp.dot(yp_ref[rows, :], wp_ref[...],
                                                                   preferred_element_type=F32)
        merged += gates_ref[rows, 2 * d:3 * d].astype(F32) * jnp.dot(ym_ref[rows, :], wm_ref[...],
                                                                       preferred_element_type=F32)
        z = jnp.dot(merged.astype(BF16), wo_ref[...], preferred_element_type=F32)
        o_ref[rows, :] = x_ref[rows, :] + _rms(z, gain_ref[...])


def _merge(gates, y_attn, y_pool, y_mem, x, wa, wp, wm, wo, gain):
    n, d = x.shape
    tm = MERGE_TM
    wbytes = (wa.size + wp.size + wm.size + wo.size) * 2

    def const(shape):
        return pl.BlockSpec(shape, lambda i: (0, 0), pipeline_mode=pl.Buffered(1))

    est = (2 * tm * (N_BRANCH * d + y_attn.shape[1] + y_pool.shape[1] + y_mem.shape[1]) * 2
           + 4 * tm * d * 4 + wbytes + 4 * tm * d * 4)
    return pl.pallas_call(
        _merge_kernel,
        grid=(n // tm,),
        in_specs=[
            pl.BlockSpec((tm, N_BRANCH * d), lambda i: (i, 0)),
            pl.BlockSpec((tm, y_attn.shape[1]), lambda i: (i, 0)),
            pl.BlockSpec((tm, y_pool.shape[1]), lambda i: (i, 0)),
            pl.BlockSpec((tm, y_mem.shape[1]), lambda i: (i, 0)),
            pl.BlockSpec((tm, d), lambda i: (i, 0)),
            const(wa.shape), const(wp.shape), const(wm.shape), const(wo.shape),
            pl.BlockSpec((1, d), lambda i: (0, 0)),
        ],
        out_specs=pl.BlockSpec((tm, d), lambda i: (i, 0)),
        out_shape=jax.ShapeDtypeStruct((n, d), F32),
        compiler_params=pltpu.CompilerParams(
            dimension_semantics=("arbitrary",), vmem_limit_bytes=_vmem_limit(est)),
        name="merge",
    )(gates, y_attn, y_pool, y_mem, x, wa, wp, wm, wo, gain)


def _cast_kernel(w_ref, *o_refs):
    lo = 0
    for o_ref in o_refs:
        width = o_ref.shape[1]
        o_ref[...] = w_ref[:, lo:lo + width].astype(o_ref.dtype)
        lo += width


def _cast_bf16(w, widths=None, target_bytes=CAST_BLOCK_BYTES):
    rows, cols = w.shape
    widths = (cols,) if widths is None else tuple(widths)
    br = rows
    while br * cols * 4 > target_bytes and br % 32 == 0:
        br //= 2
    est = 2 * br * cols * 4 + 2 * br * cols * 2
    outs = pl.pallas_call(
        _cast_kernel,
        grid=(rows // br,),
        in_specs=[pl.BlockSpec((br, cols), lambda i: (i, 0))],
        out_specs=[pl.BlockSpec((br, wd), lambda i: (i, 0)) for wd in widths],
        out_shape=[jax.ShapeDtypeStruct((rows, wd), BF16) for wd in widths],
        compiler_params=pltpu.CompilerParams(
            dimension_semantics=("arbitrary",), vmem_limit_bytes=_vmem_limit(est)),
        name="cast",
    )(w)
    return outs if len(widths) > 1 else outs[0]


def _cast_chunked_kernel(w_ref, *o_refs, width):
    for k, o_ref in enumerate(o_refs):
        nchunk, _, tf = o_ref.shape
        for c in range(nchunk):
            valid = min(tf, width - c * tf)
            lo = k * width + c * tf
            o_ref[c, :, :valid] = w_ref[:, lo:lo + valid].astype(o_ref.dtype)
            if valid < tf:
                o_ref[c, :, valid:] = jnp.zeros((o_ref.shape[1], tf - valid), o_ref.dtype)


def _cast_chunked(w, n_out, tf, target_bytes=CAST_BLOCK_BYTES):
    rows, cols = w.shape
    width = cols // n_out
    nchunk = pl.cdiv(width, tf)
    br = rows
    while br * cols * 4 > target_bytes and br % 32 == 0:
        br //= 2
    est = 2 * br * cols * 4 + 2 * n_out * nchunk * br * tf * 2
    return pl.pallas_call(
        functools.partial(_cast_chunked_kernel, width=width),
        grid=(rows // br,),
        in_specs=[pl.BlockSpec((br, cols), lambda i: (i, 0))],
        out_specs=[pl.BlockSpec((nchunk, br, tf), lambda i: (0, i, 0)) for _ in range(n_out)],
        out_shape=[jax.ShapeDtypeStruct((nchunk, rows, tf), BF16) for _ in range(n_out)],
        compiler_params=pltpu.CompilerParams(
            dimension_semantics=("arbitrary",), vmem_limit_bytes=_vmem_limit(est)),
        name="cast_chunked",
    )(w)


def _row(v):
    return v.reshape(1, -1).astype(F32)


def _trunk(x, mem, p, slopes):
    batch, seq, d = x.shape
    xf = x.reshape(batch * seq, d)
    x1 = _ffn(xf, p["ffn1_norm_pre"], p["ffn1_wg"], p["ffn1_wu"], p["ffn1_wd"], p["ffn1_norm_post"],
              p["ffn1_norm_post"], final=False)
    gates, nat, qkv4, qkv16 = _mixproj(x1, p["mix_norm_pre"], p["w_gate"], p["b_gate"], p["w_in"],
                                       batch=batch, seq=seq)
    col_pool = D_XATTN
    col_qkv0 = col_pool + D_POOL
    kv = _kvproj(mem.reshape(batch * N_MEM, d), p["mem_norm"], p["w_mem_kv"])
    y_attn = _attention(nat, qkv4, qkv16, slopes, batch=batch, seq=seq, col0=col_qkv0)
    y_pool = _pool(nat, p["w_pool"], p["pool_scale"], batch=batch, seq=seq, col0=col_pool)
    y_mem = _memattn(nat, kv, seq=seq, col0=0)
    x2 = _merge(gates, y_attn, y_pool, y_mem, x1, p["w_br_attn"], p["w_br_pool"], p["w_br_mem"], p["w_out"],
                p["mix_norm_post"])
    y = _ffn(x2, p["ffn2_norm_pre"], p["ffn2_wg"], p["ffn2_wu"], p["ffn2_wd"], p["ffn2_norm_post"],
             p["final_norm"], final=True)
    return y.reshape(batch, seq, d)


def _prep_params(ffn1_norm_pre, ffn1_w_up, ffn1_w_down, ffn1_norm_post, mix_norm_pre, mem_norm, w_in, w_mem_kv,
                 w_pool, pool_scale, w_br_attn, w_br_pool, w_br_mem, w_gate, b_gate, w_out, mix_norm_post,
                 ffn2_norm_pre, ffn2_w_up, ffn2_w_down, ffn2_norm_post, final_norm):
    p = {}
    for name, w_up, w_down in (("ffn1", ffn1_w_up, ffn1_w_down), ("ffn2", ffn2_w_up, ffn2_w_down)):
        dff = w_down.shape[0]
        p[name + "_wg"], p[name + "_wu"] = _cast_chunked(w_up, 2, FFN_TF)
        p[name + "_wd"] = _cast_bf16(w_down)
    p["w_gate"] = _cast_bf16(w_gate)
    p["w_in"] = _cast_bf16(w_in)
    p["b_gate"] = _row(b_gate)
    p["w_mem_kv"] = _cast_bf16(w_mem_kv)
    p["w_pool"] = w_pool.astype(BF16)
    p["pool_scale"] = _row(pool_scale)
    for name, w in (("w_br_attn", w_br_attn), ("w_br_pool", w_br_pool), ("w_br_mem", w_br_mem), ("w_out", w_out)):
        p[name] = _cast_bf16(w)
    for name, v in (("ffn1_norm_pre", ffn1_norm_pre), ("ffn1_norm_post", ffn1_norm_post),
                    ("mix_norm_pre", mix_norm_pre), ("mem_norm", mem_norm), ("mix_norm_post", mix_norm_post),
                    ("ffn2_norm_pre", ffn2_norm_pre), ("ffn2_norm_post", ffn2_norm_post),
                    ("final_norm", final_norm)):
        p[name] = _row(v)
    return p


def kernel(x_prompt, x_sample, mem_prompt, mem_sample, ffn1_norm_pre, ffn1_w_up, ffn1_w_down, ffn1_norm_post,
           mix_norm_pre, mem_norm, w_in, w_mem_kv, w_pool, pool_scale, w_br_attn, w_br_pool, w_br_mem, w_gate, b_gate,
           w_out, mix_norm_post, ffn2_norm_pre, ffn2_w_up, ffn2_w_down, ffn2_norm_post, final_norm):
    layer = [ffn1_norm_pre, ffn1_w_up, ffn1_w_down, ffn1_norm_post, mix_norm_pre, mem_norm, w_in, w_mem_kv, w_pool,
             pool_scale, w_br_attn, w_br_pool, w_br_mem, w_gate, b_gate, w_out, mix_norm_post, ffn2_norm_pre,
             ffn2_w_up, ffn2_w_down, ffn2_norm_post, final_norm]
    depth = ffn1_norm_pre.shape[0]
    slopes = jnp.asarray(_alibi_slopes())
    y_prompt, y_sample = x_prompt, x_sample
    for layer_idx in range(depth):
        p = _prep_params(*[w[layer_idx] for w in layer])
        y_prompt = _trunk(y_prompt, mem_prompt, p, slopes)
        y_sample = _trunk(y_sample, mem_sample, p, slopes)
    return (y_prompt, y_sample)
```

```python
import functools

import numpy as np
import jax
import jax.numpy as jnp
from jax import lax
from jax.experimental import pallas as pl
from jax.experimental.pallas import tpu as pltpu

F32 = jnp.float32
BF16 = jnp.bfloat16

D_MODEL = 2048
N_MEM = 256
HEAD_DIM = 128
DIL_PAIRS = ((128, 1), (512, 4), (2048, 16))
HEADS_PER_GROUP = 4
N_GROUPS = len(DIL_PAIRS)
N_DIL_HEADS = HEADS_PER_GROUP * N_GROUPS
D_ATTN = N_DIL_HEADS * HEAD_DIM
POOL_WINDOWS = (2, 4, 8, 16)
POOL_GROUP = 256
D_POOL = POOL_GROUP * len(POOL_WINDOWS)
N_XHEADS = 4
XHEAD_DIM = 256
D_XATTN = N_XHEADS * XHEAD_DIM
N_BRANCH = 3
EPS = 1e-6
NEG = -1e30
BAND_RADIUS = 64

V7X_VMEM_BYTES = 64 * 1024 * 1024
MIB = 1024 * 1024

FFN_TM = 1024
FFN_TF = 1024
PROJ_TM = 1024
PROJ_TN = 512
PROJ_TG = 1536
PROJ_ROWS = 256
ATT_QB = 128
ATT_KW = 256
ATT_UNROLL = 8
ATT_FIN_ROWS = 256
POOL_ROWS = 256
POOL_HALO = 16
MEM_TM = 512
MERGE_TM = 512
MERGE_ROWS = 512
CAST_BLOCK_BYTES = 8 * MIB


def _vmem_limit(nbytes):
    return int(min(nbytes * 5 // 4 + 8 * MIB, V7X_VMEM_BYTES - 2 * MIB))


def _rms(x, gain):
    ms = jnp.mean(x * x, axis=-1, keepdims=True)
    return x * lax.rsqrt(ms + EPS) * gain


def _alibi_slopes():
    s = 2.0 ** (-8.0 * np.arange(1, N_DIL_HEADS + 1) / N_DIL_HEADS)
    return s.reshape(HEADS_PER_GROUP, N_GROUPS).T.astype(np.float32)


def _ffn_kernel(x_ref, gpre_ref, wg_ref, wu_ref, wd_ref, gpost_ref, gfin_ref, o_hbm, acc_ref, hs_ref, sem,
                *, tf, rem, final):
    i = pl.program_id(0)
    f = pl.program_id(1)
    last = pl.num_programs(1) - 1
    tm = acc_ref.shape[0]

    def writeback(tile):
        return pltpu.make_async_copy(acc_ref, o_hbm.at[pl.ds(pl.multiple_of(tile * tm, tm), tm), :], sem)

    @pl.when(f == 0)
    def _():
        hs_ref[...] = _rms(x_ref[...], gpre_ref[...]).astype(BF16)

        @pl.when(i > 0)
        def _():
            writeback(i - 1).wait()

        acc_ref[...] = jnp.zeros_like(acc_ref)

    def step(valid):
        hs = hs_ref[...]
        g = jnp.dot(hs, wg_ref[:, :valid], preferred_element_type=F32)
        u = jnp.dot(hs, wu_ref[:, :valid], preferred_element_type=F32)
        a = (g * jax.nn.sigmoid(g) * u).astype(BF16)
        acc_ref[...] += jnp.dot(a, wd_ref[:valid, :], preferred_element_type=F32)

    if rem == tf:
        step(tf)
    else:
        pl.when(f < last)(functools.partial(step, tf))
        pl.when(f == last)(functools.partial(step, rem))

    @pl.when(f == last)
    def _():
        y = x_ref[...] + 0.5 * _rms(acc_ref[...], gpost_ref[...])
        if final:
            y = _rms(y, gfin_ref[...])
        acc_ref[...] = y
        writeback(i).start()

        @pl.when(i == pl.num_programs(0) - 1)
        def _():
            writeback(i).wait()


def _ffn(x, gpre, wg, wu, wd, gpost, gfin, *, final):
    n, d = x.shape
    tm, tf = FFN_TM, FFN_TF
    dff = wd.shape[0]
    nf = pl.cdiv(dff, tf)
    rem = dff - (nf - 1) * tf
    est = 2 * tm * d * 4 + tm * d * 4 + tm * d * 2 + 2 * 3 * d * tf * 2 + 2 * tm * tf * 4
    vec = pl.BlockSpec((1, d), lambda i, f: (0, 0))
    return pl.pallas_call(
        functools.partial(_ffn_kernel, tf=tf, rem=rem, final=final),
        grid=(n // tm, nf),
        in_specs=[
            pl.BlockSpec((tm, d), lambda i, f: (i, 0)),
            vec,
            pl.BlockSpec((None, d, tf), lambda i, f: (f, 0, 0)),
            pl.BlockSpec((None, d, tf), lambda i, f: (f, 0, 0)),
            pl.BlockSpec((tf, d), lambda i, f: (f, 0)),
            vec,
            vec,
        ],
        out_specs=pl.BlockSpec(memory_space=pl.ANY),
        out_shape=jax.ShapeDtypeStruct((n, d), F32),
        scratch_shapes=[pltpu.VMEM((tm, d), F32), pltpu.VMEM((tm, d), BF16), pltpu.SemaphoreType.DMA(())],
        compiler_params=pltpu.CompilerParams(
            dimension_semantics=("arbitrary", "arbitrary"), vmem_limit_bytes=_vmem_limit(est)),
        name="ffn_final" if final else "ffn",
    )(x, gpre, wg, wu, wd, gpost, gfin)


def _kvproj_kernel(x_ref, g_ref, w_ref, o_ref, hs_ref):
    @pl.when(pl.program_id(1) == 0)
    def _():
        hs_ref[...] = _rms(x_ref[...], g_ref[...]).astype(BF16)

    o_ref[...] = jnp.dot(hs_ref[...], w_ref[...], preferred_element_type=F32).astype(o_ref.dtype)


def _kvproj(x, gain, w):
    n, d = x.shape
    ncol = w.shape[1]
    tm, tn = n, PROJ_TN
    est = 2 * tm * d * 4 + tm * d * 2 + 2 * d * tn * 2 + 2 * tm * tn * 2 + 2 * tm * tn * 4
    return pl.pallas_call(
        _kvproj_kernel,
        grid=(n // tm, ncol // tn),
        in_specs=[
            pl.BlockSpec((tm, d), lambda i, j: (i, 0)),
            pl.BlockSpec((1, d), lambda i, j: (0, 0)),
            pl.BlockSpec((d, tn), lambda i, j: (0, j)),
        ],
        out_specs=pl.BlockSpec((tm, tn), lambda i, j: (i, j)),
        out_shape=jax.ShapeDtypeStruct((n, ncol), BF16),
        scratch_shapes=[pltpu.VMEM((tm, d), BF16)],
        compiler_params=pltpu.CompilerParams(
            dimension_semantics=("arbitrary", "arbitrary"), vmem_limit_bytes=_vmem_limit(est)),
        name="kvproj",
    )(x, gain, w)


NAT_CHUNKS = (11, 12, 9, 10, 0, 3, 6)
N_NAT_STEPS = len(NAT_CHUNKS)
N_DIL_STEPS = 3
N_GATE_STEPS = N_BRANCH * D_MODEL // PROJ_TG
FIRST_D4 = N_NAT_STEPS
FIRST_D16 = FIRST_D4 + N_DIL_STEPS
FIRST_GATE = FIRST_D16 + N_DIL_STEPS
MIX_STEPS = FIRST_GATE + N_GATE_STEPS


def _lin_chunk(s):
    t = jnp.minimum(s, FIRST_GATE - 1)
    nat = jnp.where(t < 2, 11 + t, jnp.where(t < 4, 7 + t, 3 * (t - 4)))
    d4 = 3 * (t - FIRST_D4) + 1
    d16 = 3 * (t - FIRST_D16) + 2
    return jnp.where(t < FIRST_D4, nat, jnp.where(t < FIRST_D16, d4, d16))


def _mixproj_kernel(x_ref, g_ref, wg_ref, b_ref, wi_ref, og_ref, on_ref, o4_ref, o16_ref, hs_ref, stage_ref):
    s = pl.program_id(1)
    tm = x_ref.shape[0]
    row_chunks = [pl.ds(r, PROJ_ROWS) for r in range(0, tm, PROJ_ROWS)]

    @pl.when(s == 0)
    def _():
        hs_ref[...] = _rms(x_ref[...], g_ref[...]).astype(BF16)

    @pl.when(s < FIRST_D4)
    def _():
        for rows in row_chunks:
            on_ref[rows, :] = jnp.dot(hs_ref[rows, :], wi_ref[...], preferred_element_type=F32).astype(on_ref.dtype)

    def dilated(o_ref, d):
        per_res = PROJ_ROWS // d
        for rc, rows in enumerate(row_chunks):
            acc = jnp.dot(hs_ref[rows, :], wi_ref[...], preferred_element_type=F32)
            for c in range(PROJ_TN // HEAD_DIM):
                stage_ref[c] = acc[:, c * HEAD_DIM:(c + 1) * HEAD_DIM]
            for res in range(d):
                for c in range(PROJ_TN // HEAD_DIM):
                    o_ref[res, pl.ds(rc * per_res, per_res), c * HEAD_DIM:(c + 1) * HEAD_DIM] = (
                        stage_ref[c, pl.ds(res, per_res, stride=d), :].astype(o_ref.dtype))

    pl.when((s >= FIRST_D4) & (s < FIRST_D16))(functools.partial(dilated, o4_ref, DIL_PAIRS[1][1]))
    pl.when((s >= FIRST_D16) & (s < FIRST_GATE))(functools.partial(dilated, o16_ref, DIL_PAIRS[2][1]))

    @pl.when(s >= FIRST_GATE)
    def _():
        for rows in row_chunks:
            acc = jnp.dot(hs_ref[rows, :], wg_ref[...], preferred_element_type=F32)
            og_ref[rows, :] = jax.nn.sigmoid(acc + b_ref[...]).astype(og_ref.dtype)


def _mixproj(x, gain, w_gate, b_gate, w_in, *, batch, seq):
    n, d = x.shape
    tm, tn, tg = PROJ_TM, PROJ_TN, PROJ_TG
    tiles_per_seq = seq // tm
    d4, d16 = DIL_PAIRS[1][1], DIL_PAIRS[2][1]

    def gate_idx(s):
        return jnp.maximum(s - FIRST_GATE, 0)

    def dil_spec(dil, first):
        return pl.BlockSpec(
            (None, dil, tm // dil, tn),
            lambda i, s: (i // tiles_per_seq, 0, i % tiles_per_seq, jnp.clip(s - first, 0, N_DIL_STEPS - 1)))

    est = (2 * tm * d * 4 + tm * d * 2 + 2 * d * (tg + tn) * 2 + 2 * tm * (tg + 3 * tn) * 2
           + PROJ_ROWS * tn * 4 + 2 * PROJ_ROWS * tg * 4)
    return pl.pallas_call(
        _mixproj_kernel,
        grid=(n // tm, MIX_STEPS),
        in_specs=[
            pl.BlockSpec((tm, d), lambda i, s: (i, 0)),
            pl.BlockSpec((1, d), lambda i, s: (0, 0)),
            pl.BlockSpec((d, tg), lambda i, s: (0, gate_idx(s))),
            pl.BlockSpec((1, tg), lambda i, s: (0, gate_idx(s))),
            pl.BlockSpec((d, tn), lambda i, s: (0, _lin_chunk(s))),
        ],
        out_specs=[
            pl.BlockSpec((tm, tg), lambda i, s: (i, gate_idx(s))),
            pl.BlockSpec((tm, tn), lambda i, s: (i, jnp.minimum(s, N_NAT_STEPS - 1))),
            dil_spec(d4, FIRST_D4),
            dil_spec(d16, FIRST_D16),
        ],
        out_shape=[
            jax.ShapeDtypeStruct((n, N_BRANCH * d), BF16),
            jax.ShapeDtypeStruct((n, N_NAT_STEPS * tn), BF16),
            jax.ShapeDtypeStruct((batch, d4, seq // d4, N_DIL_STEPS * tn), BF16),
            jax.ShapeDtypeStruct((batch, d16, seq // d16, N_DIL_STEPS * tn), BF16),
        ],
        scratch_shapes=[pltpu.VMEM((tm, d), BF16), pltpu.VMEM((tn // HEAD_DIM, PROJ_ROWS, HEAD_DIM), F32)],
        compiler_params=pltpu.CompilerParams(
            dimension_semantics=("arbitrary", "arbitrary"), vmem_limit_bytes=_vmem_limit(est)),
        name="mixproj",
    )(x, gain, w_gate, b_gate, w_in)


def _attn_kernel(slopes_ref, q0_ref, k0_ref, v0_ref, q1_ref, k1_ref, v1_ref, q2_ref, k2_ref, v2_ref, o_ref,
                 m0_ref, m1_ref, m2_ref, l0_ref, l1_ref, l2_ref, n0_ref, n1_ref, n2_ref, va_ref, bias_ref, *, seq):
    h = pl.program_id(1)
    scale = HEAD_DIM ** -0.5
    qb = ATT_QB
    qkv_refs = ((q0_ref, k0_ref, v0_ref), (q1_ref, k1_ref, v1_ref), (q2_ref, k2_ref, v2_ref))
    m_refs = (m0_ref, m1_ref, m2_ref)
    l_refs = (l0_ref, l1_ref, l2_ref)
    n_refs = (n0_ref, n1_ref, n2_ref)

    va_ref[:, HEAD_DIM:] = jnp.ones((seq, HEAD_DIM), BF16)

    for gi, (_, d) in enumerate(DIL_PAIRS):
        q_ref, k_ref, v_ref = qkv_refs[gi]
        sub_len = seq // d
        kw = min(ATT_KW, sub_len)
        nb = sub_len // qb
        slope_d = slopes_ref[gi, h] * float(d)

        row = lax.broadcasted_iota(jnp.int32, (qb, kw), 0)
        col = lax.broadcasted_iota(jnp.int32, (qb, kw), 1)
        for kind, off in enumerate((0, -BAND_RADIUS, qb - kw)):
            arel = jnp.abs(col - row + off)
            bias_ref[kind, :, :kw] = jnp.where(arel <= BAND_RADIUS, -slope_d * arel.astype(F32), NEG)

        if d == 1:
            va_ref[:, :HEAD_DIM] = v_ref[...]
        else:
            for res in range(d):
                va_ref[pl.ds(res * sub_len, sub_len), :HEAD_DIM] = v_ref[res]

        def block(idx, gi=gi, d=d, q_ref=q_ref, k_ref=k_ref, sub_len=sub_len, kw=kw, nb=nb):
            res = idx // nb
            i = idx - res * nb
            q0 = pl.multiple_of(i * qb, qb)
            k0 = pl.multiple_of(jnp.clip(q0 - BAND_RADIUS, 0, sub_len - kw), BAND_RADIUS)
            if d == 1:
                qblk = q_ref[pl.ds(q0, qb), :]
                kwin = k_ref[pl.ds(k0, kw), :]
            else:
                qblk = q_ref[res, pl.ds(q0, qb), :]
                kwin = k_ref[res, pl.ds(k0, kw), :]
            s = lax.dot_general(qblk, kwin, (((1,), (1,)), ((), ())), preferred_element_type=F32)
            if nb == 1:
                bias = bias_ref[0, :, :kw]
            else:
                kind = jnp.where(i == 0, 0, jnp.where(i == nb - 1, 2, 1))
                bias = bias_ref[kind]
            s = s * scale + bias
            m = jnp.max(s, axis=-1, keepdims=True)
            p = jnp.exp(s - m).astype(BF16)
            kstart = pl.multiple_of(res * sub_len + k0, BAND_RADIUS)
            na = jnp.dot(p, va_ref[pl.ds(kstart, kw), :], preferred_element_type=F32)
            if d == 1:
                rows = pl.ds(q0, qb)
            else:
                rows = pl.ds(q0 * d + res, qb, stride=d)
            m_refs[gi][rows, :] = jnp.broadcast_to(m, (qb, HEAD_DIM))
            l_refs[gi][rows, :] = na[:, HEAD_DIM:]
            n_refs[gi][rows, :] = na[:, :HEAD_DIM]

        def blocks(it, carry, block=block):
            for u in range(ATT_UNROLL):
                block(it * ATT_UNROLL + u)
            return carry

        lax.fori_loop(0, d * nb // ATT_UNROLL, blocks, 0)

    def fin(c, carry):
        r = pl.ds(pl.multiple_of(c * ATT_FIN_ROWS, ATT_FIN_ROWS), ATT_FIN_ROWS)
        ms = [m_ref[r, :] for m_ref in m_refs]
        mx = jnp.maximum(jnp.maximum(ms[0], ms[1]), ms[2])
        num = jnp.zeros((ATT_FIN_ROWS, HEAD_DIM), F32)
        den = jnp.zeros((ATT_FIN_ROWS, HEAD_DIM), F32)
        for gi in range(N_GROUPS):
            e = jnp.exp(ms[gi] - mx)
            num += e * n_refs[gi][r, :]
            den += e * l_refs[gi][r, :]
        o_ref[r, :] = (num / den).astype(o_ref.dtype)
        return carry

    lax.fori_loop(0, seq // ATT_FIN_ROWS, fin, 0)


def _attention(nat, qkv4, qkv16, slopes, *, batch, seq, col0):
    blk0 = col0 // HEAD_DIM
    d4, d16 = DIL_PAIRS[1][1], DIL_PAIRS[2][1]

    def nat_map(b, h, *, which):
        return (b, blk0 + which * HEADS_PER_GROUP + h)

    def dil_map(b, h, *, which):
        return (b, 0, 0, which * HEADS_PER_GROUP + h)

    in_specs = [pl.BlockSpec(memory_space=pltpu.SMEM)]
    in_specs += [pl.BlockSpec((seq, HEAD_DIM), functools.partial(nat_map, which=w)) for w in range(3)]
    for dil in (d4, d16):
        in_specs += [pl.BlockSpec((None, dil, seq // dil, HEAD_DIM), functools.partial(dil_map, which=w))
                     for w in range(3)]
    tile = seq * HEAD_DIM
    est = 10 * 2 * tile * 2 + 9 * tile * 4 + 2 * tile * 2 + 3 * ATT_QB * ATT_KW * 4
    state = [pltpu.VMEM((seq, HEAD_DIM), F32) for _ in range(3 * N_GROUPS)]
    return pl.pallas_call(
        functools.partial(_attn_kernel, seq=seq),
        grid=(batch, HEADS_PER_GROUP),
        in_specs=in_specs,
        out_specs=pl.BlockSpec((seq, HEAD_DIM), lambda b, h: (b, h)),
        out_shape=jax.ShapeDtypeStruct((batch * seq, HEADS_PER_GROUP * HEAD_DIM), BF16),
        scratch_shapes=state + [
            pltpu.VMEM((seq, 2 * HEAD_DIM), BF16),
            pltpu.VMEM((3, ATT_QB, ATT_KW), F32),
        ],
        compiler_params=pltpu.CompilerParams(
            dimension_semantics=("arbitrary", "arbitrary"), vmem_limit_bytes=_vmem_limit(est)),
        name="attention",
    )(slopes, nat, nat, nat, qkv4, qkv4, qkv4, qkv16, qkv16, qkv16)


def _pool_kernel(u_ref, w_ref, sc_ref, o_ref, pad_ref, pooled_ref, *, seq):
    grp = pl.program_id(1)
    halo, rows = POOL_HALO, POOL_ROWS
    ext = rows + 2 * halo
    zeros = jnp.zeros((halo, POOL_GROUP), F32)
    pad_ref[pl.ds(0, halo), :] = zeros
    pad_ref[pl.ds(halo + seq, halo), :] = zeros
    pad_ref[pl.ds(halo, seq), :] = u_ref[...].astype(F32)

    def run(window):
        def chunk(c, carry):
            base = pl.multiple_of(c * rows, rows)
            p = pad_ref[pl.ds(base, ext), :]
            s = p + pltpu.roll(p, 1, 0)
            half = 1
            while 2 * half < window:
                s = pltpu.roll(s, ext - half, 0) + pltpu.roll(s, half, 0)
                half *= 2
            s = s[halo:halo + rows, :]
            tok = p[halo:halo + rows, :]
            t = base + lax.broadcasted_iota(jnp.int32, (rows, 1), 0)
            lo = jnp.maximum(t - window // 2, 0)
            hi = jnp.minimum(t - window // 2 + window, seq)
            pooled_ref[pl.ds(base, rows), :] = (s / (hi - lo).astype(F32) - tok).astype(BF16)
            return carry

        lax.fori_loop(0, seq // rows, chunk, 0)

    for gi, window in enumerate(POOL_WINDOWS):
        pl.when(grp == gi)(functools.partial(run, window))

    y = jnp.dot(pooled_ref[...], w_ref[...], preferred_element_type=F32) * sc_ref[...]
    o_ref[...] = y.astype(o_ref.dtype)


def _pool(pg, w_pool, pool_scale, *, batch, seq, col0):
    blk0 = col0 // POOL_GROUP
    ngrp = len(POOL_WINDOWS)
    est = (2 * seq * POOL_GROUP * 2 * 2 + (seq + 2 * POOL_HALO) * POOL_GROUP * 4 + 2 * POOL_GROUP * POOL_GROUP * 2
           + seq * POOL_GROUP * (2 + 4))
    return pl.pallas_call(
        functools.partial(_pool_kernel, seq=seq),
        grid=(batch, ngrp),
        in_specs=[
            pl.BlockSpec((seq, POOL_GROUP), lambda b, g: (b, blk0 + g)),
            pl.BlockSpec((None, POOL_GROUP, POOL_GROUP), lambda b, g: (g, 0, 0)),
            pl.BlockSpec((1, POOL_GROUP), lambda b, g: (0, g)),
        ],
        out_specs=pl.BlockSpec((seq, POOL_GROUP), lambda b, g: (b, g)),
        out_shape=jax.ShapeDtypeStruct((batch * seq, D_POOL), BF16),
        scratch_shapes=[pltpu.VMEM((seq + 2 * POOL_HALO, POOL_GROUP), F32), pltpu.VMEM((seq, POOL_GROUP), BF16)],
        compiler_params=pltpu.CompilerParams(
            dimension_semantics=("arbitrary", "arbitrary"), vmem_limit_bytes=_vmem_limit(est)),
        name="pool",
    )(pg, w_pool, pool_scale)


def _memattn_kernel(q_ref, kv_ref, o_ref):
    scale = XHEAD_DIM ** -0.5
    for hh in range(N_XHEADS):
        lo = hh * XHEAD_DIM
        q = q_ref[:, lo:lo + XHEAD_DIM]
        k = kv_ref[:, lo:lo + XHEAD_DIM]
        v = kv_ref[:, D_XATTN + lo:D_XATTN + lo + XHEAD_DIM]
        s = lax.dot_general(q, k, (((1,), (1,)), ((), ())), preferred_element_type=F32) * scale
        m = jnp.max(s, axis=-1, keepdims=True)
        p = jnp.exp(s - m)
        l = jnp.sum(p, axis=-1, keepdims=True)
        y = jnp.dot(p.astype(BF16), v, preferred_element_type=F32) / l
        o_ref[:, lo:lo + XHEAD_DIM] = y.astype(o_ref.dtype)


def _memattn(pg, kv, *, seq, col0):
    n = pg.shape[0]
    tm = MEM_TM
    blk0 = col0 // D_XATTN
    tiles_per_seq = seq // tm
    est = 2 * tm * D_XATTN * 2 * 2 + 2 * N_MEM * 2 * D_XATTN * 2 + 4 * tm * N_MEM * 4
    return pl.pallas_call(
        _memattn_kernel,
        grid=(n // tm,),
        in_specs=[
            pl.BlockSpec((tm, D_XATTN), lambda i: (i, blk0)),
            pl.BlockSpec((N_MEM, 2 * D_XATTN), lambda i: (i // tiles_per_seq, 0)),
        ],
        out_specs=pl.BlockSpec((tm, D_XATTN), lambda i: (i, 0)),
        out_shape=jax.ShapeDtypeStruct((n, D_XATTN), BF16),
        compiler_params=pltpu.CompilerParams(
            dimension_semantics=("arbitrary",), vmem_limit_bytes=_vmem_limit(est)),
        name="memattn",
    )(pg, kv)


def _merge_kernel(gates_ref, ya_ref, yp_ref, ym_ref, x_ref, wa_ref, wp_ref, wm_ref, wo_ref, gain_ref, o_ref):
    d = x_ref.shape[1]
    for r in range(0, x_ref.shape[0], MERGE_ROWS):
        rows = pl.ds(r, MERGE_ROWS)
        merged = gates_ref[rows, 0:d].astype(F32) * jnp.dot(ya_ref[rows, :], wa_ref[...], preferred_element_type=F32)
        merged += gates_ref[rows, d:2 * d].astype(F32) * jnp.dot(yp_ref[rows, :], wp_ref[...],
                                                                   preferred_element_type=F32)
        merged += gates_ref[rows, 2 * d:3 * d].astype(F32) * jnp.dot(ym_ref[rows, :], wm_ref[...],
                                                                       preferred_element_type=F32)
        z = jnp.dot(merged.astype(BF16), wo_ref[...], preferred_element_type=F32)
        o_ref[rows, :] = x_ref[rows, :] + _rms(z, gain_ref[...])


def _merge(gates, y_attn, y_pool, y_mem, x, wa, wp, wm, wo, gain):
    n, d = x.shape
    tm = MERGE_TM
    wbytes = (wa.size + wp.size + wm.size + wo.size) * 2

    def const(shape):
        return pl.BlockSpec(shape, lambda i: (0, 0), pipeline_mode=pl.Buffered(1))

    est = (2 * tm * (N_BRANCH * d + y_attn.shape[1] + y_pool.shape[1] + y_mem.shape[1]) * 2
           + 4 * tm * d * 4 + wbytes + 4 * tm * d * 4)
    return pl.pallas_call(
        _merge_kernel,
        grid=(n // tm,),
        in_specs=[
            pl.BlockSpec((tm, N_BRANCH * d), lambda i: (i, 0)),
            pl.BlockSpec((tm, y_attn.shape[1]), lambda i: (i, 0)),
            pl.BlockSpec((tm, y_pool.shape[1]), lambda i: (i, 0)),
            pl.BlockSpec((tm, y_mem.shape[1]), lambda i: (i, 0)),
            pl.BlockSpec((tm, d), lambda i: (i, 0)),
            const(wa.shape), const(wp.shape), const(wm.shape), const(wo.shape),
            pl.BlockSpec((1, d), lambda i: (0, 0)),
        ],
        out_specs=pl.BlockSpec((tm, d), lambda i: (i, 0)),
        out_shape=jax.ShapeDtypeStruct((n, d), F32),
        compiler_params=pltpu.CompilerParams(
            dimension_semantics=("arbitrary",), vmem_limit_bytes=_vmem_limit(est)),
        name="merge",
    )(gates, y_attn, y_pool, y_mem, x, wa, wp, wm, wo, gain)


def _cast_kernel(w_ref, *o_refs):
    lo = 0
    for o_ref in o_refs:
        width = o_ref.shape[1]
        o_ref[...] = w_ref[:, lo:lo + width].astype(o_ref.dtype)
        lo += width


def _cast_bf16(w, widths=None, target_bytes=CAST_BLOCK_BYTES):
    rows, cols = w.shape
    widths = (cols,) if widths is None else tuple(widths)
    br = rows
    while br * cols * 4 > target_bytes and br % 32 == 0:
        br //= 2
    est = 2 * br * cols * 4 + 2 * br * cols * 2
    outs = pl.pallas_call(
        _cast_kernel,
        grid=(rows // br,),
        in_specs=[pl.BlockSpec((br, cols), lambda i: (i, 0))],
        out_specs=[pl.BlockSpec((br, wd), lambda i: (i, 0)) for wd in widths],
        out_shape=[jax.ShapeDtypeStruct((rows, wd), BF16) for wd in widths],
        compiler_params=pltpu.CompilerParams(
            dimension_semantics=("arbitrary",), vmem_limit_bytes=_vmem_limit(est)),
        name="cast",
    )(w)
    return outs if len(widths) > 1 else outs[0]


def _cast_chunked_kernel(w_ref, *o_refs, width):
    for k, o_ref in enumerate(o_refs):
        nchunk, _, tf = o_ref.shape
        for c in range(nchunk):
            valid = min(tf, width - c * tf)
            lo = k * width + c * tf
            o_ref[c, :, :valid] = w_ref[:, lo:lo + valid].astype(o_ref.dtype)
            if valid < tf:
                o_ref[c, :, valid:] = jnp.zeros((o_ref.shape[1], tf - valid), o_ref.dtype)


def _cast_chunked(w, n_out, tf, target_bytes=CAST_BLOCK_BYTES):
    rows, cols = w.shape
    width = cols // n_out
    nchunk = pl.cdiv(width, tf)
    br = rows
    while br * cols * 4 > target_bytes and br % 32 == 0:
        br //= 2
    est = 2 * br * cols * 4 + 2 * n_out * nchunk * br * tf * 2
    return pl.pallas_call(
        functools.partial(_cast_chunked_kernel, width=width),
        grid=(rows // br,),
        in_specs=[pl.BlockSpec((br, cols), lambda i: (i, 0))],
        out_specs=[pl.BlockSpec((nchunk, br, tf), lambda i: (0, i, 0)) for _ in range(n_out)],
        out_shape=[jax.ShapeDtypeStruct((nchunk, rows, tf), BF16) for _ in range(n_out)],
        compiler_params=pltpu.CompilerParams(
            dimension_semantics=("arbitrary",), vmem_limit_bytes=_vmem_limit(est)),
        name="cast_chunked",
    )(w)


def _row(v):
    return v.reshape(1, -1).astype(F32)


def _trunk(x, mem, p, slopes):
    batch, seq, d = x.shape
    xf = x.reshape(batch * seq, d)
    x1 = _ffn(xf, p["ffn1_norm_pre"], p["ffn1_wg"], p["ffn1_wu"], p["ffn1_wd"], p["ffn1_norm_post"],
              p["ffn1_norm_post"], final=False)
    gates, nat, qkv4, qkv16 = _mixproj(x1, p["mix_norm_pre"], p["w_gate"], p["b_gate"], p["w_in"],
                                       batch=batch, seq=seq)
    col_pool = D_XATTN
    col_qkv0 = col_pool + D_POOL
    kv = _kvproj(mem.reshape(batch * N_MEM, d), p["mem_norm"], p["w_mem_kv"])
    y_attn = _attention(nat, qkv4, qkv16, slopes, batch=batch, seq=seq, col0=col_qkv0)
    y_pool = _pool(nat, p["w_pool"], p["pool_scale"], batch=batch, seq=seq, col0=col_pool)
    y_mem = _memattn(nat, kv, seq=seq, col0=0)
    x2 = _merge(gates, y_attn, y_pool, y_mem, x1, p["w_br_attn"], p["w_br_pool"], p["w_br_mem"], p["w_out"],
                p["mix_norm_post"])
    y = _ffn(x2, p["ffn2_norm_pre"], p["ffn2_wg"], p["ffn2_wu"], p["ffn2_wd"], p["ffn2_norm_post"],
             p["final_norm"], final=True)
    return y.reshape(batch, seq, d)


def _prep_params(ffn1_norm_pre, ffn1_w_up, ffn1_w_down, ffn1_norm_post, mix_norm_pre, mem_norm, w_in, w_mem_kv,
                 w_pool, pool_scale, w_br_attn, w_br_pool, w_br_mem, w_gate, b_gate, w_out, mix_norm_post,
                 ffn2_norm_pre, ffn2_w_up, ffn2_w_down, ffn2_norm_post, final_norm):
    p = {}
    for name, w_up, w_down in (("ffn1", ffn1_w_up, ffn1_w_down), ("ffn2", ffn2_w_up, ffn2_w_down)):
        dff = w_down.shape[0]
        p[name + "_wg"], p[name + "_wu"] = _cast_chunked(w_up, 2, FFN_TF)
        p[name + "_wd"] = _cast_bf16(w_down)
    p["w_gate"] = _cast_bf16(w_gate)
    p["w_in"] = _cast_bf16(w_in)
    p["b_gate"] = _row(b_gate)
    p["w_mem_kv"] = _cast_bf16(w_mem_kv)
    p["w_pool"] = w_pool.astype(BF16)
    p["pool_scale"] = _row(pool_scale)
    for name, w in (("w_br_attn", w_br_attn), ("w_br_pool", w_br_pool), ("w_br_mem", w_br_mem), ("w_out", w_out)):
        p[name] = _cast_bf16(w)
    for name, v in (("ffn1_norm_pre", ffn1_norm_pre), ("ffn1_norm_post", ffn1_norm_post),
                    ("mix_norm_pre", mix_norm_pre), ("mem_norm", mem_norm), ("mix_norm_post", mix_norm_post),
                    ("ffn2_norm_pre", ffn2_norm_pre), ("ffn2_norm_post", ffn2_norm_post),
                    ("final_norm", final_norm)):
        p[name] = _row(v)
    return p


def kernel(x_prompt, x_sample, mem_prompt, mem_sample, ffn1_norm_pre, ffn1_w_up, ffn1_w_down, ffn1_norm_post,
           mix_norm_pre, mem_norm, w_in, w_mem_kv, w_pool, pool_scale, w_br_attn, w_br_pool, w_br_mem, w_gate, b_gate,
           w_out, mix_norm_post, ffn2_norm_pre, ffn2_w_up, ffn2_w_down, ffn2_norm_post, final_norm):
    layer = [ffn1_norm_pre, ffn1_w_up, ffn1_w_down, ffn1_norm_post, mix_norm_pre, mem_norm, w_in, w_mem_kv, w_pool,
             pool_scale, w_br_attn, w_br_pool, w_br_mem, w_gate, b_gate, w_out, mix_norm_post, ffn2_norm_pre,
             ffn2_w_up, ffn2_w_down, ffn2_norm_post, final_norm]
    depth = ffn1_norm_pre.shape[0]
    slopes = jnp.asarray(_alibi_slopes())
    y_prompt, y_sample = x_prompt, x_sample
    for layer_idx in range(depth):
        p = _prep_params(*[w[layer_idx] for w in layer])
        y_prompt = _trunk(y_prompt, mem_prompt, p, slopes)
        y_sample = _trunk(y_sample, mem_sample, p, slopes)
    return (y_prompt, y_sample)
```

```python
import functools

import numpy as np
import jax
import jax.numpy as jnp
from jax import lax
from jax.experimental import pallas as pl
from jax.experimental.pallas import tpu as pltpu

F32 = jnp.float32
BF16 = jnp.bfloat16

D_MODEL = 2048
N_MEM = 256
HEAD_DIM = 128
DIL_PAIRS = ((128, 1), (512, 4), (2048, 16))
HEADS_PER_GROUP = 4
N_GROUPS = len(DIL_PAIRS)
N_DIL_HEADS = HEADS_PER_GROUP * N_GROUPS
D_ATTN = N_DIL_HEADS * HEAD_DIM
POOL_WINDOWS = (2, 4, 8, 16)
POOL_GROUP = 256
D_POOL = POOL_GROUP * len(POOL_WINDOWS)
N_XHEADS = 4
XHEAD_DIM = 256
D_XATTN = N_XHEADS * XHEAD_DIM
N_BRANCH = 3
EPS = 1e-6
NEG = -1e30
BAND_RADIUS = 64

V7X_VMEM_BYTES = 64 * 1024 * 1024
MIB = 1024 * 1024

FFN_TM = 1024
FFN_TF = 1024
PROJ_TM = 1024
PROJ_TN = 512
PROJ_TG = 1536
PROJ_ROWS = 256
ATT_QB = 128
ATT_KW = 256
ATT_UNROLL = 8
ATT_FIN_ROWS = 256
POOL_ROWS = 256
POOL_HALO = 16
MEM_TM = 512
MERGE_TM = 512
MERGE_ROWS = 512
CAST_BLOCK_BYTES = 8 * MIB


def _vmem_limit(nbytes):
    return int(min(nbytes * 5 // 4 + 8 * MIB, V7X_VMEM_BYTES - 2 * MIB))


def _rms(x, gain):
    ms = jnp.mean(x * x, axis=-1, keepdims=True)
    return x * lax.rsqrt(ms + EPS) * gain


def _alibi_slopes():
    s = 2.0 ** (-8.0 * np.arange(1, N_DIL_HEADS + 1) / N_DIL_HEADS)
    return s.reshape(HEADS_PER_GROUP, N_GROUPS).T.astype(np.float32)


def _ffn_kernel(x_ref, gpre_ref, wg_ref, wu_ref, wd_ref, gpost_ref, gfin_ref, o_hbm, acc_ref, hs_ref, sem,
                *, tf, rem, final):
    i = pl.program_id(0)
    f = pl.program_id(1)
    last = pl.num_programs(1) - 1
    tm = acc_ref.shape[0]

    def writeback(tile):
        return pltpu.make_async_copy(acc_ref, o_hbm.at[pl.ds(pl.multiple_of(tile * tm, tm), tm), :], sem)

    @pl.when(f == 0)
    def _():
        hs_ref[...] = _rms(x_ref[...], gpre_ref[...]).astype(BF16)

    def step(valid, first):
        hs = hs_ref[...]
        g = jnp.dot(hs, wg_ref[:, :valid], preferred_element_type=F32)
        u = jnp.dot(hs, wu_ref[:, :valid], preferred_element_type=F32)
        a = (g * jax.nn.sigmoid(g) * u).astype(BF16)
        if first:
            @pl.when(i > 0)
            def _():
                writeback(i - 1).wait()

            acc_ref[...] = jnp.dot(a, wd_ref[:valid, :], preferred_element_type=F32)
        else:
            acc_ref[...] += jnp.dot(a, wd_ref[:valid, :], preferred_element_type=F32)

    pl.when(f == 0)(functools.partial(step, tf, True))
    pl.when((f > 0) & (f < last))(functools.partial(step, tf, False))
    pl.when(f == last)(functools.partial(step, rem, False))

    @pl.when(f == last)
    def _():
        y = x_ref[...] + 0.5 * _rms(acc_ref[...], gpost_ref[...])
        if final:
            y = _rms(y, gfin_ref[...])
        acc_ref[...] = y
        writeback(i).start()

        @pl.when(i == pl.num_programs(0) - 1)
        def _():
            writeback(i).wait()


def _ffn(x, gpre, wg, wu, wd, gpost, gfin, *, final):
    n, d = x.shape
    tm, tf = FFN_TM, FFN_TF
    dff = wd.shape[0]
    nf = pl.cdiv(dff, tf)
    rem = dff - (nf - 1) * tf
    est = 2 * tm * d * 4 + tm * d * 4 + tm * d * 2 + 2 * 3 * d * tf * 2 + 2 * tm * tf * 4
    vec = pl.BlockSpec((1, d), lambda i, f: (0, 0))
    return pl.pallas_call(
        functools.partial(_ffn_kernel, tf=tf, rem=rem, final=final),
        grid=(n // tm, nf),
        in_specs=[
            pl.BlockSpec((tm, d), lambda i, f: (i, 0)),
            vec,
            pl.BlockSpec((None, d, tf), lambda i, f: (f, 0, 0)),
            pl.BlockSpec((None, d, tf), lambda i, f: (f, 0, 0)),
            pl.BlockSpec((tf, d), lambda i, f: (f, 0)),
            vec,
            vec,
        ],
        out_specs=pl.BlockSpec(memory_space=pl.ANY),
        out_shape=jax.ShapeDtypeStruct((n, d), F32),
        scratch_shapes=[pltpu.VMEM((tm, d), F32), pltpu.VMEM((tm, d), BF16), pltpu.SemaphoreType.DMA(())],
        compiler_params=pltpu.CompilerParams(
            dimension_semantics=("arbitrary", "arbitrary"), vmem_limit_bytes=_vmem_limit(est)),
        name="ffn_final" if final else "ffn",
    )(x, gpre, wg, wu, wd, gpost, gfin)


def _kvproj_kernel(x_ref, g_ref, w_ref, o_ref, hs_ref):
    @pl.when(pl.program_id(1) == 0)
    def _():
        hs_ref[...] = _rms(x_ref[...], g_ref[...]).astype(BF16)

    o_ref[...] = jnp.dot(hs_ref[...], w_ref[...], preferred_element_type=F32).astype(o_ref.dtype)


def _kvproj(x, gain, w):
    n, d = x.shape
    ncol = w.shape[1]
    tm, tn = n, PROJ_TN
    est = 2 * tm * d * 4 + tm * d * 2 + 2 * d * tn * 2 + 2 * tm * tn * 2 + 2 * tm * tn * 4
    return pl.pallas_call(
        _kvproj_kernel,
        grid=(n // tm, ncol // tn),
        in_specs=[
            pl.BlockSpec((tm, d), lambda i, j: (i, 0)),
            pl.BlockSpec((1, d), lambda i, j: (0, 0)),
            pl.BlockSpec((d, tn), lambda i, j: (0, j)),
        ],
        out_specs=pl.BlockSpec((tm, tn), lambda i, j: (i, j)),
        out_shape=jax.ShapeDtypeStruct((n, ncol), BF16),
        scratch_shapes=[pltpu.VMEM((tm, d), BF16)],
        compiler_params=pltpu.CompilerParams(
            dimension_semantics=("arbitrary", "arbitrary"), vmem_limit_bytes=_vmem_limit(est)),
        name="kvproj",
    )(x, gain, w)


NAT_CHUNKS = (11, 12, 9, 10, 0, 3, 6)
N_NAT_STEPS = len(NAT_CHUNKS)
N_DIL_STEPS = 3
N_GATE_STEPS = N_BRANCH * D_MODEL // PROJ_TG
FIRST_D4 = N_NAT_STEPS
FIRST_D16 = FIRST_D4 + N_DIL_STEPS
FIRST_GATE = FIRST_D16 + N_DIL_STEPS
MIX_STEPS = FIRST_GATE + N_GATE_STEPS


def _lin_chunk(s):
    t = jnp.minimum(s, FIRST_GATE - 1)
    nat = jnp.where(t < 2, 11 + t, jnp.where(t < 4, 7 + t, 3 * (t - 4)))
    d4 = 3 * (t - FIRST_D4) + 1
    d16 = 3 * (t - FIRST_D16) + 2
    return jnp.where(t < FIRST_D4, nat, jnp.where(t < FIRST_D16, d4, d16))


def _mixproj_kernel(x_ref, g_ref, wg_ref, b_ref, wi_ref, og_ref, on_ref, o4_ref, o16_ref, hs_ref, stage_ref):
    s = pl.program_id(1)
    tm = x_ref.shape[0]
    row_chunks = [pl.ds(r, PROJ_ROWS) for r in range(0, tm, PROJ_ROWS)]

    @pl.when(s == 0)
    def _():
        hs_ref[...] = _rms(x_ref[...], g_ref[...]).astype(BF16)

    @pl.when(s < FIRST_D4)
    def _():
        for rows in row_chunks:
            on_ref[rows, :] = jnp.dot(hs_ref[rows, :], wi_ref[...], preferred_element_type=F32).astype(on_ref.dtype)

    def dilated(o_ref, d):
        per_res = PROJ_ROWS // d
        for rc, rows in enumerate(row_chunks):
            acc = jnp.dot(hs_ref[rows, :], wi_ref[...], preferred_element_type=F32)
            for c in range(PROJ_TN // HEAD_DIM):
                stage_ref[c] = acc[:, c * HEAD_DIM:(c + 1) * HEAD_DIM]
            for res in range(d):
                for c in range(PROJ_TN // HEAD_DIM):
                    o_ref[res, pl.ds(rc * per_res, per_res), c * HEAD_DIM:(c + 1) * HEAD_DIM] = (
                        stage_ref[c, pl.ds(res, per_res, stride=d), :].astype(o_ref.dtype))

    pl.when((s >= FIRST_D4) & (s < FIRST_D16))(functools.partial(dilated, o4_ref, DIL_PAIRS[1][1]))
    pl.when((s >= FIRST_D16) & (s < FIRST_GATE))(functools.partial(dilated, o16_ref, DIL_PAIRS[2][1]))

    @pl.when(s >= FIRST_GATE)
    def _():
        for rows in row_chunks:
            acc = jnp.dot(hs_ref[rows, :], wg_ref[...], preferred_element_type=F32)
            og_ref[rows, :] = jax.nn.sigmoid(acc + b_ref[...]).astype(og_ref.dtype)


def _mixproj(x, gain, w_gate, b_gate, w_in, *, batch, seq):
    n, d = x.shape
    tm, tn, tg = PROJ_TM, PROJ_TN, PROJ_TG
    tiles_per_seq = seq // tm
    d4, d16 = DIL_PAIRS[1][1], DIL_PAIRS[2][1]

    def gate_idx(s):
        return jnp.maximum(s - FIRST_GATE, 0)

    def dil_spec(dil, first):
        return pl.BlockSpec(
            (None, dil, tm // dil, tn),
            lambda i, s: (i // tiles_per_seq, 0, i % tiles_per_seq, jnp.clip(s - first, 0, N_DIL_STEPS - 1)))

    est = (2 * tm * d * 4 + tm * d * 2 + 2 * d * (tg + tn) * 2 + 2 * tm * (tg + 3 * tn) * 2
           + PROJ_ROWS * tn * 4 + 2 * PROJ_ROWS * tg * 4)
    return pl.pallas_call(
        _mixproj_kernel,
        grid=(n // tm, MIX_STEPS),
        in_specs=[
            pl.BlockSpec((tm, d), lambda i, s: (i, 0)),
            pl.BlockSpec((1, d), lambda i, s: (0, 0)),
            pl.BlockSpec((d, tg), lambda i, s: (0, gate_idx(s))),
            pl.BlockSpec((1, tg), lambda i, s: (0, gate_idx(s))),
            pl.BlockSpec((d, tn), lambda i, s: (0, _lin_chunk(s))),
        ],
        out_specs=[
            pl.BlockSpec((tm, tg), lambda i, s: (i, gate_idx(s))),
            pl.BlockSpec((tm, tn), lambda i, s: (i, jnp.minimum(s, N_NAT_STEPS - 1))),
            dil_spec(d4, FIRST_D4),
            dil_spec(d16, FIRST_D16),
        ],
        out_shape=[
            jax.ShapeDtypeStruct((n, N_BRANCH * d), BF16),
            jax.ShapeDtypeStruct((n, N_NAT_STEPS * tn), BF16),
            jax.ShapeDtypeStruct((batch, d4, seq // d4, N_DIL_STEPS * tn), BF16),
            jax.ShapeDtypeStruct((batch, d16, seq // d16, N_DIL_STEPS * tn), BF16),
        ],
        scratch_shapes=[pltpu.VMEM((tm, d), BF16), pltpu.VMEM((tn // HEAD_DIM, PROJ_ROWS, HEAD_DIM), F32)],
        compiler_params=pltpu.CompilerParams(
            dimension_semantics=("arbitrary", "arbitrary"), vmem_limit_bytes=_vmem_limit(est)),
        name="mixproj",
    )(x, gain, w_gate, b_gate, w_in)


def _attn_kernel(slopes_ref, q0_ref, k0_ref, v0_ref, q1_ref, k1_ref, v1_ref, q2_ref, k2_ref, v2_ref, o_ref,
                 m0_ref, m1_ref, m2_ref, l0_ref, l1_ref, l2_ref, n0_ref, n1_ref, n2_ref, va_ref, bias_ref, *, seq):
    h = pl.program_id(1)
    scale = HEAD_DIM ** -0.5
    qb = ATT_QB
    qkv_refs = ((q0_ref, k0_ref, v0_ref), (q1_ref, k1_ref, v1_ref), (q2_ref, k2_ref, v2_ref))
    m_refs = (m0_ref, m1_ref, m2_ref)
    l_refs = (l0_ref, l1_ref, l2_ref)
    n_refs = (n0_ref, n1_ref, n2_ref)

    va_ref[:, HEAD_DIM:] = jnp.ones((seq, HEAD_DIM), BF16)

    for gi, (_, d) in enumerate(DIL_PAIRS):
        q_ref, k_ref, v_ref = qkv_refs[gi]
        sub_len = seq // d
        kw = min(ATT_KW, sub_len)
        nb = sub_len // qb
        slope_d = slopes_ref[gi, h] * float(d)

        row = lax.broadcasted_iota(jnp.int32, (qb, kw), 0)
        col = lax.broadcasted_iota(jnp.int32, (qb, kw), 1)
        for kind, off in enumerate((0, -BAND_RADIUS, qb - kw)):
            arel = jnp.abs(col - row + off)
            bias_ref[kind, :, :kw] = jnp.where(arel <= BAND_RADIUS, -slope_d * arel.astype(F32), NEG)

        if d == 1:
            va_ref[:, :HEAD_DIM] = v_ref[...]
        else:
            for res in range(d):
                va_ref[pl.ds(res * sub_len, sub_len), :HEAD_DIM] = v_ref[res]

        def block(idx, gi=gi, d=d, q_ref=q_ref, k_ref=k_ref, sub_len=sub_len, kw=kw, nb=nb):
            res = idx // nb
            i = idx - res * nb
            q0 = pl.multiple_of(i * qb, qb)
            k0 = pl.multiple_of(jnp.clip(q0 - BAND_RADIUS, 0, sub_len - kw), BAND_RADIUS)
            if d == 1:
                qblk = q_ref[pl.ds(q0, qb), :]
                kwin = k_ref[pl.ds(k0, kw), :]
            else:
                qblk = q_ref[res, pl.ds(q0, qb), :]
                kwin = k_ref[res, pl.ds(k0, kw), :]
            s = lax.dot_general(qblk, kwin, (((1,), (1,)), ((), ())), preferred_element_type=F32)
            if nb == 1:
                bias = bias_ref[0, :, :kw]
            else:
                kind = jnp.where(i == 0, 0, jnp.where(i == nb - 1, 2, 1))
                bias = bias_ref[kind]
            s = s * scale + bias
            m = jnp.max(s, axis=-1, keepdims=True)
            p = jnp.exp(s - m).astype(BF16)
            kstart = pl.multiple_of(res * sub_len + k0, BAND_RADIUS)
            na = jnp.dot(p, va_ref[pl.ds(kstart, kw), :], preferred_element_type=F32)
            if d == 1:
                rows = pl.ds(q0, qb)
            else:
                rows = pl.ds(q0 * d + res, qb, stride=d)
            m_refs[gi][rows, :] = jnp.broadcast_to(m, (qb, HEAD_DIM))
            l_refs[gi][rows, :] = na[:, HEAD_DIM:]
            n_refs[gi][rows, :] = na[:, :HEAD_DIM]

        def blocks(it, carry, block=block):
            for u in range(ATT_UNROLL):
                block(it * ATT_UNROLL + u)
            return carry

        lax.fori_loop(0, d * nb // ATT_UNROLL, blocks, 0)

    def fin(c, carry):
        r = pl.ds(pl.multiple_of(c * ATT_FIN_ROWS, ATT_FIN_ROWS), ATT_FIN_ROWS)
        ms = [m_ref[r, :] for m_ref in m_refs]
        mx = jnp.maximum(jnp.maximum(ms[0], ms[1]), ms[2])
        num = jnp.zeros((ATT_FIN_ROWS, HEAD_DIM), F32)
        den = jnp.zeros((ATT_FIN_ROWS, HEAD_DIM), F32)
        for gi in range(N_GROUPS):
            e = jnp.exp(ms[gi] - mx)
            num += e * n_refs[gi][r, :]
            den += e * l_refs[gi][r, :]
        o_ref[r, :] = (num / den).astype(o_ref.dtype)
        return carry

    lax.fori_loop(0, seq // ATT_FIN_ROWS, fin, 0)


def _attention(nat, qkv4, qkv16, slopes, *, batch, seq, col0):
    blk0 = col0 // HEAD_DIM
    d4, d16 = DIL_PAIRS[1][1], DIL_PAIRS[2][1]

    def nat_map(b, h, *, which):
        return (b, blk0 + which * HEADS_PER_GROUP + h)

    def dil_map(b, h, *, which):
        return (b, 0, 0, which * HEADS_PER_GROUP + h)

    in_specs = [pl.BlockSpec(memory_space=pltpu.SMEM)]
    in_specs += [pl.BlockSpec((seq, HEAD_DIM), functools.partial(nat_map, which=w)) for w in range(3)]
    for dil in (d4, d16):
        in_specs += [pl.BlockSpec((None, dil, seq // dil, HEAD_DIM), functools.partial(dil_map, which=w))
                     for w in range(3)]
    tile = seq * HEAD_DIM
    est = 10 * 2 * tile * 2 + 9 * tile * 4 + 2 * tile * 2 + 3 * ATT_QB * ATT_KW * 4
    state = [pltpu.VMEM((seq, HEAD_DIM), F32) for _ in range(3 * N_GROUPS)]
    return pl.pallas_call(
        functools.partial(_attn_kernel, seq=seq),
        grid=(batch, HEADS_PER_GROUP),
        in_specs=in_specs,
        out_specs=pl.BlockSpec((seq, HEAD_DIM), lambda b, h: (b, h)),
        out_shape=jax.ShapeDtypeStruct((batch * seq, HEADS_PER_GROUP * HEAD_DIM), BF16),
        scratch_shapes=state + [
            pltpu.VMEM((seq, 2 * HEAD_DIM), BF16),
            pltpu.VMEM((3, ATT_QB, ATT_KW), F32),
        ],
        compiler_params=pltpu.CompilerParams(
            dimension_semantics=("arbitrary", "arbitrary"), vmem_limit_bytes=_vmem_limit(est)),
        name="attention",
    )(slopes, nat, nat, nat, qkv4, qkv4, qkv4, qkv16, qkv16, qkv16)


def _pool_kernel(u_ref, w_ref, sc_ref, o_ref, pad_ref, pooled_ref, *, seq):
    grp = pl.program_id(1)
    halo, rows = POOL_HALO, POOL_ROWS
    ext = rows + 2 * halo
    zeros = jnp.zeros((halo, POOL_GROUP), F32)
    pad_ref[pl.ds(0, halo), :] = zeros
    pad_ref[pl.ds(halo + seq, halo), :] = zeros
    pad_ref[pl.ds(halo, seq), :] = u_ref[...].astype(F32)

    def run(window):
        def chunk(c, carry):
            base = pl.multiple_of(c * rows, rows)
            p = pad_ref[pl.ds(base, ext), :]
            s = p + pltpu.roll(p, 1, 0)
            half = 1
            while 2 * half < window:
                s = pltpu.roll(s, ext - half, 0) + pltpu.roll(s, half, 0)
                half *= 2
            s = s[halo:halo + rows, :]
            tok = p[halo:halo + rows, :]
            t = base + lax.broadcasted_iota(jnp.int32, (rows, 1), 0)
            lo = jnp.maximum(t - window // 2, 0)
            hi = jnp.minimum(t - window // 2 + window, seq)
            pooled_ref[pl.ds(base, rows), :] = (s / (hi - lo).astype(F32) - tok).astype(BF16)
            return carry

        lax.fori_loop(0, seq // rows, chunk, 0)

    for gi, window in enumerate(POOL_WINDOWS):
        pl.when(grp == gi)(functools.partial(run, window))

    y = jnp.dot(pooled_ref[...], w_ref[...], preferred_element_type=F32) * sc_ref[...]
    o_ref[...] = y.astype(o_ref.dtype)


def _pool(pg, w_pool, pool_scale, *, batch, seq, col0):
    blk0 = col0 // POOL_GROUP
    ngrp = len(POOL_WINDOWS)
    est = (2 * seq * POOL_GROUP * 2 * 2 + (seq + 2 * POOL_HALO) * POOL_GROUP * 4 + 2 * POOL_GROUP * POOL_GROUP * 2
           + seq * POOL_GROUP * (2 + 4))
    return pl.pallas_call(
        functools.partial(_pool_kernel, seq=seq),
        grid=(batch, ngrp),
        in_specs=[
            pl.BlockSpec((seq, POOL_GROUP), lambda b, g: (b, blk0 + g)),
            pl.BlockSpec((None, POOL_GROUP, POOL_GROUP), lambda b, g: (g, 0, 0)),
            pl.BlockSpec((1, POOL_GROUP), lambda b, g: (0, g)),
        ],
        out_specs=pl.BlockSpec((seq, POOL_GROUP), lambda b, g: (b, g)),
        out_shape=jax.ShapeDtypeStruct((batch * seq, D_POOL), BF16),
        scratch_shapes=[pltpu.VMEM((seq + 2 * POOL_HALO, POOL_GROUP), F32), pltpu.VMEM((seq, POOL_GROUP), BF16)],
        compiler_params=pltpu.CompilerParams(
            dimension_semantics=("arbitrary", "arbitrary"), vmem_limit_bytes=_vmem_limit(est)),
        name="pool",
    )(pg, w_pool, pool_scale)


def _memattn_kernel(q_ref, kv_ref, o_ref):
    scale = XHEAD_DIM ** -0.5
    for hh in range(N_XHEADS):
        lo = hh * XHEAD_DIM
        q = q_ref[:, lo:lo + XHEAD_DIM]
        k = kv_ref[:, lo:lo + XHEAD_DIM]
        v = kv_ref[:, D_XATTN + lo:D_XATTN + lo + XHEAD_DIM]
        s = lax.dot_general(q, k, (((1,), (1,)), ((), ())), preferred_element_type=F32) * scale
        m = jnp.max(s, axis=-1, keepdims=True)
        p = jnp.exp(s - m)
        l = jnp.sum(p, axis=-1, keepdims=True)
        y = jnp.dot(p.astype(BF16), v, preferred_element_type=F32) / l
        o_ref[:, lo:lo + XHEAD_DIM] = y.astype(o_ref.dtype)


def _memattn(pg, kv, *, seq, col0):
    n = pg.shape[0]
    tm = MEM_TM
    blk0 = col0 // D_XATTN
    tiles_per_seq = seq // tm
    est = 2 * tm * D_XATTN * 2 * 2 + 2 * N_MEM * 2 * D_XATTN * 2 + 4 * tm * N_MEM * 4
    return pl.pallas_call(
        _memattn_kernel,
        grid=(n // tm,),
        in_specs=[
            pl.BlockSpec((tm, D_XATTN), lambda i: (i, blk0)),
            pl.BlockSpec((N_MEM, 2 * D_XATTN), lambda i: (i // tiles_per_seq, 0)),
        ],
        out_specs=pl.BlockSpec((tm, D_XATTN), lambda i: (i, 0)),
        out_shape=jax.ShapeDtypeStruct((n, D_XATTN), BF16),
        compiler_params=pltpu.CompilerParams(
            dimension_semantics=("arbitrary",), vmem_limit_bytes=_vmem_limit(est)),
        name="memattn",
    )(pg, kv)


def _merge_kernel(gates_ref, ya_ref, yp_ref, ym_ref, x_ref, wa_ref, wp_ref, wm_ref, wo_ref, gain_ref, o_ref):
    d = x_ref.shape[1]
    for r in range(0, x_ref.shape[0], MERGE_ROWS):
        rows = pl.ds(r, MERGE_ROWS)
        merged = gates_ref[rows, 0:d].astype(F32) * jnp.dot(ya_ref[rows, :], wa_ref[...], preferred_element_type=F32)
        merged += gates_ref[rows, d:2 * d].astype(F32) * jnp.dot(yp_ref[rows, :], wp_ref[...],
                                                                   preferred_element_type=F32)
        merged += gates_ref[rows, 2 * d:3 * d].astype(F32) * jnp.dot(ym_ref[rows, :], wm_ref[...],
                                                                       preferred_element_type=F32)
        z = jnp.dot(merged.astype(BF16), wo_ref[...], preferred_element_type=F32)
        o_ref[rows, :] = x_ref[rows, :] + _rms(z, gain_ref[...])


def _merge(gates, y_attn, y_pool, y_mem, x, wa, wp, wm, wo, gain):
    n, d = x.shape
    tm = MERGE_TM
    wbytes = (wa.size + wp.size + wm.size + wo.size) * 2

    def const(shape):
        return pl.BlockSpec(shape, lambda i: (0, 0), pipeline_mode=pl.Buffered(1))

    est = (2 * tm * (N_BRANCH * d + y_attn.shape[1] + y_pool.shape[1] + y_mem.shape[1]) * 2
           + 4 * tm * d * 4 + wbytes + 4 * tm * d * 4)
    return pl.pallas_call(
        _merge_kernel,
        grid=(n // tm,),
        in_specs=[
            pl.BlockSpec((tm, N_BRANCH * d), lambda i: (i, 0)),
            pl.BlockSpec((tm, y_attn.shape[1]), lambda i: (i, 0)),
            pl.BlockSpec((tm, y_pool.shape[1]), lambda i: (i, 0)),
            pl.BlockSpec((tm, y_mem.shape[1]), lambda i: (i, 0)),
            pl.BlockSpec((tm, d), lambda i: (i, 0)),
            const(wa.shape), const(wp.shape), const(wm.shape), const(wo.shape),
            pl.BlockSpec((1, d), lambda i: (0, 0)),
        ],
        out_specs=pl.BlockSpec((tm, d), lambda i: (i, 0)),
        out_shape=jax.ShapeDtypeStruct((n, d), F32),
        compiler_params=pltpu.CompilerParams(
            dimension_semantics=("arbitrary",), vmem_limit_bytes=_vmem_limit(est)),
        name="merge",
    )(gates, y_attn, y_pool, y_mem, x, wa, wp, wm, wo, gain)


def _cast_kernel(w_ref, *o_refs):
    lo = 0
    for o_ref in o_refs:
        width = o_ref.shape[1]
        o_ref[...] = w_ref[:, lo:lo + width].astype(o_ref.dtype)
        lo += width


def _cast_bf16(w, widths=None, target_bytes=CAST_BLOCK_BYTES):
    rows, cols = w.shape
    widths = (cols,) if widths is None else tuple(widths)
    br = rows
    while br * cols * 4 > target_bytes and br % 32 == 0:
        br //= 2
    est = 2 * br * cols * 4 + 2 * br * cols * 2
    outs = pl.pallas_call(
        _cast_kernel,
        grid=(rows // br,),
        in_specs=[pl.BlockSpec((br, cols), lambda i: (i, 0))],
        out_specs=[pl.BlockSpec((br, wd), lambda i: (i, 0)) for wd in widths],
        out_shape=[jax.ShapeDtypeStruct((rows, wd), BF16) for wd in widths],
        compiler_params=pltpu.CompilerParams(
            dimension_semantics=("arbitrary",), vmem_limit_bytes=_vmem_limit(est)),
        name="cast",
    )(w)
    return outs if len(widths) > 1 else outs[0]


def _cast_chunked_kernel(w_ref, *o_refs, width):
    for k, o_ref in enumerate(o_refs):
        nchunk, _, tf = o_ref.shape
        for c in range(nchunk):
            valid = min(tf, width - c * tf)
            lo = k * width + c * tf
            o_ref[c, :, :valid] = w_ref[:, lo:lo + valid].astype(o_ref.dtype)
            if valid < tf:
                o_ref[c, :, valid:] = jnp.zeros((o_ref.shape[1], tf - valid), o_ref.dtype)


def _cast_chunked(w, n_out, tf, target_bytes=CAST_BLOCK_BYTES):
    rows, cols = w.shape
    width = cols // n_out
    nchunk = pl.cdiv(width, tf)
    br = rows
    while br * cols * 4 > target_bytes and br % 32 == 0:
        br //= 2
    est = 2 * br * cols * 4 + 2 * n_out * nchunk * br * tf * 2
    return pl.pallas_call(
        functools.partial(_cast_chunked_kernel, width=width),
        grid=(rows // br,),
        in_specs=[pl.BlockSpec((br, cols), lambda i: (i, 0))],
        out_specs=[pl.BlockSpec((nchunk, br, tf), lambda i: (0, i, 0)) for _ in range(n_out)],
        out_shape=[jax.ShapeDtypeStruct((nchunk, rows, tf), BF16) for _ in range(n_out)],
        compiler_params=pltpu.CompilerParams(
            dimension_semantics=("arbitrary",), vmem_limit_bytes=_vmem_limit(est)),
        name="cast_chunked",
    )(w)


def _row(v):
    return v.reshape(1, -1).astype(F32)


def _trunk(x, mem, p, slopes):
    batch, seq, d = x.shape
    xf = x.reshape(batch * seq, d)
    x1 = _ffn(xf, p["ffn1_norm_pre"], p["ffn1_wg"], p["ffn1_wu"], p["ffn1_wd"], p["ffn1_norm_post"],
              p["ffn1_norm_post"], final=False)
    gates, nat, qkv4, qkv16 = _mixproj(x1, p["mix_norm_pre"], p["w_gate"], p["b_gate"], p["w_in"],
                                       batch=batch, seq=seq)
    col_pool = D_XATTN
    col_qkv0 = col_pool + D_POOL
    kv = _kvproj(mem.reshape(batch * N_MEM, d), p["mem_norm"], p["w_mem_kv"])
    y_attn = _attention(nat, qkv4, qkv16, slopes, batch=batch, seq=seq, col0=col_qkv0)
    y_pool = _pool(nat, p["w_pool"], p["pool_scale"], batch=batch, seq=seq, col0=col_pool)
    y_mem = _memattn(nat, kv, seq=seq, col0=0)
    x2 = _merge(gates, y_attn, y_pool, y_mem, x1, p["w_br_attn"], p["w_br_pool"], p["w_br_mem"], p["w_out"],
                p["mix_norm_post"])
    y = _ffn(x2, p["ffn2_norm_pre"], p["ffn2_wg"], p["ffn2_wu"], p["ffn2_wd"], p["ffn2_norm_post"],
             p["final_norm"], final=True)
    return y.reshape(batch, seq, d)


def _prep_params(ffn1_norm_pre, ffn1_w_up, ffn1_w_down, ffn1_norm_post, mix_norm_pre, mem_norm, w_in, w_mem_kv,
                 w_pool, pool_scale, w_br_attn, w_br_pool, w_br_mem, w_gate, b_gate, w_out, mix_norm_post,
                 ffn2_norm_pre, ffn2_w_up, ffn2_w_down, ffn2_norm_post, final_norm):
    p = {}
    for name, w_up, w_down in (("ffn1", ffn1_w_up, ffn1_w_down), ("ffn2", ffn2_w_up, ffn2_w_down)):
        dff = w_down.shape[0]
        p[name + "_wg"], p[name + "_wu"] = _cast_chunked(w_up, 2, FFN_TF)
        p[name + "_wd"] = _cast_bf16(w_down)
    p["w_gate"] = _cast_bf16(w_gate)
    p["w_in"] = _cast_bf16(w_in)
    p["b_gate"] = _row(b_gate)
    p["w_mem_kv"] = _cast_bf16(w_mem_kv)
    p["w_pool"] = w_pool.astype(BF16)
    p["pool_scale"] = _row(pool_scale)
    for name, w in (("w_br_attn", w_br_attn), ("w_br_pool", w_br_pool), ("w_br_mem", w_br_mem), ("w_out", w_out)):
        p[name] = _cast_bf16(w)
    for name, v in (("ffn1_norm_pre", ffn1_norm_pre), ("ffn1_norm_post", ffn1_norm_post),
                    ("mix_norm_pre", mix_norm_pre), ("mem_norm", mem_norm), ("mix_norm_post", mix_norm_post),
                    ("ffn2_norm_pre", ffn2_norm_pre), ("ffn2_norm_post", ffn2_norm_post),
                    ("final_norm", final_norm)):
        p[name] = _row(v)
    return p


def kernel(x_prompt, x_sample, mem_prompt, mem_sample, ffn1_norm_pre, ffn1_w_up, ffn1_w_down, ffn1_norm_post,
           mix_norm_pre, mem_norm, w_in, w_mem_kv, w_pool, pool_scale, w_br_attn, w_br_pool, w_br_mem, w_gate, b_gate,
           w_out, mix_norm_post, ffn2_norm_pre, ffn2_w_up, ffn2_w_down, ffn2_norm_post, final_norm):
    layer = [ffn1_norm_pre, ffn1_w_up, ffn1_w_down, ffn1_norm_post, mix_norm_pre, mem_norm, w_in, w_mem_kv, w_pool,
             pool_scale, w_br_attn, w_br_pool, w_br_mem, w_gate, b_gate, w_out, mix_norm_post, ffn2_norm_pre,
             ffn2_w_up, ffn2_w_down, ffn2_norm_post, final_norm]
    depth = ffn1_norm_pre.shape[0]
    slopes = jnp.asarray(_alibi_slopes())
    y_prompt, y_sample = x_prompt, x_sample
    for layer_idx in range(depth):
        p = _prep_params(*[w[layer_idx] for w in layer])
        y_prompt = _trunk(y_prompt, mem_prompt, p, slopes)
        y_sample = _trunk(y_sample, mem_sample, p, slopes)
    return (y_prompt, y_sample)
```

```python
import functools

import numpy as np
import jax
import jax.numpy as jnp
from jax import lax
from jax.experimental import pallas as pl
from jax.experimental.pallas import tpu as pltpu

F32 = jnp.float32
BF16 = jnp.bfloat16

D_MODEL = 2048
N_MEM = 256
HEAD_DIM = 128
DIL_PAIRS = ((128, 1), (512, 4), (2048, 16))
HEADS_PER_GROUP = 4
N_GROUPS = len(DIL_PAIRS)
N_DIL_HEADS = HEADS_PER_GROUP * N_GROUPS
D_ATTN = N_DIL_HEADS * HEAD_DIM
POOL_WINDOWS = (2, 4, 8, 16)
POOL_GROUP = 256
D_POOL = POOL_GROUP * len(POOL_WINDOWS)
N_XHEADS = 4
XHEAD_DIM = 256
D_XATTN = N_XHEADS * XHEAD_DIM
N_BRANCH = 3
EPS = 1e-6
NEG = -1e30
BAND_RADIUS = 64

V7X_VMEM_BYTES = 64 * 1024 * 1024
MIB = 1024 * 1024

FFN_TM = 1024
FFN_TF = 1024
PROJ_TM = 1024
PROJ_TN = 512
PROJ_TG = 1536
PROJ_ROWS = 256
ATT_QB = 128
ATT_KW = 256
ATT_UNROLL = 8
ATT_FIN_ROWS = 256
POOL_ROWS = 256
POOL_HALO = 16
MEM_TM = 512
MERGE_TM = 512
MERGE_ROWS = 512
CAST_BLOCK_BYTES = 8 * MIB


def _vmem_limit(nbytes):
    return int(min(nbytes * 5 // 4 + 8 * MIB, V7X_VMEM_BYTES - 2 * MIB))


def _rms(x, gain):
    ms = jnp.mean(x * x, axis=-1, keepdims=True)
    return x * lax.rsqrt(ms + EPS) * gain


def _alibi_slopes():
    s = 2.0 ** (-8.0 * np.arange(1, N_DIL_HEADS + 1) / N_DIL_HEADS)
    return s.reshape(HEADS_PER_GROUP, N_GROUPS).T.astype(np.float32)


def _ffn_kernel(x_hbm, gpre_ref, wg_ref, wu_ref, wd_ref, gpost_ref, gfin_ref, o_hbm, xbuf_ref, acc_ref, hs_ref,
                sem_x, sem, *, tf, rem, final):
    i = pl.program_id(0)
    f = pl.program_id(1)
    n_tiles = pl.num_programs(0)
    last = pl.num_programs(1) - 1
    tm = acc_ref.shape[0]
    slot = lax.rem(i, 2)

    def tile_rows(tile):
        return pl.ds(pl.multiple_of(tile * tm, tm), tm)

    def fetch(tile, s):
        return pltpu.make_async_copy(x_hbm.at[tile_rows(tile), :], xbuf_ref.at[s], sem_x)

    def writeback(tile):
        return pltpu.make_async_copy(acc_ref, o_hbm.at[tile_rows(tile), :], sem)

    @pl.when(f == 0)
    def _():
        @pl.when(i == 0)
        def _():
            fetch(0, 0).start()

        fetch(i, slot).wait()
        hs_ref[...] = _rms(xbuf_ref[slot], gpre_ref[...]).astype(BF16)

    @pl.when((f == 1) & (i + 1 < n_tiles))
    def _():
        fetch(i + 1, 1 - slot).start()

    def step(valid, first):
        hs = hs_ref[...]
        g = jnp.dot(hs, wg_ref[:, :valid], preferred_element_type=F32)
        u = jnp.dot(hs, wu_ref[:, :valid], preferred_element_type=F32)
        a = (g * jax.nn.sigmoid(g) * u).astype(BF16)
        if first:
            @pl.when(i > 0)
            def _():
                writeback(i - 1).wait()

            acc_ref[...] = jnp.dot(a, wd_ref[:valid, :], preferred_element_type=F32)
        else:
            acc_ref[...] += jnp.dot(a, wd_ref[:valid, :], preferred_element_type=F32)

    pl.when(f == 0)(functools.partial(step, tf, True))
    pl.when((f > 0) & (f < last))(functools.partial(step, tf, False))
    pl.when(f == last)(functools.partial(step, rem, False))

    @pl.when(f == last)
    def _():
        y = xbuf_ref[slot] + 0.5 * _rms(acc_ref[...], gpost_ref[...])
        if final:
            y = _rms(y, gfin_ref[...])
        acc_ref[...] = y
        writeback(i).start()

        @pl.when(i == n_tiles - 1)
        def _():
            writeback(i).wait()


def _ffn(x, gpre, wg, wu, wd, gpost, gfin, *, final):
    n, d = x.shape
    tm, tf = FFN_TM, FFN_TF
    dff = wd.shape[0]
    nf = pl.cdiv(dff, tf)
    rem = dff - (nf - 1) * tf
    assert nf >= 2
    est = 2 * tm * d * 4 + tm * d * 4 + tm * d * 2 + 2 * 3 * d * tf * 2 + 2 * tm * tf * 4
    vec = pl.BlockSpec((1, d), lambda i, f: (0, 0))
    return pl.pallas_call(
        functools.partial(_ffn_kernel, tf=tf, rem=rem, final=final),
        grid=(n // tm, nf),
        in_specs=[
            pl.BlockSpec(memory_space=pl.ANY),
            vec,
            pl.BlockSpec((None, d, tf), lambda i, f: (f, 0, 0)),
            pl.BlockSpec((None, d, tf), lambda i, f: (f, 0, 0)),
            pl.BlockSpec((tf, d), lambda i, f: (f, 0)),
            vec,
            vec,
        ],
        out_specs=pl.BlockSpec(memory_space=pl.ANY),
        out_shape=jax.ShapeDtypeStruct((n, d), F32),
        scratch_shapes=[pltpu.VMEM((2, tm, d), F32), pltpu.VMEM((tm, d), F32), pltpu.VMEM((tm, d), BF16),
                        pltpu.SemaphoreType.DMA(()), pltpu.SemaphoreType.DMA(())],
        compiler_params=pltpu.CompilerParams(
            dimension_semantics=("arbitrary", "arbitrary"), vmem_limit_bytes=_vmem_limit(est)),
        name="ffn_final" if final else "ffn",
    )(x, gpre, wg, wu, wd, gpost, gfin)


def _kvproj_kernel(x_ref, g_ref, w_ref, o_ref, hs_ref):
    @pl.when(pl.program_id(1) == 0)
    def _():
        hs_ref[...] = _rms(x_ref[...], g_ref[...]).astype(BF16)

    o_ref[...] = jnp.dot(hs_ref[...], w_ref[...], preferred_element_type=F32).astype(o_ref.dtype)


def _kvproj(x, gain, w):
    n, d = x.shape
    ncol = w.shape[1]
    tm, tn = n, PROJ_TN
    est = 2 * tm * d * 4 + tm * d * 2 + 2 * d * tn * 2 + 2 * tm * tn * 2 + 2 * tm * tn * 4
    return pl.pallas_call(
        _kvproj_kernel,
        grid=(n // tm, ncol // tn),
        in_specs=[
            pl.BlockSpec((tm, d), lambda i, j: (i, 0)),
            pl.BlockSpec((1, d), lambda i, j: (0, 0)),
            pl.BlockSpec((d, tn), lambda i, j: (0, j)),
        ],
        out_specs=pl.BlockSpec((tm, tn), lambda i, j: (i, j)),
        out_shape=jax.ShapeDtypeStruct((n, ncol), BF16),
        scratch_shapes=[pltpu.VMEM((tm, d), BF16)],
        compiler_params=pltpu.CompilerParams(
            dimension_semantics=("arbitrary", "arbitrary"), vmem_limit_bytes=_vmem_limit(est)),
        name="kvproj",
    )(x, gain, w)


NAT_CHUNKS = (11, 12, 9, 10, 0, 3, 6)
N_NAT_STEPS = len(NAT_CHUNKS)
N_DIL_STEPS = 3
N_GATE_STEPS = N_BRANCH * D_MODEL // PROJ_TG
FIRST_D4 = N_NAT_STEPS
FIRST_D16 = FIRST_D4 + N_DIL_STEPS
FIRST_GATE = FIRST_D16 + N_DIL_STEPS
MIX_STEPS = FIRST_GATE + N_GATE_STEPS


def _lin_chunk(s):
    t = jnp.minimum(s, FIRST_GATE - 1)
    nat = jnp.where(t < 2, 11 + t, jnp.where(t < 4, 7 + t, 3 * (t - 4)))
    d4 = 3 * (t - FIRST_D4) + 1
    d16 = 3 * (t - FIRST_D16) + 2
    return jnp.where(t < FIRST_D4, nat, jnp.where(t < FIRST_D16, d4, d16))


def _mixproj_kernel(x_ref, g_ref, wg_ref, b_ref, wi_ref, og_ref, on_ref, o4_ref, o16_ref, hs_ref, stage_ref):
    s = pl.program_id(1)
    tm = x_ref.shape[0]
    row_chunks = [pl.ds(r, PROJ_ROWS) for r in range(0, tm, PROJ_ROWS)]

    @pl.when(s == 0)
    def _():
        hs_ref[...] = _rms(x_ref[...], g_ref[...]).astype(BF16)

    @pl.when(s < FIRST_D4)
    def _():
        for rows in row_chunks:
            on_ref[rows, :] = jnp.dot(hs_ref[rows, :], wi_ref[...], preferred_element_type=F32).astype(on_ref.dtype)

    def dilated(o_ref, d):
        per_res = PROJ_ROWS // d
        for rc, rows in enumerate(row_chunks):
            acc = jnp.dot(hs_ref[rows, :], wi_ref[...], preferred_element_type=F32)
            for c in range(PROJ_TN // HEAD_DIM):
                stage_ref[c] = acc[:, c * HEAD_DIM:(c + 1) * HEAD_DIM]
            for res in range(d):
                for c in range(PROJ_TN // HEAD_DIM):
                    o_ref[res, pl.ds(rc * per_res, per_res), c * HEAD_DIM:(c + 1) * HEAD_DIM] = (
                        stage_ref[c, pl.ds(res, per_res, stride=d), :].astype(o_ref.dtype))

    pl.when((s >= FIRST_D4) & (s < FIRST_D16))(functools.partial(dilated, o4_ref, DIL_PAIRS[1][1]))
    pl.when((s >= FIRST_D16) & (s < FIRST_GATE))(functools.partial(dilated, o16_ref, DIL_PAIRS[2][1]))

    @pl.when(s >= FIRST_GATE)
    def _():
        for rows in row_chunks:
            acc = jnp.dot(hs_ref[rows, :], wg_ref[...], preferred_element_type=F32)
            og_ref[rows, :] = jax.nn.sigmoid(acc + b_ref[...]).astype(og_ref.dtype)


def _mixproj(x, gain, w_gate, b_gate, w_in, *, batch, seq):
    n, d = x.shape
    tm, tn, tg = PROJ_TM, PROJ_TN, PROJ_TG
    tiles_per_seq = seq // tm
    d4, d16 = DIL_PAIRS[1][1], DIL_PAIRS[2][1]

    def gate_idx(s):
        return jnp.maximum(s - FIRST_GATE, 0)

    def dil_spec(dil, first):
        return pl.BlockSpec(
            (None, dil, tm // dil, tn),
            lambda i, s: (i // tiles_per_seq, 0, i % tiles_per_seq, jnp.clip(s - first, 0, N_DIL_STEPS - 1)))

    est = (2 * tm * d * 4 + tm * d * 2 + 2 * d * (tg + tn) * 2 + 2 * tm * (tg + 3 * tn) * 2
           + PROJ_ROWS * tn * 4 + 2 * PROJ_ROWS * tg * 4)
    return pl.pallas_call(
        _mixproj_kernel,
        grid=(n // tm, MIX_STEPS),
        in_specs=[
            pl.BlockSpec((tm, d), lambda i, s: (i, 0)),
            pl.BlockSpec((1, d), lambda i, s: (0, 0)),
            pl.BlockSpec((d, tg), lambda i, s: (0, gate_idx(s))),
            pl.BlockSpec((1, tg), lambda i, s: (0, gate_idx(s))),
            pl.BlockSpec((d, tn), lambda i, s: (0, _lin_chunk(s))),
        ],
        out_specs=[
            pl.BlockSpec((tm, tg), lambda i, s: (i, gate_idx(s))),
            pl.BlockSpec((tm, tn), lambda i, s: (i, jnp.minimum(s, N_NAT_STEPS - 1))),
            dil_spec(d4, FIRST_D4),
            dil_spec(d16, FIRST_D16),
        ],
        out_shape=[
            jax.ShapeDtypeStruct((n, N_BRANCH * d), BF16),
            jax.ShapeDtypeStruct((n, N_NAT_STEPS * tn), BF16),
            jax.ShapeDtypeStruct((batch, d4, seq // d4, N_DIL_STEPS * tn), BF16),
            jax.ShapeDtypeStruct((batch, d16, seq // d16, N_DIL_STEPS * tn), BF16),
        ],
        scratch_shapes=[pltpu.VMEM((tm, d), BF16), pltpu.VMEM((tn // HEAD_DIM, PROJ_ROWS, HEAD_DIM), F32)],
        compiler_params=pltpu.CompilerParams(
            dimension_semantics=("arbitrary", "arbitrary"), vmem_limit_bytes=_vmem_limit(est)),
        name="mixproj",
    )(x, gain, w_gate, b_gate, w_in)


def _attn_kernel(slopes_ref, q0_ref, k0_ref, v0_ref, q1_ref, k1_ref, v1_ref, q2_ref, k2_ref, v2_ref, o_ref,
                 m0_ref, m1_ref, m2_ref, l0_ref, l1_ref, l2_ref, n0_ref, n1_ref, n2_ref, va_ref, bias_ref, *, seq):
    h = pl.program_id(1)
    scale = HEAD_DIM ** -0.5
    qb = ATT_QB
    qkv_refs = ((q0_ref, k0_ref, v0_ref), (q1_ref, k1_ref, v1_ref), (q2_ref, k2_ref, v2_ref))
    m_refs = (m0_ref, m1_ref, m2_ref)
    l_refs = (l0_ref, l1_ref, l2_ref)
    n_refs = (n0_ref, n1_ref, n2_ref)

    va_ref[:, HEAD_DIM:] = jnp.ones((seq, HEAD_DIM), BF16)

    for gi, (_, d) in enumerate(DIL_PAIRS):
        q_ref, k_ref, v_ref = qkv_refs[gi]
        sub_len = seq // d
        kw = min(ATT_KW, sub_len)
        nb = sub_len // qb
        slope_d = slopes_ref[gi, h] * float(d)

        row = lax.broadcasted_iota(jnp.int32, (qb, kw), 0)
        col = lax.broadcasted_iota(jnp.int32, (qb, kw), 1)
        for kind, off in enumerate((0, -BAND_RADIUS, qb - kw)):
            arel = jnp.abs(col - row + off)
            bias_ref[kind, :, :kw] = jnp.where(arel <= BAND_RADIUS, -slope_d * arel.astype(F32), NEG)

        if d == 1:
            va_ref[:, :HEAD_DIM] = v_ref[...]
        else:
            for res in range(d):
                va_ref[pl.ds(res * sub_len, sub_len), :HEAD_DIM] = v_ref[res]

        def block(idx, gi=gi, d=d, q_ref=q_ref, k_ref=k_ref, sub_len=sub_len, kw=kw, nb=nb):
            res = idx // nb
            i = idx - res * nb
            q0 = pl.multiple_of(i * qb, qb)
            k0 = pl.multiple_of(jnp.clip(q0 - BAND_RADIUS, 0, sub_len - kw), BAND_RADIUS)
            if d == 1:
                qblk = q_ref[pl.ds(q0, qb), :]
                kwin = k_ref[pl.ds(k0, kw), :]
            else:
                qblk = q_ref[res, pl.ds(q0, qb), :]
                kwin = k_ref[res, pl.ds(k0, kw), :]
            s = lax.dot_general(qblk, kwin, (((1,), (1,)), ((), ())), preferred_element_type=F32)
            if nb == 1:
                bias = bias_ref[0, :, :kw]
            else:
                kind = jnp.where(i == 0, 0, jnp.where(i == nb - 1, 2, 1))
                bias = bias_ref[kind]
            s = s * scale + bias
            m = jnp.max(s, axis=-1, keepdims=True)
            p = jnp.exp(s - m).astype(BF16)
            kstart = pl.multiple_of(res * sub_len + k0, BAND_RADIUS)
            na = jnp.dot(p, va_ref[pl.ds(kstart, kw), :], preferred_element_type=F32)
            if d == 1:
                rows = pl.ds(q0, qb)
            else:
                rows = pl.ds(q0 * d + res, qb, stride=d)
            m_refs[gi][rows, :] = jnp.broadcast_to(m, (qb, HEAD_DIM))
            l_refs[gi][rows, :] = na[:, HEAD_DIM:]
            n_refs[gi][rows, :] = na[:, :HEAD_DIM]

        def blocks(it, carry, block=block):
            for u in range(ATT_UNROLL):
                block(it * ATT_UNROLL + u)
            return carry

        lax.fori_loop(0, d * nb // ATT_UNROLL, blocks, 0)

    def fin(c, carry):
        r = pl.ds(pl.multiple_of(c * ATT_FIN_ROWS, ATT_FIN_ROWS), ATT_FIN_ROWS)
        ms = [m_ref[r, :] for m_ref in m_refs]
        mx = jnp.maximum(jnp.maximum(ms[0], ms[1]), ms[2])
        num = jnp.zeros((ATT_FIN_ROWS, HEAD_DIM), F32)
        den = jnp.zeros((ATT_FIN_ROWS, HEAD_DIM), F32)
        for gi in range(N_GROUPS):
            e = jnp.exp(ms[gi] - mx)
            num += e * n_refs[gi][r, :]
            den += e * l_refs[gi][r, :]
        o_ref[r, :] = (num / den).astype(o_ref.dtype)
        return carry

    lax.fori_loop(0, seq // ATT_FIN_ROWS, fin, 0)


def _attention(nat, qkv4, qkv16, slopes, *, batch, seq, col0):
    blk0 = col0 // HEAD_DIM
    d4, d16 = DIL_PAIRS[1][1], DIL_PAIRS[2][1]

    def nat_map(b, h, *, which):
        return (b, blk0 + which * HEADS_PER_GROUP + h)

    def dil_map(b, h, *, which):
        return (b, 0, 0, which * HEADS_PER_GROUP + h)

    in_specs = [pl.BlockSpec(memory_space=pltpu.SMEM)]
    in_specs += [pl.BlockSpec((seq, HEAD_DIM), functools.partial(nat_map, which=w)) for w in range(3)]
    for dil in (d4, d16):
        in_specs += [pl.BlockSpec((None, dil, seq // dil, HEAD_DIM), functools.partial(dil_map, which=w))
                     for w in range(3)]
    tile = seq * HEAD_DIM
    est = 10 * 2 * tile * 2 + 9 * tile * 4 + 2 * tile * 2 + 3 * ATT_QB * ATT_KW * 4
    state = [pltpu.VMEM((seq, HEAD_DIM), F32) for _ in range(3 * N_GROUPS)]
    return pl.pallas_call(
        functools.partial(_attn_kernel, seq=seq),
        grid=(batch, HEADS_PER_GROUP),
        in_specs=in_specs,
        out_specs=pl.BlockSpec((seq, HEAD_DIM), lambda b, h: (b, h)),
        out_shape=jax.ShapeDtypeStruct((batch * seq, HEADS_PER_GROUP * HEAD_DIM), BF16),
        scratch_shapes=state + [
            pltpu.VMEM((seq, 2 * HEAD_DIM), BF16),
            pltpu.VMEM((3, ATT_QB, ATT_KW), F32),
        ],
        compiler_params=pltpu.CompilerParams(
            dimension_semantics=("arbitrary", "arbitrary"), vmem_limit_bytes=_vmem_limit(est)),
        name="attention",
    )(slopes, nat, nat, nat, qkv4, qkv4, qkv4, qkv16, qkv16, qkv16)


def _pool_kernel(u_ref, w_ref, sc_ref, o_ref, pad_ref, pooled_ref, *, seq):
    grp = pl.program_id(1)
    halo, rows = POOL_HALO, POOL_ROWS
    ext = rows + 2 * halo
    zeros = jnp.zeros((halo, POOL_GROUP), F32)
    pad_ref[pl.ds(0, halo), :] = zeros
    pad_ref[pl.ds(halo + seq, halo), :] = zeros
    pad_ref[pl.ds(halo, seq), :] = u_ref[...].astype(F32)

    def run(window):
        def chunk(c, carry):
            base = pl.multiple_of(c * rows, rows)
            p = pad_ref[pl.ds(base, ext), :]
            s = p + pltpu.roll(p, 1, 0)
            half = 1
            while 2 * half < window:
                s = pltpu.roll(s, ext - half, 0) + pltpu.roll(s, half, 0)
                half *= 2
            s = s[halo:halo + rows, :]
            tok = p[halo:halo + rows, :]
            t = base + lax.broadcasted_iota(jnp.int32, (rows, 1), 0)
            lo = jnp.maximum(t - window // 2, 0)
            hi = jnp.minimum(t - window // 2 + window, seq)
            pooled_ref[pl.ds(base, rows), :] = (s / (hi - lo).astype(F32) - tok).astype(BF16)
            return carry

        lax.fori_loop(0, seq // rows, chunk, 0)

    for gi, window in enumerate(POOL_WINDOWS):
        pl.when(grp == gi)(functools.partial(run, window))

    y = jnp.dot(pooled_ref[...], w_ref[...], preferred_element_type=F32) * sc_ref[...]
    o_ref[...] = y.astype(o_ref.dtype)


def _pool(pg, w_pool, pool_scale, *, batch, seq, col0):
    blk0 = col0 // POOL_GROUP
    ngrp = len(POOL_WINDOWS)
    est = (2 * seq * POOL_GROUP * 2 * 2 + (seq + 2 * POOL_HALO) * POOL_GROUP * 4 + 2 * POOL_GROUP * POOL_GROUP * 2
           + seq * POOL_GROUP * (2 + 4))
    return pl.pallas_call(
        functools.partial(_pool_kernel, seq=seq),
        grid=(batch, ngrp),
        in_specs=[
            pl.BlockSpec((seq, POOL_GROUP), lambda b, g: (b, blk0 + g)),
            pl.BlockSpec((None, POOL_GROUP, POOL_GROUP), lambda b, g: (g, 0, 0)),
            pl.BlockSpec((1, POOL_GROUP), lambda b, g: (0, g)),
        ],
        out_specs=pl.BlockSpec((seq, POOL_GROUP), lambda b, g: (b, g)),
        out_shape=jax.ShapeDtypeStruct((batch * seq, D_POOL), BF16),
        scratch_shapes=[pltpu.VMEM((seq + 2 * POOL_HALO, POOL_GROUP), F32), pltpu.VMEM((seq, POOL_GROUP), BF16)],
        compiler_params=pltpu.CompilerParams(
            dimension_semantics=("arbitrary", "arbitrary"), vmem_limit_bytes=_vmem_limit(est)),
        name="pool",
    )(pg, w_pool, pool_scale)


def _memattn_kernel(q_ref, kv_ref, o_ref):
    scale = XHEAD_DIM ** -0.5
    for hh in range(N_XHEADS):
        lo = hh * XHEAD_DIM
        q = q_ref[:, lo:lo + XHEAD_DIM]
        k = kv_ref[:, lo:lo + XHEAD_DIM]
        v = kv_ref[:, D_XATTN + lo:D_XATTN + lo + XHEAD_DIM]
        s = lax.dot_general(q, k, (((1,), (1,)), ((), ())), preferred_element_type=F32) * scale
        m = jnp.max(s, axis=-1, keepdims=True)
        p = jnp.exp(s - m)
        l = jnp.sum(p, axis=-1, keepdims=True)
        y = jnp.dot(p.astype(BF16), v, preferred_element_type=F32) / l
        o_ref[:, lo:lo + XHEAD_DIM] = y.astype(o_ref.dtype)


def _memattn(pg, kv, *, seq, col0):
    n = pg.shape[0]
    tm = MEM_TM
    blk0 = col0 // D_XATTN
    tiles_per_seq = seq // tm
    est = 2 * tm * D_XATTN * 2 * 2 + 2 * N_MEM * 2 * D_XATTN * 2 + 4 * tm * N_MEM * 4
    return pl.pallas_call(
        _memattn_kernel,
        grid=(n // tm,),
        in_specs=[
            pl.BlockSpec((tm, D_XATTN), lambda i: (i, blk0)),
            pl.BlockSpec((N_MEM, 2 * D_XATTN), lambda i: (i // tiles_per_seq, 0)),
        ],
        out_specs=pl.BlockSpec((tm, D_XATTN), lambda i: (i, 0)),
        out_shape=jax.ShapeDtypeStruct((n, D_XATTN), BF16),
        compiler_params=pltpu.CompilerParams(
            dimension_semantics=("arbitrary",), vmem_limit_bytes=_vmem_limit(est)),
        name="memattn",
    )(pg, kv)


def _merge_kernel(gates_ref, ya_ref, yp_ref, ym_ref, x_ref, wa_ref, wp_ref, wm_ref, wo_ref, gain_ref, o_ref):
    d = x_ref.shape[1]
    for r in range(0, x_ref.shape[0], MERGE_ROWS):
        rows = pl.ds(r, MERGE_ROWS)
        merged = gates_ref[rows, 0:d].astype(F32) * jnp.dot(ya_ref[rows, :], wa_ref[...], preferred_element_type=F32)
        merged += gates_ref[rows, d:2 * d].astype(F32) * jnp.dot(yp_ref[rows, :], wp_ref[...],
                                                                   preferred_element_type=F32)
        merged += gates_ref[rows, 2 * d:3 * d].astype(F32) * jnp.dot(ym_ref[rows, :], wm_ref[...],
                                                                       preferred_element_type=F32)
        z = jnp.dot(merged.astype(BF16), wo_ref[...], preferred_element_type=F32)
        o_ref[rows, :] = x_ref[rows, :] + _rms(z, gain_ref[...])


def _merge(gates, y_attn, y_pool, y_mem, x, wa, wp, wm, wo, gain):
    n, d = x.shape
    tm = MERGE_TM
    wbytes = (wa.size + wp.size + wm.size + wo.size) * 2

    def const(shape):
        return pl.BlockSpec(shape, lambda i: (0, 0), pipeline_mode=pl.Buffered(1))

    est = (2 * tm * (N_BRANCH * d + y_attn.shape[1] + y_pool.shape[1] + y_mem.shape[1]) * 2
           + 4 * tm * d * 4 + wbytes + 4 * tm * d * 4)
    return pl.pallas_call(
        _merge_kernel,
        grid=(n // tm,),
        in_specs=[
            pl.BlockSpec((tm, N_BRANCH * d), lambda i: (i, 0)),
            pl.BlockSpec((tm, y_attn.shape[1]), lambda i: (i, 0)),
            pl.BlockSpec((tm, y_pool.shape[1]), lambda i: (i, 0)),
            pl.BlockSpec((tm, y_mem.shape[1]), lambda i: (i, 0)),
            pl.BlockSpec((tm, d), lambda i: (i, 0)),
            const(wa.shape), const(wp.shape), const(wm.shape), const(wo.shape),
            pl.BlockSpec((1, d), lambda i: (0, 0)),
        ],
        out_specs=pl.BlockSpec((tm, d), lambda i: (i, 0)),
        out_shape=jax.ShapeDtypeStruct((n, d), F32),
        compiler_params=pltpu.CompilerParams(
            dimension_semantics=("arbitrary",), vmem_limit_bytes=_vmem_limit(est)),
        name="merge",
    )(gates, y_attn, y_pool, y_mem, x, wa, wp, wm, wo, gain)


def _cast_kernel(w_ref, *o_refs):
    lo = 0
    for o_ref in o_refs:
        width = o_ref.shape[1]
        o_ref[...] = w_ref[:, lo:lo + width].astype(o_ref.dtype)
        lo += width


def _cast_bf16(w, widths=None, target_bytes=CAST_BLOCK_BYTES):
    rows, cols = w.shape
    widths = (cols,) if widths is None else tuple(widths)
    br = rows
    while br * cols * 4 > target_bytes and br % 32 == 0:
        br //= 2
    est = 2 * br * cols * 4 + 2 * br * cols * 2
    outs = pl.pallas_call(
        _cast_kernel,
        grid=(rows // br,),
        in_specs=[pl.BlockSpec((br, cols), lambda i: (i, 0))],
        out_specs=[pl.BlockSpec((br, wd), lambda i: (i, 0)) for wd in widths],
        out_shape=[jax.ShapeDtypeStruct((rows, wd), BF16) for wd in widths],
        compiler_params=pltpu.CompilerParams(
            dimension_semantics=("arbitrary",), vmem_limit_bytes=_vmem_limit(est)),
        name="cast",
    )(w)
    return outs if len(widths) > 1 else outs[0]


def _cast_chunked_kernel(w_ref, *o_refs, width):
    for k, o_ref in enumerate(o_refs):
        nchunk, _, tf = o_ref.shape
        for c in range(nchunk):
            valid = min(tf, width - c * tf)
            lo = k * width + c * tf
            o_ref[c, :, :valid] = w_ref[:, lo:lo + valid].astype(o_ref.dtype)
            if valid < tf:
                o_ref[c, :, valid:] = jnp.zeros((o_ref.shape[1], tf - valid), o_ref.dtype)


def _cast_chunked(w, n_out, tf, target_bytes=CAST_BLOCK_BYTES):
    rows, cols = w.shape
    width = cols // n_out
    nchunk = pl.cdiv(width, tf)
    br = rows
    while br * cols * 4 > target_bytes and br % 32 == 0:
        br //= 2
    est = 2 * br * cols * 4 + 2 * n_out * nchunk * br * tf * 2
    return pl.pallas_call(
        functools.partial(_cast_chunked_kernel, width=width),
        grid=(rows // br,),
        in_specs=[pl.BlockSpec((br, cols), lambda i: (i, 0))],
        out_specs=[pl.BlockSpec((nchunk, br, tf), lambda i: (0, i, 0)) for _ in range(n_out)],
        out_shape=[jax.ShapeDtypeStruct((nchunk, rows, tf), BF16) for _ in range(n_out)],
        compiler_params=pltpu.CompilerParams(
            dimension_semantics=("arbitrary",), vmem_limit_bytes=_vmem_limit(est)),
        name="cast_chunked",
    )(w)


def _row(v):
    return v.reshape(1, -1).astype(F32)


def _trunk(x, mem, p, slopes):
    batch, seq, d = x.shape
    xf = x.reshape(batch * seq, d)
    x1 = _ffn(xf, p["ffn1_norm_pre"], p["ffn1_wg"], p["ffn1_wu"], p["ffn1_wd"], p["ffn1_norm_post"],
              p["ffn1_norm_post"], final=False)
    gates, nat, qkv4, qkv16 = _mixproj(x1, p["mix_norm_pre"], p["w_gate"], p["b_gate"], p["w_in"],
                                       batch=batch, seq=seq)
    col_pool = D_XATTN
    col_qkv0 = col_pool + D_POOL
    kv = _kvproj(mem.reshape(batch * N_MEM, d), p["mem_norm"], p["w_mem_kv"])
    y_attn = _attention(nat, qkv4, qkv16, slopes, batch=batch, seq=seq, col0=col_qkv0)
    y_pool = _pool(nat, p["w_pool"], p["pool_scale"], batch=batch, seq=seq, col0=col_pool)
    y_mem = _memattn(nat, kv, seq=seq, col0=0)
    x2 = _merge(gates, y_attn, y_pool, y_mem, x1, p["w_br_attn"], p["w_br_pool"], p["w_br_mem"], p["w_out"],
                p["mix_norm_post"])
    y = _ffn(x2, p["ffn2_norm_pre"], p["ffn2_wg"], p["ffn2_wu"], p["ffn2_wd"], p["ffn2_norm_post"],
             p["final_norm"], final=True)
    return y.reshape(batch, seq, d)


def _prep_params(ffn1_norm_pre, ffn1_w_up, ffn1_w_down, ffn1_norm_post, mix_norm_pre, mem_norm, w_in, w_mem_kv,
                 w_pool, pool_scale, w_br_attn, w_br_pool, w_br_mem, w_gate, b_gate, w_out, mix_norm_post,
                 ffn2_norm_pre, ffn2_w_up, ffn2_w_down, ffn2_norm_post, final_norm):
    p = {}
    for name, w_up, w_down in (("ffn1", ffn1_w_up, ffn1_w_down), ("ffn2", ffn2_w_up, ffn2_w_down)):
        dff = w_down.shape[0]
        p[name + "_wg"], p[name + "_wu"] = _cast_chunked(w_up, 2, FFN_TF)
        p[name + "_wd"] = _cast_bf16(w_down)
    p["w_gate"] = _cast_bf16(w_gate)
    p["w_in"] = _cast_bf16(w_in)
    p["b_gate"] = _row(b_gate)
    p["w_mem_kv"] = _cast_bf16(w_mem_kv)
    p["w_pool"] = w_pool.astype(BF16)
    p["pool_scale"] = _row(pool_scale)
    for name, w in (("w_br_attn", w_br_attn), ("w_br_pool", w_br_pool), ("w_br_mem", w_br_mem), ("w_out", w_out)):
        p[name] = _cast_bf16(w)
    for name, v in (("ffn1_norm_pre", ffn1_norm_pre), ("ffn1_norm_post", ffn1_norm_post),
                    ("mix_norm_pre", mix_norm_pre), ("mem_norm", mem_norm), ("mix_norm_post", mix_norm_post),
                    ("ffn2_norm_pre", ffn2_norm_pre), ("ffn2_norm_post", ffn2_norm_post),
                    ("final_norm", final_norm)):
        p[name] = _row(v)
    return p


def kernel(x_prompt, x_sample, mem_prompt, mem_sample, ffn1_norm_pre, ffn1_w_up, ffn1_w_down, ffn1_norm_post,
           mix_norm_pre, mem_norm, w_in, w_mem_kv, w_pool, pool_scale, w_br_attn, w_br_pool, w_br_mem, w_gate, b_gate,
           w_out, mix_norm_post, ffn2_norm_pre, ffn2_w_up, ffn2_w_down, ffn2_norm_post, final_norm):
    layer = [ffn1_norm_pre, ffn1_w_up, ffn1_w_down, ffn1_norm_post, mix_norm_pre, mem_norm, w_in, w_mem_kv, w_pool,
             pool_scale, w_br_attn, w_br_pool, w_br_mem, w_gate, b_gate, w_out, mix_norm_post, ffn2_norm_pre,
             ffn2_w_up, ffn2_w_down, ffn2_norm_post, final_norm]
    depth = ffn1_norm_pre.shape[0]
    slopes = jnp.asarray(_alibi_slopes())
    y_prompt, y_sample = x_prompt, x_sample
    for layer_idx in range(depth):
        p = _prep_params(*[w[layer_idx] for w in layer])
        y_prompt = _trunk(y_prompt, mem_prompt, p, slopes)
        y_sample = _trunk(y_sample, mem_sample, p, slopes)
    return (y_prompt, y_sample)
```

```python
import functools

import numpy as np
import jax
import jax.numpy as jnp
from jax import lax
from jax.experimental import pallas as pl
from jax.experimental.pallas import tpu as pltpu

F32 = jnp.float32
BF16 = jnp.bfloat16

D_MODEL = 2048
N_MEM = 256
HEAD_DIM = 128
DIL_PAIRS = ((128, 1), (512, 4), (2048, 16))
HEADS_PER_GROUP = 4
N_GROUPS = len(DIL_PAIRS)
N_DIL_HEADS = HEADS_PER_GROUP * N_GROUPS
D_ATTN = N_DIL_HEADS * HEAD_DIM
POOL_WINDOWS = (2, 4, 8, 16)
POOL_GROUP = 256
D_POOL = POOL_GROUP * len(POOL_WINDOWS)
N_XHEADS = 4
XHEAD_DIM = 256
D_XATTN = N_XHEADS * XHEAD_DIM
N_BRANCH = 3
EPS = 1e-6
NEG = -1e30
BAND_RADIUS = 64

V7X_VMEM_BYTES = 64 * 1024 * 1024
MIB = 1024 * 1024

FFN_TM = 1024
FFN_TF = 1024
PROJ_TM = 1024
PROJ_TN = 512
PROJ_TG = 1536
PROJ_ROWS = 256
ATT_QB = 128
ATT_KW = 256
ATT_UNROLL = 8
POOL_ROWS = 256
POOL_HALO = 64
POOL_UNROLL = 4
MEM_TM = 512
MERGE_TM = 512
MERGE_ROWS = 512
CAST_BLOCK_BYTES = 8 * MIB


def _vmem_limit(nbytes):
    return int(min(nbytes * 5 // 4 + 8 * MIB, V7X_VMEM_BYTES - 2 * MIB))


def _rms(x, gain):
    ms = jnp.mean(x * x, axis=-1, keepdims=True)
    return x * lax.rsqrt(ms + EPS) * gain


def _alibi_slopes():
    s = 2.0 ** (-8.0 * np.arange(1, N_DIL_HEADS + 1) / N_DIL_HEADS)
    return s.reshape(HEADS_PER_GROUP, N_GROUPS).T.astype(np.float32)


def _ffn_kernel(x_ref, gpre_ref, wg_ref, wu_ref, wd_ref, gpost_ref, gfin_ref, o_hbm, acc_ref, hs_ref, sem,
                *, tf, rem, final):
    i = pl.program_id(0)
    f = pl.program_id(1)
    n_tiles = pl.num_programs(0)
    last = pl.num_programs(1) - 1
    tm = acc_ref.shape[0]

    def writeback(tile):
        return pltpu.make_async_copy(acc_ref, o_hbm.at[pl.ds(pl.multiple_of(tile * tm, tm), tm), :], sem)

    @pl.when(f == 0)
    def _():
        hs_ref[...] = _rms(x_ref[...], gpre_ref[...]).astype(BF16)

    def step(valid, first):
        hs = hs_ref[...]
        g = jnp.dot(hs, wg_ref[:, :valid], preferred_element_type=F32)
        u = jnp.dot(hs, wu_ref[:, :valid], preferred_element_type=F32)
        a = (g * jax.nn.sigmoid(g) * u).astype(BF16)
        if first:
            @pl.when(i > 0)
            def _():
                writeback(i - 1).wait()

            acc_ref[...] = jnp.dot(a, wd_ref[:valid, :], preferred_element_type=F32)
        else:
            acc_ref[...] += jnp.dot(a, wd_ref[:valid, :], preferred_element_type=F32)

    pl.when(f == 0)(functools.partial(step, tf, True))
    pl.when((f > 0) & (f < last))(functools.partial(step, tf, False))
    pl.when(f == last)(functools.partial(step, rem, False))

    @pl.when(f == last)
    def _():
        y = x_ref[...] + 0.5 * _rms(acc_ref[...], gpost_ref[...])
        if final:
            y = _rms(y, gfin_ref[...])
        acc_ref[...] = y
        writeback(i).start()

        @pl.when(i == n_tiles - 1)
        def _():
            writeback(i).wait()


def _ffn(x, gpre, wg, wu, wd, gpost, gfin, *, final):
    n, d = x.shape
    tm, tf = FFN_TM, FFN_TF
    dff = wd.shape[0]
    nf = pl.cdiv(dff, tf)
    rem = dff - (nf - 1) * tf
    assert nf >= 2
    est = 2 * tm * d * 4 + tm * d * 4 + tm * d * 2 + 2 * 3 * d * tf * 2 + 2 * tm * tf * 4
    vec = pl.BlockSpec((1, d), lambda i, f: (0, 0))
    return pl.pallas_call(
        functools.partial(_ffn_kernel, tf=tf, rem=rem, final=final),
        grid=(n // tm, nf),
        in_specs=[
            pl.BlockSpec((tm, d), lambda i, f: (i, 0)),
            vec,
            pl.BlockSpec((None, d, tf), lambda i, f: (f, 0, 0)),
            pl.BlockSpec((None, d, tf), lambda i, f: (f, 0, 0)),
            pl.BlockSpec((tf, d), lambda i, f: (f, 0)),
            vec,
            vec,
        ],
        out_specs=pl.BlockSpec(memory_space=pl.ANY),
        out_shape=jax.ShapeDtypeStruct((n, d), F32),
        scratch_shapes=[pltpu.VMEM((tm, d), F32), pltpu.VMEM((tm, d), BF16), pltpu.SemaphoreType.DMA(())],
        compiler_params=pltpu.CompilerParams(
            dimension_semantics=("arbitrary", "arbitrary"), vmem_limit_bytes=_vmem_limit(est)),
        name="ffn_final" if final else "ffn",
    )(x, gpre, wg, wu, wd, gpost, gfin)


def _kvproj_kernel(x_ref, g_ref, w_ref, o_ref, hs_ref):
    @pl.when(pl.program_id(1) == 0)
    def _():
        hs_ref[...] = _rms(x_ref[...], g_ref[...]).astype(BF16)

    o_ref[...] = jnp.dot(hs_ref[...], w_ref[...], preferred_element_type=F32).astype(o_ref.dtype)


def _kvproj(x, gain, w):
    n, d = x.shape
    ncol = w.shape[1]
    tm, tn = n, PROJ_TN
    est = 2 * tm * d * 4 + tm * d * 2 + 2 * d * tn * 2 + 2 * tm * tn * 2 + 2 * tm * tn * 4
    return pl.pallas_call(
        _kvproj_kernel,
        grid=(n // tm, ncol // tn),
        in_specs=[
            pl.BlockSpec((tm, d), lambda i, j: (i, 0)),
            pl.BlockSpec((1, d), lambda i, j: (0, 0)),
            pl.BlockSpec((d, tn), lambda i, j: (0, j)),
        ],
        out_specs=pl.BlockSpec((tm, tn), lambda i, j: (i, j)),
        out_shape=jax.ShapeDtypeStruct((n, ncol), BF16),
        scratch_shapes=[pltpu.VMEM((tm, d), BF16)],
        compiler_params=pltpu.CompilerParams(
            dimension_semantics=("arbitrary", "arbitrary"), vmem_limit_bytes=_vmem_limit(est)),
        name="kvproj",
    )(x, gain, w)


NAT_CHUNKS = (11, 12, 9, 10, 0, 3, 6)
N_NAT_STEPS = len(NAT_CHUNKS)
N_DIL_STEPS = 3
N_GATE_STEPS = N_BRANCH * D_MODEL // PROJ_TG
FIRST_D4 = N_NAT_STEPS
FIRST_D16 = FIRST_D4 + N_DIL_STEPS
FIRST_GATE = FIRST_D16 + N_DIL_STEPS
MIX_STEPS = FIRST_GATE + N_GATE_STEPS


def _lin_chunk(s):
    t = jnp.minimum(s, FIRST_GATE - 1)
    nat = jnp.where(t < 2, 11 + t, jnp.where(t < 4, 7 + t, 3 * (t - 4)))
    d4 = 3 * (t - FIRST_D4) + 1
    d16 = 3 * (t - FIRST_D16) + 2
    return jnp.where(t < FIRST_D4, nat, jnp.where(t < FIRST_D16, d4, d16))


def _mixproj_kernel(x_ref, g_ref, wg_ref, b_ref, wi_ref, og_ref, on_ref, o4_ref, o16_ref, hs_ref, stage_ref):
    s = pl.program_id(1)
    tm = x_ref.shape[0]
    row_chunks = [pl.ds(r, PROJ_ROWS) for r in range(0, tm, PROJ_ROWS)]

    @pl.when(s == 0)
    def _():
        hs_ref[...] = _rms(x_ref[...], g_ref[...]).astype(BF16)

    @pl.when(s < FIRST_D4)
    def _():
        for rows in row_chunks:
            on_ref[rows, :] = jnp.dot(hs_ref[rows, :], wi_ref[...], preferred_element_type=F32).astype(on_ref.dtype)

    def dilated(o_ref, d):
        per_res = PROJ_ROWS // d
        for rc, rows in enumerate(row_chunks):
            acc = jnp.dot(hs_ref[rows, :], wi_ref[...], preferred_element_type=F32)
            for c in range(PROJ_TN // HEAD_DIM):
                stage_ref[c] = acc[:, c * HEAD_DIM:(c + 1) * HEAD_DIM]
            for res in range(d):
                for c in range(PROJ_TN // HEAD_DIM):
                    o_ref[res, pl.ds(rc * per_res, per_res), c * HEAD_DIM:(c + 1) * HEAD_DIM] = (
                        stage_ref[c, pl.ds(res, per_res, stride=d), :].astype(o_ref.dtype))

    pl.when((s >= FIRST_D4) & (s < FIRST_D16))(functools.partial(dilated, o4_ref, DIL_PAIRS[1][1]))
    pl.when((s >= FIRST_D16) & (s < FIRST_GATE))(functools.partial(dilated, o16_ref, DIL_PAIRS[2][1]))

    @pl.when(s >= FIRST_GATE)
    def _():
        for rows in row_chunks:
            acc = jnp.dot(hs_ref[rows, :], wg_ref[...], preferred_element_type=F32)
            og_ref[rows, :] = jax.nn.sigmoid(acc + b_ref[...]).astype(og_ref.dtype)


def _mixproj(x, gain, w_gate, b_gate, w_in, *, batch, seq):
    n, d = x.shape
    tm, tn, tg = PROJ_TM, PROJ_TN, PROJ_TG
    tiles_per_seq = seq // tm
    d4, d16 = DIL_PAIRS[1][1], DIL_PAIRS[2][1]

    def gate_idx(s):
        return jnp.maximum(s - FIRST_GATE, 0)

    def dil_spec(dil, first):
        return pl.BlockSpec(
            (None, dil, tm // dil, tn),
            lambda i, s: (i // tiles_per_seq, 0, i % tiles_per_seq, jnp.clip(s - first, 0, N_DIL_STEPS - 1)))

    est = (2 * tm * d * 4 + tm * d * 2 + 2 * d * (tg + tn) * 2 + 2 * tm * (tg + 3 * tn) * 2
           + PROJ_ROWS * tn * 4 + 2 * PROJ_ROWS * tg * 4)
    return pl.pallas_call(
        _mixproj_kernel,
        grid=(n // tm, MIX_STEPS),
        in_specs=[
            pl.BlockSpec((tm, d), lambda i, s: (i, 0)),
            pl.BlockSpec((1, d), lambda i, s: (0, 0)),
            pl.BlockSpec((d, tg), lambda i, s: (0, gate_idx(s))),
            pl.BlockSpec((1, tg), lambda i, s: (0, gate_idx(s))),
            pl.BlockSpec((d, tn), lambda i, s: (0, _lin_chunk(s))),
        ],
        out_specs=[
            pl.BlockSpec((tm, tg), lambda i, s: (i, gate_idx(s))),
            pl.BlockSpec((tm, tn), lambda i, s: (i, jnp.minimum(s, N_NAT_STEPS - 1))),
            dil_spec(d4, FIRST_D4),
            dil_spec(d16, FIRST_D16),
        ],
        out_shape=[
            jax.ShapeDtypeStruct((n, N_BRANCH * d), BF16),
            jax.ShapeDtypeStruct((n, N_NAT_STEPS * tn), BF16),
            jax.ShapeDtypeStruct((batch, d4, seq // d4, N_DIL_STEPS * tn), BF16),
            jax.ShapeDtypeStruct((batch, d16, seq // d16, N_DIL_STEPS * tn), BF16),
        ],
        scratch_shapes=[pltpu.VMEM((tm, d), BF16), pltpu.VMEM((tn // HEAD_DIM, PROJ_ROWS, HEAD_DIM), F32)],
        compiler_params=pltpu.CompilerParams(
            dimension_semantics=("arbitrary", "arbitrary"), vmem_limit_bytes=_vmem_limit(est)),
        name="mixproj",
    )(x, gain, w_gate, b_gate, w_in)


def _attn_kernel(slopes_ref, q0_ref, k0_ref, v0_ref, q1_ref, k1_ref, v1_ref, q2_ref, k2_ref, v2_ref, o_ref,
                 m1_ref, m2_ref, l1_ref, l2_ref, n1_ref, n2_ref, va_ref, bias_ref, *, seq):
    h = pl.program_id(1)
    scale = HEAD_DIM ** -0.5
    qb = ATT_QB
    qkv_refs = ((q0_ref, k0_ref, v0_ref), (q1_ref, k1_ref, v1_ref), (q2_ref, k2_ref, v2_ref))
    m_refs = (None, m1_ref, m2_ref)
    l_refs = (None, l1_ref, l2_ref)
    n_refs = (None, n1_ref, n2_ref)

    va_ref[:, HEAD_DIM:] = jnp.ones((seq, HEAD_DIM), BF16)

    for gi in reversed(range(N_GROUPS)):
        d = DIL_PAIRS[gi][1]
        q_ref, k_ref, v_ref = qkv_refs[gi]
        sub_len = seq // d
        kw = min(ATT_KW, sub_len)
        nb = sub_len // qb
        slope_d = slopes_ref[gi, h] * float(d)

        row = lax.broadcasted_iota(jnp.int32, (qb, kw), 0)
        col = lax.broadcasted_iota(jnp.int32, (qb, kw), 1)
        for kind, off in enumerate((0, -BAND_RADIUS, qb - kw)):
            arel = jnp.abs(col - row + off)
            bias_ref[kind, :, :kw] = jnp.where(arel <= BAND_RADIUS, -slope_d * arel.astype(F32), NEG)

        if d == 1:
            va_ref[:, :HEAD_DIM] = v_ref[...]
        else:
            for res in range(d):
                va_ref[pl.ds(res * sub_len, sub_len), :HEAD_DIM] = v_ref[res]

        def block(idx, gi=gi, d=d, q_ref=q_ref, k_ref=k_ref, sub_len=sub_len, kw=kw, nb=nb):
            res = idx // nb
            i = idx - res * nb
            q0 = pl.multiple_of(i * qb, qb)
            k0 = pl.multiple_of(jnp.clip(q0 - BAND_RADIUS, 0, sub_len - kw), BAND_RADIUS)
            if d == 1:
                qblk = q_ref[pl.ds(q0, qb), :]
                kwin = k_ref[pl.ds(k0, kw), :]
            else:
                qblk = q_ref[res, pl.ds(q0, qb), :]
                kwin = k_ref[res, pl.ds(k0, kw), :]
            s = lax.dot_general(qblk, kwin, (((1,), (1,)), ((), ())), preferred_element_type=F32)
            if nb == 1:
                bias = bias_ref[0, :, :kw]
            else:
                kind = jnp.where(i == 0, 0, jnp.where(i == nb - 1, 2, 1))
                bias = bias_ref[kind]
            s = s * scale + bias
            m = jnp.max(s, axis=-1, keepdims=True)
            p = jnp.exp(s - m).astype(BF16)
            kstart = pl.multiple_of(res * sub_len + k0, BAND_RADIUS)
            na = jnp.dot(p, va_ref[pl.ds(kstart, kw), :], preferred_element_type=F32)
            m = jnp.broadcast_to(m, (qb, HEAD_DIM))
            if d > 1:
                rows = pl.ds(q0 * d + res, qb, stride=d)
                m_refs[gi][rows, :] = m
                l_refs[gi][rows, :] = na[:, HEAD_DIM:]
                n_refs[gi][rows, :] = na[:, :HEAD_DIM]
            else:
                rows = pl.ds(q0, qb)
                ms = [m] + [m_refs[g][rows, :] for g in range(1, N_GROUPS)]
                mx = functools.reduce(jnp.maximum, ms)
                e = jnp.exp(m - mx)
                num = e * na[:, :HEAD_DIM]
                den = e * na[:, HEAD_DIM:]
                for g in range(1, N_GROUPS):
                    e = jnp.exp(ms[g] - mx)
                    num += e * n_refs[g][rows, :]
                    den += e * l_refs[g][rows, :]
                o_ref[rows, :] = (num / den).astype(o_ref.dtype)

        def blocks(it, carry, block=block):
            for u in range(ATT_UNROLL):
                block(it * ATT_UNROLL + u)
            return carry

        lax.fori_loop(0, d * nb // ATT_UNROLL, blocks, 0)


def _attention(nat, qkv4, qkv16, slopes, *, batch, seq, col0):
    blk0 = col0 // HEAD_DIM
    d4, d16 = DIL_PAIRS[1][1], DIL_PAIRS[2][1]

    def nat_map(b, h, *, which):
        return (b, blk0 + which * HEADS_PER_GROUP + h)

    def dil_map(b, h, *, which):
        return (b, 0, 0, which * HEADS_PER_GROUP + h)

    in_specs = [pl.BlockSpec(memory_space=pltpu.SMEM)]
    in_specs += [pl.BlockSpec((seq, HEAD_DIM), functools.partial(nat_map, which=w)) for w in range(3)]
    for dil in (d4, d16):
        in_specs += [pl.BlockSpec((None, dil, seq // dil, HEAD_DIM), functools.partial(dil_map, which=w))
                     for w in range(3)]
    tile = seq * HEAD_DIM
    est = 10 * 2 * tile * 2 + 6 * tile * 4 + 2 * tile * 2 + 3 * ATT_QB * ATT_KW * 4
    state = [pltpu.VMEM((seq, HEAD_DIM), F32) for _ in range(3 * (N_GROUPS - 1))]
    return pl.pallas_call(
        functools.partial(_attn_kernel, seq=seq),
        grid=(batch, HEADS_PER_GROUP),
        in_specs=in_specs,
        out_specs=pl.BlockSpec((seq, HEAD_DIM), lambda b, h: (b, h)),
        out_shape=jax.ShapeDtypeStruct((batch * seq, HEADS_PER_GROUP * HEAD_DIM), BF16),
        scratch_shapes=state + [
            pltpu.VMEM((seq, 2 * HEAD_DIM), BF16),
            pltpu.VMEM((3, ATT_QB, ATT_KW), F32),
        ],
        compiler_params=pltpu.CompilerParams(
            dimension_semantics=("arbitrary", "arbitrary"), vmem_limit_bytes=_vmem_limit(est)),
        name="attention",
    )(slopes, nat, nat, nat, qkv4, qkv4, qkv4, qkv16, qkv16, qkv16)


def _pool_kernel(u_ref, w_ref, sc_ref, o_ref, pad_ref, band_ref, pooled_ref, *, seq):
    grp = pl.program_id(1)
    halo, rows = POOL_HALO, POOL_ROWS
    ext = rows + 2 * halo
    zeros = jnp.zeros((halo, POOL_GROUP), BF16)
    pad_ref[pl.ds(0, halo), :] = zeros
    pad_ref[pl.ds(halo + seq, halo), :] = zeros
    pad_ref[pl.ds(halo, seq), :] = u_ref[...]

    def run(window):
        first = lax.broadcasted_iota(jnp.int32, (rows, ext), 0) + (halo - window // 2)
        col = lax.broadcasted_iota(jnp.int32, (rows, ext), 1)
        band_ref[...] = jnp.where(col < first, 0.0, jnp.where(col < first + window, 1.0, 0.0)).astype(BF16)

        def chunk(c):
            base = pl.multiple_of(c * rows, rows)
            sums = jnp.dot(band_ref[...], pad_ref[pl.ds(base, ext), :], preferred_element_type=F32)
            tok = pad_ref[pl.ds(base + halo, rows), :].astype(F32)
            t = base + lax.broadcasted_iota(jnp.int32, (rows, 1), 0)
            lo = jnp.maximum(t - window // 2, 0)
            hi = jnp.minimum(t - window // 2 + window, seq)
            pooled_ref[pl.ds(base, rows), :] = (sums / (hi - lo).astype(F32) - tok).astype(BF16)

        def chunks(it, carry):
            for u in range(POOL_UNROLL):
                chunk(it * POOL_UNROLL + u)
            return carry

        lax.fori_loop(0, seq // rows // POOL_UNROLL, chunks, 0)

    for gi, window in enumerate(POOL_WINDOWS):
        pl.when(grp == gi)(functools.partial(run, window))

    y = jnp.dot(pooled_ref[...], w_ref[...], preferred_element_type=F32) * sc_ref[...]
    o_ref[...] = y.astype(o_ref.dtype)


def _pool(pg, w_pool, pool_scale, *, batch, seq, col0):
    blk0 = col0 // POOL_GROUP
    ngrp = len(POOL_WINDOWS)
    est = (2 * seq * POOL_GROUP * 2 * 2 + (seq + 2 * POOL_HALO) * POOL_GROUP * 4 + 2 * POOL_GROUP * POOL_GROUP * 2
           + seq * POOL_GROUP * (2 + 4))
    return pl.pallas_call(
        functools.partial(_pool_kernel, seq=seq),
        grid=(batch, ngrp),
        in_specs=[
            pl.BlockSpec((seq, POOL_GROUP), lambda b, g: (b, blk0 + g)),
            pl.BlockSpec((None, POOL_GROUP, POOL_GROUP), lambda b, g: (g, 0, 0)),
            pl.BlockSpec((1, POOL_GROUP), lambda b, g: (0, g)),
        ],
        out_specs=pl.BlockSpec((seq, POOL_GROUP), lambda b, g: (b, g)),
        out_shape=jax.ShapeDtypeStruct((batch * seq, D_POOL), BF16),
        scratch_shapes=[pltpu.VMEM((seq + 2 * POOL_HALO, POOL_GROUP), BF16),
                        pltpu.VMEM((POOL_ROWS, POOL_ROWS + 2 * POOL_HALO), BF16),
                        pltpu.VMEM((seq, POOL_GROUP), BF16)],
        compiler_params=pltpu.CompilerParams(
            dimension_semantics=("arbitrary", "arbitrary"), vmem_limit_bytes=_vmem_limit(est)),
        name="pool",
    )(pg, w_pool, pool_scale)


def _memattn_kernel(q_ref, kv_ref, o_ref):
    scale = XHEAD_DIM ** -0.5
    for hh in range(N_XHEADS):
        lo = hh * XHEAD_DIM
        q = q_ref[:, lo:lo + XHEAD_DIM]
        k = kv_ref[:, lo:lo + XHEAD_DIM]
        v = kv_ref[:, D_XATTN + lo:D_XATTN + lo + XHEAD_DIM]
        s = lax.dot_general(q, k, (((1,), (1,)), ((), ())), preferred_element_type=F32) * scale
        m = jnp.max(s, axis=-1, keepdims=True)
        p = jnp.exp(s - m)
        l = jnp.sum(p, axis=-1, keepdims=True)
        y = jnp.dot(p.astype(BF16), v, preferred_element_type=F32) / l
        o_ref[:, lo:lo + XHEAD_DIM] = y.astype(o_ref.dtype)


def _memattn(pg, kv, *, seq, col0):
    n = pg.shape[0]
    tm = MEM_TM
    blk0 = col0 // D_XATTN
    tiles_per_seq = seq // tm
    est = 2 * tm * D_XATTN * 2 * 2 + 2 * N_MEM * 2 * D_XATTN * 2 + 4 * tm * N_MEM * 4
    return pl.pallas_call(
        _memattn_kernel,
        grid=(n // tm,),
        in_specs=[
            pl.BlockSpec((tm, D_XATTN), lambda i: (i, blk0)),
            pl.BlockSpec((N_MEM, 2 * D_XATTN), lambda i: (i // tiles_per_seq, 0)),
        ],
        out_specs=pl.BlockSpec((tm, D_XATTN), lambda i: (i, 0)),
        out_shape=jax.ShapeDtypeStruct((n, D_XATTN), BF16),
        compiler_params=pltpu.CompilerParams(
            dimension_semantics=("arbitrary",), vmem_limit_bytes=_vmem_limit(est)),
        name="memattn",
    )(pg, kv)


def _merge_kernel(gates_ref, ya_ref, yp_ref, ym_ref, x_ref, wa_ref, wp_ref, wm_ref, wo_ref, gain_ref, o_ref):
    d = x_ref.shape[1]
    for r in range(0, x_ref.shape[0], MERGE_ROWS):
        rows = pl.ds(r, MERGE_ROWS)
        merged = gates_ref[rows, 0:d].astype(F32) * jnp.dot(ya_ref[rows, :], wa_ref[...], preferred_element_type=F32)
        merged += gates_ref[rows, d:2 * d].astype(F32) * jnp.dot(yp_ref[rows, :], wp_ref[...],
                                                                   preferred_element_type=F32)
        merged += gates_ref[rows, 2 * d:3 * d].astype(F32) * jnp.dot(ym_ref[rows, :], wm_ref[...],
                                                                       preferred_element_type=F32)
        z = jnp.dot(merged.astype(BF16), wo_ref[...], preferred_element_type=F32)
        o_ref[rows, :] = x_ref[rows, :] + _rms(z, gain_ref[...])


def _merge(gates, y_attn, y_pool, y_mem, x, wa, wp, wm, wo, gain):
    n, d = x.shape
    tm = MERGE_TM
    wbytes = (wa.size + wp.size + wm.size + wo.size) * 2

    def const(shape):
        return pl.BlockSpec(shape, lambda i: (0, 0), pipeline_mode=pl.Buffered(1))

    est = (2 * tm * (N_BRANCH * d + y_attn.shape[1] + y_pool.shape[1] + y_mem.shape[1]) * 2
           + 4 * tm * d * 4 + wbytes + 4 * tm * d * 4)
    return pl.pallas_call(
        _merge_kernel,
        grid=(n // tm,),
        in_specs=[
            pl.BlockSpec((tm, N_BRANCH * d), lambda i: (i, 0)),
            pl.BlockSpec((tm, y_attn.shape[1]), lambda i: (i, 0)),
            pl.BlockSpec((tm, y_pool.shape[1]), lambda i: (i, 0)),
            pl.BlockSpec((tm, y_mem.shape[1]), lambda i: (i, 0)),
            pl.BlockSpec((tm, d), lambda i: (i, 0)),
            const(wa.shape), const(wp.shape), const(wm.shape), const(wo.shape),
            pl.BlockSpec((1, d), lambda i: (0, 0)),
        ],
        out_specs=pl.BlockSpec((tm, d), lambda i: (i, 0)),
        out_shape=jax.ShapeDtypeStruct((n, d), F32),
        compiler_params=pltpu.CompilerParams(
            dimension_semantics=("arbitrary",), vmem_limit_bytes=_vmem_limit(est)),
        name="merge",
    )(gates, y_attn, y_pool, y_mem, x, wa, wp, wm, wo, gain)


def _cast_kernel(w_ref, *o_refs):
    lo = 0
    for o_ref in o_refs:
        width = o_ref.shape[1]
        o_ref[...] = w_ref[:, lo:lo + width].astype(o_ref.dtype)
        lo += width


def _cast_bf16(w, widths=None, target_bytes=CAST_BLOCK_BYTES):
    rows, cols = w.shape
    widths = (cols,) if widths is None else tuple(widths)
    br = rows
    while br * cols * 4 > target_bytes and br % 32 == 0:
        br //= 2
    est = 2 * br * cols * 4 + 2 * br * cols * 2
    outs = pl.pallas_call(
        _cast_kernel,
        grid=(rows // br,),
        in_specs=[pl.BlockSpec((br, cols), lambda i: (i, 0))],
        out_specs=[pl.BlockSpec((br, wd), lambda i: (i, 0)) for wd in widths],
        out_shape=[jax.ShapeDtypeStruct((rows, wd), BF16) for wd in widths],
        compiler_params=pltpu.CompilerParams(
            dimension_semantics=("arbitrary",), vmem_limit_bytes=_vmem_limit(est)),
        name="cast",
    )(w)
    return outs if len(widths) > 1 else outs[0]


def _cast_chunked_kernel(w_ref, *o_refs, width):
    for k, o_ref in enumerate(o_refs):
        nchunk, _, tf = o_ref.shape
        for c in range(nchunk):
            valid = min(tf, width - c * tf)
            lo = k * width + c * tf
            o_ref[c, :, :valid] = w_ref[:, lo:lo + valid].astype(o_ref.dtype)
            if valid < tf:
                o_ref[c, :, valid:] = jnp.zeros((o_ref.shape[1], tf - valid), o_ref.dtype)


def _cast_chunked(w, n_out, tf, target_bytes=CAST_BLOCK_BYTES):
    rows, cols = w.shape
    width = cols // n_out
    nchunk = pl.cdiv(width, tf)
    br = rows
    while br * cols * 4 > target_bytes and br % 32 == 0:
        br //= 2
    est = 2 * br * cols * 4 + 2 * n_out * nchunk * br * tf * 2
    return pl.pallas_call(
        functools.partial(_cast_chunked_kernel, width=width),
        grid=(rows // br,),
        in_specs=[pl.BlockSpec((br, cols), lambda i: (i, 0))],
        out_specs=[pl.BlockSpec((nchunk, br, tf), lambda i: (0, i, 0)) for _ in range(n_out)],
        out_shape=[jax.ShapeDtypeStruct((nchunk, rows, tf), BF16) for _ in range(n_out)],
        compiler_params=pltpu.CompilerParams(
            dimension_semantics=("arbitrary",), vmem_limit_bytes=_vmem_limit(est)),
        name="cast_chunked",
    )(w)


def _row(v):
    return v.reshape(1, -1).astype(F32)


def _trunk(x, mem, p, slopes):
    batch, seq, d = x.shape
    xf = x.reshape(batch * seq, d)
    x1 = _ffn(xf, p["ffn1_norm_pre"], p["ffn1_wg"], p["ffn1_wu"], p["ffn1_wd"], p["ffn1_norm_post"],
              p["ffn1_norm_post"], final=False)
    gates, nat, qkv4, qkv16 = _mixproj(x1, p["mix_norm_pre"], p["w_gate"], p["b_gate"], p["w_in"],
                                       batch=batch, seq=seq)
    col_pool = D_XATTN
    col_qkv0 = col_pool + D_POOL
    kv = _kvproj(mem.reshape(batch * N_MEM, d), p["mem_norm"], p["w_mem_kv"])
    y_attn = _attention(nat, qkv4, qkv16, slopes, batch=batch, seq=seq, col0=col_qkv0)
    y_pool = _pool(nat, p["w_pool"], p["pool_scale"], batch=batch, seq=seq, col0=col_pool)
    y_mem = _memattn(nat, kv, seq=seq, col0=0)
    x2 = _merge(gates, y_attn, y_pool, y_mem, x1, p["w_br_attn"], p["w_br_pool"], p["w_br_mem"], p["w_out"],
                p["mix_norm_post"])
    y = _ffn(x2, p["ffn2_norm_pre"], p["ffn2_wg"], p["ffn2_wu"], p["ffn2_wd"], p["ffn2_norm_post"],
             p["final_norm"], final=True)
    return y.reshape(batch, seq, d)


def _prep_params(ffn1_norm_pre, ffn1_w_up, ffn1_w_down, ffn1_norm_post, mix_norm_pre, mem_norm, w_in, w_mem_kv,
                 w_pool, pool_scale, w_br_attn, w_br_pool, w_br_mem, w_gate, b_gate, w_out, mix_norm_post,
                 ffn2_norm_pre, ffn2_w_up, ffn2_w_down, ffn2_norm_post, final_norm):
    p = {}
    for name, w_up, w_down in (("ffn1", ffn1_w_up, ffn1_w_down), ("ffn2", ffn2_w_up, ffn2_w_down)):
        dff = w_down.shape[0]
        p[name + "_wg"], p[name + "_wu"] = _cast_chunked(w_up, 2, FFN_TF)
        p[name + "_wd"] = _cast_bf16(w_down)
    p["w_gate"] = _cast_bf16(w_gate)
    p["w_in"] = _cast_bf16(w_in)
    p["b_gate"] = _row(b_gate)
    p["w_mem_kv"] = _cast_bf16(w_mem_kv)
    p["w_pool"] = w_pool.astype(BF16)
    p["pool_scale"] = _row(pool_scale)
    for name, w in (("w_br_attn", w_br_attn), ("w_br_pool", w_br_pool), ("w_br_mem", w_br_mem), ("w_out", w_out)):
        p[name] = _cast_bf16(w)
    for name, v in (("ffn1_norm_pre", ffn1_norm_pre), ("ffn1_norm_post", ffn1_norm_post),
                    ("mix_norm_pre", mix_norm_pre), ("mem_norm", mem_norm), ("mix_norm_post", mix_norm_post),
                    ("ffn2_norm_pre", ffn2_norm_pre), ("ffn2_norm_post", ffn2_norm_post),
                    ("final_norm", final_norm)):
        p[name] = _row(v)
    return p


def kernel(x_prompt, x_sample, mem_prompt, mem_sample, ffn1_norm_pre, ffn1_w_up, ffn1_w_down, ffn1_norm_post,
           mix_norm_pre, mem_norm, w_in, w_mem_kv, w_pool, pool_scale, w_br_attn, w_br_pool, w_br_mem, w_gate, b_gate,
           w_out, mix_norm_post, ffn2_norm_pre, ffn2_w_up, ffn2_w_down, ffn2_norm_post, final_norm):
    layer = [ffn1_norm_pre, ffn1_w_up, ffn1_w_down, ffn1_norm_post, mix_norm_pre, mem_norm, w_in, w_mem_kv, w_pool,
             pool_scale, w_br_attn, w_br_pool, w_br_mem, w_gate, b_gate, w_out, mix_norm_post, ffn2_norm_pre,
             ffn2_w_up, ffn2_w_down, ffn2_norm_post, final_norm]
    depth = ffn1_norm_pre.shape[0]
    slopes = jnp.asarray(_alibi_slopes())
    y_prompt, y_sample = x_prompt, x_sample
    for layer_idx in range(depth):
        p = _prep_params(*[w[layer_idx] for w in layer])
        y_prompt = _trunk(y_prompt, mem_prompt, p, slopes)
        y_sample = _trunk(y_sample, mem_sample, p, slopes)
    return (y_prompt, y_sample)
```

```python
import functools

import numpy as np
import jax
import jax.numpy as jnp
from jax import lax
from jax.experimental import pallas as pl
from jax.experimental.pallas import tpu as pltpu

F32 = jnp.float32
BF16 = jnp.bfloat16

D_MODEL = 2048
N_MEM = 256
HEAD_DIM = 128
DIL_PAIRS = ((128, 1), (512, 4), (2048, 16))
HEADS_PER_GROUP = 4
N_GROUPS = len(DIL_PAIRS)
N_DIL_HEADS = HEADS_PER_GROUP * N_GROUPS
D_ATTN = N_DIL_HEADS * HEAD_DIM
POOL_WINDOWS = (2, 4, 8, 16)
POOL_GROUP = 256
D_POOL = POOL_GROUP * len(POOL_WINDOWS)
N_XHEADS = 4
XHEAD_DIM = 256
D_XATTN = N_XHEADS * XHEAD_DIM
N_BRANCH = 3
EPS = 1e-6
NEG = -1e30
BAND_RADIUS = 64

V7X_VMEM_BYTES = 64 * 1024 * 1024
MIB = 1024 * 1024

FFN_TM = 1024
FFN_TF = 1024
PROJ_TM = 1024
PROJ_TN = 512
PROJ_TG = 1536
PROJ_ROWS = 256
DEINT_STRIDE = 4
ATT_QB = 128
ATT_KW = 256
ATT_UNROLL = 8
POOL_ROWS = 256
POOL_HALO = 64
POOL_UNROLL = 4
MEM_TM = 512
MERGE_TM = 512
MERGE_ROWS = 512
CAST_BLOCK_BYTES = 8 * MIB


def _vmem_limit(nbytes):
    return int(min(nbytes * 5 // 4 + 8 * MIB, V7X_VMEM_BYTES - 2 * MIB))


def _rms(x, gain):
    ms = jnp.mean(x * x, axis=-1, keepdims=True)
    return x * lax.rsqrt(ms + EPS) * gain


def _alibi_slopes():
    s = 2.0 ** (-8.0 * np.arange(1, N_DIL_HEADS + 1) / N_DIL_HEADS)
    return s.reshape(HEADS_PER_GROUP, N_GROUPS).T.astype(np.float32)


def _ffn_kernel(x_ref, gpre_ref, wg_ref, wu_ref, wd_ref, gpost_ref, gfin_ref, o_hbm, acc_ref, hs_ref, sem,
                *, tf, rem, final):
    i = pl.program_id(0)
    f = pl.program_id(1)
    n_tiles = pl.num_programs(0)
    last = pl.num_programs(1) - 1
    tm = acc_ref.shape[0]

    def writeback(tile):
        return pltpu.make_async_copy(acc_ref, o_hbm.at[pl.ds(pl.multiple_of(tile * tm, tm), tm), :], sem)

    @pl.when(f == 0)
    def _():
        hs_ref[...] = _rms(x_ref[...], gpre_ref[...]).astype(BF16)

    def step(valid, first):
        hs = hs_ref[...]
        g = jnp.dot(hs, wg_ref[:, :valid], preferred_element_type=F32)
        u = jnp.dot(hs, wu_ref[:, :valid], preferred_element_type=F32)
        a = (g * jax.nn.sigmoid(g) * u).astype(BF16)
        if first:
            @pl.when(i > 0)
            def _():
                writeback(i - 1).wait()

            acc_ref[...] = jnp.dot(a, wd_ref[:valid, :], preferred_element_type=F32)
        else:
            acc_ref[...] += jnp.dot(a, wd_ref[:valid, :], preferred_element_type=F32)

    pl.when(f == 0)(functools.partial(step, tf, True))
    pl.when((f > 0) & (f < last))(functools.partial(step, tf, False))
    pl.when(f == last)(functools.partial(step, rem, False))

    @pl.when(f == last)
    def _():
        y = x_ref[...] + 0.5 * _rms(acc_ref[...], gpost_ref[...])
        if final:
            y = _rms(y, gfin_ref[...])
        acc_ref[...] = y
        writeback(i).start()

        @pl.when(i == n_tiles - 1)
        def _():
            writeback(i).wait()


def _ffn(x, gpre, wg, wu, wd, gpost, gfin, *, final):
    n, d = x.shape
    tm, tf = FFN_TM, FFN_TF
    dff = wd.shape[0]
    nf = pl.cdiv(dff, tf)
    rem = dff - (nf - 1) * tf
    assert nf >= 2
    est = 2 * tm * d * 4 + tm * d * 4 + tm * d * 2 + 2 * 3 * d * tf * 2 + 2 * tm * tf * 4
    vec = pl.BlockSpec((1, d), lambda i, f: (0, 0))
    return pl.pallas_call(
        functools.partial(_ffn_kernel, tf=tf, rem=rem, final=final),
        grid=(n // tm, nf),
        in_specs=[
            pl.BlockSpec((tm, d), lambda i, f: (i, 0)),
            vec,
            pl.BlockSpec((None, d, tf), lambda i, f: (f, 0, 0)),
            pl.BlockSpec((None, d, tf), lambda i, f: (f, 0, 0)),
            pl.BlockSpec((tf, d), lambda i, f: (f, 0)),
            vec,
            vec,
        ],
        out_specs=pl.BlockSpec(memory_space=pl.ANY),
        out_shape=jax.ShapeDtypeStruct((n, d), F32),
        scratch_shapes=[pltpu.VMEM((tm, d), F32), pltpu.VMEM((tm, d), BF16), pltpu.SemaphoreType.DMA(())],
        compiler_params=pltpu.CompilerParams(
            dimension_semantics=("arbitrary", "arbitrary"), vmem_limit_bytes=_vmem_limit(est)),
        name="ffn_final" if final else "ffn",
    )(x, gpre, wg, wu, wd, gpost, gfin)


def _kvproj_kernel(x_ref, g_ref, w_ref, o_ref, hs_ref):
    @pl.when(pl.program_id(1) == 0)
    def _():
        hs_ref[...] = _rms(x_ref[...], g_ref[...]).astype(BF16)

    o_ref[...] = jnp.dot(hs_ref[...], w_ref[...], preferred_element_type=F32).astype(o_ref.dtype)


def _kvproj(x, gain, w):
    n, d = x.shape
    ncol = w.shape[1]
    tm, tn = n, PROJ_TN
    est = 2 * tm * d * 4 + tm * d * 2 + 2 * d * tn * 2 + 2 * tm * tn * 2 + 2 * tm * tn * 4
    return pl.pallas_call(
        _kvproj_kernel,
        grid=(n // tm, ncol // tn),
        in_specs=[
            pl.BlockSpec((tm, d), lambda i, j: (i, 0)),
            pl.BlockSpec((1, d), lambda i, j: (0, 0)),
            pl.BlockSpec((d, tn), lambda i, j: (0, j)),
        ],
        out_specs=pl.BlockSpec((tm, tn), lambda i, j: (i, j)),
        out_shape=jax.ShapeDtypeStruct((n, ncol), BF16),
        scratch_shapes=[pltpu.VMEM((tm, d), BF16)],
        compiler_params=pltpu.CompilerParams(
            dimension_semantics=("arbitrary", "arbitrary"), vmem_limit_bytes=_vmem_limit(est)),
        name="kvproj",
    )(x, gain, w)


NAT_CHUNKS = (11, 12, 9, 10, 0, 3, 6)
N_NAT_STEPS = len(NAT_CHUNKS)
N_DIL_STEPS = 3
N_GATE_STEPS = N_BRANCH * D_MODEL // PROJ_TG
FIRST_D4 = N_NAT_STEPS
FIRST_D16 = FIRST_D4 + N_DIL_STEPS
FIRST_GATE = FIRST_D16 + N_DIL_STEPS
MIX_STEPS = FIRST_GATE + N_GATE_STEPS


def _lin_chunk(s):
    t = jnp.minimum(s, FIRST_GATE - 1)
    nat = jnp.where(t < 2, 11 + t, jnp.where(t < 4, 7 + t, 3 * (t - 4)))
    d4 = 3 * (t - FIRST_D4) + 1
    d16 = 3 * (t - FIRST_D16) + 2
    return jnp.where(t < FIRST_D4, nat, jnp.where(t < FIRST_D16, d4, d16))


def _mixproj_kernel(x_ref, g_ref, wg_ref, b_ref, wi_ref, og_ref, on_ref, o4_ref, o16_ref, hs_ref, stage_ref,
                    stage2_ref):
    s = pl.program_id(1)
    tm = x_ref.shape[0]
    row_chunks = [pl.ds(r, PROJ_ROWS) for r in range(0, tm, PROJ_ROWS)]

    @pl.when(s == 0)
    def _():
        hs_ref[...] = _rms(x_ref[...], g_ref[...]).astype(BF16)

    @pl.when(s < FIRST_D4)
    def _():
        for rows in row_chunks:
            on_ref[rows, :] = jnp.dot(hs_ref[rows, :], wi_ref[...], preferred_element_type=F32).astype(on_ref.dtype)

    def dilated(o_ref, d):
        per_res = PROJ_ROWS // d
        slabs = range(PROJ_TN // HEAD_DIM)
        for rc, rows in enumerate(row_chunks):
            acc = jnp.dot(hs_ref[rows, :], wi_ref[...], preferred_element_type=F32)
            for c in slabs:
                stage_ref[c] = acc[:, c * HEAD_DIM:(c + 1) * HEAD_DIM]
            if d == DEINT_STRIDE:
                for res in range(d):
                    for c in slabs:
                        o_ref[res, pl.ds(rc * per_res, per_res), c * HEAD_DIM:(c + 1) * HEAD_DIM] = (
                            stage_ref[c, pl.ds(res, per_res, stride=d), :].astype(o_ref.dtype))
            else:
                for c in slabs:
                    for r1 in range(DEINT_STRIDE):
                        stage2_ref[c, r1] = stage_ref[c, pl.ds(r1, PROJ_ROWS // DEINT_STRIDE, stride=DEINT_STRIDE), :]
                for r1 in range(DEINT_STRIDE):
                    for r2 in range(DEINT_STRIDE):
                        res = r1 + DEINT_STRIDE * r2
                        for c in slabs:
                            o_ref[res, pl.ds(rc * per_res, per_res), c * HEAD_DIM:(c + 1) * HEAD_DIM] = (
                                stage2_ref[c, r1, pl.ds(r2, per_res, stride=DEINT_STRIDE), :].astype(o_ref.dtype))

    pl.when((s >= FIRST_D4) & (s < FIRST_D16))(functools.partial(dilated, o4_ref, DIL_PAIRS[1][1]))
    pl.when((s >= FIRST_D16) & (s < FIRST_GATE))(functools.partial(dilated, o16_ref, DIL_PAIRS[2][1]))

    @pl.when(s >= FIRST_GATE)
    def _():
        for rows in row_chunks:
            z = jnp.dot(hs_ref[rows, :], wg_ref[...], preferred_element_type=F32) + b_ref[...]
            og_ref[rows, :] = (0.5 * jnp.tanh(0.5 * z) + 0.5).astype(og_ref.dtype)


def _mixproj(x, gain, w_gate, b_gate, w_in, *, batch, seq):
    n, d = x.shape
    tm, tn, tg = PROJ_TM, PROJ_TN, PROJ_TG
    tiles_per_seq = seq // tm
    d4, d16 = DIL_PAIRS[1][1], DIL_PAIRS[2][1]

    def gate_idx(s):
        return jnp.maximum(s - FIRST_GATE, 0)

    def dil_spec(dil, first):
        return pl.BlockSpec(
            (None, dil, tm // dil, tn),
            lambda i, s: (i // tiles_per_seq, 0, i % tiles_per_seq, jnp.clip(s - first, 0, N_DIL_STEPS - 1)))

    est = (2 * tm * d * 4 + tm * d * 2 + 2 * d * (tg + tn) * 2 + 2 * tm * (tg + 3 * tn) * 2
           + PROJ_ROWS * tn * 4 + 2 * PROJ_ROWS * tg * 4)
    return pl.pallas_call(
        _mixproj_kernel,
        grid=(n // tm, MIX_STEPS),
        in_specs=[
            pl.BlockSpec((tm, d), lambda i, s: (i, 0)),
            pl.BlockSpec((1, d), lambda i, s: (0, 0)),
            pl.BlockSpec((d, tg), lambda i, s: (0, gate_idx(s))),
            pl.BlockSpec((1, tg), lambda i, s: (0, gate_idx(s))),
            pl.BlockSpec((d, tn), lambda i, s: (0, _lin_chunk(s))),
        ],
        out_specs=[
            pl.BlockSpec((tm, tg), lambda i, s: (i, gate_idx(s))),
            pl.BlockSpec((tm, tn), lambda i, s: (i, jnp.minimum(s, N_NAT_STEPS - 1))),
            dil_spec(d4, FIRST_D4),
            dil_spec(d16, FIRST_D16),
        ],
        out_shape=[
            jax.ShapeDtypeStruct((n, N_BRANCH * d), BF16),
            jax.ShapeDtypeStruct((n, N_NAT_STEPS * tn), BF16),
            jax.ShapeDtypeStruct((batch, d4, seq // d4, N_DIL_STEPS * tn), BF16),
            jax.ShapeDtypeStruct((batch, d16, seq // d16, N_DIL_STEPS * tn), BF16),
        ],
        scratch_shapes=[
            pltpu.VMEM((tm, d), BF16),
            pltpu.VMEM((tn // HEAD_DIM, PROJ_ROWS, HEAD_DIM), F32),
            pltpu.VMEM((tn // HEAD_DIM, DEINT_STRIDE, PROJ_ROWS // DEINT_STRIDE, HEAD_DIM), F32),
        ],
        compiler_params=pltpu.CompilerParams(
            dimension_semantics=("arbitrary", "arbitrary"), vmem_limit_bytes=_vmem_limit(est)),
        name="mixproj",
    )(x, gain, w_gate, b_gate, w_in)


def _attn_kernel(slopes_ref, q0_ref, k0_ref, v0_ref, q1_ref, k1_ref, v1_ref, q2_ref, k2_ref, v2_ref, o_ref,
                 m1_ref, m2_ref, l1_ref, l2_ref, n1_ref, n2_ref, va_ref, bias_ref, *, seq):
    h = pl.program_id(1)
    scale = HEAD_DIM ** -0.5
    qb = ATT_QB
    qkv_refs = ((q0_ref, k0_ref, v0_ref), (q1_ref, k1_ref, v1_ref), (q2_ref, k2_ref, v2_ref))
    m_refs = (None, m1_ref, m2_ref)
    l_refs = (None, l1_ref, l2_ref)
    n_refs = (None, n1_ref, n2_ref)

    va_ref[:, HEAD_DIM:] = jnp.ones((seq, HEAD_DIM), BF16)

    for gi in reversed(range(N_GROUPS)):
        d = DIL_PAIRS[gi][1]
        q_ref, k_ref, v_ref = qkv_refs[gi]
        sub_len = seq // d
        kw = min(ATT_KW, sub_len)
        nb = sub_len // qb
        slope_d = slopes_ref[gi, h] * float(d)

        row = lax.broadcasted_iota(jnp.int32, (qb, kw), 0)
        col = lax.broadcasted_iota(jnp.int32, (qb, kw), 1)
        for kind, off in enumerate((0, -BAND_RADIUS, qb - kw)):
            arel = jnp.abs(col - row + off)
            bias_ref[kind, :, :kw] = jnp.where(arel <= BAND_RADIUS, -slope_d * arel.astype(F32), NEG)

        if d == 1:
            va_ref[:, :HEAD_DIM] = v_ref[...]
        else:
            for res in range(d):
                va_ref[pl.ds(res * sub_len, sub_len), :HEAD_DIM] = v_ref[res]

        def block(idx, gi=gi, d=d, q_ref=q_ref, k_ref=k_ref, sub_len=sub_len, kw=kw, nb=nb):
            res = idx // nb
            i = idx - res * nb
            q0 = pl.multiple_of(i * qb, qb)
            k0 = pl.multiple_of(jnp.clip(q0 - BAND_RADIUS, 0, sub_len - kw), BAND_RADIUS)
            if d == 1:
                qblk = q_ref[pl.ds(q0, qb), :]
                kwin = k_ref[pl.ds(k0, kw), :]
            else:
                qblk = q_ref[res, pl.ds(q0, qb), :]
                kwin = k_ref[res, pl.ds(k0, kw), :]
            s = lax.dot_general(qblk, kwin, (((1,), (1,)), ((), ())), preferred_element_type=F32)
            if nb == 1:
                bias = bias_ref[0, :, :kw]
            else:
                kind = jnp.where(i == 0, 0, jnp.where(i == nb - 1, 2, 1))
                bias = bias_ref[kind]
            s = s * scale + bias
            m = jnp.max(s, axis=-1, keepdims=True)
            p = jnp.exp(s - m).astype(BF16)
            kstart = pl.multiple_of(res * sub_len + k0, BAND_RADIUS)
            na = jnp.dot(p, va_ref[pl.ds(kstart, kw), :], preferred_element_type=F32)
            m = jnp.broadcast_to(m, (qb, HEAD_DIM))
            if d > 1:
                rows = pl.ds(q0 * d + res, qb, stride=d)
                m_refs[gi][rows, :] = m
                l_refs[gi][rows, :] = na[:, HEAD_DIM:]
                n_refs[gi][rows, :] = na[:, :HEAD_DIM]
            else:
                rows = pl.ds(q0, qb)
                ms = [m] + [m_refs[g][rows, :] for g in range(1, N_GROUPS)]
                mx = functools.reduce(jnp.maximum, ms)
                e = jnp.exp(m - mx)
                num = e * na[:, :HEAD_DIM]
                den = e * na[:, HEAD_DIM:]
                for g in range(1, N_GROUPS):
                    e = jnp.exp(ms[g] - mx)
                    num += e * n_refs[g][rows, :]
                    den += e * l_refs[g][rows, :]
                o_ref[rows, :] = (num / den).astype(o_ref.dtype)

        def blocks(it, carry, block=block):
            for u in range(ATT_UNROLL):
                block(it * ATT_UNROLL + u)
            return carry

        lax.fori_loop(0, d * nb // ATT_UNROLL, blocks, 0)


def _attention(nat, qkv4, qkv16, slopes, *, batch, seq, col0):
    blk0 = col0 // HEAD_DIM
    d4, d16 = DIL_PAIRS[1][1], DIL_PAIRS[2][1]

    def nat_map(b, h, *, which):
        return (b, blk0 + which * HEADS_PER_GROUP + h)

    def dil_map(b, h, *, which):
        return (b, 0, 0, which * HEADS_PER_GROUP + h)

    in_specs = [pl.BlockSpec(memory_space=pltpu.SMEM)]
    in_specs += [pl.BlockSpec((seq, HEAD_DIM), functools.partial(nat_map, which=w)) for w in range(3)]
    for dil in (d4, d16):
        in_specs += [pl.BlockSpec((None, dil, seq // dil, HEAD_DIM), functools.partial(dil_map, which=w))
                     for w in range(3)]
    tile = seq * HEAD_DIM
    est = 10 * 2 * tile * 2 + 6 * tile * 4 + 2 * tile * 2 + 3 * ATT_QB * ATT_KW * 4
    state = [pltpu.VMEM((seq, HEAD_DIM), F32) for _ in range(3 * (N_GROUPS - 1))]
    return pl.pallas_call(
        functools.partial(_attn_kernel, seq=seq),
        grid=(batch, HEADS_PER_GROUP),
        in_specs=in_specs,
        out_specs=pl.BlockSpec((seq, HEAD_DIM), lambda b, h: (b, h)),
        out_shape=jax.ShapeDtypeStruct((batch * seq, HEADS_PER_GROUP * HEAD_DIM), BF16),
        scratch_shapes=state + [
            pltpu.VMEM((seq, 2 * HEAD_DIM), BF16),
            pltpu.VMEM((3, ATT_QB, ATT_KW), F32),
        ],
        compiler_params=pltpu.CompilerParams(
            dimension_semantics=("arbitrary", "arbitrary"), vmem_limit_bytes=_vmem_limit(est)),
        name="attention",
    )(slopes, nat, nat, nat, qkv4, qkv4, qkv4, qkv16, qkv16, qkv16)


def _pool_kernel(u_ref, w_ref, sc_ref, o_ref, pad_ref, band_ref, pooled_ref, *, seq):
    grp = pl.program_id(1)
    halo, rows = POOL_HALO, POOL_ROWS
    ext = rows + 2 * halo
    zeros = jnp.zeros((halo, POOL_GROUP), BF16)
    pad_ref[pl.ds(0, halo), :] = zeros
    pad_ref[pl.ds(halo + seq, halo), :] = zeros
    pad_ref[pl.ds(halo, seq), :] = u_ref[...]

    def run(window):
        first = lax.broadcasted_iota(jnp.int32, (rows, ext), 0) + (halo - window // 2)
        col = lax.broadcasted_iota(jnp.int32, (rows, ext), 1)
        band_ref[...] = jnp.where(col < first, 0.0, jnp.where(col < first + window, 1.0, 0.0)).astype(BF16)

        def chunk(c):
            base = pl.multiple_of(c * rows, rows)
            sums = jnp.dot(band_ref[...], pad_ref[pl.ds(base, ext), :], preferred_element_type=F32)
            tok = pad_ref[pl.ds(base + halo, rows), :].astype(F32)
            t = base + lax.broadcasted_iota(jnp.int32, (rows, 1), 0)
            lo = jnp.maximum(t - window // 2, 0)
            hi = jnp.minimum(t - window // 2 + window, seq)
            pooled_ref[pl.ds(base, rows), :] = (sums / (hi - lo).astype(F32) - tok).astype(BF16)

        def chunks(it, carry):
            for u in range(POOL_UNROLL):
                chunk(it * POOL_UNROLL + u)
            return carry

        lax.fori_loop(0, seq // rows // POOL_UNROLL, chunks, 0)

    for gi, window in enumerate(POOL_WINDOWS):
        pl.when(grp == gi)(functools.partial(run, window))

    y = jnp.dot(pooled_ref[...], w_ref[...], preferred_element_type=F32) * sc_ref[...]
    o_ref[...] = y.astype(o_ref.dtype)


def _pool(pg, w_pool, pool_scale, *, batch, seq, col0):
    blk0 = col0 // POOL_GROUP
    ngrp = len(POOL_WINDOWS)
    est = (2 * seq * POOL_GROUP * 2 * 2 + (seq + 2 * POOL_HALO) * POOL_GROUP * 4 + 2 * POOL_GROUP * POOL_GROUP * 2
           + seq * POOL_GROUP * (2 + 4))
    return pl.pallas_call(
        functools.partial(_pool_kernel, seq=seq),
        grid=(batch, ngrp),
        in_specs=[
            pl.BlockSpec((seq, POOL_GROUP), lambda b, g: (b, blk0 + g)),
            pl.BlockSpec((None, POOL_GROUP, POOL_GROUP), lambda b, g: (g, 0, 0)),
            pl.BlockSpec((1, POOL_GROUP), lambda b, g: (0, g)),
        ],
        out_specs=pl.BlockSpec((seq, POOL_GROUP), lambda b, g: (b, g)),
        out_shape=jax.ShapeDtypeStruct((batch * seq, D_POOL), BF16),
        scratch_shapes=[pltpu.VMEM((seq + 2 * POOL_HALO, POOL_GROUP), BF16),
                        pltpu.VMEM((POOL_ROWS, POOL_ROWS + 2 * POOL_HALO), BF16),
                        pltpu.VMEM((seq, POOL_GROUP), BF16)],
        compiler_params=pltpu.CompilerParams(
            dimension_semantics=("arbitrary", "arbitrary"), vmem_limit_bytes=_vmem_limit(est)),
        name="pool",
    )(pg, w_pool, pool_scale)


def _memattn_kernel(q_ref, kv_ref, o_ref):
    scale = XHEAD_DIM ** -0.5
    for hh in range(N_XHEADS):
        lo = hh * XHEAD_DIM
        q = q_ref[:, lo:lo + XHEAD_DIM]
        k = kv_ref[:, lo:lo + XHEAD_DIM]
        v = kv_ref[:, D_XATTN + lo:D_XATTN + lo + XHEAD_DIM]
        s = lax.dot_general(q, k, (((1,), (1,)), ((), ())), preferred_element_type=F32) * scale
        m = jnp.max(s, axis=-1, keepdims=True)
        p = jnp.exp(s - m)
        l = jnp.sum(p, axis=-1, keepdims=True)
        y = jnp.dot(p.astype(BF16), v, preferred_element_type=F32) / l
        o_ref[:, lo:lo + XHEAD_DIM] = y.astype(o_ref.dtype)


def _memattn(pg, kv, *, seq, col0):
    n = pg.shape[0]
    tm = MEM_TM
    blk0 = col0 // D_XATTN
    tiles_per_seq = seq // tm
    est = 2 * tm * D_XATTN * 2 * 2 + 2 * N_MEM * 2 * D_XATTN * 2 + 4 * tm * N_MEM * 4
    return pl.pallas_call(
        _memattn_kernel,
        grid=(n // tm,),
        in_specs=[
            pl.BlockSpec((tm, D_XATTN), lambda i: (i, blk0)),
            pl.BlockSpec((N_MEM, 2 * D_XATTN), lambda i: (i // tiles_per_seq, 0)),
        ],
        out_specs=pl.BlockSpec((tm, D_XATTN), lambda i: (i, 0)),
        out_shape=jax.ShapeDtypeStruct((n, D_XATTN), BF16),
        compiler_params=pltpu.CompilerParams(
            dimension_semantics=("arbitrary",), vmem_limit_bytes=_vmem_limit(est)),
        name="memattn",
    )(pg, kv)


def _merge_kernel(gates_ref, ya_ref, yp_ref, ym_ref, x_ref, wa_ref, wp_ref, wm_ref, wo_ref, gain_ref, o_ref):
    d = x_ref.shape[1]
    for r in range(0, x_ref.shape[0], MERGE_ROWS):
        rows = pl.ds(r, MERGE_ROWS)
        merged = gates_ref[rows, 0:d].astype(F32) * jnp.dot(ya_ref[rows, :], wa_ref[...], preferred_element_type=F32)
        merged += gates_ref[rows, d:2 * d].astype(F32) * jnp.dot(yp_ref[rows, :], wp_ref[...],
                                                                   preferred_element_type=F32)
        merged += gates_ref[rows, 2 * d:3 * d].astype(F32) * jnp.dot(ym_ref[rows, :], wm_ref[...],
                                                                       preferred_element_type=F32)
        z = jnp.dot(merged.astype(BF16), wo_ref[...], preferred_element_type=F32)
        o_ref[rows, :] = x_ref[rows, :] + _rms(z, gain_ref[...])


def _merge(gates, y_attn, y_pool, y_mem, x, wa, wp, wm, wo, gain):
    n, d = x.shape
    tm = MERGE_TM
    wbytes = (wa.size + wp.size + wm.size + wo.size) * 2

    def const(shape):
        return pl.BlockSpec(shape, lambda i: (0, 0), pipeline_mode=pl.Buffered(1))

    est = (2 * tm * (N_BRANCH * d + y_attn.shape[1] + y_pool.shape[1] + y_mem.shape[1]) * 2
           + 4 * tm * d * 4 + wbytes + 4 * tm * d * 4)
    return pl.pallas_call(
        _merge_kernel,
        grid=(n // tm,),
        in_specs=[
            pl.BlockSpec((tm, N_BRANCH * d), lambda i: (i, 0)),
            pl.BlockSpec((tm, y_attn.shape[1]), lambda i: (i, 0)),
            pl.BlockSpec((tm, y_pool.shape[1]), lambda i: (i, 0)),
            pl.BlockSpec((tm, y_mem.shape[1]), lambda i: (i, 0)),
            pl.BlockSpec((tm, d), lambda i: (i, 0)),
            const(wa.shape), const(wp.shape), const(wm.shape), const(wo.shape),
            pl.BlockSpec((1, d), lambda i: (0, 0)),
        ],
        out_specs=pl.BlockSpec((tm, d), lambda i: (i, 0)),
        out_shape=jax.ShapeDtypeStruct((n, d), F32),
        compiler_params=pltpu.CompilerParams(
            dimension_semantics=("arbitrary",), vmem_limit_bytes=_vmem_limit(est)),
        name="merge",
    )(gates, y_attn, y_pool, y_mem, x, wa, wp, wm, wo, gain)


def _cast_kernel(w_ref, *o_refs):
    lo = 0
    for o_ref in o_refs:
        width = o_ref.shape[1]
        o_ref[...] = w_ref[:, lo:lo + width].astype(o_ref.dtype)
        lo += width


def _cast_bf16(w, widths=None, target_bytes=CAST_BLOCK_BYTES):
    rows, cols = w.shape
    widths = (cols,) if widths is None else tuple(widths)
    br = rows
    while br * cols * 4 > target_bytes and br % 32 == 0:
        br //= 2
    est = 2 * br * cols * 4 + 2 * br * cols * 2
    outs = pl.pallas_call(
        _cast_kernel,
        grid=(rows // br,),
        in_specs=[pl.BlockSpec((br, cols), lambda i: (i, 0))],
        out_specs=[pl.BlockSpec((br, wd), lambda i: (i, 0)) for wd in widths],
        out_shape=[jax.ShapeDtypeStruct((rows, wd), BF16) for wd in widths],
        compiler_params=pltpu.CompilerParams(
            dimension_semantics=("arbitrary",), vmem_limit_bytes=_vmem_limit(est)),
        name="cast",
    )(w)
    return outs if len(widths) > 1 else outs[0]


def _cast_chunked_kernel(w_ref, *o_refs, width):
    for k, o_ref in enumerate(o_refs):
        nchunk, _, tf = o_ref.shape
        for c in range(nchunk):
            valid = min(tf, width - c * tf)
            lo = k * width + c * tf
            o_ref[c, :, :valid] = w_ref[:, lo:lo + valid].astype(o_ref.dtype)
            if valid < tf:
                o_ref[c, :, valid:] = jnp.zeros((o_ref.shape[1], tf - valid), o_ref.dtype)


def _cast_chunked(w, n_out, tf, target_bytes=CAST_BLOCK_BYTES):
    rows, cols = w.shape
    width = cols // n_out
    nchunk = pl.cdiv(width, tf)
    br = rows
    while br * cols * 4 > target_bytes and br % 32 == 0:
        br //= 2
    est = 2 * br * cols * 4 + 2 * n_out * nchunk * br * tf * 2
    return pl.pallas_call(
        functools.partial(_cast_chunked_kernel, width=width),
        grid=(rows // br,),
        in_specs=[pl.BlockSpec((br, cols), lambda i: (i, 0))],
        out_specs=[pl.BlockSpec((nchunk, br, tf), lambda i: (0, i, 0)) for _ in range(n_out)],
        out_shape=[jax.ShapeDtypeStruct((nchunk, rows, tf), BF16) for _ in range(n_out)],
        compiler_params=pltpu.CompilerParams(
            dimension_semantics=("arbitrary",), vmem_limit_bytes=_vmem_limit(est)),
        name="cast_chunked",
    )(w)


def _row(v):
    return v.reshape(1, -1).astype(F32)


def _trunk(x, mem, p, slopes):
    batch, seq, d = x.shape
    xf = x.reshape(batch * seq, d)
    x1 = _ffn(xf, p["ffn1_norm_pre"], p["ffn1_wg"], p["ffn1_wu"], p["ffn1_wd"], p["ffn1_norm_post"],
              p["ffn1_norm_post"], final=False)
    gates, nat, qkv4, qkv16 = _mixproj(x1, p["mix_norm_pre"], p["w_gate"], p["b_gate"], p["w_in"],
                                       batch=batch, seq=seq)
    col_pool = D_XATTN
    col_qkv0 = col_pool + D_POOL
    kv = _kvproj(mem.reshape(batch * N_MEM, d), p["mem_norm"], p["w_mem_kv"])
    y_attn = _attention(nat, qkv4, qkv16, slopes, batch=batch, seq=seq, col0=col_qkv0)
    y_pool = _pool(nat, p["w_pool"], p["pool_scale"], batch=batch, seq=seq, col0=col_pool)
    y_mem = _memattn(nat, kv, seq=seq, col0=0)
    x2 = _merge(gates, y_attn, y_pool, y_mem, x1, p["w_br_attn"], p["w_br_pool"], p["w_br_mem"], p["w_out"],
                p["mix_norm_post"])
    y = _ffn(x2, p["ffn2_norm_pre"], p["ffn2_wg"], p["ffn2_wu"], p["ffn2_wd"], p["ffn2_norm_post"],
             p["final_norm"], final=True)
    return y.reshape(batch, seq, d)


def _prep_params(ffn1_norm_pre, ffn1_w_up, ffn1_w_down, ffn1_norm_post, mix_norm_pre, mem_norm, w_in, w_mem_kv,
                 w_pool, pool_scale, w_br_attn, w_br_pool, w_br_mem, w_gate, b_gate, w_out, mix_norm_post,
                 ffn2_norm_pre, ffn2_w_up, ffn2_w_down, ffn2_norm_post, final_norm):
    p = {}
    for name, w_up, w_down in (("ffn1", ffn1_w_up, ffn1_w_down), ("ffn2", ffn2_w_up, ffn2_w_down)):
        dff = w_down.shape[0]
        p[name + "_wg"], p[name + "_wu"] = _cast_chunked(w_up, 2, FFN_TF)
        p[name + "_wd"] = _cast_bf16(w_down)
    p["w_gate"] = _cast_bf16(w_gate)
    p["w_in"] = _cast_bf16(w_in)
    p["b_gate"] = _row(b_gate)
    p["w_mem_kv"] = _cast_bf16(w_mem_kv)
    p["w_pool"] = w_pool.astype(BF16)
    p["pool_scale"] = _row(pool_scale)
    for name, w in (("w_br_attn", w_br_attn), ("w_br_pool", w_br_pool), ("w_br_mem", w_br_mem), ("w_out", w_out)):
        p[name] = _cast_bf16(w)
    for name, v in (("ffn1_norm_pre", ffn1_norm_pre), ("ffn1_norm_post", ffn1_norm_post),
                    ("mix_norm_pre", mix_norm_pre), ("mem_norm", mem_norm), ("mix_norm_post", mix_norm_post),
                    ("ffn2_norm_pre", ffn2_norm_pre), ("ffn2_norm_post", ffn2_norm_post),
                    ("final_norm", final_norm)):
        p[name] = _row(v)
    return p


def kernel(x_prompt, x_sample, mem_prompt, mem_sample, ffn1_norm_pre, ffn1_w_up, ffn1_w_down, ffn1_norm_post,
           mix_norm_pre, mem_norm, w_in, w_mem_kv, w_pool, pool_scale, w_br_attn, w_br_pool, w_br_mem, w_gate, b_gate,
           w_out, mix_norm_post, ffn2_norm_pre, ffn2_w_up, ffn2_w_down, ffn2_norm_post, final_norm):
    layer = [ffn1_norm_pre, ffn1_w_up, ffn1_w_down, ffn1_norm_post, mix_norm_pre, mem_norm, w_in, w_mem_kv, w_pool,
             pool_scale, w_br_attn, w_br_pool, w_br_mem, w_gate, b_gate, w_out, mix_norm_post, ffn2_norm_pre,
             ffn2_w_up, ffn2_w_down, ffn2_norm_post, final_norm]
    depth = ffn1_norm_pre.shape[0]
    slopes = jnp.asarray(_alibi_slopes())
    y_prompt, y_sample = x_prompt, x_sample
    for layer_idx in range(depth):
        p = _prep_params(*[w[layer_idx] for w in layer])
        y_prompt = _trunk(y_prompt, mem_prompt, p, slopes)
        y_sample = _trunk(y_sample, mem_sample, p, slopes)
    return (y_prompt, y_sample)
```

```python
import functools

import numpy as np
import jax
import jax.numpy as jnp
from jax import lax
from jax.experimental import pallas as pl
from jax.experimental.pallas import tpu as pltpu

F32 = jnp.float32
BF16 = jnp.bfloat16

D_MODEL = 2048
N_MEM = 256
HEAD_DIM = 128
DIL_PAIRS = ((128, 1), (512, 4), (2048, 16))
HEADS_PER_GROUP = 4
N_GROUPS = len(DIL_PAIRS)
N_DIL_HEADS = HEADS_PER_GROUP * N_GROUPS
D_ATTN = N_DIL_HEADS * HEAD_DIM
POOL_WINDOWS = (2, 4, 8, 16)
POOL_GROUP = 256
D_POOL = POOL_GROUP * len(POOL_WINDOWS)
N_XHEADS = 4
XHEAD_DIM = 256
D_XATTN = N_XHEADS * XHEAD_DIM
N_BRANCH = 3
EPS = 1e-6
NEG = -1e30
BAND_RADIUS = 64

V7X_VMEM_BYTES = 64 * 1024 * 1024
MIB = 1024 * 1024

FFN_TM = 1024
FFN_TF = 1024
PROJ_TM = 1024
PROJ_TN = 512
PROJ_TG = 1536
PROJ_ROWS = 256
DEINT_STRIDE = 4
ATT_QB = 128
ATT_KW = 256
ATT_UNROLL = 8
POOL_ROWS = 256
POOL_HALO = 64
POOL_UNROLL = 4
MEM_TM = 512
MERGE_TM = 512
MERGE_ROWS = 512
CAST_BLOCK_BYTES = 8 * MIB


def _vmem_limit(nbytes):
    return int(min(nbytes * 5 // 4 + 8 * MIB, V7X_VMEM_BYTES - 2 * MIB))


def _rms(x, gain):
    ms = jnp.mean(x * x, axis=-1, keepdims=True)
    return x * lax.rsqrt(ms + EPS) * gain


def _alibi_slopes():
    s = 2.0 ** (-8.0 * np.arange(1, N_DIL_HEADS + 1) / N_DIL_HEADS)
    return s.reshape(HEADS_PER_GROUP, N_GROUPS).T.astype(np.float32)


def _ffn_kernel(x_ref, gpre_ref, wg_ref, wu_ref, wd_ref, gpost_ref, gfin_ref, o_hbm, acc_ref, hs_ref, sem,
                *, tf, rem, final):
    i = pl.program_id(0)
    f = pl.program_id(1)
    n_tiles = pl.num_programs(0)
    last = pl.num_programs(1) - 1
    tm = acc_ref.shape[0]

    def writeback(tile):
        return pltpu.make_async_copy(acc_ref, o_hbm.at[pl.ds(pl.multiple_of(tile * tm, tm), tm), :], sem)

    @pl.when(f == 0)
    def _():
        hs_ref[...] = _rms(x_ref[...], gpre_ref[...]).astype(BF16)

    def step(valid, first):
        hs = hs_ref[...]
        g = jnp.dot(hs, wg_ref[:, :valid], preferred_element_type=F32)
        u = jnp.dot(hs, wu_ref[:, :valid], preferred_element_type=F32)
        a = (g * jax.nn.sigmoid(g) * u).astype(BF16)
        if first:
            @pl.when(i > 0)
            def _():
                writeback(i - 1).wait()

            acc_ref[...] = jnp.dot(a, wd_ref[:valid, :], preferred_element_type=F32)
        else:
            acc_ref[...] += jnp.dot(a, wd_ref[:valid, :], preferred_element_type=F32)

    pl.when(f == 0)(functools.partial(step, tf, True))
    pl.when((f > 0) & (f < last))(functools.partial(step, tf, False))
    pl.when(f == last)(functools.partial(step, rem, False))

    @pl.when(f == last)
    def _():
        y = x_ref[...] + 0.5 * _rms(acc_ref[...], gpost_ref[...])
        if final:
            y = _rms(y, gfin_ref[...])
        acc_ref[...] = y
        writeback(i).start()

        @pl.when(i == n_tiles - 1)
        def _():
            writeback(i).wait()


def _ffn(x, gpre, wg, wu, wd, gpost, gfin, *, final):
    n, d = x.shape
    tm, tf = FFN_TM, FFN_TF
    dff = wd.shape[0]
    nf = pl.cdiv(dff, tf)
    rem = dff - (nf - 1) * tf
    assert nf >= 2
    est = 2 * tm * d * 4 + tm * d * 4 + tm * d * 2 + 2 * 3 * d * tf * 2 + 2 * tm * tf * 4
    vec = pl.BlockSpec((1, d), lambda i, f: (0, 0))
    return pl.pallas_call(
        functools.partial(_ffn_kernel, tf=tf, rem=rem, final=final),
        grid=(n // tm, nf),
        in_specs=[
            pl.BlockSpec((tm, d), lambda i, f: (i, 0)),
            vec,
            pl.BlockSpec((None, d, tf), lambda i, f: (f, 0, 0)),
            pl.BlockSpec((None, d, tf), lambda i, f: (f, 0, 0)),
            pl.BlockSpec((tf, d), lambda i, f: (f, 0)),
            vec,
            vec,
        ],
        out_specs=pl.BlockSpec(memory_space=pl.ANY),
        out_shape=jax.ShapeDtypeStruct((n, d), F32),
        scratch_shapes=[pltpu.VMEM((tm, d), F32), pltpu.VMEM((tm, d), BF16), pltpu.SemaphoreType.DMA(())],
        compiler_params=pltpu.CompilerParams(
            dimension_semantics=("arbitrary", "arbitrary"), vmem_limit_bytes=_vmem_limit(est)),
        name="ffn_final" if final else "ffn",
    )(x, gpre, wg, wu, wd, gpost, gfin)


def _kvproj_kernel(x_ref, g_ref, w_ref, o_ref, hs_ref):
    @pl.when(pl.program_id(1) == 0)
    def _():
        hs_ref[...] = _rms(x_ref[...], g_ref[...]).astype(BF16)

    o_ref[...] = jnp.dot(hs_ref[...], w_ref[...], preferred_element_type=F32).astype(o_ref.dtype)


def _kvproj(x, gain, w):
    n, d = x.shape
    ncol = w.shape[1]
    tm, tn = n, PROJ_TN
    est = 2 * tm * d * 4 + tm * d * 2 + 2 * d * tn * 2 + 2 * tm * tn * 2 + 2 * tm * tn * 4
    return pl.pallas_call(
        _kvproj_kernel,
        grid=(n // tm, ncol // tn),
        in_specs=[
            pl.BlockSpec((tm, d), lambda i, j: (i, 0)),
            pl.BlockSpec((1, d), lambda i, j: (0, 0)),
            pl.BlockSpec((d, tn), lambda i, j: (0, j)),
        ],
        out_specs=pl.BlockSpec((tm, tn), lambda i, j: (i, j)),
        out_shape=jax.ShapeDtypeStruct((n, ncol), BF16),
        scratch_shapes=[pltpu.VMEM((tm, d), BF16)],
        compiler_params=pltpu.CompilerParams(
            dimension_semantics=("arbitrary", "arbitrary"), vmem_limit_bytes=_vmem_limit(est)),
        name="kvproj",
    )(x, gain, w)


NAT_CHUNKS = (11, 12, 9, 10, 0, 3, 6)
N_NAT_STEPS = len(NAT_CHUNKS)
N_DIL_STEPS = 3
N_GATE_STEPS = N_BRANCH * D_MODEL // PROJ_TG
FIRST_D4 = N_NAT_STEPS
FIRST_D16 = FIRST_D4 + N_DIL_STEPS
FIRST_GATE = FIRST_D16 + N_DIL_STEPS
MIX_STEPS = FIRST_GATE + N_GATE_STEPS


def _lin_chunk(s):
    t = jnp.minimum(s, FIRST_GATE - 1)
    nat = jnp.where(t < 2, 11 + t, jnp.where(t < 4, 7 + t, 3 * (t - 4)))
    d4 = 3 * (t - FIRST_D4) + 1
    d16 = 3 * (t - FIRST_D16) + 2
    return jnp.where(t < FIRST_D4, nat, jnp.where(t < FIRST_D16, d4, d16))


def _mixproj_kernel(x_ref, g_ref, wg_ref, b_ref, wi_ref, og_ref, on_ref, o4_ref, o16_ref, hs_ref, stage_ref,
                    stage2_ref):
    s = pl.program_id(1)
    tm = x_ref.shape[0]
    row_chunks = [pl.ds(r, PROJ_ROWS) for r in range(0, tm, PROJ_ROWS)]

    @pl.when(s == 0)
    def _():
        hs_ref[...] = _rms(x_ref[...], g_ref[...]).astype(BF16)

    @pl.when(s < FIRST_D4)
    def _():
        for rows in row_chunks:
            on_ref[rows, :] = jnp.dot(hs_ref[rows, :], wi_ref[...], preferred_element_type=F32).astype(on_ref.dtype)

    def dilated(o_ref, d):
        per_res = PROJ_ROWS // d
        slabs = range(PROJ_TN // HEAD_DIM)
        for rc, rows in enumerate(row_chunks):
            acc = jnp.dot(hs_ref[rows, :], wi_ref[...], preferred_element_type=F32)
            for c in slabs:
                stage_ref[c] = acc[:, c * HEAD_DIM:(c + 1) * HEAD_DIM]
            if d == DEINT_STRIDE:
                for res in range(d):
                    for c in slabs:
                        o_ref[res, pl.ds(rc * per_res, per_res), c * HEAD_DIM:(c + 1) * HEAD_DIM] = (
                            stage_ref[c, pl.ds(res, per_res, stride=d), :].astype(o_ref.dtype))
            else:
                for c in slabs:
                    for r1 in range(DEINT_STRIDE):
                        stage2_ref[c, r1] = stage_ref[c, pl.ds(r1, PROJ_ROWS // DEINT_STRIDE, stride=DEINT_STRIDE), :]
                for r1 in range(DEINT_STRIDE):
                    for r2 in range(DEINT_STRIDE):
                        res = r1 + DEINT_STRIDE * r2
                        for c in slabs:
                            o_ref[res, pl.ds(rc * per_res, per_res), c * HEAD_DIM:(c + 1) * HEAD_DIM] = (
                                stage2_ref[c, r1, pl.ds(r2, per_res, stride=DEINT_STRIDE), :].astype(o_ref.dtype))

    pl.when((s >= FIRST_D4) & (s < FIRST_D16))(functools.partial(dilated, o4_ref, DIL_PAIRS[1][1]))
    pl.when((s >= FIRST_D16) & (s < FIRST_GATE))(functools.partial(dilated, o16_ref, DIL_PAIRS[2][1]))

    @pl.when(s >= FIRST_GATE)
    def _():
        for rows in row_chunks:
            z = jnp.dot(hs_ref[rows, :], wg_ref[...], preferred_element_type=F32) + b_ref[...]
            og_ref[rows, :] = (0.5 * jnp.tanh(0.5 * z) + 0.5).astype(og_ref.dtype)


def _mixproj(x, gain, w_gate, b_gate, w_in, *, batch, seq):
    n, d = x.shape
    tm, tn, tg = PROJ_TM, PROJ_TN, PROJ_TG
    tiles_per_seq = seq // tm
    d4, d16 = DIL_PAIRS[1][1], DIL_PAIRS[2][1]

    def gate_idx(s):
        return jnp.maximum(s - FIRST_GATE, 0)

    def dil_spec(dil, first):
        return pl.BlockSpec(
            (None, dil, tm // dil, tn),
            lambda i, s: (i // tiles_per_seq, 0, i % tiles_per_seq, jnp.clip(s - first, 0, N_DIL_STEPS - 1)))

    est = (2 * tm * d * 4 + tm * d * 2 + 2 * d * (tg + tn) * 2 + 2 * tm * (tg + 3 * tn) * 2
           + PROJ_ROWS * tn * 4 + 2 * PROJ_ROWS * tg * 4)
    return pl.pallas_call(
        _mixproj_kernel,
        grid=(n // tm, MIX_STEPS),
        in_specs=[
            pl.BlockSpec((tm, d), lambda i, s: (i, 0)),
            pl.BlockSpec((1, d), lambda i, s: (0, 0)),
            pl.BlockSpec((None, d, tg), lambda i, s: (gate_idx(s), 0, 0)),
            pl.BlockSpec((1, tg), lambda i, s: (0, gate_idx(s))),
            pl.BlockSpec((None, d, tn), lambda i, s: (_lin_chunk(s), 0, 0)),
        ],
        out_specs=[
            pl.BlockSpec((tm, tg), lambda i, s: (i, gate_idx(s))),
            pl.BlockSpec((tm, tn), lambda i, s: (i, jnp.minimum(s, N_NAT_STEPS - 1))),
            dil_spec(d4, FIRST_D4),
            dil_spec(d16, FIRST_D16),
        ],
        out_shape=[
            jax.ShapeDtypeStruct((n, N_BRANCH * d), BF16),
            jax.ShapeDtypeStruct((n, N_NAT_STEPS * tn), BF16),
            jax.ShapeDtypeStruct((batch, d4, seq // d4, N_DIL_STEPS * tn), BF16),
            jax.ShapeDtypeStruct((batch, d16, seq // d16, N_DIL_STEPS * tn), BF16),
        ],
        scratch_shapes=[
            pltpu.VMEM((tm, d), BF16),
            pltpu.VMEM((tn // HEAD_DIM, PROJ_ROWS, HEAD_DIM), F32),
            pltpu.VMEM((tn // HEAD_DIM, DEINT_STRIDE, PROJ_ROWS // DEINT_STRIDE, HEAD_DIM), F32),
        ],
        compiler_params=pltpu.CompilerParams(
            dimension_semantics=("arbitrary", "arbitrary"), vmem_limit_bytes=_vmem_limit(est)),
        name="mixproj",
    )(x, gain, w_gate, b_gate, w_in)


def _attn_kernel(slopes_ref, q0_ref, k0_ref, v0_ref, q1_ref, k1_ref, v1_ref, q2_ref, k2_ref, v2_ref, o_ref,
                 m1_ref, m2_ref, l1_ref, l2_ref, n1_ref, n2_ref, va_ref, bias_ref, *, seq):
    h = pl.program_id(1)
    scale = HEAD_DIM ** -0.5
    qb = ATT_QB
    qkv_refs = ((q0_ref, k0_ref, v0_ref), (q1_ref, k1_ref, v1_ref), (q2_ref, k2_ref, v2_ref))
    m_refs = (None, m1_ref, m2_ref)
    l_refs = (None, l1_ref, l2_ref)
    n_refs = (None, n1_ref, n2_ref)

    va_ref[:, HEAD_DIM:] = jnp.ones((seq, HEAD_DIM), BF16)

    for gi in reversed(range(N_GROUPS)):
        d = DIL_PAIRS[gi][1]
        q_ref, k_ref, v_ref = qkv_refs[gi]
        sub_len = seq // d
        kw = min(ATT_KW, sub_len)
        nb = sub_len // qb
        slope_d = slopes_ref[gi, h] * float(d)

        row = lax.broadcasted_iota(jnp.int32, (qb, kw), 0)
        col = lax.broadcasted_iota(jnp.int32, (qb, kw), 1)
        for kind, off in enumerate((0, -BAND_RADIUS, qb - kw)):
            arel = jnp.abs(col - row + off)
            bias_ref[kind, :, :kw] = jnp.where(arel <= BAND_RADIUS, -slope_d * arel.astype(F32), NEG)

        if d == 1:
            va_ref[:, :HEAD_DIM] = v_ref[...]
        else:
            for res in range(d):
                va_ref[pl.ds(res * sub_len, sub_len), :HEAD_DIM] = v_ref[res]

        def block(idx, gi=gi, d=d, q_ref=q_ref, k_ref=k_ref, sub_len=sub_len, kw=kw, nb=nb):
            res = idx // nb
            i = idx - res * nb
            q0 = pl.multiple_of(i * qb, qb)
            k0 = pl.multiple_of(jnp.clip(q0 - BAND_RADIUS, 0, sub_len - kw), BAND_RADIUS)
            if d == 1:
                qblk = q_ref[pl.ds(q0, qb), :]
                kwin = k_ref[pl.ds(k0, kw), :]
            else:
                qblk = q_ref[res, pl.ds(q0, qb), :]
                kwin = k_ref[res, pl.ds(k0, kw), :]
            s = lax.dot_general(qblk, kwin, (((1,), (1,)), ((), ())), preferred_element_type=F32)
            if nb == 1:
                bias = bias_ref[0, :, :kw]
            else:
                kind = jnp.where(i == 0, 0, jnp.where(i == nb - 1, 2, 1))
                bias = bias_ref[kind]
            s = s * scale + bias
            m = jnp.max(s, axis=-1, keepdims=True)
            p = jnp.exp(s - m).astype(BF16)
            kstart = pl.multiple_of(res * sub_len + k0, BAND_RADIUS)
            na = jnp.dot(p, va_ref[pl.ds(kstart, kw), :], preferred_element_type=F32)
            m = jnp.broadcast_to(m, (qb, HEAD_DIM))
            if d > 1:
                rows = pl.ds(q0 * d + res, qb, stride=d)
                m_refs[gi][rows, :] = m
                l_refs[gi][rows, :] = na[:, HEAD_DIM:]
                n_refs[gi][rows, :] = na[:, :HEAD_DIM]
            else:
                rows = pl.ds(q0, qb)
                ms = [m] + [m_refs[g][rows, :] for g in range(1, N_GROUPS)]
                mx = functools.reduce(jnp.maximum, ms)
                e = jnp.exp(m - mx)
                num = e * na[:, :HEAD_DIM]
                den = e * na[:, HEAD_DIM:]
                for g in range(1, N_GROUPS):
                    e = jnp.exp(ms[g] - mx)
                    num += e * n_refs[g][rows, :]
                    den += e * l_refs[g][rows, :]
                o_ref[rows, :] = (num / den).astype(o_ref.dtype)

        def blocks(it, carry, block=block):
            for u in range(ATT_UNROLL):
                block(it * ATT_UNROLL + u)
            return carry

        lax.fori_loop(0, d * nb // ATT_UNROLL, blocks, 0)


def _attention(nat, qkv4, qkv16, slopes, *, batch, seq, col0):
    blk0 = col0 // HEAD_DIM
    d4, d16 = DIL_PAIRS[1][1], DIL_PAIRS[2][1]

    def nat_map(b, h, *, which):
        return (b, blk0 + which * HEADS_PER_GROUP + h)

    def dil_map(b, h, *, which):
        return (b, 0, 0, which * HEADS_PER_GROUP + h)

    in_specs = [pl.BlockSpec(memory_space=pltpu.SMEM)]
    in_specs += [pl.BlockSpec((seq, HEAD_DIM), functools.partial(nat_map, which=w)) for w in range(3)]
    for dil in (d4, d16):
        in_specs += [pl.BlockSpec((None, dil, seq // dil, HEAD_DIM), functools.partial(dil_map, which=w))
                     for w in range(3)]
    tile = seq * HEAD_DIM
    est = 10 * 2 * tile * 2 + 6 * tile * 4 + 2 * tile * 2 + 3 * ATT_QB * ATT_KW * 4
    state = [pltpu.VMEM((seq, HEAD_DIM), F32) for _ in range(3 * (N_GROUPS - 1))]
    return pl.pallas_call(
        functools.partial(_attn_kernel, seq=seq),
        grid=(batch, HEADS_PER_GROUP),
        in_specs=in_specs,
        out_specs=pl.BlockSpec((seq, HEAD_DIM), lambda b, h: (b, h)),
        out_shape=jax.ShapeDtypeStruct((batch * seq, HEADS_PER_GROUP * HEAD_DIM), BF16),
        scratch_shapes=state + [
            pltpu.VMEM((seq, 2 * HEAD_DIM), BF16),
            pltpu.VMEM((3, ATT_QB, ATT_KW), F32),
        ],
        compiler_params=pltpu.CompilerParams(
            dimension_semantics=("arbitrary", "arbitrary"), vmem_limit_bytes=_vmem_limit(est)),
        name="attention",
    )(slopes, nat, nat, nat, qkv4, qkv4, qkv4, qkv16, qkv16, qkv16)


def _pool_kernel(u_ref, w_ref, sc_ref, o_ref, pad_ref, band_ref, pooled_ref, *, seq):
    grp = pl.program_id(1)
    halo, rows = POOL_HALO, POOL_ROWS
    ext = rows + 2 * halo
    zeros = jnp.zeros((halo, POOL_GROUP), BF16)
    pad_ref[pl.ds(0, halo), :] = zeros
    pad_ref[pl.ds(halo + seq, halo), :] = zeros
    pad_ref[pl.ds(halo, seq), :] = u_ref[...]

    def run(window):
        first = lax.broadcasted_iota(jnp.int32, (rows, ext), 0) + (halo - window // 2)
        col = lax.broadcasted_iota(jnp.int32, (rows, ext), 1)
        band_ref[...] = jnp.where(col < first, 0.0, jnp.where(col < first + window, 1.0, 0.0)).astype(BF16)

        def chunk(c):
            base = pl.multiple_of(c * rows, rows)
            sums = jnp.dot(band_ref[...], pad_ref[pl.ds(base, ext), :], preferred_element_type=F32)
            tok = pad_ref[pl.ds(base + halo, rows), :].astype(F32)
            t = base + lax.broadcasted_iota(jnp.int32, (rows, 1), 0)
            lo = jnp.maximum(t - window // 2, 0)
            hi = jnp.minimum(t - window // 2 + window, seq)
            pooled_ref[pl.ds(base, rows), :] = (sums / (hi - lo).astype(F32) - tok).astype(BF16)

        def chunks(it, carry):
            for u in range(POOL_UNROLL):
                chunk(it * POOL_UNROLL + u)
            return carry

        lax.fori_loop(0, seq // rows // POOL_UNROLL, chunks, 0)

    for gi, window in enumerate(POOL_WINDOWS):
        pl.when(grp == gi)(functools.partial(run, window))

    y = jnp.dot(pooled_ref[...], w_ref[...], preferred_element_type=F32) * sc_ref[...]
    o_ref[...] = y.astype(o_ref.dtype)


def _pool(pg, w_pool, pool_scale, *, batch, seq, col0):
    blk0 = col0 // POOL_GROUP
    ngrp = len(POOL_WINDOWS)
    est = (2 * seq * POOL_GROUP * 2 * 2 + (seq + 2 * POOL_HALO) * POOL_GROUP * 4 + 2 * POOL_GROUP * POOL_GROUP * 2
           + seq * POOL_GROUP * (2 + 4))
    return pl.pallas_call(
        functools.partial(_pool_kernel, seq=seq),
        grid=(batch, ngrp),
        in_specs=[
            pl.BlockSpec((seq, POOL_GROUP), lambda b, g: (b, blk0 + g)),
            pl.BlockSpec((None, POOL_GROUP, POOL_GROUP), lambda b, g: (g, 0, 0)),
            pl.BlockSpec((1, POOL_GROUP), lambda b, g: (0, g)),
        ],
        out_specs=pl.BlockSpec((seq, POOL_GROUP), lambda b, g: (b, g)),
        out_shape=jax.ShapeDtypeStruct((batch * seq, D_POOL), BF16),
        scratch_shapes=[pltpu.VMEM((seq + 2 * POOL_HALO, POOL_GROUP), BF16),
                        pltpu.VMEM((POOL_ROWS, POOL_ROWS + 2 * POOL_HALO), BF16),
                        pltpu.VMEM((seq, POOL_GROUP), BF16)],
        compiler_params=pltpu.CompilerParams(
            dimension_semantics=("arbitrary", "arbitrary"), vmem_limit_bytes=_vmem_limit(est)),
        name="pool",
    )(pg, w_pool, pool_scale)


def _memattn_kernel(q_ref, kv_ref, o_ref):
    scale = XHEAD_DIM ** -0.5
    for hh in range(N_XHEADS):
        lo = hh * XHEAD_DIM
        q = q_ref[:, lo:lo + XHEAD_DIM]
        k = kv_ref[:, lo:lo + XHEAD_DIM]
        v = kv_ref[:, D_XATTN + lo:D_XATTN + lo + XHEAD_DIM]
        s = lax.dot_general(q, k, (((1,), (1,)), ((), ())), preferred_element_type=F32) * scale
        m = jnp.max(s, axis=-1, keepdims=True)
        p = jnp.exp(s - m)
        l = jnp.sum(p, axis=-1, keepdims=True)
        y = jnp.dot(p.astype(BF16), v, preferred_element_type=F32) / l
        o_ref[:, lo:lo + XHEAD_DIM] = y.astype(o_ref.dtype)


def _memattn(pg, kv, *, seq, col0):
    n = pg.shape[0]
    tm = MEM_TM
    blk0 = col0 // D_XATTN
    tiles_per_seq = seq // tm
    est = 2 * tm * D_XATTN * 2 * 2 + 2 * N_MEM * 2 * D_XATTN * 2 + 4 * tm * N_MEM * 4
    return pl.pallas_call(
        _memattn_kernel,
        grid=(n // tm,),
        in_specs=[
            pl.BlockSpec((tm, D_XATTN), lambda i: (i, blk0)),
            pl.BlockSpec((N_MEM, 2 * D_XATTN), lambda i: (i // tiles_per_seq, 0)),
        ],
        out_specs=pl.BlockSpec((tm, D_XATTN), lambda i: (i, 0)),
        out_shape=jax.ShapeDtypeStruct((n, D_XATTN), BF16),
        compiler_params=pltpu.CompilerParams(
            dimension_semantics=("arbitrary",), vmem_limit_bytes=_vmem_limit(est)),
        name="memattn",
    )(pg, kv)


def _merge_kernel(gates_ref, ya_ref, yp_ref, ym_ref, x_ref, wa_ref, wp_ref, wm_ref, wo_ref, gain_ref, o_ref):
    d = x_ref.shape[1]
    for r in range(0, x_ref.shape[0], MERGE_ROWS):
        rows = pl.ds(r, MERGE_ROWS)
        merged = gates_ref[rows, 0:d].astype(F32) * jnp.dot(ya_ref[rows, :], wa_ref[...], preferred_element_type=F32)
        merged += gates_ref[rows, d:2 * d].astype(F32) * jnp.dot(yp_ref[rows, :], wp_ref[...],
                                                                   preferred_element_type=F32)
        merged += gates_ref[rows, 2 * d:3 * d].astype(F32) * jnp.dot(ym_ref[rows, :], wm_ref[...],
                                                                       preferred_element_type=F32)
        z = jnp.dot(merged.astype(BF16), wo_ref[...], preferred_element_type=F32)
        o_ref[rows, :] = x_ref[rows, :] + _rms(z, gain_ref[...])


def _merge(gates, y_attn, y_pool, y_mem, x, wa, wp, wm, wo, gain):
    n, d = x.shape
    tm = MERGE_TM
    wbytes = (wa.size + wp.size + wm.size + wo.size) * 2

    def const(shape):
        return pl.BlockSpec(shape, lambda i: (0, 0), pipeline_mode=pl.Buffered(1))

    est = (2 * tm * (N_BRANCH * d + y_attn.shape[1] + y_pool.shape[1] + y_mem.shape[1]) * 2
           + 4 * tm * d * 4 + wbytes + 4 * tm * d * 4)
    return pl.pallas_call(
        _merge_kernel,
        grid=(n // tm,),
        in_specs=[
            pl.BlockSpec((tm, N_BRANCH * d), lambda i: (i, 0)),
            pl.BlockSpec((tm, y_attn.shape[1]), lambda i: (i, 0)),
            pl.BlockSpec((tm, y_pool.shape[1]), lambda i: (i, 0)),
            pl.BlockSpec((tm, y_mem.shape[1]), lambda i: (i, 0)),
            pl.BlockSpec((tm, d), lambda i: (i, 0)),
            const(wa.shape), const(wp.shape), const(wm.shape), const(wo.shape),
            pl.BlockSpec((1, d), lambda i: (0, 0)),
        ],
        out_specs=pl.BlockSpec((tm, d), lambda i: (i, 0)),
        out_shape=jax.ShapeDtypeStruct((n, d), F32),
        compiler_params=pltpu.CompilerParams(
            dimension_semantics=("arbitrary",), vmem_limit_bytes=_vmem_limit(est)),
        name="merge",
    )(gates, y_attn, y_pool, y_mem, x, wa, wp, wm, wo, gain)


def _cast_kernel(w_ref, *o_refs):
    lo = 0
    for o_ref in o_refs:
        width = o_ref.shape[1]
        o_ref[...] = w_ref[:, lo:lo + width].astype(o_ref.dtype)
        lo += width


def _cast_bf16(w, widths=None, target_bytes=CAST_BLOCK_BYTES):
    rows, cols = w.shape
    widths = (cols,) if widths is None else tuple(widths)
    br = rows
    while br * cols * 4 > target_bytes and br % 32 == 0:
        br //= 2
    est = 2 * br * cols * 4 + 2 * br * cols * 2
    outs = pl.pallas_call(
        _cast_kernel,
        grid=(rows // br,),
        in_specs=[pl.BlockSpec((br, cols), lambda i: (i, 0))],
        out_specs=[pl.BlockSpec((br, wd), lambda i: (i, 0)) for wd in widths],
        out_shape=[jax.ShapeDtypeStruct((rows, wd), BF16) for wd in widths],
        compiler_params=pltpu.CompilerParams(
            dimension_semantics=("arbitrary",), vmem_limit_bytes=_vmem_limit(est)),
        name="cast",
    )(w)
    return outs if len(widths) > 1 else outs[0]


def _cast_chunked_kernel(w_ref, *o_refs, width):
    for k, o_ref in enumerate(o_refs):
        nchunk, _, tf = o_ref.shape
        for c in range(nchunk):
            valid = min(tf, width - c * tf)
            lo = k * width + c * tf
            o_ref[c, :, :valid] = w_ref[:, lo:lo + valid].astype(o_ref.dtype)
            if valid < tf:
                o_ref[c, :, valid:] = jnp.zeros((o_ref.shape[1], tf - valid), o_ref.dtype)


def _cast_chunked(w, n_out, tf, target_bytes=CAST_BLOCK_BYTES):
    rows, cols = w.shape
    width = cols // n_out
    nchunk = pl.cdiv(width, tf)
    br = rows
    while br * cols * 4 > target_bytes and br % 32 == 0:
        br //= 2
    est = 2 * br * cols * 4 + 2 * n_out * nchunk * br * tf * 2
    return pl.pallas_call(
        functools.partial(_cast_chunked_kernel, width=width),
        grid=(rows // br,),
        in_specs=[pl.BlockSpec((br, cols), lambda i: (i, 0))],
        out_specs=[pl.BlockSpec((nchunk, br, tf), lambda i: (0, i, 0)) for _ in range(n_out)],
        out_shape=[jax.ShapeDtypeStruct((nchunk, rows, tf), BF16) for _ in range(n_out)],
        compiler_params=pltpu.CompilerParams(
            dimension_semantics=("arbitrary",), vmem_limit_bytes=_vmem_limit(est)),
        name="cast_chunked",
    )(w)


def _row(v):
    return v.reshape(1, -1).astype(F32)


def _trunk(x, mem, p, slopes):
    batch, seq, d = x.shape
    xf = x.reshape(batch * seq, d)
    x1 = _ffn(xf, p["ffn1_norm_pre"], p["ffn1_wg"], p["ffn1_wu"], p["ffn1_wd"], p["ffn1_norm_post"],
              p["ffn1_norm_post"], final=False)
    gates, nat, qkv4, qkv16 = _mixproj(x1, p["mix_norm_pre"], p["w_gate"], p["b_gate"], p["w_in"],
                                       batch=batch, seq=seq)
    col_pool = D_XATTN
    col_qkv0 = col_pool + D_POOL
    kv = _kvproj(mem.reshape(batch * N_MEM, d), p["mem_norm"], p["w_mem_kv"])
    y_attn = _attention(nat, qkv4, qkv16, slopes, batch=batch, seq=seq, col0=col_qkv0)
    y_pool = _pool(nat, p["w_pool"], p["pool_scale"], batch=batch, seq=seq, col0=col_pool)
    y_mem = _memattn(nat, kv, seq=seq, col0=0)
    x2 = _merge(gates, y_attn, y_pool, y_mem, x1, p["w_br_attn"], p["w_br_pool"], p["w_br_mem"], p["w_out"],
                p["mix_norm_post"])
    y = _ffn(x2, p["ffn2_norm_pre"], p["ffn2_wg"], p["ffn2_wu"], p["ffn2_wd"], p["ffn2_norm_post"],
             p["final_norm"], final=True)
    return y.reshape(batch, seq, d)


def _prep_params(ffn1_norm_pre, ffn1_w_up, ffn1_w_down, ffn1_norm_post, mix_norm_pre, mem_norm, w_in, w_mem_kv,
                 w_pool, pool_scale, w_br_attn, w_br_pool, w_br_mem, w_gate, b_gate, w_out, mix_norm_post,
                 ffn2_norm_pre, ffn2_w_up, ffn2_w_down, ffn2_norm_post, final_norm):
    p = {}
    for name, w_up, w_down in (("ffn1", ffn1_w_up, ffn1_w_down), ("ffn2", ffn2_w_up, ffn2_w_down)):
        dff = w_down.shape[0]
        p[name + "_wg"], p[name + "_wu"] = _cast_chunked(w_up, 2, FFN_TF)
        p[name + "_wd"] = _cast_bf16(w_down)
    p["w_gate"] = _cast_chunked(w_gate, 1, PROJ_TG)[0]
    p["w_in"] = _cast_chunked(w_in, 1, PROJ_TN)[0]
    p["b_gate"] = _row(b_gate)
    p["w_mem_kv"] = _cast_bf16(w_mem_kv)
    p["w_pool"] = w_pool.astype(BF16)
    p["pool_scale"] = _row(pool_scale)
    for name, w in (("w_br_attn", w_br_attn), ("w_br_pool", w_br_pool), ("w_br_mem", w_br_mem), ("w_out", w_out)):
        p[name] = _cast_bf16(w)
    for name, v in (("ffn1_norm_pre", ffn1_norm_pre), ("ffn1_norm_post", ffn1_norm_post),
                    ("mix_norm_pre", mix_norm_pre), ("mem_norm", mem_norm), ("mix_norm_post", mix_norm_post),
                    ("ffn2_norm_pre", ffn2_norm_pre), ("ffn2_norm_post", ffn2_norm_post),
                    ("final_norm", final_norm)):
        p[name] = _row(v)
    return p


def kernel(x_prompt, x_sample, mem_prompt, mem_sample, ffn1_norm_pre, ffn1_w_up, ffn1_w_down, ffn1_norm_post,
           mix_norm_pre, mem_norm, w_in, w_mem_kv, w_pool, pool_scale, w_br_attn, w_br_pool, w_br_mem, w_gate, b_gate,
           w_out, mix_norm_post, ffn2_norm_pre, ffn2_w_up, ffn2_w_down, ffn2_norm_post, final_norm):
    layer = [ffn1_norm_pre, ffn1_w_up, ffn1_w_down, ffn1_norm_post, mix_norm_pre, mem_norm, w_in, w_mem_kv, w_pool,
             pool_scale, w_br_attn, w_br_pool, w_br_mem, w_gate, b_gate, w_out, mix_norm_post, ffn2_norm_pre,
             ffn2_w_up, ffn2_w_down, ffn2_norm_post, final_norm]
    depth = ffn1_norm_pre.shape[0]
    slopes = jnp.asarray(_alibi_slopes())
    y_prompt, y_sample = x_prompt, x_sample
    for layer_idx in range(depth):
        p = _prep_params(*[w[layer_idx] for w in layer])
        y_prompt = _trunk(y_prompt, mem_prompt, p, slopes)
        y_sample = _trunk(y_sample, mem_sample, p, slopes)
    return (y_prompt, y_sample)
```

```python
import functools

import numpy as np
import jax
import jax.numpy as jnp
from jax import lax
from jax.experimental import pallas as pl
from jax.experimental.pallas import tpu as pltpu

F32 = jnp.float32
BF16 = jnp.bfloat16

D_MODEL = 2048
N_MEM = 256
HEAD_DIM = 128
DIL_PAIRS = ((128, 1), (512, 4), (2048, 16))
HEADS_PER_GROUP = 4
N_GROUPS = len(DIL_PAIRS)
N_DIL_HEADS = HEADS_PER_GROUP * N_GROUPS
D_ATTN = N_DIL_HEADS * HEAD_DIM
POOL_WINDOWS = (2, 4, 8, 16)
POOL_GROUP = 256
D_POOL = POOL_GROUP * len(POOL_WINDOWS)
N_XHEADS = 4
XHEAD_DIM = 256
D_XATTN = N_XHEADS * XHEAD_DIM
N_BRANCH = 3
EPS = 1e-6
NEG = -1e30
BAND_RADIUS = 64

V7X_VMEM_BYTES = 64 * 1024 * 1024
MIB = 1024 * 1024

FFN_TM = 1024
FFN_TF = 1024
PROJ_TM = 1024
PROJ_TN = 512
PROJ_TG = 1536
PROJ_ROWS = 256
DEINT_STRIDE = 4
ATT_QB = 128
ATT_KW = 256
ATT_UNROLL = 16
POOL_ROWS = 256
POOL_HALO = 64
POOL_UNROLL = 4
MEM_TM = 512
MERGE_TM = 512
MERGE_ROWS = 512
CAST_BLOCK_BYTES = 8 * MIB


def _vmem_limit(nbytes):
    return int(min(nbytes * 5 // 4 + 8 * MIB, V7X_VMEM_BYTES - 2 * MIB))


def _rms(x, gain):
    ms = jnp.mean(x * x, axis=-1, keepdims=True)
    return x * lax.rsqrt(ms + EPS) * gain


def _alibi_slopes():
    s = 2.0 ** (-8.0 * np.arange(1, N_DIL_HEADS + 1) / N_DIL_HEADS)
    return s.reshape(HEADS_PER_GROUP, N_GROUPS).T.astype(np.float32)


def _ffn_kernel(x_ref, gpre_ref, wg_ref, wu_ref, wd_ref, gpost_ref, gfin_ref, o_hbm, acc_ref, hs_ref, sem,
                *, tf, rem, final):
    i = pl.program_id(0)
    f = pl.program_id(1)
    n_tiles = pl.num_programs(0)
    last = pl.num_programs(1) - 1
    tm = acc_ref.shape[0]

    def writeback(tile):
        return pltpu.make_async_copy(acc_ref, o_hbm.at[pl.ds(pl.multiple_of(tile * tm, tm), tm), :], sem)

    @pl.when(f == 0)
    def _():
        hs_ref[...] = _rms(x_ref[...], gpre_ref[...]).astype(BF16)

    def step(valid, first):
        hs = hs_ref[...]
        g = jnp.dot(hs, wg_ref[:, :valid], preferred_element_type=F32)
        u = jnp.dot(hs, wu_ref[:, :valid], preferred_element_type=F32)
        a = (g * jax.nn.sigmoid(g) * u).astype(BF16)
        if first:
            @pl.when(i > 0)
            def _():
                writeback(i - 1).wait()

            acc_ref[...] = jnp.dot(a, wd_ref[:valid, :], preferred_element_type=F32)
        else:
            acc_ref[...] += jnp.dot(a, wd_ref[:valid, :], preferred_element_type=F32)

    pl.when(f == 0)(functools.partial(step, tf, True))
    pl.when((f > 0) & (f < last))(functools.partial(step, tf, False))
    pl.when(f == last)(functools.partial(step, rem, False))

    @pl.when(f == last)
    def _():
        y = x_ref[...] + 0.5 * _rms(acc_ref[...], gpost_ref[...])
        if final:
            y = _rms(y, gfin_ref[...])
        acc_ref[...] = y
        writeback(i).start()

        @pl.when(i == n_tiles - 1)
        def _():
            writeback(i).wait()


def _ffn(x, gpre, wg, wu, wd, gpost, gfin, *, final):
    n, d = x.shape
    tm, tf = FFN_TM, FFN_TF
    dff = wd.shape[0]
    nf = pl.cdiv(dff, tf)
    rem = dff - (nf - 1) * tf
    assert nf >= 2
    est = 2 * tm * d * 4 + tm * d * 4 + tm * d * 2 + 2 * 3 * d * tf * 2 + 2 * tm * tf * 4
    vec = pl.BlockSpec((1, d), lambda i, f: (0, 0))
    return pl.pallas_call(
        functools.partial(_ffn_kernel, tf=tf, rem=rem, final=final),
        grid=(n // tm, nf),
        in_specs=[
            pl.BlockSpec((tm, d), lambda i, f: (i, 0)),
            vec,
            pl.BlockSpec((None, d, tf), lambda i, f: (f, 0, 0)),
            pl.BlockSpec((None, d, tf), lambda i, f: (f, 0, 0)),
            pl.BlockSpec((tf, d), lambda i, f: (f, 0)),
            vec,
            vec,
        ],
        out_specs=pl.BlockSpec(memory_space=pl.ANY),
        out_shape=jax.ShapeDtypeStruct((n, d), F32),
        scratch_shapes=[pltpu.VMEM((tm, d), F32), pltpu.VMEM((tm, d), BF16), pltpu.SemaphoreType.DMA(())],
        compiler_params=pltpu.CompilerParams(
            dimension_semantics=("arbitrary", "arbitrary"), vmem_limit_bytes=_vmem_limit(est)),
        name="ffn_final" if final else "ffn",
    )(x, gpre, wg, wu, wd, gpost, gfin)


def _kvproj_kernel(x_ref, g_ref, w_ref, o_ref, hs_ref):
    @pl.when(pl.program_id(1) == 0)
    def _():
        hs_ref[...] = _rms(x_ref[...], g_ref[...]).astype(BF16)

    o_ref[...] = jnp.dot(hs_ref[...], w_ref[...], preferred_element_type=F32).astype(o_ref.dtype)


def _kvproj(x, gain, w):
    n, d = x.shape
    ncol = w.shape[1]
    tm, tn = n, PROJ_TN
    est = 2 * tm * d * 4 + tm * d * 2 + 2 * d * tn * 2 + 2 * tm * tn * 2 + 2 * tm * tn * 4
    return pl.pallas_call(
        _kvproj_kernel,
        grid=(n // tm, ncol // tn),
        in_specs=[
            pl.BlockSpec((tm, d), lambda i, j: (i, 0)),
            pl.BlockSpec((1, d), lambda i, j: (0, 0)),
            pl.BlockSpec((d, tn), lambda i, j: (0, j)),
        ],
        out_specs=pl.BlockSpec((tm, tn), lambda i, j: (i, j)),
        out_shape=jax.ShapeDtypeStruct((n, ncol), BF16),
        scratch_shapes=[pltpu.VMEM((tm, d), BF16)],
        compiler_params=pltpu.CompilerParams(
            dimension_semantics=("arbitrary", "arbitrary"), vmem_limit_bytes=_vmem_limit(est)),
        name="kvproj",
    )(x, gain, w)


NAT_CHUNKS = (11, 12, 9, 10, 0, 3, 6)
N_NAT_STEPS = len(NAT_CHUNKS)
N_DIL_STEPS = 3
N_GATE_STEPS = N_BRANCH * D_MODEL // PROJ_TG
FIRST_D4 = N_NAT_STEPS
FIRST_D16 = FIRST_D4 + N_DIL_STEPS
FIRST_GATE = FIRST_D16 + N_DIL_STEPS
MIX_STEPS = FIRST_GATE + N_GATE_STEPS


def _lin_chunk(s):
    t = jnp.minimum(s, FIRST_GATE - 1)
    nat = jnp.where(t < 2, 11 + t, jnp.where(t < 4, 7 + t, 3 * (t - 4)))
    d4 = 3 * (t - FIRST_D4) + 1
    d16 = 3 * (t - FIRST_D16) + 2
    return jnp.where(t < FIRST_D4, nat, jnp.where(t < FIRST_D16, d4, d16))


def _mixproj_kernel(x_ref, g_ref, wg_ref, b_ref, wi_ref, og_ref, on_ref, o4_ref, o16_ref, hs_ref, stage_ref,
                    stage2_ref):
    s = pl.program_id(1)
    tm = x_ref.shape[0]
    row_chunks = [pl.ds(r, PROJ_ROWS) for r in range(0, tm, PROJ_ROWS)]

    @pl.when(s == 0)
    def _():
        hs_ref[...] = _rms(x_ref[...], g_ref[...]).astype(BF16)

    @pl.when(s < FIRST_D4)
    def _():
        for rows in row_chunks:
            on_ref[rows, :] = jnp.dot(hs_ref[rows, :], wi_ref[...], preferred_element_type=F32).astype(on_ref.dtype)

    def dilated(o_ref, d):
        per_res = PROJ_ROWS // d
        slabs = range(PROJ_TN // HEAD_DIM)
        for rc, rows in enumerate(row_chunks):
            acc = jnp.dot(hs_ref[rows, :], wi_ref[...], preferred_element_type=F32)
            for c in slabs:
                stage_ref[c] = acc[:, c * HEAD_DIM:(c + 1) * HEAD_DIM]
            if d == DEINT_STRIDE:
                for res in range(d):
                    for c in slabs:
                        o_ref[res, pl.ds(rc * per_res, per_res), c * HEAD_DIM:(c + 1) * HEAD_DIM] = (
                            stage_ref[c, pl.ds(res, per_res, stride=d), :].astype(o_ref.dtype))
            else:
                for c in slabs:
                    for r1 in range(DEINT_STRIDE):
                        stage2_ref[c, r1] = stage_ref[c, pl.ds(r1, PROJ_ROWS // DEINT_STRIDE, stride=DEINT_STRIDE), :]
                for r1 in range(DEINT_STRIDE):
                    for r2 in range(DEINT_STRIDE):
                        res = r1 + DEINT_STRIDE * r2
                        for c in slabs:
                            o_ref[res, pl.ds(rc * per_res, per_res), c * HEAD_DIM:(c + 1) * HEAD_DIM] = (
                                stage2_ref[c, r1, pl.ds(r2, per_res, stride=DEINT_STRIDE), :].astype(o_ref.dtype))

    pl.when((s >= FIRST_D4) & (s < FIRST_D16))(functools.partial(dilated, o4_ref, DIL_PAIRS[1][1]))
    pl.when((s >= FIRST_D16) & (s < FIRST_GATE))(functools.partial(dilated, o16_ref, DIL_PAIRS[2][1]))

    @pl.when(s >= FIRST_GATE)
    def _():
        for rows in row_chunks:
            z = jnp.dot(hs_ref[rows, :], wg_ref[...], preferred_element_type=F32) + b_ref[...]
            og_ref[rows, :] = (0.5 * jnp.tanh(0.5 * z) + 0.5).astype(og_ref.dtype)


def _mixproj(x, gain, w_gate, b_gate, w_in, *, batch, seq):
    n, d = x.shape
    tm, tn, tg = PROJ_TM, PROJ_TN, PROJ_TG
    tiles_per_seq = seq // tm
    d4, d16 = DIL_PAIRS[1][1], DIL_PAIRS[2][1]

    def gate_idx(s):
        return jnp.maximum(s - FIRST_GATE, 0)

    def dil_spec(dil, first):
        return pl.BlockSpec(
            (None, dil, tm // dil, tn),
            lambda i, s: (i // tiles_per_seq, 0, i % tiles_per_seq, jnp.clip(s - first, 0, N_DIL_STEPS - 1)))

    est = (2 * tm * d * 4 + tm * d * 2 + 2 * d * (tg + tn) * 2 + 2 * tm * (tg + 3 * tn) * 2
           + PROJ_ROWS * tn * 4 + 2 * PROJ_ROWS * tg * 4)
    return pl.pallas_call(
        _mixproj_kernel,
        grid=(n // tm, MIX_STEPS),
        in_specs=[
            pl.BlockSpec((tm, d), lambda i, s: (i, 0)),
            pl.BlockSpec((1, d), lambda i, s: (0, 0)),
            pl.BlockSpec((d, tg), lambda i, s: (0, gate_idx(s))),
            pl.BlockSpec((1, tg), lambda i, s: (0, gate_idx(s))),
            pl.BlockSpec((d, tn), lambda i, s: (0, _lin_chunk(s))),
        ],
        out_specs=[
            pl.BlockSpec((tm, tg), lambda i, s: (i, gate_idx(s))),
            pl.BlockSpec((tm, tn), lambda i, s: (i, jnp.minimum(s, N_NAT_STEPS - 1))),
            dil_spec(d4, FIRST_D4),
            dil_spec(d16, FIRST_D16),
        ],
        out_shape=[
            jax.ShapeDtypeStruct((n, N_BRANCH * d), BF16),
            jax.ShapeDtypeStruct((n, N_NAT_STEPS * tn), BF16),
            jax.ShapeDtypeStruct((batch, d4, seq // d4, N_DIL_STEPS * tn), BF16),
            jax.ShapeDtypeStruct((batch, d16, seq // d16, N_DIL_STEPS * tn), BF16),
        ],
        scratch_shapes=[
            pltpu.VMEM((tm, d), BF16),
            pltpu.VMEM((tn // HEAD_DIM, PROJ_ROWS, HEAD_DIM), F32),
            pltpu.VMEM((tn // HEAD_DIM, DEINT_STRIDE, PROJ_ROWS // DEINT_STRIDE, HEAD_DIM), F32),
        ],
        compiler_params=pltpu.CompilerParams(
            dimension_semantics=("arbitrary", "arbitrary"), vmem_limit_bytes=_vmem_limit(est)),
        name="mixproj",
    )(x, gain, w_gate, b_gate, w_in)


def _attn_kernel(slopes_ref, q0_ref, k0_ref, v0_ref, q1_ref, k1_ref, v1_ref, q2_ref, k2_ref, v2_ref, o_ref,
                 m1_ref, m2_ref, l1_ref, l2_ref, n1_ref, n2_ref, va_ref, bias_ref, *, seq):
    h = pl.program_id(1)
    scale = HEAD_DIM ** -0.5
    qb = ATT_QB
    qkv_refs = ((q0_ref, k0_ref, v0_ref), (q1_ref, k1_ref, v1_ref), (q2_ref, k2_ref, v2_ref))
    m_refs = (None, m1_ref, m2_ref)
    l_refs = (None, l1_ref, l2_ref)
    n_refs = (None, n1_ref, n2_ref)

    va_ref[:, HEAD_DIM:] = jnp.ones((seq, HEAD_DIM), BF16)

    for gi in reversed(range(N_GROUPS)):
        d = DIL_PAIRS[gi][1]
        q_ref, k_ref, v_ref = qkv_refs[gi]
        sub_len = seq // d
        kw = min(ATT_KW, sub_len)
        nb = sub_len // qb
        slope_d = slopes_ref[gi, h] * float(d)

        row = lax.broadcasted_iota(jnp.int32, (qb, kw), 0)
        col = lax.broadcasted_iota(jnp.int32, (qb, kw), 1)
        for kind, off in enumerate((0, -BAND_RADIUS, qb - kw)):
            arel = jnp.abs(col - row + off)
            bias_ref[kind, :, :kw] = jnp.where(arel <= BAND_RADIUS, -slope_d * arel.astype(F32), NEG)

        if d == 1:
            va_ref[:, :HEAD_DIM] = v_ref[...]
        else:
            for res in range(d):
                va_ref[pl.ds(res * sub_len, sub_len), :HEAD_DIM] = v_ref[res]

        def block(idx, gi=gi, d=d, q_ref=q_ref, k_ref=k_ref, sub_len=sub_len, kw=kw, nb=nb):
            res = idx // nb
            i = idx - res * nb
            q0 = pl.multiple_of(i * qb, qb)
            k0 = pl.multiple_of(jnp.clip(q0 - BAND_RADIUS, 0, sub_len - kw), BAND_RADIUS)
            if d == 1:
                qblk = q_ref[pl.ds(q0, qb), :]
                kwin = k_ref[pl.ds(k0, kw), :]
            else:
                qblk = q_ref[res, pl.ds(q0, qb), :]
                kwin = k_ref[res, pl.ds(k0, kw), :]
            s = lax.dot_general(qblk, kwin, (((1,), (1,)), ((), ())), preferred_element_type=F32)
            if nb == 1:
                bias = bias_ref[0, :, :kw]
            else:
                kind = jnp.where(i == 0, 0, jnp.where(i == nb - 1, 2, 1))
                bias = bias_ref[kind]
            s = s * scale + bias
            m = jnp.max(s, axis=-1, keepdims=True)
            p = jnp.exp(s - m).astype(BF16)
            kstart = pl.multiple_of(res * sub_len + k0, BAND_RADIUS)
            na = jnp.dot(p, va_ref[pl.ds(kstart, kw), :], preferred_element_type=F32)
            m = jnp.broadcast_to(m, (qb, HEAD_DIM))
            if d > 1:
                rows = pl.ds(q0 * d + res, qb, stride=d)
                m_refs[gi][rows, :] = m
                l_refs[gi][rows, :] = na[:, HEAD_DIM:]
                n_refs[gi][rows, :] = na[:, :HEAD_DIM]
            else:
                rows = pl.ds(q0, qb)
                ms = [m] + [m_refs[g][rows, :] for g in range(1, N_GROUPS)]
                mx = functools.reduce(jnp.maximum, ms)
                e = jnp.exp(m - mx)
                num = e * na[:, :HEAD_DIM]
                den = e * na[:, HEAD_DIM:]
                for g in range(1, N_GROUPS):
                    e = jnp.exp(ms[g] - mx)
                    num += e * n_refs[g][rows, :]
                    den += e * l_refs[g][rows, :]
                o_ref[rows, :] = (num / den).astype(o_ref.dtype)

        unroll = min(ATT_UNROLL, d * nb // 2)

        def blocks(it, carry, block=block, unroll=unroll):
            for u in range(unroll):
                block(it * unroll + u)
            return carry

        lax.fori_loop(0, d * nb // unroll, blocks, 0)


def _attention(nat, qkv4, qkv16, slopes, *, batch, seq, col0):
    blk0 = col0 // HEAD_DIM
    d4, d16 = DIL_PAIRS[1][1], DIL_PAIRS[2][1]

    def nat_map(b, h, *, which):
        return (b, blk0 + which * HEADS_PER_GROUP + h)

    def dil_map(b, h, *, which):
        return (b, 0, 0, which * HEADS_PER_GROUP + h)

    in_specs = [pl.BlockSpec(memory_space=pltpu.SMEM)]
    in_specs += [pl.BlockSpec((seq, HEAD_DIM), functools.partial(nat_map, which=w)) for w in range(3)]
    for dil in (d4, d16):
        in_specs += [pl.BlockSpec((None, dil, seq // dil, HEAD_DIM), functools.partial(dil_map, which=w))
                     for w in range(3)]
    tile = seq * HEAD_DIM
    est = 10 * 2 * tile * 2 + 6 * tile * 4 + 2 * tile * 2 + 3 * ATT_QB * ATT_KW * 4
    state = [pltpu.VMEM((seq, HEAD_DIM), F32) for _ in range(3 * (N_GROUPS - 1))]
    return pl.pallas_call(
        functools.partial(_attn_kernel, seq=seq),
        grid=(batch, HEADS_PER_GROUP),
        in_specs=in_specs,
        out_specs=pl.BlockSpec((seq, HEAD_DIM), lambda b, h: (b, h)),
        out_shape=jax.ShapeDtypeStruct((batch * seq, HEADS_PER_GROUP * HEAD_DIM), BF16),
        scratch_shapes=state + [
            pltpu.VMEM((seq, 2 * HEAD_DIM), BF16),
            pltpu.VMEM((3, ATT_QB, ATT_KW), F32),
        ],
        compiler_params=pltpu.CompilerParams(
            dimension_semantics=("arbitrary", "arbitrary"), vmem_limit_bytes=_vmem_limit(est)),
        name="attention",
    )(slopes, nat, nat, nat, qkv4, qkv4, qkv4, qkv16, qkv16, qkv16)


def _pool_kernel(u_ref, w_ref, sc_ref, o_ref, pad_ref, band_ref, pooled_ref, *, seq):
    grp = pl.program_id(1)
    halo, rows = POOL_HALO, POOL_ROWS
    ext = rows + 2 * halo
    zeros = jnp.zeros((halo, POOL_GROUP), BF16)
    pad_ref[pl.ds(0, halo), :] = zeros
    pad_ref[pl.ds(halo + seq, halo), :] = zeros
    pad_ref[pl.ds(halo, seq), :] = u_ref[...]

    def run(window):
        first = lax.broadcasted_iota(jnp.int32, (rows, ext), 0) + (halo - window // 2)
        col = lax.broadcasted_iota(jnp.int32, (rows, ext), 1)
        band_ref[...] = jnp.where(col < first, 0.0, jnp.where(col < first + window, 1.0, 0.0)).astype(BF16)

        def chunk(c):
            base = pl.multiple_of(c * rows, rows)
            sums = jnp.dot(band_ref[...], pad_ref[pl.ds(base, ext), :], preferred_element_type=F32)
            tok = pad_ref[pl.ds(base + halo, rows), :].astype(F32)
            t = base + lax.broadcasted_iota(jnp.int32, (rows, 1), 0)
            lo = jnp.maximum(t - window // 2, 0)
            hi = jnp.minimum(t - window // 2 + window, seq)
            pooled_ref[pl.ds(base, rows), :] = (sums / (hi - lo).astype(F32) - tok).astype(BF16)

        def chunks(it, carry):
            for u in range(POOL_UNROLL):
                chunk(it * POOL_UNROLL + u)
            return carry

        lax.fori_loop(0, seq // rows // POOL_UNROLL, chunks, 0)

    for gi, window in enumerate(POOL_WINDOWS):
        pl.when(grp == gi)(functools.partial(run, window))

    y = jnp.dot(pooled_ref[...], w_ref[...], preferred_element_type=F32) * sc_ref[...]
    o_ref[...] = y.astype(o_ref.dtype)


def _pool(pg, w_pool, pool_scale, *, batch, seq, col0):
    blk0 = col0 // POOL_GROUP
    ngrp = len(POOL_WINDOWS)
    est = (2 * seq * POOL_GROUP * 2 * 2 + (seq + 2 * POOL_HALO) * POOL_GROUP * 4 + 2 * POOL_GROUP * POOL_GROUP * 2
           + seq * POOL_GROUP * (2 + 4))
    return pl.pallas_call(
        functools.partial(_pool_kernel, seq=seq),
        grid=(batch, ngrp),
        in_specs=[
            pl.BlockSpec((seq, POOL_GROUP), lambda b, g: (b, blk0 + g)),
            pl.BlockSpec((None, POOL_GROUP, POOL_GROUP), lambda b, g: (g, 0, 0)),
            pl.BlockSpec((1, POOL_GROUP), lambda b, g: (0, g)),
        ],
        out_specs=pl.BlockSpec((seq, POOL_GROUP), lambda b, g: (b, g)),
        out_shape=jax.ShapeDtypeStruct((batch * seq, D_POOL), BF16),
        scratch_shapes=[pltpu.VMEM((seq + 2 * POOL_HALO, POOL_GROUP), BF16),
                        pltpu.VMEM((POOL_ROWS, POOL_ROWS + 2 * POOL_HALO), BF16),
                        pltpu.VMEM((seq, POOL_GROUP), BF16)],
        compiler_params=pltpu.CompilerParams(
            dimension_semantics=("arbitrary", "arbitrary"), vmem_limit_bytes=_vmem_limit(est)),
        name="pool",
    )(pg, w_pool, pool_scale)


def _memattn_kernel(q_ref, kv_ref, o_ref):
    scale = XHEAD_DIM ** -0.5
    for hh in range(N_XHEADS):
        lo = hh * XHEAD_DIM
        q = q_ref[:, lo:lo + XHEAD_DIM]
        k = kv_ref[:, lo:lo + XHEAD_DIM]
        v = kv_ref[:, D_XATTN + lo:D_XATTN + lo + XHEAD_DIM]
        s = lax.dot_general(q, k, (((1,), (1,)), ((), ())), preferred_element_type=F32) * scale
        m = jnp.max(s, axis=-1, keepdims=True)
        p = jnp.exp(s - m)
        l = jnp.sum(p, axis=-1, keepdims=True)
        y = jnp.dot(p.astype(BF16), v, preferred_element_type=F32) / l
        o_ref[:, lo:lo + XHEAD_DIM] = y.astype(o_ref.dtype)


def _memattn(pg, kv, *, seq, col0):
    n = pg.shape[0]
    tm = MEM_TM
    blk0 = col0 // D_XATTN
    tiles_per_seq = seq // tm
    est = 2 * tm * D_XATTN * 2 * 2 + 2 * N_MEM * 2 * D_XATTN * 2 + 4 * tm * N_MEM * 4
    return pl.pallas_call(
        _memattn_kernel,
        grid=(n // tm,),
        in_specs=[
            pl.BlockSpec((tm, D_XATTN), lambda i: (i, blk0)),
            pl.BlockSpec((N_MEM, 2 * D_XATTN), lambda i: (i // tiles_per_seq, 0)),
        ],
        out_specs=pl.BlockSpec((tm, D_XATTN), lambda i: (i, 0)),
        out_shape=jax.ShapeDtypeStruct((n, D_XATTN), BF16),
        compiler_params=pltpu.CompilerParams(
            dimension_semantics=("arbitrary",), vmem_limit_bytes=_vmem_limit(est)),
        name="memattn",
    )(pg, kv)


def _merge_kernel(gates_ref, ya_ref, yp_ref, ym_ref, x_ref, wa_ref, wp_ref, wm_ref, wo_ref, gain_ref, o_ref):
    d = x_ref.shape[1]
    for r in range(0, x_ref.shape[0], MERGE_ROWS):
        rows = pl.ds(r, MERGE_ROWS)
        merged = gates_ref[rows, 0:d].astype(F32) * jnp.dot(ya_ref[rows, :], wa_ref[...], preferred_element_type=F32)
        merged += gates_ref[rows, d:2 * d].astype(F32) * jnp.dot(yp_ref[rows, :], wp_ref[...],
                                                                   preferred_element_type=F32)
        merged += gates_ref[rows, 2 * d:3 * d].astype(F32) * jnp.dot(ym_ref[rows, :], wm_ref[...],
                                                                       preferred_element_type=F32)
        z = jnp.dot(merged.astype(BF16), wo_ref[...], preferred_element_type=F32)
        o_ref[rows, :] = x_ref[rows, :] + _rms(z, gain_ref[...])


def _merge(gates, y_attn, y_pool, y_mem, x, wa, wp, wm, wo, gain):
    n, d = x.shape
    tm = MERGE_TM
    wbytes = (wa.size + wp.size + wm.size + wo.size) * 2

    def const(shape):
        return pl.BlockSpec(shape, lambda i: (0, 0), pipeline_mode=pl.Buffered(1))

    est = (2 * tm * (N_BRANCH * d + y_attn.shape[1] + y_pool.shape[1] + y_mem.shape[1]) * 2
           + 4 * tm * d * 4 + wbytes + 4 * tm * d * 4)
    return pl.pallas_call(
        _merge_kernel,
        grid=(n // tm,),
        in_specs=[
            pl.BlockSpec((tm, N_BRANCH * d), lambda i: (i, 0)),
            pl.BlockSpec((tm, y_attn.shape[1]), lambda i: (i, 0)),
            pl.BlockSpec((tm, y_pool.shape[1]), lambda i: (i, 0)),
            pl.BlockSpec((tm, y_mem.shape[1]), lambda i: (i, 0)),
            pl.BlockSpec((tm, d), lambda i: (i, 0)),
            const(wa.shape), const(wp.shape), const(wm.shape), const(wo.shape),
            pl.BlockSpec((1, d), lambda i: (0, 0)),
        ],
        out_specs=pl.BlockSpec((tm, d), lambda i: (i, 0)),
        out_shape=jax.ShapeDtypeStruct((n, d), F32),
        compiler_params=pltpu.CompilerParams(
            dimension_semantics=("arbitrary",), vmem_limit_bytes=_vmem_limit(est)),
        name="merge",
    )(gates, y_attn, y_pool, y_mem, x, wa, wp, wm, wo, gain)


def _cast_kernel(w_ref, *o_refs):
    lo = 0
    for o_ref in o_refs:
        width = o_ref.shape[1]
        o_ref[...] = w_ref[:, lo:lo + width].astype(o_ref.dtype)
        lo += width


def _cast_bf16(w, widths=None, target_bytes=CAST_BLOCK_BYTES):
    rows, cols = w.shape
    widths = (cols,) if widths is None else tuple(widths)
    br = rows
    while br * cols * 4 > target_bytes and br % 32 == 0:
        br //= 2
    est = 2 * br * cols * 4 + 2 * br * cols * 2
    outs = pl.pallas_call(
        _cast_kernel,
        grid=(rows // br,),
        in_specs=[pl.BlockSpec((br, cols), lambda i: (i, 0))],
        out_specs=[pl.BlockSpec((br, wd), lambda i: (i, 0)) for wd in widths],
        out_shape=[jax.ShapeDtypeStruct((rows, wd), BF16) for wd in widths],
        compiler_params=pltpu.CompilerParams(
            dimension_semantics=("arbitrary",), vmem_limit_bytes=_vmem_limit(est)),
        name="cast",
    )(w)
    return outs if len(widths) > 1 else outs[0]


def _cast_chunked_kernel(w_ref, *o_refs, width):
    for k, o_ref in enumerate(o_refs):
        nchunk, _, tf = o_ref.shape
        for c in range(nchunk):
            valid = min(tf, width - c * tf)
            lo = k * width + c * tf
            o_ref[c, :, :valid] = w_ref[:, lo:lo + valid].astype(o_ref.dtype)
            if valid < tf:
                o_ref[c, :, valid:] = jnp.zeros((o_ref.shape[1], tf - valid), o_ref.dtype)


def _cast_chunked(w, n_out, tf, target_bytes=CAST_BLOCK_BYTES):
    rows, cols = w.shape
    width = cols // n_out
    nchunk = pl.cdiv(width, tf)
    br = rows
    while br * cols * 4 > target_bytes and br % 32 == 0:
        br //= 2
    est = 2 * br * cols * 4 + 2 * n_out * nchunk * br * tf * 2
    return pl.pallas_call(
        functools.partial(_cast_chunked_kernel, width=width),
        grid=(rows // br,),
        in_specs=[pl.BlockSpec((br, cols), lambda i: (i, 0))],
        out_specs=[pl.BlockSpec((nchunk, br, tf), lambda i: (0, i, 0)) for _ in range(n_out)],
        out_shape=[jax.ShapeDtypeStruct((nchunk, rows, tf), BF16) for _ in range(n_out)],
        compiler_params=pltpu.CompilerParams(
            dimension_semantics=("arbitrary",), vmem_limit_bytes=_vmem_limit(est)),
        name="cast_chunked",
    )(w)


def _row(v):
    return v.reshape(1, -1).astype(F32)


def _trunk(x, mem, p, slopes):
    batch, seq, d = x.shape
    xf = x.reshape(batch * seq, d)
    x1 = _ffn(xf, p["ffn1_norm_pre"], p["ffn1_wg"], p["ffn1_wu"], p["ffn1_wd"], p["ffn1_norm_post"],
              p["ffn1_norm_post"], final=False)
    gates, nat, qkv4, qkv16 = _mixproj(x1, p["mix_norm_pre"], p["w_gate"], p["b_gate"], p["w_in"],
                                       batch=batch, seq=seq)
    col_pool = D_XATTN
    col_qkv0 = col_pool + D_POOL
    kv = _kvproj(mem.reshape(batch * N_MEM, d), p["mem_norm"], p["w_mem_kv"])
    y_attn = _attention(nat, qkv4, qkv16, slopes, batch=batch, seq=seq, col0=col_qkv0)
    y_pool = _pool(nat, p["w_pool"], p["pool_scale"], batch=batch, seq=seq, col0=col_pool)
    y_mem = _memattn(nat, kv, seq=seq, col0=0)
    x2 = _merge(gates, y_attn, y_pool, y_mem, x1, p["w_br_attn"], p["w_br_pool"], p["w_br_mem"], p["w_out"],
                p["mix_norm_post"])
    y = _ffn(x2, p["ffn2_norm_pre"], p["ffn2_wg"], p["ffn2_wu"], p["ffn2_wd"], p["ffn2_norm_post"],
             p["final_norm"], final=True)
    return y.reshape(batch, seq, d)


def _prep_params(ffn1_norm_pre, ffn1_w_up, ffn1_w_down, ffn1_norm_post, mix_norm_pre, mem_norm, w_in, w_mem_kv,
                 w_pool, pool_scale, w_br_attn, w_br_pool, w_br_mem, w_gate, b_gate, w_out, mix_norm_post,
                 ffn2_norm_pre, ffn2_w_up, ffn2_w_down, ffn2_norm_post, final_norm):
    p = {}
    for name, w_up, w_down in (("ffn1", ffn1_w_up, ffn1_w_down), ("ffn2", ffn2_w_up, ffn2_w_down)):
        dff = w_down.shape[0]
        p[name + "_wg"], p[name + "_wu"] = _cast_chunked(w_up, 2, FFN_TF)
        p[name + "_wd"] = _cast_bf16(w_down)
    p["w_gate"] = _cast_bf16(w_gate)
    p["w_in"] = _cast_bf16(w_in)
    p["b_gate"] = _row(b_gate)
    p["w_mem_kv"] = _cast_bf16(w_mem_kv)
    p["w_pool"] = w_pool.astype(BF16)
    p["pool_scale"] = _row(pool_scale)
    for name, w in (("w_br_attn", w_br_attn), ("w_br_pool", w_br_pool), ("w_br_mem", w_br_mem), ("w_out", w_out)):
        p[name] = _cast_bf16(w)
    for name, v in (("ffn1_norm_pre", ffn1_norm_pre), ("ffn1_norm_post", ffn1_norm_post),
                    ("mix_norm_pre", mix_norm_pre), ("mem_norm", mem_norm), ("mix_norm_post", mix_norm_post),
                    ("ffn2_norm_pre", ffn2_norm_pre), ("ffn2_norm_post", ffn2_norm_post),
                    ("final_norm", final_norm)):
        p[name] = _row(v)
    return p


def kernel(x_prompt, x_sample, mem_prompt, mem_sample, ffn1_norm_pre, ffn1_w_up, ffn1_w_down, ffn1_norm_post,
           mix_norm_pre, mem_norm, w_in, w_mem_kv, w_pool, pool_scale, w_br_attn, w_br_pool, w_br_mem, w_gate, b_gate,
           w_out, mix_norm_post, ffn2_norm_pre, ffn2_w_up, ffn2_w_down, ffn2_norm_post, final_norm):
    layer = [ffn1_norm_pre, ffn1_w_up, ffn1_w_down, ffn1_norm_post, mix_norm_pre, mem_norm, w_in, w_mem_kv, w_pool,
             pool_scale, w_br_attn, w_br_pool, w_br_mem, w_gate, b_gate, w_out, mix_norm_post, ffn2_norm_pre,
             ffn2_w_up, ffn2_w_down, ffn2_norm_post, final_norm]
    depth = ffn1_norm_pre.shape[0]
    slopes = jnp.asarray(_alibi_slopes())
    y_prompt, y_sample = x_prompt, x_sample
    for layer_idx in range(depth):
        p = _prep_params(*[w[layer_idx] for w in layer])
        y_prompt = _trunk(y_prompt, mem_prompt, p, slopes)
        y_sample = _trunk(y_sample, mem_sample, p, slopes)
    return (y_prompt, y_sample)
```

```python
import functools

import numpy as np
import jax
import jax.numpy as jnp
from jax import lax
from jax.experimental import pallas as pl
from jax.experimental.pallas import tpu as pltpu

F32 = jnp.float32
BF16 = jnp.bfloat16

D_MODEL = 2048
N_MEM = 256
HEAD_DIM = 128
DIL_PAIRS = ((128, 1), (512, 4), (2048, 16))
HEADS_PER_GROUP = 4
N_GROUPS = len(DIL_PAIRS)
N_DIL_HEADS = HEADS_PER_GROUP * N_GROUPS
D_ATTN = N_DIL_HEADS * HEAD_DIM
POOL_WINDOWS = (2, 4, 8, 16)
POOL_GROUP = 256
D_POOL = POOL_GROUP * len(POOL_WINDOWS)
N_XHEADS = 4
XHEAD_DIM = 256
D_XATTN = N_XHEADS * XHEAD_DIM
N_BRANCH = 3
EPS = 1e-6
NEG = -1e30
BAND_RADIUS = 64

V7X_VMEM_BYTES = 64 * 1024 * 1024
MIB = 1024 * 1024

FFN_TM = 1024
FFN_TF = 1024
PROJ_TM = 1024
PROJ_TN = 512
PROJ_TG = 1536
PROJ_ROWS = 256
DEINT_STRIDE = 4
ATT_QB = 128
ATT_KW = 256
ATT_UNROLL = 16
POOL_ROWS = 256
POOL_HALO = 64
POOL_UNROLL = 4
MEM_TM = 512
MERGE_TM = 512
MERGE_ROWS = 512
CAST_BLOCK_BYTES = 8 * MIB


def _vmem_limit(nbytes):
    return int(min(nbytes * 5 // 4 + 8 * MIB, V7X_VMEM_BYTES - 2 * MIB))


def _rms(x, gain):
    ms = jnp.mean(x * x, axis=-1, keepdims=True)
    return x * lax.rsqrt(ms + EPS) * gain


def _alibi_slopes():
    s = 2.0 ** (-8.0 * np.arange(1, N_DIL_HEADS + 1) / N_DIL_HEADS)
    return s.reshape(HEADS_PER_GROUP, N_GROUPS).T.astype(np.float32)


def _ffn_kernel(x_ref, gpre_ref, wg_ref, wu_ref, wd_ref, gpost_ref, gfin_ref, o_hbm, acc_ref, hs_ref, sem,
                *, tf, rem, final):
    i = pl.program_id(0)
    f = pl.program_id(1)
    n_tiles = pl.num_programs(0)
    last = pl.num_programs(1) - 1
    tm = acc_ref.shape[0]

    def writeback(tile):
        return pltpu.make_async_copy(acc_ref, o_hbm.at[pl.ds(pl.multiple_of(tile * tm, tm), tm), :], sem)

    @pl.when(f == 0)
    def _():
        hs_ref[...] = _rms(x_ref[...], gpre_ref[...]).astype(BF16)

    def step(valid, first):
        hs = hs_ref[...]
        g = jnp.dot(hs, wg_ref[:, :valid], preferred_element_type=F32)
        u = jnp.dot(hs, wu_ref[:, :valid], preferred_element_type=F32)
        a = (g * jax.nn.sigmoid(g) * u).astype(BF16)
        if first:
            @pl.when(i > 0)
            def _():
                writeback(i - 1).wait()

            acc_ref[...] = jnp.dot(a, wd_ref[:valid, :], preferred_element_type=F32)
        else:
            acc_ref[...] += jnp.dot(a, wd_ref[:valid, :], preferred_element_type=F32)

    pl.when(f == 0)(functools.partial(step, tf, True))
    pl.when((f > 0) & (f < last))(functools.partial(step, tf, False))
    pl.when(f == last)(functools.partial(step, rem, False))

    @pl.when(f == last)
    def _():
        y = x_ref[...] + 0.5 * _rms(acc_ref[...], gpost_ref[...])
        if final:
            y = _rms(y, gfin_ref[...])
        acc_ref[...] = y
        writeback(i).start()

        @pl.when(i == n_tiles - 1)
        def _():
            writeback(i).wait()


def _ffn(x, gpre, wg, wu, wd, gpost, gfin, *, final):
    n, d = x.shape
    tm, tf = FFN_TM, FFN_TF
    dff = wd.shape[0]
    nf = pl.cdiv(dff, tf)
    rem = dff - (nf - 1) * tf
    assert nf >= 2
    est = 2 * tm * d * 4 + tm * d * 4 + tm * d * 2 + 2 * 3 * d * tf * 2 + 2 * tm * tf * 4
    vec = pl.BlockSpec((1, d), lambda i, f: (0, 0))
    return pl.pallas_call(
        functools.partial(_ffn_kernel, tf=tf, rem=rem, final=final),
        grid=(n // tm, nf),
        in_specs=[
            pl.BlockSpec((tm, d), lambda i, f: (i, 0)),
            vec,
            pl.BlockSpec((None, d, tf), lambda i, f: (f, 0, 0)),
            pl.BlockSpec((None, d, tf), lambda i, f: (f, 0, 0)),
            pl.BlockSpec((tf, d), lambda i, f: (f, 0)),
            vec,
            vec,
        ],
        out_specs=pl.BlockSpec(memory_space=pl.ANY),
        out_shape=jax.ShapeDtypeStruct((n, d), F32),
        scratch_shapes=[pltpu.VMEM((tm, d), F32), pltpu.VMEM((tm, d), BF16), pltpu.SemaphoreType.DMA(())],
        compiler_params=pltpu.CompilerParams(
            dimension_semantics=("arbitrary", "arbitrary"), vmem_limit_bytes=_vmem_limit(est)),
        name="ffn_final" if final else "ffn",
    )(x, gpre, wg, wu, wd, gpost, gfin)


def _kvproj_kernel(x_ref, g_ref, w_ref, o_ref, hs_ref):
    @pl.when(pl.program_id(1) == 0)
    def _():
        hs_ref[...] = _rms(x_ref[...], g_ref[...]).astype(BF16)

    o_ref[...] = jnp.dot(hs_ref[...], w_ref[...], preferred_element_type=F32).astype(o_ref.dtype)


def _kvproj(x, gain, w):
    n, d = x.shape
    ncol = w.shape[1]
    tm, tn = n, PROJ_TN
    est = 2 * tm * d * 4 + tm * d * 2 + 2 * d * tn * 2 + 2 * tm * tn * 2 + 2 * tm * tn * 4
    return pl.pallas_call(
        _kvproj_kernel,
        grid=(n // tm, ncol // tn),
        in_specs=[
            pl.BlockSpec((tm, d), lambda i, j: (i, 0)),
            pl.BlockSpec((1, d), lambda i, j: (0, 0)),
            pl.BlockSpec((d, tn), lambda i, j: (0, j)),
        ],
        out_specs=pl.BlockSpec((tm, tn), lambda i, j: (i, j)),
        out_shape=jax.ShapeDtypeStruct((n, ncol), BF16),
        scratch_shapes=[pltpu.VMEM((tm, d), BF16)],
        compiler_params=pltpu.CompilerParams(
            dimension_semantics=("arbitrary", "arbitrary"), vmem_limit_bytes=_vmem_limit(est)),
        name="kvproj",
    )(x, gain, w)


NAT_CHUNKS = (11, 12, 9, 10, 0, 3, 6)
N_NAT_STEPS = len(NAT_CHUNKS)
N_DIL_STEPS = 3
N_GATE_STEPS = N_BRANCH * D_MODEL // PROJ_TG
FIRST_D4 = N_NAT_STEPS
FIRST_D16 = FIRST_D4 + N_DIL_STEPS
FIRST_GATE = FIRST_D16 + N_DIL_STEPS
MIX_STEPS = FIRST_GATE + N_GATE_STEPS


def _lin_chunk(s):
    t = jnp.minimum(s, FIRST_GATE - 1)
    nat = jnp.where(t < 2, 11 + t, jnp.where(t < 4, 7 + t, 3 * (t - 4)))
    d4 = 3 * (t - FIRST_D4) + 1
    d16 = 3 * (t - FIRST_D16) + 2
    return jnp.where(t < FIRST_D4, nat, jnp.where(t < FIRST_D16, d4, d16))


def _mixproj_kernel(x_ref, g_ref, wg_ref, b_ref, wi_ref, og_ref, on_ref, o4_ref, o16_ref, hs_ref, stage_ref,
                    stage2_ref):
    s = pl.program_id(1)
    tm = x_ref.shape[0]
    row_chunks = [pl.ds(r, PROJ_ROWS) for r in range(0, tm, PROJ_ROWS)]

    @pl.when(s == 0)
    def _():
        hs_ref[...] = _rms(x_ref[...], g_ref[...]).astype(BF16)

    @pl.when(s < FIRST_D4)
    def _():
        for rows in row_chunks:
            on_ref[rows, :] = jnp.dot(hs_ref[rows, :], wi_ref[...], preferred_element_type=F32).astype(on_ref.dtype)

    def dilated(o_ref, d):
        per_res = PROJ_ROWS // d
        slabs = range(PROJ_TN // HEAD_DIM)
        for rc, rows in enumerate(row_chunks):
            acc = jnp.dot(hs_ref[rows, :], wi_ref[...], preferred_element_type=F32)
            for c in slabs:
                stage_ref[c] = acc[:, c * HEAD_DIM:(c + 1) * HEAD_DIM]
            if d == DEINT_STRIDE:
                for res in range(d):
                    for c in slabs:
                        o_ref[res, pl.ds(rc * per_res, per_res), c * HEAD_DIM:(c + 1) * HEAD_DIM] = (
                            stage_ref[c, pl.ds(res, per_res, stride=d), :].astype(o_ref.dtype))
            else:
                for c in slabs:
                    for r1 in range(DEINT_STRIDE):
                        stage2_ref[c, r1] = stage_ref[c, pl.ds(r1, PROJ_ROWS // DEINT_STRIDE, stride=DEINT_STRIDE), :]
                for r1 in range(DEINT_STRIDE):
                    for r2 in range(DEINT_STRIDE):
                        res = r1 + DEINT_STRIDE * r2
                        for c in slabs:
                            o_ref[res, pl.ds(rc * per_res, per_res), c * HEAD_DIM:(c + 1) * HEAD_DIM] = (
                                stage2_ref[c, r1, pl.ds(r2, per_res, stride=DEINT_STRIDE), :].astype(o_ref.dtype))

    pl.when((s >= FIRST_D4) & (s < FIRST_D16))(functools.partial(dilated, o4_ref, DIL_PAIRS[1][1]))
    pl.when((s >= FIRST_D16) & (s < FIRST_GATE))(functools.partial(dilated, o16_ref, DIL_PAIRS[2][1]))

    @pl.when(s >= FIRST_GATE)
    def _():
        for rows in row_chunks:
            z = jnp.dot(hs_ref[rows, :], wg_ref[...], preferred_element_type=F32) + b_ref[...]
            og_ref[rows, :] = (0.5 * jnp.tanh(0.5 * z) + 0.5).astype(og_ref.dtype)


def _mixproj(x, gain, w_gate, b_gate, w_in, *, batch, seq):
    n, d = x.shape
    tm, tn, tg = PROJ_TM, PROJ_TN, PROJ_TG
    tiles_per_seq = seq // tm
    d4, d16 = DIL_PAIRS[1][1], DIL_PAIRS[2][1]

    def gate_idx(s):
        return jnp.maximum(s - FIRST_GATE, 0)

    def dil_spec(dil, first):
        return pl.BlockSpec(
            (None, dil, tm // dil, tn),
            lambda i, s: (i // tiles_per_seq, 0, i % tiles_per_seq, jnp.clip(s - first, 0, N_DIL_STEPS - 1)))

    est = (2 * tm * d * 4 + tm * d * 2 + 2 * d * (tg + tn) * 2 + 2 * tm * (tg + 3 * tn) * 2
           + PROJ_ROWS * tn * 4 + 2 * PROJ_ROWS * tg * 4)
    return pl.pallas_call(
        _mixproj_kernel,
        grid=(n // tm, MIX_STEPS),
        in_specs=[
            pl.BlockSpec((tm, d), lambda i, s: (i, 0)),
            pl.BlockSpec((1, d), lambda i, s: (0, 0)),
            pl.BlockSpec((d, tg), lambda i, s: (0, gate_idx(s))),
            pl.BlockSpec((1, tg), lambda i, s: (0, gate_idx(s))),
            pl.BlockSpec((d, tn), lambda i, s: (0, _lin_chunk(s))),
        ],
        out_specs=[
            pl.BlockSpec((tm, tg), lambda i, s: (i, gate_idx(s))),
            pl.BlockSpec((tm, tn), lambda i, s: (i, jnp.minimum(s, N_NAT_STEPS - 1))),
            dil_spec(d4, FIRST_D4),
            dil_spec(d16, FIRST_D16),
        ],
        out_shape=[
            jax.ShapeDtypeStruct((n, N_BRANCH * d), BF16),
            jax.ShapeDtypeStruct((n, N_NAT_STEPS * tn), BF16),
            jax.ShapeDtypeStruct((batch, d4, seq // d4, N_DIL_STEPS * tn), BF16),
            jax.ShapeDtypeStruct((batch, d16, seq // d16, N_DIL_STEPS * tn), BF16),
        ],
        scratch_shapes=[
            pltpu.VMEM((tm, d), BF16),
            pltpu.VMEM((tn // HEAD_DIM, PROJ_ROWS, HEAD_DIM), F32),
            pltpu.VMEM((tn // HEAD_DIM, DEINT_STRIDE, PROJ_ROWS // DEINT_STRIDE, HEAD_DIM), F32),
        ],
        compiler_params=pltpu.CompilerParams(
            dimension_semantics=("arbitrary", "arbitrary"), vmem_limit_bytes=_vmem_limit(est)),
        name="mixproj",
    )(x, gain, w_gate, b_gate, w_in)


def _attn_kernel(slopes_ref, q0_ref, k0_ref, v0_ref, q1_ref, k1_ref, v1_ref, q2_ref, k2_ref, v2_ref, o_ref,
                 m1_ref, m2_ref, l1_ref, l2_ref, n1_ref, n2_ref, va_ref, bias_ref, *, seq):
    h = pl.program_id(1)
    scale = HEAD_DIM ** -0.5
    qb = ATT_QB
    qkv_refs = ((q0_ref, k0_ref, v0_ref), (q1_ref, k1_ref, v1_ref), (q2_ref, k2_ref, v2_ref))
    m_refs = (None, m1_ref, m2_ref)
    l_refs = (None, l1_ref, l2_ref)
    n_refs = (None, n1_ref, n2_ref)

    va_ref[:, HEAD_DIM:] = jnp.ones((seq, HEAD_DIM), BF16)

    for gi in reversed(range(N_GROUPS)):
        d = DIL_PAIRS[gi][1]
        q_ref, k_ref, v_ref = qkv_refs[gi]
        sub_len = seq // d
        kw = ATT_KW
        nb = sub_len // qb
        short = sub_len < kw
        slope_d = slopes_ref[gi, h] * float(d)

        row = lax.broadcasted_iota(jnp.int32, (qb, kw), 0)
        col = lax.broadcasted_iota(jnp.int32, (qb, kw), 1)
        for kind, off in enumerate((0, -BAND_RADIUS, qb - kw)):
            arel = jnp.abs(col - row + off)
            bias = jnp.where(arel <= BAND_RADIUS, -slope_d * arel.astype(F32), NEG)
            if short:
                key = col + off
                bias = jnp.where(key < 0, NEG, jnp.where(key < sub_len, bias, NEG))
            bias_ref[kind] = bias

        va_ref[:, :HEAD_DIM] = v_ref[...]

        def block(idx, gi=gi, d=d, q_ref=q_ref, k_ref=k_ref, sub_len=sub_len, kw=kw, nb=nb, short=short):
            qstart = pl.multiple_of(idx * qb, qb)
            if short:
                res, q0, pos, last = idx, 0, idx, d * nb - 1
                kstart = jnp.clip(qstart - BAND_RADIUS, 0, seq - kw)
            else:
                res = idx // nb
                pos, last = idx - res * nb, nb - 1
                q0 = pos * qb
                kstart = res * sub_len + jnp.clip(q0 - BAND_RADIUS, 0, sub_len - kw)
            kstart = pl.multiple_of(kstart, BAND_RADIUS)
            kind = jnp.where(pos == 0, 0, jnp.where(pos == last, 2, 1))
            s = lax.dot_general(q_ref[pl.ds(qstart, qb), :], k_ref[pl.ds(kstart, kw), :],
                                (((1,), (1,)), ((), ())), preferred_element_type=F32)
            s = s * scale + bias_ref[kind]
            m = jnp.max(s, axis=-1, keepdims=True)
            p = jnp.exp(s - m).astype(BF16)
            na = jnp.dot(p, va_ref[pl.ds(kstart, kw), :], preferred_element_type=F32)
            m = jnp.broadcast_to(m, (qb, HEAD_DIM))
            if d > 1:
                rows = pl.ds(q0 * d + res, qb, stride=d)
                m_refs[gi][rows, :] = m
                l_refs[gi][rows, :] = na[:, HEAD_DIM:]
                n_refs[gi][rows, :] = na[:, :HEAD_DIM]
            else:
                rows = pl.ds(qstart, qb)
                ms = [m] + [m_refs[g][rows, :] for g in range(1, N_GROUPS)]
                mx = functools.reduce(jnp.maximum, ms)
                e = jnp.exp(m - mx)
                num = e * na[:, :HEAD_DIM]
                den = e * na[:, HEAD_DIM:]
                for g in range(1, N_GROUPS):
                    e = jnp.exp(ms[g] - mx)
                    num += e * n_refs[g][rows, :]
                    den += e * l_refs[g][rows, :]
                o_ref[rows, :] = (num / den).astype(o_ref.dtype)

        unroll = min(ATT_UNROLL, d * nb // 2)

        def blocks(it, carry, block=block, unroll=unroll):
            for u in range(unroll):
                block(it * unroll + u)
            return carry

        lax.fori_loop(0, d * nb // unroll, blocks, 0)


def _attention(nat, qkv4, qkv16, slopes, *, batch, seq, col0):
    blk0 = col0 // HEAD_DIM
    qkv4 = qkv4.reshape(batch * seq, qkv4.shape[-1])
    qkv16 = qkv16.reshape(batch * seq, qkv16.shape[-1])

    def nat_map(b, h, *, which):
        return (b, blk0 + which * HEADS_PER_GROUP + h)

    def dil_map(b, h, *, which):
        return (b, which * HEADS_PER_GROUP + h)

    in_specs = [pl.BlockSpec(memory_space=pltpu.SMEM)]
    in_specs += [pl.BlockSpec((seq, HEAD_DIM), functools.partial(nat_map, which=w)) for w in range(3)]
    in_specs += 2 * [pl.BlockSpec((seq, HEAD_DIM), functools.partial(dil_map, which=w)) for w in range(3)]
    tile = seq * HEAD_DIM
    est = 10 * 2 * tile * 2 + 6 * tile * 4 + 2 * tile * 2 + 3 * ATT_QB * ATT_KW * 4
    state = [pltpu.VMEM((seq, HEAD_DIM), F32) for _ in range(3 * (N_GROUPS - 1))]
    return pl.pallas_call(
        functools.partial(_attn_kernel, seq=seq),
        grid=(batch, HEADS_PER_GROUP),
        in_specs=in_specs,
        out_specs=pl.BlockSpec((seq, HEAD_DIM), lambda b, h: (b, h)),
        out_shape=jax.ShapeDtypeStruct((batch * seq, HEADS_PER_GROUP * HEAD_DIM), BF16),
        scratch_shapes=state + [
            pltpu.VMEM((seq, 2 * HEAD_DIM), BF16),
            pltpu.VMEM((3, ATT_QB, ATT_KW), F32),
        ],
        compiler_params=pltpu.CompilerParams(
            dimension_semantics=("arbitrary", "arbitrary"), vmem_limit_bytes=_vmem_limit(est)),
        name="attention",
    )(slopes, nat, nat, nat, qkv4, qkv4, qkv4, qkv16, qkv16, qkv16)


def _pool_kernel(u_ref, w_ref, sc_ref, o_ref, pad_ref, band_ref, pooled_ref, *, seq):
    grp = pl.program_id(1)
    halo, rows = POOL_HALO, POOL_ROWS
    ext = rows + 2 * halo
    zeros = jnp.zeros((halo, POOL_GROUP), BF16)
    pad_ref[pl.ds(0, halo), :] = zeros
    pad_ref[pl.ds(halo + seq, halo), :] = zeros
    pad_ref[pl.ds(halo, seq), :] = u_ref[...]

    def run(window):
        first = lax.broadcasted_iota(jnp.int32, (rows, ext), 0) + (halo - window // 2)
        col = lax.broadcasted_iota(jnp.int32, (rows, ext), 1)
        band_ref[...] = jnp.where(col < first, 0.0, jnp.where(col < first + window, 1.0, 0.0)).astype(BF16)

        def chunk(c):
            base = pl.multiple_of(c * rows, rows)
            sums = jnp.dot(band_ref[...], pad_ref[pl.ds(base, ext), :], preferred_element_type=F32)
            tok = pad_ref[pl.ds(base + halo, rows), :].astype(F32)
            t = base + lax.broadcasted_iota(jnp.int32, (rows, 1), 0)
            lo = jnp.maximum(t - window // 2, 0)
            hi = jnp.minimum(t - window // 2 + window, seq)
            pooled_ref[pl.ds(base, rows), :] = (sums / (hi - lo).astype(F32) - tok).astype(BF16)

        def chunks(it, carry):
            for u in range(POOL_UNROLL):
                chunk(it * POOL_UNROLL + u)
            return carry

        lax.fori_loop(0, seq // rows // POOL_UNROLL, chunks, 0)

    for gi, window in enumerate(POOL_WINDOWS):
        pl.when(grp == gi)(functools.partial(run, window))

    y = jnp.dot(pooled_ref[...], w_ref[...], preferred_element_type=F32) * sc_ref[...]
    o_ref[...] = y.astype(o_ref.dtype)


def _pool(pg, w_pool, pool_scale, *, batch, seq, col0):
    blk0 = col0 // POOL_GROUP
    ngrp = len(POOL_WINDOWS)
    est = (2 * seq * POOL_GROUP * 2 * 2 + (seq + 2 * POOL_HALO) * POOL_GROUP * 4 + 2 * POOL_GROUP * POOL_GROUP * 2
           + seq * POOL_GROUP * (2 + 4))
    return pl.pallas_call(
        functools.partial(_pool_kernel, seq=seq),
        grid=(batch, ngrp),
        in_specs=[
            pl.BlockSpec((seq, POOL_GROUP), lambda b, g: (b, blk0 + g)),
            pl.BlockSpec((None, POOL_GROUP, POOL_GROUP), lambda b, g: (g, 0, 0)),
            pl.BlockSpec((1, POOL_GROUP), lambda b, g: (0, g)),
        ],
        out_specs=pl.BlockSpec((seq, POOL_GROUP), lambda b, g: (b, g)),
        out_shape=jax.ShapeDtypeStruct((batch * seq, D_POOL), BF16),
        scratch_shapes=[pltpu.VMEM((seq + 2 * POOL_HALO, POOL_GROUP), BF16),
                        pltpu.VMEM((POOL_ROWS, POOL_ROWS + 2 * POOL_HALO), BF16),
                        pltpu.VMEM((seq, POOL_GROUP), BF16)],
        compiler_params=pltpu.CompilerParams(
            dimension_semantics=("arbitrary", "arbitrary"), vmem_limit_bytes=_vmem_limit(est)),
        name="pool",
    )(pg, w_pool, pool_scale)


def _memattn_kernel(q_ref, kv_ref, o_ref):
    scale = XHEAD_DIM ** -0.5
    for hh in range(N_XHEADS):
        lo = hh * XHEAD_DIM
        q = q_ref[:, lo:lo + XHEAD_DIM]
        k = kv_ref[:, lo:lo + XHEAD_DIM]
        v = kv_ref[:, D_XATTN + lo:D_XATTN + lo + XHEAD_DIM]
        s = lax.dot_general(q, k, (((1,), (1,)), ((), ())), preferred_element_type=F32) * scale
        m = jnp.max(s, axis=-1, keepdims=True)
        p = jnp.exp(s - m)
        l = jnp.sum(p, axis=-1, keepdims=True)
        y = jnp.dot(p.astype(BF16), v, preferred_element_type=F32) / l
        o_ref[:, lo:lo + XHEAD_DIM] = y.astype(o_ref.dtype)


def _memattn(pg, kv, *, seq, col0):
    n = pg.shape[0]
    tm = MEM_TM
    blk0 = col0 // D_XATTN
    tiles_per_seq = seq // tm
    est = 2 * tm * D_XATTN * 2 * 2 + 2 * N_MEM * 2 * D_XATTN * 2 + 4 * tm * N_MEM * 4
    return pl.pallas_call(
        _memattn_kernel,
        grid=(n // tm,),
        in_specs=[
            pl.BlockSpec((tm, D_XATTN), lambda i: (i, blk0)),
            pl.BlockSpec((N_MEM, 2 * D_XATTN), lambda i: (i // tiles_per_seq, 0)),
        ],
        out_specs=pl.BlockSpec((tm, D_XATTN), lambda i: (i, 0)),
        out_shape=jax.ShapeDtypeStruct((n, D_XATTN), BF16),
        compiler_params=pltpu.CompilerParams(
            dimension_semantics=("arbitrary",), vmem_limit_bytes=_vmem_limit(est)),
        name="memattn",
    )(pg, kv)


def _merge_kernel(gates_ref, ya_ref, yp_ref, ym_ref, x_ref, wa_ref, wp_ref, wm_ref, wo_ref, gain_ref, o_ref):
    d = x_ref.shape[1]
    for r in range(0, x_ref.shape[0], MERGE_ROWS):
        rows = pl.ds(r, MERGE_ROWS)
        merged = gates_ref[rows, 0:d].astype(F32) * jnp.dot(ya_ref[rows, :], wa_ref[...], preferred_element_type=F32)
        merged += gates_ref[rows, d:2 * d].astype(F32) * jnp.dot(yp_ref[rows, :], wp_ref[...],
                                                                   preferred_element_type=F32)
        merged += gates_ref[rows, 2 * d:3 * d].astype(F32) * jnp.dot(ym_ref[rows, :], wm_ref[...],
                                                                       preferred_element_type=F32)
        z = jnp.dot(merged.astype(BF16), wo_ref[...], preferred_element_type=F32)
        o_ref[rows, :] = x_ref[rows, :] + _rms(z, gain_ref[...])


def _merge(gates, y_attn, y_pool, y_mem, x, wa, wp, wm, wo, gain):
    n, d = x.shape
    tm = MERGE_TM
    wbytes = (wa.size + wp.size + wm.size + wo.size) * 2

    def const(shape):
        return pl.BlockSpec(shape, lambda i: (0, 0), pipeline_mode=pl.Buffered(1))

    est = (2 * tm * (N_BRANCH * d + y_attn.shape[1] + y_pool.shape[1] + y_mem.shape[1]) * 2
           + 4 * tm * d * 4 + wbytes + 4 * tm * d * 4)
    return pl.pallas_call(
        _merge_kernel,
        grid=(n // tm,),
        in_specs=[
            pl.BlockSpec((tm, N_BRANCH * d), lambda i: (i, 0)),
            pl.BlockSpec((tm, y_attn.shape[1]), lambda i: (i, 0)),
            pl.BlockSpec((tm, y_pool.shape[1]), lambda i: (i, 0)),
            pl.BlockSpec((tm, y_mem.shape[1]), lambda i: (i, 0)),
            pl.BlockSpec((tm, d), lambda i: (i, 0)),
            const(wa.shape), const(wp.shape), const(wm.shape), const(wo.shape),
            pl.BlockSpec((1, d), lambda i: (0, 0)),
        ],
        out_specs=pl.BlockSpec((tm, d), lambda i: (i, 0)),
        out_shape=jax.ShapeDtypeStruct((n, d), F32),
        compiler_params=pltpu.CompilerParams(
            dimension_semantics=("arbitrary",), vmem_limit_bytes=_vmem_limit(est)),
        name="merge",
    )(gates, y_attn, y_pool, y_mem, x, wa, wp, wm, wo, gain)


def _cast_kernel(w_ref, *o_refs):
    lo = 0
    for o_ref in o_refs:
        width = o_ref.shape[1]
        o_ref[...] = w_ref[:, lo:lo + width].astype(o_ref.dtype)
        lo += width


def _cast_bf16(w, widths=None, target_bytes=CAST_BLOCK_BYTES):
    rows, cols = w.shape
    widths = (cols,) if widths is None else tuple(widths)
    br = rows
    while br * cols * 4 > target_bytes and br % 32 == 0:
        br //= 2
    est = 2 * br * cols * 4 + 2 * br * cols * 2
    outs = pl.pallas_call(
        _cast_kernel,
        grid=(rows // br,),
        in_specs=[pl.BlockSpec((br, cols), lambda i: (i, 0))],
        out_specs=[pl.BlockSpec((br, wd), lambda i: (i, 0)) for wd in widths],
        out_shape=[jax.ShapeDtypeStruct((rows, wd), BF16) for wd in widths],
        compiler_params=pltpu.CompilerParams(
            dimension_semantics=("arbitrary",), vmem_limit_bytes=_vmem_limit(est)),
        name="cast",
    )(w)
    return outs if len(widths) > 1 else outs[0]


def _cast_chunked_kernel(w_ref, *o_refs, width):
    for k, o_ref in enumerate(o_refs):
        nchunk, _, tf = o_ref.shape
        for c in range(nchunk):
            valid = min(tf, width - c * tf)
            lo = k * width + c * tf
            o_ref[c, :, :valid] = w_ref[:, lo:lo + valid].astype(o_ref.dtype)
            if valid < tf:
                o_ref[c, :, valid:] = jnp.zeros((o_ref.shape[1], tf - valid), o_ref.dtype)


def _cast_chunked(w, n_out, tf, target_bytes=CAST_BLOCK_BYTES):
    rows, cols = w.shape
    width = cols // n_out
    nchunk = pl.cdiv(width, tf)
    br = rows
    while br * cols * 4 > target_bytes and br % 32 == 0:
        br //= 2
    est = 2 * br * cols * 4 + 2 * n_out * nchunk * br * tf * 2
    return pl.pallas_call(
        functools.partial(_cast_chunked_kernel, width=width),
        grid=(rows // br,),
        in_specs=[pl.BlockSpec((br, cols), lambda i: (i, 0))],
        out_specs=[pl.BlockSpec((nchunk, br, tf), lambda i: (0, i, 0)) for _ in range(n_out)],
        out_shape=[jax.ShapeDtypeStruct((nchunk, rows, tf), BF16) for _ in range(n_out)],
        compiler_params=pltpu.CompilerParams(
            dimension_semantics=("arbitrary",), vmem_limit_bytes=_vmem_limit(est)),
        name="cast_chunked",
    )(w)


def _row(v):
    return v.reshape(1, -1).astype(F32)


def _trunk(x, mem, p, slopes):
    batch, seq, d = x.shape
    xf = x.reshape(batch * seq, d)
    x1 = _ffn(xf, p["ffn1_norm_pre"], p["ffn1_wg"], p["ffn1_wu"], p["ffn1_wd"], p["ffn1_norm_post"],
              p["ffn1_norm_post"], final=False)
    gates, nat, qkv4, qkv16 = _mixproj(x1, p["mix_norm_pre"], p["w_gate"], p["b_gate"], p["w_in"],
                                       batch=batch, seq=seq)
    col_pool = D_XATTN
    col_qkv0 = col_pool + D_POOL
    kv = _kvproj(mem.reshape(batch * N_MEM, d), p["mem_norm"], p["w_mem_kv"])
    y_attn = _attention(nat, qkv4, qkv16, slopes, batch=batch, seq=seq, col0=col_qkv0)
    y_pool = _pool(nat, p["w_pool"], p["pool_scale"], batch=batch, seq=seq, col0=col_pool)
    y_mem = _memattn(nat, kv, seq=seq, col0=0)
    x2 = _merge(gates, y_attn, y_pool, y_mem, x1, p["w_br_attn"], p["w_br_pool"], p["w_br_mem"], p["w_out"],
                p["mix_norm_post"])
    y = _ffn(x2, p["ffn2_norm_pre"], p["ffn2_wg"], p["ffn2_wu"], p["ffn2_wd"], p["ffn2_norm_post"],
             p["final_norm"], final=True)
    return y.reshape(batch, seq, d)


def _prep_params(ffn1_norm_pre, ffn1_w_up, ffn1_w_down, ffn1_norm_post, mix_norm_pre, mem_norm, w_in, w_mem_kv,
                 w_pool, pool_scale, w_br_attn, w_br_pool, w_br_mem, w_gate, b_gate, w_out, mix_norm_post,
                 ffn2_norm_pre, ffn2_w_up, ffn2_w_down, ffn2_norm_post, final_norm):
    p = {}
    for name, w_up, w_down in (("ffn1", ffn1_w_up, ffn1_w_down), ("ffn2", ffn2_w_up, ffn2_w_down)):
        dff = w_down.shape[0]
        p[name + "_wg"], p[name + "_wu"] = _cast_chunked(w_up, 2, FFN_TF)
        p[name + "_wd"] = _cast_bf16(w_down)
    p["w_gate"] = _cast_bf16(w_gate)
    p["w_in"] = _cast_bf16(w_in)
    p["b_gate"] = _row(b_gate)
    p["w_mem_kv"] = _cast_bf16(w_mem_kv)
    p["w_pool"] = w_pool.astype(BF16)
    p["pool_scale"] = _row(pool_scale)
    for name, w in (("w_br_attn", w_br_attn), ("w_br_pool", w_br_pool), ("w_br_mem", w_br_mem), ("w_out", w_out)):
        p[name] = _cast_bf16(w)
    for name, v in (("ffn1_norm_pre", ffn1_norm_pre), ("ffn1_norm_post", ffn1_norm_post),
                    ("mix_norm_pre", mix_norm_pre), ("mem_norm", mem_norm), ("mix_norm_post", mix_norm_post),
                    ("ffn2_norm_pre", ffn2_norm_pre), ("ffn2_norm_post", ffn2_norm_post),
                    ("final_norm", final_norm)):
        p[name] = _row(v)
    return p


def kernel(x_prompt, x_sample, mem_prompt, mem_sample, ffn1_norm_pre, ffn1_w_up, ffn1_w_down, ffn1_norm_post,
           mix_norm_pre, mem_norm, w_in, w_mem_kv, w_pool, pool_scale, w_br_attn, w_br_pool, w_br_mem, w_gate, b_gate,
           w_out, mix_norm_post, ffn2_norm_pre, ffn2_w_up, ffn2_w_down, ffn2_norm_post, final_norm):
    layer = [ffn1_norm_pre, ffn1_w_up, ffn1_w_down, ffn1_norm_post, mix_norm_pre, mem_norm, w_in, w_mem_kv, w_pool,
             pool_scale, w_br_attn, w_br_pool, w_br_mem, w_gate, b_gate, w_out, mix_norm_post, ffn2_norm_pre,
             ffn2_w_up, ffn2_w_down, ffn2_norm_post, final_norm]
    depth = ffn1_norm_pre.shape[0]
    slopes = jnp.asarray(_alibi_slopes())
    y_prompt, y_sample = x_prompt, x_sample
    for layer_idx in range(depth):
        p = _prep_params(*[w[layer_idx] for w in layer])
        y_prompt = _trunk(y_prompt, mem_prompt, p, slopes)
        y_sample = _trunk(y_sample, mem_sample, p, slopes)
    return (y_prompt, y_sample)
```

```python
import functools

import numpy as np
import jax
import jax.numpy as jnp
from jax import lax
from jax.experimental import pallas as pl
from jax.experimental.pallas import tpu as pltpu

F32 = jnp.float32
BF16 = jnp.bfloat16

D_MODEL = 2048
N_MEM = 256
HEAD_DIM = 128
DIL_PAIRS = ((128, 1), (512, 4), (2048, 16))
HEADS_PER_GROUP = 4
N_GROUPS = len(DIL_PAIRS)
N_DIL_HEADS = HEADS_PER_GROUP * N_GROUPS
D_ATTN = N_DIL_HEADS * HEAD_DIM
POOL_WINDOWS = (2, 4, 8, 16)
POOL_GROUP = 256
D_POOL = POOL_GROUP * len(POOL_WINDOWS)
N_XHEADS = 4
XHEAD_DIM = 256
D_XATTN = N_XHEADS * XHEAD_DIM
N_BRANCH = 3
EPS = 1e-6
NEG = -1e30
BAND_RADIUS = 64

V7X_VMEM_BYTES = 64 * 1024 * 1024
MIB = 1024 * 1024

FFN_TM = 1024
FFN_TF = 1024
PROJ_TM = 1024
PROJ_TN = 512
PROJ_TG = 1536
PROJ_ROWS = 256
DEINT_STRIDE = 4
ATT_QB = 128
ATT_KW = 256
ATT_UNROLL = 16
POOL_ROWS = 256
POOL_HALO = 64
POOL_UNROLL = 4
MEM_TM = 512
MERGE_TM = 512
MERGE_ROWS = 512
CAST_BLOCK_BYTES = 8 * MIB


def _vmem_limit(nbytes):
    return int(min(nbytes * 5 // 4 + 8 * MIB, V7X_VMEM_BYTES - 2 * MIB))


def _rms(x, gain):
    ms = jnp.mean(x * x, axis=-1, keepdims=True)
    return x * lax.rsqrt(ms + EPS) * gain


def _alibi_slopes():
    s = 2.0 ** (-8.0 * np.arange(1, N_DIL_HEADS + 1) / N_DIL_HEADS)
    return s.reshape(HEADS_PER_GROUP, N_GROUPS).T.astype(np.float32)


def _ffn_kernel(x_ref, gpre_ref, wg_ref, wu_ref, wd_ref, gpost_ref, gfin_ref, o_hbm, acc_ref, hs_ref, sem,
                *, tf, rem, final):
    i = pl.program_id(0)
    f = pl.program_id(1)
    n_tiles = pl.num_programs(0)
    last = pl.num_programs(1) - 1
    tm = acc_ref.shape[0]

    def writeback(tile):
        return pltpu.make_async_copy(acc_ref, o_hbm.at[pl.ds(pl.multiple_of(tile * tm, tm), tm), :], sem)

    @pl.when(f == 0)
    def _():
        hs_ref[...] = _rms(x_ref[...], gpre_ref[...]).astype(BF16)

    def step(valid, first):
        hs = hs_ref[...]
        g = jnp.dot(hs, wg_ref[:, :valid], preferred_element_type=F32)
        u = jnp.dot(hs, wu_ref[:, :valid], preferred_element_type=F32)
        a = (g * jax.nn.sigmoid(g) * u).astype(BF16)
        if first:
            @pl.when(i > 0)
            def _():
                writeback(i - 1).wait()

            acc_ref[...] = jnp.dot(a, wd_ref[:valid, :], preferred_element_type=F32)
        else:
            acc_ref[...] += jnp.dot(a, wd_ref[:valid, :], preferred_element_type=F32)

    pl.when(f == 0)(functools.partial(step, tf, True))
    pl.when((f > 0) & (f < last))(functools.partial(step, tf, False))
    pl.when(f == last)(functools.partial(step, rem, False))

    @pl.when(f == last)
    def _():
        y = x_ref[...] + 0.5 * _rms(acc_ref[...], gpost_ref[...])
        if final:
            y = _rms(y, gfin_ref[...])
        acc_ref[...] = y
        writeback(i).start()

        @pl.when(i == n_tiles - 1)
        def _():
            writeback(i).wait()


def _ffn(x, gpre, wg, wu, wd, gpost, gfin, *, final):
    n, d = x.shape
    tm, tf = FFN_TM, FFN_TF
    dff = wd.shape[0]
    nf = pl.cdiv(dff, tf)
    rem = dff - (nf - 1) * tf
    assert nf >= 2
    est = 2 * tm * d * 4 + tm * d * 4 + tm * d * 2 + 2 * 3 * d * tf * 2 + 2 * tm * tf * 4
    vec = pl.BlockSpec((1, d), lambda i, f: (0, 0))
    return pl.pallas_call(
        functools.partial(_ffn_kernel, tf=tf, rem=rem, final=final),
        grid=(n // tm, nf),
        in_specs=[
            pl.BlockSpec((tm, d), lambda i, f: (i, 0)),
            vec,
            pl.BlockSpec((None, d, tf), lambda i, f: (f, 0, 0)),
            pl.BlockSpec((None, d, tf), lambda i, f: (f, 0, 0)),
            pl.BlockSpec((tf, d), lambda i, f: (f, 0)),
            vec,
            vec,
        ],
        out_specs=pl.BlockSpec(memory_space=pl.ANY),
        out_shape=jax.ShapeDtypeStruct((n, d), F32),
        scratch_shapes=[pltpu.VMEM((tm, d), F32), pltpu.VMEM((tm, d), BF16), pltpu.SemaphoreType.DMA(())],
        compiler_params=pltpu.CompilerParams(
            dimension_semantics=("arbitrary", "arbitrary"), vmem_limit_bytes=_vmem_limit(est)),
        name="ffn_final" if final else "ffn",
    )(x, gpre, wg, wu, wd, gpost, gfin)


def _kvproj_kernel(x_ref, g_ref, w_ref, o_ref, hs_ref):
    @pl.when(pl.program_id(1) == 0)
    def _():
        hs_ref[...] = _rms(x_ref[...], g_ref[...]).astype(BF16)

    o_ref[...] = jnp.dot(hs_ref[...], w_ref[...], preferred_element_type=F32).astype(o_ref.dtype)


def _kvproj(x, gain, w):
    n, d = x.shape
    ncol = w.shape[1]
    tm, tn = n, PROJ_TN
    est = 2 * tm * d * 4 + tm * d * 2 + 2 * d * tn * 2 + 2 * tm * tn * 2 + 2 * tm * tn * 4
    return pl.pallas_call(
        _kvproj_kernel,
        grid=(n // tm, ncol // tn),
        in_specs=[
            pl.BlockSpec((tm, d), lambda i, j: (i, 0)),
            pl.BlockSpec((1, d), lambda i, j: (0, 0)),
            pl.BlockSpec((d, tn), lambda i, j: (0, j)),
        ],
        out_specs=pl.BlockSpec((tm, tn), lambda i, j: (i, j)),
        out_shape=jax.ShapeDtypeStruct((n, ncol), BF16),
        scratch_shapes=[pltpu.VMEM((tm, d), BF16)],
        compiler_params=pltpu.CompilerParams(
            dimension_semantics=("arbitrary", "arbitrary"), vmem_limit_bytes=_vmem_limit(est)),
        name="kvproj",
    )(x, gain, w)


NAT_CHUNKS = (11, 12, 9, 10, 0, 3, 6)
N_NAT_STEPS = len(NAT_CHUNKS)
N_DIL_STEPS = 3
N_GATE_STEPS = N_BRANCH * D_MODEL // PROJ_TG
FIRST_D4 = N_NAT_STEPS
FIRST_D16 = FIRST_D4 + N_DIL_STEPS
FIRST_GATE = FIRST_D16 + N_DIL_STEPS
MIX_STEPS = FIRST_GATE + N_GATE_STEPS


def _lin_chunk(s):
    t = jnp.minimum(s, FIRST_GATE - 1)
    nat = jnp.where(t < 2, 11 + t, jnp.where(t < 4, 7 + t, 3 * (t - 4)))
    d4 = 3 * (t - FIRST_D4) + 1
    d16 = 3 * (t - FIRST_D16) + 2
    return jnp.where(t < FIRST_D4, nat, jnp.where(t < FIRST_D16, d4, d16))


def _mixproj_kernel(x_ref, g_ref, wg_ref, b_ref, wi_ref, og_ref, on_ref, o4_ref, o16_ref, hs_ref, stage_ref,
                    stage2_ref):
    s = pl.program_id(1)
    tm = x_ref.shape[0]
    row_chunks = [pl.ds(r, PROJ_ROWS) for r in range(0, tm, PROJ_ROWS)]

    @pl.when(s == 0)
    def _():
        hs_ref[...] = _rms(x_ref[...], g_ref[...]).astype(BF16)

    @pl.when(s < FIRST_D4)
    def _():
        for rows in row_chunks:
            on_ref[rows, :] = jnp.dot(hs_ref[rows, :], wi_ref[...], preferred_element_type=F32).astype(on_ref.dtype)

    def dilated(o_ref, d):
        per_res = PROJ_ROWS // d
        slabs = range(PROJ_TN // HEAD_DIM)
        for rc, rows in enumerate(row_chunks):
            acc = jnp.dot(hs_ref[rows, :], wi_ref[...], preferred_element_type=F32)
            for c in slabs:
                stage_ref[c] = acc[:, c * HEAD_DIM:(c + 1) * HEAD_DIM]
            if d == DEINT_STRIDE:
                for res in range(d):
                    for c in slabs:
                        o_ref[res, pl.ds(rc * per_res, per_res), c * HEAD_DIM:(c + 1) * HEAD_DIM] = (
                            stage_ref[c, pl.ds(res, per_res, stride=d), :].astype(o_ref.dtype))
            else:
                for c in slabs:
                    for r1 in range(DEINT_STRIDE):
                        stage2_ref[c, r1] = stage_ref[c, pl.ds(r1, PROJ_ROWS // DEINT_STRIDE, stride=DEINT_STRIDE), :]
                for r1 in range(DEINT_STRIDE):
                    for r2 in range(DEINT_STRIDE):
                        res = r1 + DEINT_STRIDE * r2
                        for c in slabs:
                            o_ref[res, pl.ds(rc * per_res, per_res), c * HEAD_DIM:(c + 1) * HEAD_DIM] = (
                                stage2_ref[c, r1, pl.ds(r2, per_res, stride=DEINT_STRIDE), :].astype(o_ref.dtype))

    pl.when((s >= FIRST_D4) & (s < FIRST_D16))(functools.partial(dilated, o4_ref, DIL_PAIRS[1][1]))
    pl.when((s >= FIRST_D16) & (s < FIRST_GATE))(functools.partial(dilated, o16_ref, DIL_PAIRS[2][1]))

    @pl.when(s >= FIRST_GATE)
    def _():
        for rows in row_chunks:
            z = jnp.dot(hs_ref[rows, :], wg_ref[...], preferred_element_type=F32) + b_ref[...]
            og_ref[rows, :] = (0.5 * jnp.tanh(0.5 * z) + 0.5).astype(og_ref.dtype)


def _mixproj(x, gain, w_gate, b_gate, w_in, *, batch, seq):
    n, d = x.shape
    tm, tn, tg = PROJ_TM, PROJ_TN, PROJ_TG
    tiles_per_seq = seq // tm
    d4, d16 = DIL_PAIRS[1][1], DIL_PAIRS[2][1]

    def gate_idx(s):
        return jnp.maximum(s - FIRST_GATE, 0)

    def dil_spec(dil, first):
        return pl.BlockSpec(
            (None, dil, tm // dil, tn),
            lambda i, s: (i // tiles_per_seq, 0, i % tiles_per_seq, jnp.clip(s - first, 0, N_DIL_STEPS - 1)))

    est = (2 * tm * d * 4 + tm * d * 2 + 2 * d * (tg + tn) * 2 + 2 * tm * (tg + 3 * tn) * 2
           + PROJ_ROWS * tn * 4 + 2 * PROJ_ROWS * tg * 4)
    return pl.pallas_call(
        _mixproj_kernel,
        grid=(n // tm, MIX_STEPS),
        in_specs=[
            pl.BlockSpec((tm, d), lambda i, s: (i, 0)),
            pl.BlockSpec((1, d), lambda i, s: (0, 0)),
            pl.BlockSpec((d, tg), lambda i, s: (0, gate_idx(s))),
            pl.BlockSpec((1, tg), lambda i, s: (0, gate_idx(s))),
            pl.BlockSpec((d, tn), lambda i, s: (0, _lin_chunk(s))),
        ],
        out_specs=[
            pl.BlockSpec((tm, tg), lambda i, s: (i, gate_idx(s))),
            pl.BlockSpec((tm, tn), lambda i, s: (i, jnp.minimum(s, N_NAT_STEPS - 1))),
            dil_spec(d4, FIRST_D4),
            dil_spec(d16, FIRST_D16),
        ],
        out_shape=[
            jax.ShapeDtypeStruct((n, N_BRANCH * d), BF16),
            jax.ShapeDtypeStruct((n, N_NAT_STEPS * tn), BF16),
            jax.ShapeDtypeStruct((batch, d4, seq // d4, N_DIL_STEPS * tn), BF16),
            jax.ShapeDtypeStruct((batch, d16, seq // d16, N_DIL_STEPS * tn), BF16),
        ],
        scratch_shapes=[
            pltpu.VMEM((tm, d), BF16),
            pltpu.VMEM((tn // HEAD_DIM, PROJ_ROWS, HEAD_DIM), F32),
            pltpu.VMEM((tn // HEAD_DIM, DEINT_STRIDE, PROJ_ROWS // DEINT_STRIDE, HEAD_DIM), F32),
        ],
        compiler_params=pltpu.CompilerParams(
            dimension_semantics=("arbitrary", "arbitrary"), vmem_limit_bytes=_vmem_limit(est)),
        name="mixproj",
    )(x, gain, w_gate, b_gate, w_in)


def _attn_kernel(*refs, seq, side_cast, n_side_out):
    slopes_ref, q0_ref, k0_ref, v0_ref, q1_ref, k1_ref, v1_ref, q2_ref, k2_ref, v2_ref = refs[:10]
    n_side_in = 0 if side_cast is None else 1
    o_ref = refs[10 + n_side_in]
    m1_ref, m2_ref, l1_ref, l2_ref, n1_ref, n2_ref, va_ref, bias_ref = refs[11 + n_side_in + n_side_out:]
    if side_cast is not None:
        side_cast(refs[10], *refs[11 + n_side_in:11 + n_side_in + n_side_out])
    h = pl.program_id(1)
    scale = HEAD_DIM ** -0.5
    qb = ATT_QB
    qkv_refs = ((q0_ref, k0_ref, v0_ref), (q1_ref, k1_ref, v1_ref), (q2_ref, k2_ref, v2_ref))
    m_refs = (None, m1_ref, m2_ref)
    l_refs = (None, l1_ref, l2_ref)
    n_refs = (None, n1_ref, n2_ref)

    va_ref[:, HEAD_DIM:] = jnp.ones((seq, HEAD_DIM), BF16)

    for gi in reversed(range(N_GROUPS)):
        d = DIL_PAIRS[gi][1]
        q_ref, k_ref, v_ref = qkv_refs[gi]
        sub_len = seq // d
        kw = ATT_KW
        nb = sub_len // qb
        short = sub_len < kw
        slope_d = slopes_ref[gi, h] * float(d)

        row = lax.broadcasted_iota(jnp.int32, (qb, kw), 0)
        col = lax.broadcasted_iota(jnp.int32, (qb, kw), 1)
        for kind, off in enumerate((0, -BAND_RADIUS, qb - kw)):
            arel = jnp.abs(col - row + off)
            bias = jnp.where(arel <= BAND_RADIUS, -slope_d * arel.astype(F32), NEG)
            if short:
                key = col + off
                bias = jnp.where(key < 0, NEG, jnp.where(key < sub_len, bias, NEG))
            bias_ref[kind] = bias

        va_ref[:, :HEAD_DIM] = v_ref[...]

        def block(idx, gi=gi, d=d, q_ref=q_ref, k_ref=k_ref, sub_len=sub_len, kw=kw, nb=nb, short=short):
            qstart = pl.multiple_of(idx * qb, qb)
            if short:
                res, q0, pos, last = idx, 0, idx, d * nb - 1
                kstart = jnp.clip(qstart - BAND_RADIUS, 0, seq - kw)
            else:
                res = idx // nb
                pos, last = idx - res * nb, nb - 1
                q0 = pos * qb
                kstart = res * sub_len + jnp.clip(q0 - BAND_RADIUS, 0, sub_len - kw)
            kstart = pl.multiple_of(kstart, BAND_RADIUS)
            kind = jnp.where(pos == 0, 0, jnp.where(pos == last, 2, 1))
            s = lax.dot_general(q_ref[pl.ds(qstart, qb), :], k_ref[pl.ds(kstart, kw), :],
                                (((1,), (1,)), ((), ())), preferred_element_type=F32)
            s = s * scale + bias_ref[kind]
            m = jnp.max(s, axis=-1, keepdims=True)
            p = jnp.exp(s - m).astype(BF16)
            na = jnp.dot(p, va_ref[pl.ds(kstart, kw), :], preferred_element_type=F32)
            m = jnp.broadcast_to(m, (qb, HEAD_DIM))
            if d > 1:
                rows = pl.ds(q0 * d + res, qb, stride=d)
                m_refs[gi][rows, :] = m
                l_refs[gi][rows, :] = na[:, HEAD_DIM:]
                n_refs[gi][rows, :] = na[:, :HEAD_DIM]
            else:
                rows = pl.ds(qstart, qb)
                ms = [m] + [m_refs[g][rows, :] for g in range(1, N_GROUPS)]
                mx = functools.reduce(jnp.maximum, ms)
                e = jnp.exp(m - mx)
                num = e * na[:, :HEAD_DIM]
                den = e * na[:, HEAD_DIM:]
                for g in range(1, N_GROUPS):
                    e = jnp.exp(ms[g] - mx)
                    num += e * n_refs[g][rows, :]
                    den += e * l_refs[g][rows, :]
                o_ref[rows, :] = (num / den).astype(o_ref.dtype)

        unroll = min(ATT_UNROLL, d * nb // 2)

        def blocks(it, carry, block=block, unroll=unroll):
            for u in range(unroll):
                block(it * unroll + u)
            return carry

        lax.fori_loop(0, d * nb // unroll, blocks, 0)


def _side_cast_specs(kind, w, steps):
    rows, cols = w.shape

    def step(b, h):
        return b * HEADS_PER_GROUP + h

    if kind == "gate_up":
        br, dff = rows // steps, cols // 2
        nchunk = pl.cdiv(dff, FFN_TF)
        in_spec = pl.BlockSpec((br, cols), lambda b, h: (step(b, h), 0))
        out_specs = [pl.BlockSpec((nchunk, br, FFN_TF), lambda b, h: (0, step(b, h), 0)) for _ in range(2)]
        out_shapes = [jax.ShapeDtypeStruct((nchunk, rows, FFN_TF), BF16) for _ in range(2)]
        return functools.partial(_cast_chunked_kernel, width=dff), in_spec, out_specs, out_shapes, br * cols
    bc = cols // steps
    in_spec = pl.BlockSpec((rows, bc), lambda b, h: (0, step(b, h)))
    out_specs = [pl.BlockSpec((rows, bc), lambda b, h: (0, step(b, h)))]
    out_shapes = [jax.ShapeDtypeStruct((rows, cols), BF16)]
    return _cast_kernel, in_spec, out_specs, out_shapes, rows * bc


def _attention(nat, qkv4, qkv16, slopes, *, batch, seq, col0, side=None):
    blk0 = col0 // HEAD_DIM
    qkv4 = qkv4.reshape(batch * seq, qkv4.shape[-1])
    qkv16 = qkv16.reshape(batch * seq, qkv16.shape[-1])

    def nat_map(b, h, *, which):
        return (b, blk0 + which * HEADS_PER_GROUP + h)

    def dil_map(b, h, *, which):
        return (b, which * HEADS_PER_GROUP + h)

    in_specs = [pl.BlockSpec(memory_space=pltpu.SMEM)]
    in_specs += [pl.BlockSpec((seq, HEAD_DIM), functools.partial(nat_map, which=w)) for w in range(3)]
    in_specs += 2 * [pl.BlockSpec((seq, HEAD_DIM), functools.partial(dil_map, which=w)) for w in range(3)]
    tile = seq * HEAD_DIM
    est = 10 * 2 * tile * 2 + 6 * tile * 4 + 2 * tile * 2 + 3 * ATT_QB * ATT_KW * 4
    out_specs = [pl.BlockSpec((seq, HEAD_DIM), lambda b, h: (b, h))]
    out_shapes = [jax.ShapeDtypeStruct((batch * seq, HEADS_PER_GROUP * HEAD_DIM), BF16)]
    operands = [slopes, nat, nat, nat, qkv4, qkv4, qkv4, qkv16, qkv16, qkv16]
    side_cast = None
    if side is not None:
        kind, w = side
        side_cast, side_in, side_outs, side_shapes, block_elems = _side_cast_specs(
            kind, w, batch * HEADS_PER_GROUP)
        in_specs.append(side_in)
        operands.append(w)
        out_specs += side_outs
        out_shapes += side_shapes
        est += 2 * block_elems * (4 + 2)
    state = [pltpu.VMEM((seq, HEAD_DIM), F32) for _ in range(3 * (N_GROUPS - 1))]
    outs = pl.pallas_call(
        functools.partial(_attn_kernel, seq=seq, side_cast=side_cast, n_side_out=len(out_specs) - 1),
        grid=(batch, HEADS_PER_GROUP),
        in_specs=in_specs,
        out_specs=out_specs,
        out_shape=out_shapes,
        scratch_shapes=state + [
            pltpu.VMEM((seq, 2 * HEAD_DIM), BF16),
            pltpu.VMEM((3, ATT_QB, ATT_KW), F32),
        ],
        compiler_params=pltpu.CompilerParams(
            dimension_semantics=("arbitrary", "arbitrary"), vmem_limit_bytes=_vmem_limit(est)),
        name="attention",
    )(*operands)
    return outs[0], tuple(outs[1:])


def _pool_kernel(u_ref, w_ref, sc_ref, o_ref, pad_ref, band_ref, pooled_ref, *, seq):
    grp = pl.program_id(1)
    halo, rows = POOL_HALO, POOL_ROWS
    ext = rows + 2 * halo
    zeros = jnp.zeros((halo, POOL_GROUP), BF16)
    pad_ref[pl.ds(0, halo), :] = zeros
    pad_ref[pl.ds(halo + seq, halo), :] = zeros
    pad_ref[pl.ds(halo, seq), :] = u_ref[...]

    def run(window):
        first = lax.broadcasted_iota(jnp.int32, (rows, ext), 0) + (halo - window // 2)
        col = lax.broadcasted_iota(jnp.int32, (rows, ext), 1)
        band_ref[...] = jnp.where(col < first, 0.0, jnp.where(col < first + window, 1.0, 0.0)).astype(BF16)

        def chunk(c):
            base = pl.multiple_of(c * rows, rows)
            sums = jnp.dot(band_ref[...], pad_ref[pl.ds(base, ext), :], preferred_element_type=F32)
            tok = pad_ref[pl.ds(base + halo, rows), :].astype(F32)
            t = base + lax.broadcasted_iota(jnp.int32, (rows, 1), 0)
            lo = jnp.maximum(t - window // 2, 0)
            hi = jnp.minimum(t - window // 2 + window, seq)
            pooled_ref[pl.ds(base, rows), :] = (sums / (hi - lo).astype(F32) - tok).astype(BF16)

        def chunks(it, carry):
            for u in range(POOL_UNROLL):
                chunk(it * POOL_UNROLL + u)
            return carry

        lax.fori_loop(0, seq // rows // POOL_UNROLL, chunks, 0)

    for gi, window in enumerate(POOL_WINDOWS):
        pl.when(grp == gi)(functools.partial(run, window))

    y = jnp.dot(pooled_ref[...], w_ref[...], preferred_element_type=F32) * sc_ref[...]
    o_ref[...] = y.astype(o_ref.dtype)


def _pool(pg, w_pool, pool_scale, *, batch, seq, col0):
    blk0 = col0 // POOL_GROUP
    ngrp = len(POOL_WINDOWS)
    est = (2 * seq * POOL_GROUP * 2 * 2 + (seq + 2 * POOL_HALO) * POOL_GROUP * 4 + 2 * POOL_GROUP * POOL_GROUP * 2
           + seq * POOL_GROUP * (2 + 4))
    return pl.pallas_call(
        functools.partial(_pool_kernel, seq=seq),
        grid=(batch, ngrp),
        in_specs=[
            pl.BlockSpec((seq, POOL_GROUP), lambda b, g: (b, blk0 + g)),
            pl.BlockSpec((None, POOL_GROUP, POOL_GROUP), lambda b, g: (g, 0, 0)),
            pl.BlockSpec((1, POOL_GROUP), lambda b, g: (0, g)),
        ],
        out_specs=pl.BlockSpec((seq, POOL_GROUP), lambda b, g: (b, g)),
        out_shape=jax.ShapeDtypeStruct((batch * seq, D_POOL), BF16),
        scratch_shapes=[pltpu.VMEM((seq + 2 * POOL_HALO, POOL_GROUP), BF16),
                        pltpu.VMEM((POOL_ROWS, POOL_ROWS + 2 * POOL_HALO), BF16),
                        pltpu.VMEM((seq, POOL_GROUP), BF16)],
        compiler_params=pltpu.CompilerParams(
            dimension_semantics=("arbitrary", "arbitrary"), vmem_limit_bytes=_vmem_limit(est)),
        name="pool",
    )(pg, w_pool, pool_scale)


def _memattn_kernel(q_ref, kv_ref, o_ref):
    scale = XHEAD_DIM ** -0.5
    for hh in range(N_XHEADS):
        lo = hh * XHEAD_DIM
        q = q_ref[:, lo:lo + XHEAD_DIM]
        k = kv_ref[:, lo:lo + XHEAD_DIM]
        v = kv_ref[:, D_XATTN + lo:D_XATTN + lo + XHEAD_DIM]
        s = lax.dot_general(q, k, (((1,), (1,)), ((), ())), preferred_element_type=F32) * scale
        m = jnp.max(s, axis=-1, keepdims=True)
        p = jnp.exp(s - m)
        l = jnp.sum(p, axis=-1, keepdims=True)
        y = jnp.dot(p.astype(BF16), v, preferred_element_type=F32) / l
        o_ref[:, lo:lo + XHEAD_DIM] = y.astype(o_ref.dtype)


def _memattn(pg, kv, *, seq, col0):
    n = pg.shape[0]
    tm = MEM_TM
    blk0 = col0 // D_XATTN
    tiles_per_seq = seq // tm
    est = 2 * tm * D_XATTN * 2 * 2 + 2 * N_MEM * 2 * D_XATTN * 2 + 4 * tm * N_MEM * 4
    return pl.pallas_call(
        _memattn_kernel,
        grid=(n // tm,),
        in_specs=[
            pl.BlockSpec((tm, D_XATTN), lambda i: (i, blk0)),
            pl.BlockSpec((N_MEM, 2 * D_XATTN), lambda i: (i // tiles_per_seq, 0)),
        ],
        out_specs=pl.BlockSpec((tm, D_XATTN), lambda i: (i, 0)),
        out_shape=jax.ShapeDtypeStruct((n, D_XATTN), BF16),
        compiler_params=pltpu.CompilerParams(
            dimension_semantics=("arbitrary",), vmem_limit_bytes=_vmem_limit(est)),
        name="memattn",
    )(pg, kv)


def _merge_kernel(gates_ref, ya_ref, yp_ref, ym_ref, x_ref, wa_ref, wp_ref, wm_ref, wo_ref, gain_ref, o_ref):
    d = x_ref.shape[1]
    for r in range(0, x_ref.shape[0], MERGE_ROWS):
        rows = pl.ds(r, MERGE_ROWS)
        merged = gates_ref[rows, 0:d].astype(F32) * jnp.dot(ya_ref[rows, :], wa_ref[...], preferred_element_type=F32)
        merged += gates_ref[rows, d:2 * d].astype(F32) * jnp.dot(yp_ref[rows, :], wp_ref[...],
                                                                   preferred_element_type=F32)
        merged += gates_ref[rows, 2 * d:3 * d].astype(F32) * jnp.dot(ym_ref[rows, :], wm_ref[...],
                                                                       preferred_element_type=F32)
        z = jnp.dot(merged.astype(BF16), wo_ref[...], preferred_element_type=F32)
        o_ref[rows, :] = x_ref[rows, :] + _rms(z, gain_ref[...])


def _merge(gates, y_attn, y_pool, y_mem, x, wa, wp, wm, wo, gain):
    n, d = x.shape
    tm = MERGE_TM
    wbytes = (wa.size + wp.size + wm.size + wo.size) * 2

    def const(shape):
        return pl.BlockSpec(shape, lambda i: (0, 0), pipeline_mode=pl.Buffered(1))

    est = (2 * tm * (N_BRANCH * d + y_attn.shape[1] + y_pool.shape[1] + y_mem.shape[1]) * 2
           + 4 * tm * d * 4 + wbytes + 4 * tm * d * 4)
    return pl.pallas_call(
        _merge_kernel,
        grid=(n // tm,),
        in_specs=[
            pl.BlockSpec((tm, N_BRANCH * d), lambda i: (i, 0)),
            pl.BlockSpec((tm, y_attn.shape[1]), lambda i: (i, 0)),
            pl.BlockSpec((tm, y_pool.shape[1]), lambda i: (i, 0)),
            pl.BlockSpec((tm, y_mem.shape[1]), lambda i: (i, 0)),
            pl.BlockSpec((tm, d), lambda i: (i, 0)),
            const(wa.shape), const(wp.shape), const(wm.shape), const(wo.shape),
            pl.BlockSpec((1, d), lambda i: (0, 0)),
        ],
        out_specs=pl.BlockSpec((tm, d), lambda i: (i, 0)),
        out_shape=jax.ShapeDtypeStruct((n, d), F32),
        compiler_params=pltpu.CompilerParams(
            dimension_semantics=("arbitrary",), vmem_limit_bytes=_vmem_limit(est)),
        name="merge",
    )(gates, y_attn, y_pool, y_mem, x, wa, wp, wm, wo, gain)


def _cast_kernel(w_ref, *o_refs):
    lo = 0
    for o_ref in o_refs:
        width = o_ref.shape[1]
        o_ref[...] = w_ref[:, lo:lo + width].astype(o_ref.dtype)
        lo += width


def _cast_bf16(w, widths=None, target_bytes=CAST_BLOCK_BYTES):
    rows, cols = w.shape
    widths = (cols,) if widths is None else tuple(widths)
    br = rows
    while br * cols * 4 > target_bytes and br % 32 == 0:
        br //= 2
    est = 2 * br * cols * 4 + 2 * br * cols * 2
    outs = pl.pallas_call(
        _cast_kernel,
        grid=(rows // br,),
        in_specs=[pl.BlockSpec((br, cols), lambda i: (i, 0))],
        out_specs=[pl.BlockSpec((br, wd), lambda i: (i, 0)) for wd in widths],
        out_shape=[jax.ShapeDtypeStruct((rows, wd), BF16) for wd in widths],
        compiler_params=pltpu.CompilerParams(
            dimension_semantics=("arbitrary",), vmem_limit_bytes=_vmem_limit(est)),
        name="cast",
    )(w)
    return outs if len(widths) > 1 else outs[0]


def _cast_chunked_kernel(w_ref, *o_refs, width):
    for k, o_ref in enumerate(o_refs):
        nchunk, _, tf = o_ref.shape
        for c in range(nchunk):
            valid = min(tf, width - c * tf)
            lo = k * width + c * tf
            o_ref[c, :, :valid] = w_ref[:, lo:lo + valid].astype(o_ref.dtype)
            if valid < tf:
                o_ref[c, :, valid:] = jnp.zeros((o_ref.shape[1], tf - valid), o_ref.dtype)


def _cast_chunked(w, n_out, tf, target_bytes=CAST_BLOCK_BYTES):
    rows, cols = w.shape
    width = cols // n_out
    nchunk = pl.cdiv(width, tf)
    br = rows
    while br * cols * 4 > target_bytes and br % 32 == 0:
        br //= 2
    est = 2 * br * cols * 4 + 2 * n_out * nchunk * br * tf * 2
    return pl.pallas_call(
        functools.partial(_cast_chunked_kernel, width=width),
        grid=(rows // br,),
        in_specs=[pl.BlockSpec((br, cols), lambda i: (i, 0))],
        out_specs=[pl.BlockSpec((nchunk, br, tf), lambda i: (0, i, 0)) for _ in range(n_out)],
        out_shape=[jax.ShapeDtypeStruct((nchunk, rows, tf), BF16) for _ in range(n_out)],
        compiler_params=pltpu.CompilerParams(
            dimension_semantics=("arbitrary",), vmem_limit_bytes=_vmem_limit(est)),
        name="cast_chunked",
    )(w)


def _row(v):
    return v.reshape(1, -1).astype(F32)


def _mixer_half(x, mem, p, slopes, side):
    batch, seq, d = x.shape
    xf = x.reshape(batch * seq, d)
    x1 = _ffn(xf, p["ffn1_norm_pre"], p["ffn1_wg"], p["ffn1_wu"], p["ffn1_wd"], p["ffn1_norm_post"],
              p["ffn1_norm_post"], final=False)
    gates, nat, qkv4, qkv16 = _mixproj(x1, p["mix_norm_pre"], p["w_gate"], p["b_gate"], p["w_in"],
                                       batch=batch, seq=seq)
    col_pool = D_XATTN
    col_qkv0 = col_pool + D_POOL
    kv = _kvproj(mem.reshape(batch * N_MEM, d), p["mem_norm"], p["w_mem_kv"])
    y_attn, cast = _attention(nat, qkv4, qkv16, slopes, batch=batch, seq=seq, col0=col_qkv0, side=side)
    y_pool = _pool(nat, p["w_pool"], p["pool_scale"], batch=batch, seq=seq, col0=col_pool)
    y_mem = _memattn(nat, kv, seq=seq, col0=0)
    x2 = _merge(gates, y_attn, y_pool, y_mem, x1, p["w_br_attn"], p["w_br_pool"], p["w_br_mem"], p["w_out"],
                p["mix_norm_post"])
    return x2, cast


def _prep_params(ffn1_norm_pre, ffn1_w_up, ffn1_w_down, ffn1_norm_post, mix_norm_pre, mem_norm, w_in, w_mem_kv,
                 w_pool, pool_scale, w_br_attn, w_br_pool, w_br_mem, w_gate, b_gate, w_out, mix_norm_post,
                 ffn2_norm_pre, ffn2_w_up, ffn2_w_down, ffn2_norm_post, final_norm):
    p = {}
    p["ffn1_wg"], p["ffn1_wu"] = _cast_chunked(ffn1_w_up, 2, FFN_TF)
    p["ffn1_wd"] = _cast_bf16(ffn1_w_down)
    p["ffn2_w_up"], p["ffn2_w_down"] = ffn2_w_up, ffn2_w_down
    p["w_gate"] = _cast_bf16(w_gate)
    p["w_in"] = _cast_bf16(w_in)
    p["b_gate"] = _row(b_gate)
    p["w_mem_kv"] = _cast_bf16(w_mem_kv)
    p["w_pool"] = w_pool.astype(BF16)
    p["pool_scale"] = _row(pool_scale)
    for name, w in (("w_br_attn", w_br_attn), ("w_br_pool", w_br_pool), ("w_br_mem", w_br_mem), ("w_out", w_out)):
        p[name] = _cast_bf16(w)
    for name, v in (("ffn1_norm_pre", ffn1_norm_pre), ("ffn1_norm_post", ffn1_norm_post),
                    ("mix_norm_pre", mix_norm_pre), ("mem_norm", mem_norm), ("mix_norm_post", mix_norm_post),
                    ("ffn2_norm_pre", ffn2_norm_pre), ("ffn2_norm_post", ffn2_norm_post),
                    ("final_norm", final_norm)):
        p[name] = _row(v)
    return p


def kernel(x_prompt, x_sample, mem_prompt, mem_sample, ffn1_norm_pre, ffn1_w_up, ffn1_w_down, ffn1_norm_post,
           mix_norm_pre, mem_norm, w_in, w_mem_kv, w_pool, pool_scale, w_br_attn, w_br_pool, w_br_mem, w_gate, b_gate,
           w_out, mix_norm_post, ffn2_norm_pre, ffn2_w_up, ffn2_w_down, ffn2_norm_post, final_norm):
    layer = [ffn1_norm_pre, ffn1_w_up, ffn1_w_down, ffn1_norm_post, mix_norm_pre, mem_norm, w_in, w_mem_kv, w_pool,
             pool_scale, w_br_attn, w_br_pool, w_br_mem, w_gate, b_gate, w_out, mix_norm_post, ffn2_norm_pre,
             ffn2_w_up, ffn2_w_down, ffn2_norm_post, final_norm]
    depth = ffn1_norm_pre.shape[0]
    slopes = jnp.asarray(_alibi_slopes())
    y_prompt, y_sample = x_prompt, x_sample
    for layer_idx in range(depth):
        p = _prep_params(*[w[layer_idx] for w in layer])
        x2_prompt, (wg2, wu2) = _mixer_half(y_prompt, mem_prompt, p, slopes, ("gate_up", p["ffn2_w_up"]))
        x2_sample, (wd2,) = _mixer_half(y_sample, mem_sample, p, slopes, ("plain", p["ffn2_w_down"]))
        y_prompt, y_sample = (
            _ffn(x2, p["ffn2_norm_pre"], wg2, wu2, wd2, p["ffn2_norm_post"], p["final_norm"],
                 final=True).reshape(x.shape)
            for x2, x in ((x2_prompt, y_prompt), (x2_sample, y_sample)))
    return (y_prompt, y_sample)
```

```python
import functools

import numpy as np
import jax
import jax.numpy as jnp
from jax import lax
from jax.experimental import pallas as pl
from jax.experimental.pallas import tpu as pltpu

F32 = jnp.float32
BF16 = jnp.bfloat16

D_MODEL = 2048
N_MEM = 256
HEAD_DIM = 128
DIL_PAIRS = ((128, 1), (512, 4), (2048, 16))
HEADS_PER_GROUP = 4
N_GROUPS = len(DIL_PAIRS)
N_DIL_HEADS = HEADS_PER_GROUP * N_GROUPS
D_ATTN = N_DIL_HEADS * HEAD_DIM
POOL_WINDOWS = (2, 4, 8, 16)
POOL_GROUP = 256
D_POOL = POOL_GROUP * len(POOL_WINDOWS)
N_XHEADS = 4
XHEAD_DIM = 256
D_XATTN = N_XHEADS * XHEAD_DIM
N_BRANCH = 3
EPS = 1e-6
NEG = -1e30
BAND_RADIUS = 64

V7X_VMEM_BYTES = 64 * 1024 * 1024
MIB = 1024 * 1024

FFN_TM = 1024
FFN_TF = 1024
PROJ_TM = 1024
PROJ_TN = 512
PROJ_TG = 1536
PROJ_ROWS = 256
DEINT_STRIDE = 4
ATT_QB = 128
ATT_KW = 256
ATT_UNROLL = 16
POOL_ROWS = 256
POOL_HALO = 64
POOL_UNROLL = 4
MERGE_TM = 512
MERGE_ROWS = 512
CAST_BLOCK_BYTES = 8 * MIB


def _vmem_limit(nbytes):
    return int(min(nbytes * 5 // 4 + 8 * MIB, V7X_VMEM_BYTES - 2 * MIB))


def _rms(x, gain):
    ms = jnp.mean(x * x, axis=-1, keepdims=True)
    return x * lax.rsqrt(ms + EPS) * gain


def _alibi_slopes():
    s = 2.0 ** (-8.0 * np.arange(1, N_DIL_HEADS + 1) / N_DIL_HEADS)
    return s.reshape(HEADS_PER_GROUP, N_GROUPS).T.astype(np.float32)


def _ffn_kernel(x_ref, gpre_ref, wg_ref, wu_ref, wd_ref, gpost_ref, gfin_ref, o_hbm, acc_ref, hs_ref, sem,
                *, tf, rem, final):
    i = pl.program_id(0)
    f = pl.program_id(1)
    n_tiles = pl.num_programs(0)
    last = pl.num_programs(1) - 1
    tm = acc_ref.shape[0]

    def writeback(tile):
        return pltpu.make_async_copy(acc_ref, o_hbm.at[pl.ds(pl.multiple_of(tile * tm, tm), tm), :], sem)

    @pl.when(f == 0)
    def _():
        hs_ref[...] = _rms(x_ref[...], gpre_ref[...]).astype(BF16)

    def step(valid, first):
        hs = hs_ref[...]
        g = jnp.dot(hs, wg_ref[:, :valid], preferred_element_type=F32)
        u = jnp.dot(hs, wu_ref[:, :valid], preferred_element_type=F32)
        a = (g * jax.nn.sigmoid(g) * u).astype(BF16)
        if first:
            @pl.when(i > 0)
            def _():
                writeback(i - 1).wait()

            acc_ref[...] = jnp.dot(a, wd_ref[:valid, :], preferred_element_type=F32)
        else:
            acc_ref[...] += jnp.dot(a, wd_ref[:valid, :], preferred_element_type=F32)

    pl.when(f == 0)(functools.partial(step, tf, True))
    pl.when((f > 0) & (f < last))(functools.partial(step, tf, False))
    pl.when(f == last)(functools.partial(step, rem, False))

    @pl.when(f == last)
    def _():
        y = x_ref[...] + 0.5 * _rms(acc_ref[...], gpost_ref[...])
        if final:
            y = _rms(y, gfin_ref[...])
        acc_ref[...] = y
        writeback(i).start()

        @pl.when(i == n_tiles - 1)
        def _():
            writeback(i).wait()


def _ffn(x, gpre, wg, wu, wd, gpost, gfin, *, final):
    n, d = x.shape
    tm, tf = FFN_TM, FFN_TF
    dff = wd.shape[0]
    nf = pl.cdiv(dff, tf)
    rem = dff - (nf - 1) * tf
    assert nf >= 2
    est = 2 * tm * d * 4 + tm * d * 4 + tm * d * 2 + 2 * 3 * d * tf * 2 + 2 * tm * tf * 4
    vec = pl.BlockSpec((1, d), lambda i, f: (0, 0))
    return pl.pallas_call(
        functools.partial(_ffn_kernel, tf=tf, rem=rem, final=final),
        grid=(n // tm, nf),
        in_specs=[
            pl.BlockSpec((tm, d), lambda i, f: (i, 0)),
            vec,
            pl.BlockSpec((None, d, tf), lambda i, f: (f, 0, 0)),
            pl.BlockSpec((None, d, tf), lambda i, f: (f, 0, 0)),
            pl.BlockSpec((tf, d), lambda i, f: (f, 0)),
            vec,
            vec,
        ],
        out_specs=pl.BlockSpec(memory_space=pl.ANY),
        out_shape=jax.ShapeDtypeStruct((n, d), F32),
        scratch_shapes=[pltpu.VMEM((tm, d), F32), pltpu.VMEM((tm, d), BF16), pltpu.SemaphoreType.DMA(())],
        compiler_params=pltpu.CompilerParams(
            dimension_semantics=("arbitrary", "arbitrary"), vmem_limit_bytes=_vmem_limit(est)),
        name="ffn_final" if final else "ffn",
    )(x, gpre, wg, wu, wd, gpost, gfin)


def _kvproj_kernel(x_ref, g_ref, w_ref, o_ref, hs_ref):
    @pl.when(pl.program_id(1) == 0)
    def _():
        hs_ref[...] = _rms(x_ref[...], g_ref[...]).astype(BF16)

    o_ref[...] = jnp.dot(hs_ref[...], w_ref[...], preferred_element_type=F32).astype(o_ref.dtype)


def _kvproj(x, gain, w):
    n, d = x.shape
    ncol = w.shape[1]
    tm, tn = n, PROJ_TN
    est = 2 * tm * d * 4 + tm * d * 2 + 2 * d * tn * 2 + 2 * tm * tn * 2 + 2 * tm * tn * 4
    return pl.pallas_call(
        _kvproj_kernel,
        grid=(n // tm, ncol // tn),
        in_specs=[
            pl.BlockSpec((tm, d), lambda i, j: (i, 0)),
            pl.BlockSpec((1, d), lambda i, j: (0, 0)),
            pl.BlockSpec((d, tn), lambda i, j: (0, j)),
        ],
        out_specs=pl.BlockSpec((tm, tn), lambda i, j: (i, j)),
        out_shape=jax.ShapeDtypeStruct((n, ncol), BF16),
        scratch_shapes=[pltpu.VMEM((tm, d), BF16)],
        compiler_params=pltpu.CompilerParams(
            dimension_semantics=("arbitrary", "arbitrary"), vmem_limit_bytes=_vmem_limit(est)),
        name="kvproj",
    )(x, gain, w)


NAT_CHUNKS = (11, 12, 9, 10, 0, 3, 6)
N_NAT_STEPS = len(NAT_CHUNKS)
N_DIL_STEPS = 3
N_GATE_STEPS = N_BRANCH * D_MODEL // PROJ_TG
FIRST_D4 = N_NAT_STEPS
FIRST_D16 = FIRST_D4 + N_DIL_STEPS
FIRST_GATE = FIRST_D16 + N_DIL_STEPS
MIX_STEPS = FIRST_GATE + N_GATE_STEPS


def _lin_chunk(s):
    t = jnp.minimum(s, FIRST_GATE - 1)
    nat = jnp.where(t < 2, 11 + t, jnp.where(t < 4, 7 + t, 3 * (t - 4)))
    d4 = 3 * (t - FIRST_D4) + 1
    d16 = 3 * (t - FIRST_D16) + 2
    return jnp.where(t < FIRST_D4, nat, jnp.where(t < FIRST_D16, d4, d16))


def _mixproj_kernel(x_ref, g_ref, wg_ref, b_ref, wi_ref, og_ref, on_ref, o4_ref, o16_ref, hs_ref, stage_ref,
                    stage2_ref):
    s = pl.program_id(1)
    tm = x_ref.shape[0]
    row_chunks = [pl.ds(r, PROJ_ROWS) for r in range(0, tm, PROJ_ROWS)]

    @pl.when(s == 0)
    def _():
        hs_ref[...] = _rms(x_ref[...], g_ref[...]).astype(BF16)

    @pl.when(s < FIRST_D4)
    def _():
        for rows in row_chunks:
            on_ref[rows, :] = jnp.dot(hs_ref[rows, :], wi_ref[...], preferred_element_type=F32).astype(on_ref.dtype)

    def dilated(o_ref, d):
        per_res = PROJ_ROWS // d
        slabs = range(PROJ_TN // HEAD_DIM)
        for rc, rows in enumerate(row_chunks):
            acc = jnp.dot(hs_ref[rows, :], wi_ref[...], preferred_element_type=F32)
            for c in slabs:
                stage_ref[c] = acc[:, c * HEAD_DIM:(c + 1) * HEAD_DIM]
            if d == DEINT_STRIDE:
                for res in range(d):
                    for c in slabs:
                        o_ref[res, pl.ds(rc * per_res, per_res), c * HEAD_DIM:(c + 1) * HEAD_DIM] = (
                            stage_ref[c, pl.ds(res, per_res, stride=d), :].astype(o_ref.dtype))
            else:
                for c in slabs:
                    for r1 in range(DEINT_STRIDE):
                        stage2_ref[c, r1] = stage_ref[c, pl.ds(r1, PROJ_ROWS // DEINT_STRIDE, stride=DEINT_STRIDE), :]
                for r1 in range(DEINT_STRIDE):
                    for r2 in range(DEINT_STRIDE):
                        res = r1 + DEINT_STRIDE * r2
                        for c in slabs:
                            o_ref[res, pl.ds(rc * per_res, per_res), c * HEAD_DIM:(c + 1) * HEAD_DIM] = (
                                stage2_ref[c, r1, pl.ds(r2, per_res, stride=DEINT_STRIDE), :].astype(o_ref.dtype))

    pl.when((s >= FIRST_D4) & (s < FIRST_D16))(functools.partial(dilated, o4_ref, DIL_PAIRS[1][1]))
    pl.when((s >= FIRST_D16) & (s < FIRST_GATE))(functools.partial(dilated, o16_ref, DIL_PAIRS[2][1]))

    @pl.when(s >= FIRST_GATE)
    def _():
        for rows in row_chunks:
            z = jnp.dot(hs_ref[rows, :], wg_ref[...], preferred_element_type=F32) + b_ref[...]
            og_ref[rows, :] = (0.5 * jnp.tanh(0.5 * z) + 0.5).astype(og_ref.dtype)


def _mixproj(x, gain, w_gate, b_gate, w_in, *, batch, seq):
    n, d = x.shape
    tm, tn, tg = PROJ_TM, PROJ_TN, PROJ_TG
    tiles_per_seq = seq // tm
    d4, d16 = DIL_PAIRS[1][1], DIL_PAIRS[2][1]

    def gate_idx(s):
        return jnp.maximum(s - FIRST_GATE, 0)

    def dil_spec(dil, first):
        return pl.BlockSpec(
            (None, dil, tm // dil, tn),
            lambda i, s: (i // tiles_per_seq, 0, i % tiles_per_seq, jnp.clip(s - first, 0, N_DIL_STEPS - 1)))

    est = (2 * tm * d * 4 + tm * d * 2 + 2 * d * (tg + tn) * 2 + 2 * tm * (tg + 3 * tn) * 2
           + PROJ_ROWS * tn * 4 + 2 * PROJ_ROWS * tg * 4)
    return pl.pallas_call(
        _mixproj_kernel,
        grid=(n // tm, MIX_STEPS),
        in_specs=[
            pl.BlockSpec((tm, d), lambda i, s: (i, 0)),
            pl.BlockSpec((1, d), lambda i, s: (0, 0)),
            pl.BlockSpec((d, tg), lambda i, s: (0, gate_idx(s))),
            pl.BlockSpec((1, tg), lambda i, s: (0, gate_idx(s))),
            pl.BlockSpec((d, tn), lambda i, s: (0, _lin_chunk(s))),
        ],
        out_specs=[
            pl.BlockSpec((tm, tg), lambda i, s: (i, gate_idx(s))),
            pl.BlockSpec((tm, tn), lambda i, s: (i, jnp.minimum(s, N_NAT_STEPS - 1))),
            dil_spec(d4, FIRST_D4),
            dil_spec(d16, FIRST_D16),
        ],
        out_shape=[
            jax.ShapeDtypeStruct((n, N_BRANCH * d), BF16),
            jax.ShapeDtypeStruct((n, N_NAT_STEPS * tn), BF16),
            jax.ShapeDtypeStruct((batch, d4, seq // d4, N_DIL_STEPS * tn), BF16),
            jax.ShapeDtypeStruct((batch, d16, seq // d16, N_DIL_STEPS * tn), BF16),
        ],
        scratch_shapes=[
            pltpu.VMEM((tm, d), BF16),
            pltpu.VMEM((tn // HEAD_DIM, PROJ_ROWS, HEAD_DIM), F32),
            pltpu.VMEM((tn // HEAD_DIM, DEINT_STRIDE, PROJ_ROWS // DEINT_STRIDE, HEAD_DIM), F32),
        ],
        compiler_params=pltpu.CompilerParams(
            dimension_semantics=("arbitrary", "arbitrary"), vmem_limit_bytes=_vmem_limit(est)),
        name="mixproj",
    )(x, gain, w_gate, b_gate, w_in)


def _attn_kernel(*refs, seq, side_cast, n_side_out):
    slopes_ref, q0_ref, k0_ref, v0_ref, q1_ref, k1_ref, v1_ref, q2_ref, k2_ref, v2_ref = refs[:10]
    n_side_in = 0 if side_cast is None else 1
    o_ref = refs[10 + n_side_in]
    m1_ref, m2_ref, l1_ref, l2_ref, n1_ref, n2_ref, va_ref, bias_ref = refs[11 + n_side_in + n_side_out:]
    if side_cast is not None:
        side_cast(refs[10], *refs[11 + n_side_in:11 + n_side_in + n_side_out])
    h = pl.program_id(1)
    scale = HEAD_DIM ** -0.5
    qb = ATT_QB
    qkv_refs = ((q0_ref, k0_ref, v0_ref), (q1_ref, k1_ref, v1_ref), (q2_ref, k2_ref, v2_ref))
    m_refs = (None, m1_ref, m2_ref)
    l_refs = (None, l1_ref, l2_ref)
    n_refs = (None, n1_ref, n2_ref)

    va_ref[:, HEAD_DIM:] = jnp.ones((seq, HEAD_DIM), BF16)

    for gi in reversed(range(N_GROUPS)):
        d = DIL_PAIRS[gi][1]
        q_ref, k_ref, v_ref = qkv_refs[gi]
        sub_len = seq // d
        kw = ATT_KW
        nb = sub_len // qb
        short = sub_len < kw
        slope_d = slopes_ref[gi, h] * float(d)

        row = lax.broadcasted_iota(jnp.int32, (qb, kw), 0)
        col = lax.broadcasted_iota(jnp.int32, (qb, kw), 1)
        for kind, off in enumerate((0, -BAND_RADIUS, qb - kw)):
            arel = jnp.abs(col - row + off)
            bias = jnp.where(arel <= BAND_RADIUS, -slope_d * arel.astype(F32), NEG)
            if short:
                key = col + off
                bias = jnp.where(key < 0, NEG, jnp.where(key < sub_len, bias, NEG))
            bias_ref[kind] = bias

        va_ref[:, :HEAD_DIM] = v_ref[...]

        def block(idx, gi=gi, d=d, q_ref=q_ref, k_ref=k_ref, sub_len=sub_len, kw=kw, nb=nb, short=short):
            qstart = pl.multiple_of(idx * qb, qb)
            if short:
                res, q0, pos, last = idx, 0, idx, d * nb - 1
                kstart = jnp.clip(qstart - BAND_RADIUS, 0, seq - kw)
            else:
                res = idx // nb
                pos, last = idx - res * nb, nb - 1
                q0 = pos * qb
                kstart = res * sub_len + jnp.clip(q0 - BAND_RADIUS, 0, sub_len - kw)
            kstart = pl.multiple_of(kstart, BAND_RADIUS)
            kind = jnp.where(pos == 0, 0, jnp.where(pos == last, 2, 1))
            s = lax.dot_general(q_ref[pl.ds(qstart, qb), :], k_ref[pl.ds(kstart, kw), :],
                                (((1,), (1,)), ((), ())), preferred_element_type=F32)
            s = s * scale + bias_ref[kind]
            m = jnp.max(s, axis=-1, keepdims=True)
            p = jnp.exp(s - m).astype(BF16)
            na = jnp.dot(p, va_ref[pl.ds(kstart, kw), :], preferred_element_type=F32)
            m = jnp.broadcast_to(m, (qb, HEAD_DIM))
            if d > 1:
                rows = pl.ds(q0 * d + res, qb, stride=d)
                m_refs[gi][rows, :] = m
                l_refs[gi][rows, :] = na[:, HEAD_DIM:]
                n_refs[gi][rows, :] = na[:, :HEAD_DIM]
            else:
                rows = pl.ds(qstart, qb)
                ms = [m] + [m_refs[g][rows, :] for g in range(1, N_GROUPS)]
                mx = functools.reduce(jnp.maximum, ms)
                e = jnp.exp(m - mx)
                num = e * na[:, :HEAD_DIM]
                den = e * na[:, HEAD_DIM:]
                for g in range(1, N_GROUPS):
                    e = jnp.exp(ms[g] - mx)
                    num += e * n_refs[g][rows, :]
                    den += e * l_refs[g][rows, :]
                o_ref[rows, :] = (num / den).astype(o_ref.dtype)

        unroll = min(ATT_UNROLL, d * nb // 2)

        def blocks(it, carry, block=block, unroll=unroll):
            for u in range(unroll):
                block(it * unroll + u)
            return carry

        lax.fori_loop(0, d * nb // unroll, blocks, 0)


def _side_cast_specs(kind, w, steps):
    rows, cols = w.shape

    def step(b, h):
        return b * HEADS_PER_GROUP + h

    if kind == "gate_up":
        br, dff = rows // steps, cols // 2
        nchunk = pl.cdiv(dff, FFN_TF)
        in_spec = pl.BlockSpec((br, cols), lambda b, h: (step(b, h), 0))
        out_specs = [pl.BlockSpec((nchunk, br, FFN_TF), lambda b, h: (0, step(b, h), 0)) for _ in range(2)]
        out_shapes = [jax.ShapeDtypeStruct((nchunk, rows, FFN_TF), BF16) for _ in range(2)]
        return functools.partial(_cast_chunked_kernel, width=dff), in_spec, out_specs, out_shapes, br * cols
    bc = cols // steps
    in_spec = pl.BlockSpec((rows, bc), lambda b, h: (0, step(b, h)))
    out_specs = [pl.BlockSpec((rows, bc), lambda b, h: (0, step(b, h)))]
    out_shapes = [jax.ShapeDtypeStruct((rows, cols), BF16)]
    return _cast_kernel, in_spec, out_specs, out_shapes, rows * bc


def _attention(nat, qkv4, qkv16, slopes, *, batch, seq, col0, side=None):
    blk0 = col0 // HEAD_DIM
    qkv4 = qkv4.reshape(batch * seq, qkv4.shape[-1])
    qkv16 = qkv16.reshape(batch * seq, qkv16.shape[-1])

    def nat_map(b, h, *, which):
        return (b, blk0 + which * HEADS_PER_GROUP + h)

    def dil_map(b, h, *, which):
        return (b, which * HEADS_PER_GROUP + h)

    in_specs = [pl.BlockSpec(memory_space=pltpu.SMEM)]
    in_specs += [pl.BlockSpec((seq, HEAD_DIM), functools.partial(nat_map, which=w)) for w in range(3)]
    in_specs += 2 * [pl.BlockSpec((seq, HEAD_DIM), functools.partial(dil_map, which=w)) for w in range(3)]
    tile = seq * HEAD_DIM
    est = 10 * 2 * tile * 2 + 6 * tile * 4 + 2 * tile * 2 + 3 * ATT_QB * ATT_KW * 4
    out_specs = [pl.BlockSpec((seq, HEAD_DIM), lambda b, h: (b, h))]
    out_shapes = [jax.ShapeDtypeStruct((batch * seq, HEADS_PER_GROUP * HEAD_DIM), BF16)]
    operands = [slopes, nat, nat, nat, qkv4, qkv4, qkv4, qkv16, qkv16, qkv16]
    side_cast = None
    if side is not None:
        kind, w = side
        side_cast, side_in, side_outs, side_shapes, block_elems = _side_cast_specs(
            kind, w, batch * HEADS_PER_GROUP)
        in_specs.append(side_in)
        operands.append(w)
        out_specs += side_outs
        out_shapes += side_shapes
        est += 2 * block_elems * (4 + 2)
    state = [pltpu.VMEM((seq, HEAD_DIM), F32) for _ in range(3 * (N_GROUPS - 1))]
    outs = pl.pallas_call(
        functools.partial(_attn_kernel, seq=seq, side_cast=side_cast, n_side_out=len(out_specs) - 1),
        grid=(batch, HEADS_PER_GROUP),
        in_specs=in_specs,
        out_specs=out_specs,
        out_shape=out_shapes,
        scratch_shapes=state + [
            pltpu.VMEM((seq, 2 * HEAD_DIM), BF16),
            pltpu.VMEM((3, ATT_QB, ATT_KW), F32),
        ],
        compiler_params=pltpu.CompilerParams(
            dimension_semantics=("arbitrary", "arbitrary"), vmem_limit_bytes=_vmem_limit(est)),
        name="attention",
    )(*operands)
    return outs[0], tuple(outs[1:])


def _pool_kernel(u_ref, w_ref, sc_ref, o_ref, pad_ref, band_ref, pooled_ref, *, seq):
    grp = pl.program_id(1)
    halo, rows = POOL_HALO, POOL_ROWS
    ext = rows + 2 * halo
    zeros = jnp.zeros((halo, POOL_GROUP), BF16)
    pad_ref[pl.ds(0, halo), :] = zeros
    pad_ref[pl.ds(halo + seq, halo), :] = zeros
    pad_ref[pl.ds(halo, seq), :] = u_ref[...]

    def run(window):
        first = lax.broadcasted_iota(jnp.int32, (rows, ext), 0) + (halo - window // 2)
        col = lax.broadcasted_iota(jnp.int32, (rows, ext), 1)
        band_ref[...] = jnp.where(col < first, 0.0, jnp.where(col < first + window, 1.0, 0.0)).astype(BF16)

        def chunk(c):
            base = pl.multiple_of(c * rows, rows)
            sums = jnp.dot(band_ref[...], pad_ref[pl.ds(base, ext), :], preferred_element_type=F32)
            tok = pad_ref[pl.ds(base + halo, rows), :].astype(F32)
            t = base + lax.broadcasted_iota(jnp.int32, (rows, 1), 0)
            lo = jnp.maximum(t - window // 2, 0)
            hi = jnp.minimum(t - window // 2 + window, seq)
            pooled_ref[pl.ds(base, rows), :] = (sums / (hi - lo).astype(F32) - tok).astype(BF16)

        def chunks(it, carry):
            for u in range(POOL_UNROLL):
                chunk(it * POOL_UNROLL + u)
            return carry

        lax.fori_loop(0, seq // rows // POOL_UNROLL, chunks, 0)

    for gi, window in enumerate(POOL_WINDOWS):
        pl.when(grp == gi)(functools.partial(run, window))

    y = jnp.dot(pooled_ref[...], w_ref[...], preferred_element_type=F32) * sc_ref[...]
    o_ref[...] = y.astype(o_ref.dtype)


def _pool(pg, w_pool, pool_scale, *, batch, seq, col0):
    blk0 = col0 // POOL_GROUP
    ngrp = len(POOL_WINDOWS)
    est = (2 * seq * POOL_GROUP * 2 * 2 + (seq + 2 * POOL_HALO) * POOL_GROUP * 4 + 2 * POOL_GROUP * POOL_GROUP * 2
           + seq * POOL_GROUP * (2 + 4))
    return pl.pallas_call(
        functools.partial(_pool_kernel, seq=seq),
        grid=(batch, ngrp),
        in_specs=[
            pl.BlockSpec((seq, POOL_GROUP), lambda b, g: (b, blk0 + g)),
            pl.BlockSpec((None, POOL_GROUP, POOL_GROUP), lambda b, g: (g, 0, 0)),
            pl.BlockSpec((1, POOL_GROUP), lambda b, g: (0, g)),
        ],
        out_specs=pl.BlockSpec((seq, POOL_GROUP), lambda b, g: (b, g)),
        out_shape=jax.ShapeDtypeStruct((batch * seq, D_POOL), BF16),
        scratch_shapes=[pltpu.VMEM((seq + 2 * POOL_HALO, POOL_GROUP), BF16),
                        pltpu.VMEM((POOL_ROWS, POOL_ROWS + 2 * POOL_HALO), BF16),
                        pltpu.VMEM((seq, POOL_GROUP), BF16)],
        compiler_params=pltpu.CompilerParams(
            dimension_semantics=("arbitrary", "arbitrary"), vmem_limit_bytes=_vmem_limit(est)),
        name="pool",
    )(pg, w_pool, pool_scale)


def _mem_attention(q, kv_ref):
    scale = XHEAD_DIM ** -0.5
    heads = []
    for hh in range(N_XHEADS):
        lo = hh * XHEAD_DIM
        k = kv_ref[:, lo:lo + XHEAD_DIM]
        v = kv_ref[:, D_XATTN + lo:D_XATTN + lo + XHEAD_DIM]
        s = lax.dot_general(q[:, lo:lo + XHEAD_DIM], k, (((1,), (1,)), ((), ())), preferred_element_type=F32) * scale
        m = jnp.max(s, axis=-1, keepdims=True)
        p = jnp.exp(s - m)
        l = jnp.sum(p, axis=-1, keepdims=True)
        heads.append((jnp.dot(p.astype(BF16), v, preferred_element_type=F32) / l).astype(BF16))
    return jnp.concatenate(heads, axis=1)


def _merge_kernel(gates_ref, ya_ref, yp_ref, qm_ref, kv_ref, x_ref, wa_ref, wp_ref, wm_ref, wo_ref, gain_ref, o_ref):
    d = x_ref.shape[1]
    for r in range(0, x_ref.shape[0], MERGE_ROWS):
        rows = pl.ds(r, MERGE_ROWS)
        merged = gates_ref[rows, 0:d].astype(F32) * jnp.dot(ya_ref[rows, :], wa_ref[...], preferred_element_type=F32)
        merged += gates_ref[rows, d:2 * d].astype(F32) * jnp.dot(yp_ref[rows, :], wp_ref[...],
                                                                   preferred_element_type=F32)
        y_mem = _mem_attention(qm_ref[rows, :], kv_ref)
        merged += gates_ref[rows, 2 * d:3 * d].astype(F32) * jnp.dot(y_mem, wm_ref[...],
                                                                       preferred_element_type=F32)
        z = jnp.dot(merged.astype(BF16), wo_ref[...], preferred_element_type=F32)
        o_ref[rows, :] = x_ref[rows, :] + _rms(z, gain_ref[...])


def _merge(gates, y_attn, y_pool, nat, kv, x, wa, wp, wm, wo, gain, *, seq, col_qmem):
    n, d = x.shape
    tm = MERGE_TM
    tiles_per_seq = seq // tm
    wbytes = (wa.size + wp.size + wm.size + wo.size) * 2

    def const(shape):
        return pl.BlockSpec(shape, lambda i: (0, 0), pipeline_mode=pl.Buffered(1))

    est = (2 * tm * (N_BRANCH * d + y_attn.shape[1] + y_pool.shape[1] + D_XATTN) * 2 + 2 * N_MEM * 2 * D_XATTN * 2
           + 4 * tm * d * 4 + wbytes + 4 * tm * d * 4)
    return pl.pallas_call(
        _merge_kernel,
        grid=(n // tm,),
        in_specs=[
            pl.BlockSpec((tm, N_BRANCH * d), lambda i: (i, 0)),
            pl.BlockSpec((tm, y_attn.shape[1]), lambda i: (i, 0)),
            pl.BlockSpec((tm, y_pool.shape[1]), lambda i: (i, 0)),
            pl.BlockSpec((tm, D_XATTN), lambda i: (i, col_qmem // D_XATTN)),
            pl.BlockSpec((N_MEM, 2 * D_XATTN), lambda i: (i // tiles_per_seq, 0)),
            pl.BlockSpec((tm, d), lambda i: (i, 0)),
            const(wa.shape), const(wp.shape), const(wm.shape), const(wo.shape),
            pl.BlockSpec((1, d), lambda i: (0, 0)),
        ],
        out_specs=pl.BlockSpec((tm, d), lambda i: (i, 0)),
        out_shape=jax.ShapeDtypeStruct((n, d), F32),
        compiler_params=pltpu.CompilerParams(
            dimension_semantics=("arbitrary",), vmem_limit_bytes=_vmem_limit(est)),
        name="merge",
    )(gates, y_attn, y_pool, nat, kv, x, wa, wp, wm, wo, gain)


def _cast_kernel(w_ref, *o_refs):
    lo = 0
    for o_ref in o_refs:
        width = o_ref.shape[1]
        o_ref[...] = w_ref[:, lo:lo + width].astype(o_ref.dtype)
        lo += width


def _cast_bf16(w, widths=None, target_bytes=CAST_BLOCK_BYTES):
    rows, cols = w.shape
    widths = (cols,) if widths is None else tuple(widths)
    br = rows
    while br * cols * 4 > target_bytes and br % 32 == 0:
        br //= 2
    est = 2 * br * cols * 4 + 2 * br * cols * 2
    outs = pl.pallas_call(
        _cast_kernel,
        grid=(rows // br,),
        in_specs=[pl.BlockSpec((br, cols), lambda i: (i, 0))],
        out_specs=[pl.BlockSpec((br, wd), lambda i: (i, 0)) for wd in widths],
        out_shape=[jax.ShapeDtypeStruct((rows, wd), BF16) for wd in widths],
        compiler_params=pltpu.CompilerParams(
            dimension_semantics=("arbitrary",), vmem_limit_bytes=_vmem_limit(est)),
        name="cast",
    )(w)
    return outs if len(widths) > 1 else outs[0]


def _cast_chunked_kernel(w_ref, *o_refs, width):
    for k, o_ref in enumerate(o_refs):
        nchunk, _, tf = o_ref.shape
        for c in range(nchunk):
            valid = min(tf, width - c * tf)
            lo = k * width + c * tf
            o_ref[c, :, :valid] = w_ref[:, lo:lo + valid].astype(o_ref.dtype)
            if valid < tf:
                o_ref[c, :, valid:] = jnp.zeros((o_ref.shape[1], tf - valid), o_ref.dtype)


def _cast_chunked(w, n_out, tf, target_bytes=CAST_BLOCK_BYTES):
    rows, cols = w.shape
    width = cols // n_out
    nchunk = pl.cdiv(width, tf)
    br = rows
    while br * cols * 4 > target_bytes and br % 32 == 0:
        br //= 2
    est = 2 * br * cols * 4 + 2 * n_out * nchunk * br * tf * 2
    return pl.pallas_call(
        functools.partial(_cast_chunked_kernel, width=width),
        grid=(rows // br,),
        in_specs=[pl.BlockSpec((br, cols), lambda i: (i, 0))],
        out_specs=[pl.BlockSpec((nchunk, br, tf), lambda i: (0, i, 0)) for _ in range(n_out)],
        out_shape=[jax.ShapeDtypeStruct((nchunk, rows, tf), BF16) for _ in range(n_out)],
        compiler_params=pltpu.CompilerParams(
            dimension_semantics=("arbitrary",), vmem_limit_bytes=_vmem_limit(est)),
        name="cast_chunked",
    )(w)


def _row(v):
    return v.reshape(1, -1).astype(F32)


def _mixer_half(x, mem, p, slopes, side):
    batch, seq, d = x.shape
    xf = x.reshape(batch * seq, d)
    x1 = _ffn(xf, p["ffn1_norm_pre"], p["ffn1_wg"], p["ffn1_wu"], p["ffn1_wd"], p["ffn1_norm_post"],
              p["ffn1_norm_post"], final=False)
    gates, nat, qkv4, qkv16 = _mixproj(x1, p["mix_norm_pre"], p["w_gate"], p["b_gate"], p["w_in"],
                                       batch=batch, seq=seq)
    col_pool = D_XATTN
    col_qkv0 = col_pool + D_POOL
    kv = _kvproj(mem.reshape(batch * N_MEM, d), p["mem_norm"], p["w_mem_kv"])
    y_attn, cast = _attention(nat, qkv4, qkv16, slopes, batch=batch, seq=seq, col0=col_qkv0, side=side)
    y_pool = _pool(nat, p["w_pool"], p["pool_scale"], batch=batch, seq=seq, col0=col_pool)
    x2 = _merge(gates, y_attn, y_pool, nat, kv, x1, p["w_br_attn"], p["w_br_pool"], p["w_br_mem"], p["w_out"],
                p["mix_norm_post"], seq=seq, col_qmem=0)
    return x2, cast


def _prep_params(ffn1_norm_pre, ffn1_w_up, ffn1_w_down, ffn1_norm_post, mix_norm_pre, mem_norm, w_in, w_mem_kv,
                 w_pool, pool_scale, w_br_attn, w_br_pool, w_br_mem, w_gate, b_gate, w_out, mix_norm_post,
                 ffn2_norm_pre, ffn2_w_up, ffn2_w_down, ffn2_norm_post, final_norm):
    p = {}
    p["ffn1_wg"], p["ffn1_wu"] = _cast_chunked(ffn1_w_up, 2, FFN_TF)
    p["ffn1_wd"] = _cast_bf16(ffn1_w_down)
    p["ffn2_w_up"], p["ffn2_w_down"] = ffn2_w_up, ffn2_w_down
    p["w_gate"] = _cast_bf16(w_gate)
    p["w_in"] = _cast_bf16(w_in)
    p["b_gate"] = _row(b_gate)
    p["w_mem_kv"] = _cast_bf16(w_mem_kv)
    p["w_pool"] = w_pool.astype(BF16)
    p["pool_scale"] = _row(pool_scale)
    for name, w in (("w_br_attn", w_br_attn), ("w_br_pool", w_br_pool), ("w_br_mem", w_br_mem), ("w_out", w_out)):
        p[name] = _cast_bf16(w)
    for name, v in (("ffn1_norm_pre", ffn1_norm_pre), ("ffn1_norm_post", ffn1_norm_post),
                    ("mix_norm_pre", mix_norm_pre), ("mem_norm", mem_norm), ("mix_norm_post", mix_norm_post),
                    ("ffn2_norm_pre", ffn2_norm_pre), ("ffn2_norm_post", ffn2_norm_post),
                    ("final_norm", final_norm)):
        p[name] = _row(v)
    return p


def kernel(x_prompt, x_sample, mem_prompt, mem_sample, ffn1_norm_pre, ffn1_w_up, ffn1_w_down, ffn1_norm_post,
           mix_norm_pre, mem_norm, w_in, w_mem_kv, w_pool, pool_scale, w_br_attn, w_br_pool, w_br_mem, w_gate, b_gate,
           w_out, mix_norm_post, ffn2_norm_pre, ffn2_w_up, ffn2_w_down, ffn2_norm_post, final_norm):
    layer = [ffn1_norm_pre, ffn1_w_up, ffn1_w_down, ffn1_norm_post, mix_norm_pre, mem_norm, w_in, w_mem_kv, w_pool,
             pool_scale, w_br_attn, w_br_pool, w_br_mem, w_gate, b_gate, w_out, mix_norm_post, ffn2_norm_pre,
             ffn2_w_up, ffn2_w_down, ffn2_norm_post, final_norm]
    depth = ffn1_norm_pre.shape[0]
    slopes = jnp.asarray(_alibi_slopes())
    y_prompt, y_sample = x_prompt, x_sample
    for layer_idx in range(depth):
        p = _prep_params(*[w[layer_idx] for w in layer])
        x2_prompt, (wg2, wu2) = _mixer_half(y_prompt, mem_prompt, p, slopes, ("gate_up", p["ffn2_w_up"]))
        x2_sample, (wd2,) = _mixer_half(y_sample, mem_sample, p, slopes, ("plain", p["ffn2_w_down"]))
        y_prompt, y_sample = (
            _ffn(x2, p["ffn2_norm_pre"], wg2, wu2, wd2, p["ffn2_norm_post"], p["final_norm"],
                 final=True).reshape(x.shape)
            for x2, x in ((x2_prompt, y_prompt), (x2_sample, y_sample)))
    return (y_prompt, y_sample)
```

```python
import functools

import numpy as np
import jax
import jax.numpy as jnp
from jax import lax
from jax.experimental import pallas as pl
from jax.experimental.pallas import tpu as pltpu

F32 = jnp.float32
BF16 = jnp.bfloat16

D_MODEL = 2048
N_MEM = 256
HEAD_DIM = 128
DIL_PAIRS = ((128, 1), (512, 4), (2048, 16))
HEADS_PER_GROUP = 4
N_GROUPS = len(DIL_PAIRS)
N_DIL_HEADS = HEADS_PER_GROUP * N_GROUPS
D_ATTN = N_DIL_HEADS * HEAD_DIM
POOL_WINDOWS = (2, 4, 8, 16)
POOL_GROUP = 256
D_POOL = POOL_GROUP * len(POOL_WINDOWS)
N_XHEADS = 4
XHEAD_DIM = 256
D_XATTN = N_XHEADS * XHEAD_DIM
N_BRANCH = 3
EPS = 1e-6
NEG = -1e30
BAND_RADIUS = 64

V7X_VMEM_BYTES = 64 * 1024 * 1024
MIB = 1024 * 1024

FFN_TM = 1024
FFN_TF = 1024
PROJ_TM = 1024
PROJ_TN = 512
PROJ_TG = 1536
PROJ_ROWS = 256
KV_TN = 256
DEINT_STRIDE = 4
ATT_QB = 128
ATT_KW = 256
ATT_UNROLL = 16
POOL_ROWS = 256
POOL_HALO = 64
POOL_UNROLL = 4
MERGE_TM = 512
MERGE_ROWS = 512
CAST_BLOCK_BYTES = 8 * MIB


def _vmem_limit(nbytes):
    return int(min(nbytes * 5 // 4 + 8 * MIB, V7X_VMEM_BYTES - 2 * MIB))


def _rms(x, gain):
    ms = jnp.mean(x * x, axis=-1, keepdims=True)
    return x * lax.rsqrt(ms + EPS) * gain


def _alibi_slopes():
    s = 2.0 ** (-8.0 * np.arange(1, N_DIL_HEADS + 1) / N_DIL_HEADS)
    return s.reshape(HEADS_PER_GROUP, N_GROUPS).T.astype(np.float32)


def _ffn_kernel(x_ref, gpre_ref, wg_ref, wu_ref, wd_ref, gpost_ref, gfin_ref, o_hbm, acc_ref, hs_ref, sem,
                *, tf, rem, final):
    i = pl.program_id(0)
    f = pl.program_id(1)
    n_tiles = pl.num_programs(0)
    last = pl.num_programs(1) - 1
    tm = acc_ref.shape[0]

    def writeback(tile):
        return pltpu.make_async_copy(acc_ref, o_hbm.at[pl.ds(pl.multiple_of(tile * tm, tm), tm), :], sem)

    @pl.when(f == 0)
    def _():
        hs_ref[...] = _rms(x_ref[...], gpre_ref[...]).astype(BF16)

    def step(valid, first):
        hs = hs_ref[...]
        g = jnp.dot(hs, wg_ref[:, :valid], preferred_element_type=F32)
        u = jnp.dot(hs, wu_ref[:, :valid], preferred_element_type=F32)
        a = (g * jax.nn.sigmoid(g) * u).astype(BF16)
        if first:
            @pl.when(i > 0)
            def _():
                writeback(i - 1).wait()

            acc_ref[...] = jnp.dot(a, wd_ref[:valid, :], preferred_element_type=F32)
        else:
            acc_ref[...] += jnp.dot(a, wd_ref[:valid, :], preferred_element_type=F32)

    pl.when(f == 0)(functools.partial(step, tf, True))
    pl.when((f > 0) & (f < last))(functools.partial(step, tf, False))
    pl.when(f == last)(functools.partial(step, rem, False))

    @pl.when(f == last)
    def _():
        y = x_ref[...] + 0.5 * _rms(acc_ref[...], gpost_ref[...])
        if final:
            y = _rms(y, gfin_ref[...])
        acc_ref[...] = y
        writeback(i).start()

        @pl.when(i == n_tiles - 1)
        def _():
            writeback(i).wait()


def _ffn(x, gpre, wg, wu, wd, gpost, gfin, *, final):
    n, d = x.shape
    tm, tf = FFN_TM, FFN_TF
    dff = wd.shape[0]
    nf = pl.cdiv(dff, tf)
    rem = dff - (nf - 1) * tf
    assert nf >= 2
    est = 2 * tm * d * 4 + tm * d * 4 + tm * d * 2 + 2 * 3 * d * tf * 2 + 2 * tm * tf * 4
    vec = pl.BlockSpec((1, d), lambda i, f: (0, 0))
    return pl.pallas_call(
        functools.partial(_ffn_kernel, tf=tf, rem=rem, final=final),
        grid=(n // tm, nf),
        in_specs=[
            pl.BlockSpec((tm, d), lambda i, f: (i, 0)),
            vec,
            pl.BlockSpec((None, d, tf), lambda i, f: (f, 0, 0)),
            pl.BlockSpec((None, d, tf), lambda i, f: (f, 0, 0)),
            pl.BlockSpec((tf, d), lambda i, f: (f, 0)),
            vec,
            vec,
        ],
        out_specs=pl.BlockSpec(memory_space=pl.ANY),
        out_shape=jax.ShapeDtypeStruct((n, d), F32),
        scratch_shapes=[pltpu.VMEM((tm, d), F32), pltpu.VMEM((tm, d), BF16), pltpu.SemaphoreType.DMA(())],
        compiler_params=pltpu.CompilerParams(
            dimension_semantics=("arbitrary", "arbitrary"), vmem_limit_bytes=_vmem_limit(est)),
        name="ffn_final" if final else "ffn",
    )(x, gpre, wg, wu, wd, gpost, gfin)


def _kvproj_kernel(x_ref, g_ref, w_ref, side_ref, o_ref, side_o_ref, hs_ref):
    @pl.when(pl.program_id(0) == 0)
    def _():
        hs_ref[...] = _rms(x_ref[...], g_ref[...]).astype(BF16)

    o_ref[...] = jnp.dot(hs_ref[...], w_ref[...], preferred_element_type=F32).astype(o_ref.dtype)
    side_o_ref[...] = side_ref[...].astype(side_o_ref.dtype)


def _kvproj(x, gain, w, side_w):
    n, d = x.shape
    ncol = w.shape[1]
    tn = KV_TN
    steps = ncol // tn
    side_rows, side_cols = side_w.shape
    br = side_rows // steps
    est = (2 * n * d * 4 + n * d * 2 + 2 * d * tn * 2 + 2 * n * tn * 2 + 2 * n * tn * 4
           + 2 * br * side_cols * (4 + 2))
    return pl.pallas_call(
        _kvproj_kernel,
        grid=(steps,),
        in_specs=[
            pl.BlockSpec((n, d), lambda j: (0, 0)),
            pl.BlockSpec((1, d), lambda j: (0, 0)),
            pl.BlockSpec((d, tn), lambda j: (0, j)),
            pl.BlockSpec((br, side_cols), lambda j: (j, 0)),
        ],
        out_specs=[pl.BlockSpec((n, tn), lambda j: (0, j)), pl.BlockSpec((br, side_cols), lambda j: (j, 0))],
        out_shape=[jax.ShapeDtypeStruct((n, ncol), BF16), jax.ShapeDtypeStruct(side_w.shape, BF16)],
        scratch_shapes=[pltpu.VMEM((n, d), BF16)],
        compiler_params=pltpu.CompilerParams(
            dimension_semantics=("arbitrary",), vmem_limit_bytes=_vmem_limit(est)),
        name="kvproj",
    )(x, gain, w, side_w)


NAT_CHUNKS = (11, 12, 9, 10, 0, 3, 6)
N_NAT_STEPS = len(NAT_CHUNKS)
N_DIL_STEPS = 3
N_GATE_STEPS = N_BRANCH * D_MODEL // PROJ_TG
FIRST_D4 = N_NAT_STEPS
FIRST_D16 = FIRST_D4 + N_DIL_STEPS
FIRST_GATE = FIRST_D16 + N_DIL_STEPS
MIX_STEPS = FIRST_GATE + N_GATE_STEPS


def _lin_chunk(s):
    t = jnp.minimum(s, FIRST_GATE - 1)
    nat = jnp.where(t < 2, 11 + t, jnp.where(t < 4, 7 + t, 3 * (t - 4)))
    d4 = 3 * (t - FIRST_D4) + 1
    d16 = 3 * (t - FIRST_D16) + 2
    return jnp.where(t < FIRST_D4, nat, jnp.where(t < FIRST_D16, d4, d16))


def _mixproj_kernel(x_ref, g_ref, wg_ref, b_ref, wi_ref, og_ref, on_ref, o4_ref, o16_ref, hs_ref, stage_ref,
                    stage2_ref):
    s = pl.program_id(1)
    tm = x_ref.shape[0]
    row_chunks = [pl.ds(r, PROJ_ROWS) for r in range(0, tm, PROJ_ROWS)]

    @pl.when(s == 0)
    def _():
        hs_ref[...] = _rms(x_ref[...], g_ref[...]).astype(BF16)

    @pl.when(s < FIRST_D4)
    def _():
        for rows in row_chunks:
            on_ref[rows, :] = jnp.dot(hs_ref[rows, :], wi_ref[...], preferred_element_type=F32).astype(on_ref.dtype)

    def dilated(o_ref, d):
        per_res = PROJ_ROWS // d
        slabs = range(PROJ_TN // HEAD_DIM)
        for rc, rows in enumerate(row_chunks):
            acc = jnp.dot(hs_ref[rows, :], wi_ref[...], preferred_element_type=F32)
            for c in slabs:
                stage_ref[c] = acc[:, c * HEAD_DIM:(c + 1) * HEAD_DIM]
            if d == DEINT_STRIDE:
                for res in range(d):
                    for c in slabs:
                        o_ref[res, pl.ds(rc * per_res, per_res), c * HEAD_DIM:(c + 1) * HEAD_DIM] = (
                            stage_ref[c, pl.ds(res, per_res, stride=d), :].astype(o_ref.dtype))
            else:
                for c in slabs:
                    for r1 in range(DEINT_STRIDE):
                        stage2_ref[c, r1] = stage_ref[c, pl.ds(r1, PROJ_ROWS // DEINT_STRIDE, stride=DEINT_STRIDE), :]
                for r1 in range(DEINT_STRIDE):
                    for r2 in range(DEINT_STRIDE):
                        res = r1 + DEINT_STRIDE * r2
                        for c in slabs:
                            o_ref[res, pl.ds(rc * per_res, per_res), c * HEAD_DIM:(c + 1) * HEAD_DIM] = (
                                stage2_ref[c, r1, pl.ds(r2, per_res, stride=DEINT_STRIDE), :].astype(o_ref.dtype))

    pl.when((s >= FIRST_D4) & (s < FIRST_D16))(functools.partial(dilated, o4_ref, DIL_PAIRS[1][1]))
    pl.when((s >= FIRST_D16) & (s < FIRST_GATE))(functools.partial(dilated, o16_ref, DIL_PAIRS[2][1]))

    @pl.when(s >= FIRST_GATE)
    def _():
        for rows in row_chunks:
            z = jnp.dot(hs_ref[rows, :], wg_ref[...], preferred_element_type=F32) + b_ref[...]
            og_ref[rows, :] = (0.5 * jnp.tanh(0.5 * z) + 0.5).astype(og_ref.dtype)


def _mixproj(x, gain, w_gate, b_gate, w_in, *, batch, seq):
    n, d = x.shape
    tm, tn, tg = PROJ_TM, PROJ_TN, PROJ_TG
    tiles_per_seq = seq // tm
    d4, d16 = DIL_PAIRS[1][1], DIL_PAIRS[2][1]

    def gate_idx(s):
        return jnp.maximum(s - FIRST_GATE, 0)

    def dil_spec(dil, first):
        return pl.BlockSpec(
            (None, dil, tm // dil, tn),
            lambda i, s: (i // tiles_per_seq, 0, i % tiles_per_seq, jnp.clip(s - first, 0, N_DIL_STEPS - 1)))

    est = (2 * tm * d * 4 + tm * d * 2 + 2 * d * (tg + tn) * 2 + 2 * tm * (tg + 3 * tn) * 2
           + PROJ_ROWS * tn * 4 + 2 * PROJ_ROWS * tg * 4)
    return pl.pallas_call(
        _mixproj_kernel,
        grid=(n // tm, MIX_STEPS),
        in_specs=[
            pl.BlockSpec((tm, d), lambda i, s: (i, 0)),
            pl.BlockSpec((1, d), lambda i, s: (0, 0)),
            pl.BlockSpec((d, tg), lambda i, s: (0, gate_idx(s))),
            pl.BlockSpec((1, tg), lambda i, s: (0, gate_idx(s))),
            pl.BlockSpec((d, tn), lambda i, s: (0, _lin_chunk(s))),
        ],
        out_specs=[
            pl.BlockSpec((tm, tg), lambda i, s: (i, gate_idx(s))),
            pl.BlockSpec((tm, tn), lambda i, s: (i, jnp.minimum(s, N_NAT_STEPS - 1))),
            dil_spec(d4, FIRST_D4),
            dil_spec(d16, FIRST_D16),
        ],
        out_shape=[
            jax.ShapeDtypeStruct((n, N_BRANCH * d), BF16),
            jax.ShapeDtypeStruct((n, N_NAT_STEPS * tn), BF16),
            jax.ShapeDtypeStruct((batch, d4, seq // d4, N_DIL_STEPS * tn), BF16),
            jax.ShapeDtypeStruct((batch, d16, seq // d16, N_DIL_STEPS * tn), BF16),
        ],
        scratch_shapes=[
            pltpu.VMEM((tm, d), BF16),
            pltpu.VMEM((tn // HEAD_DIM, PROJ_ROWS, HEAD_DIM), F32),
            pltpu.VMEM((tn // HEAD_DIM, DEINT_STRIDE, PROJ_ROWS // DEINT_STRIDE, HEAD_DIM), F32),
        ],
        compiler_params=pltpu.CompilerParams(
            dimension_semantics=("arbitrary", "arbitrary"), vmem_limit_bytes=_vmem_limit(est)),
        name="mixproj",
    )(x, gain, w_gate, b_gate, w_in)


def _attn_kernel(*refs, seq, side_cast, n_side_out):
    slopes_ref, q0_ref, k0_ref, v0_ref, q1_ref, k1_ref, v1_ref, q2_ref, k2_ref, v2_ref = refs[:10]
    n_side_in = 0 if side_cast is None else 1
    o_ref = refs[10 + n_side_in]
    m1_ref, m2_ref, l1_ref, l2_ref, n1_ref, n2_ref, va_ref, bias_ref = refs[11 + n_side_in + n_side_out:]
    if side_cast is not None:
        side_cast(refs[10], *refs[11 + n_side_in:11 + n_side_in + n_side_out])
    h = pl.program_id(1)
    scale = HEAD_DIM ** -0.5
    qb = ATT_QB
    qkv_refs = ((q0_ref, k0_ref, v0_ref), (q1_ref, k1_ref, v1_ref), (q2_ref, k2_ref, v2_ref))
    m_refs = (None, m1_ref, m2_ref)
    l_refs = (None, l1_ref, l2_ref)
    n_refs = (None, n1_ref, n2_ref)

    va_ref[:, HEAD_DIM:] = jnp.ones((seq, HEAD_DIM), BF16)

    for gi in reversed(range(N_GROUPS)):
        d = DIL_PAIRS[gi][1]
        q_ref, k_ref, v_ref = qkv_refs[gi]
        sub_len = seq // d
        kw = ATT_KW
        nb = sub_len // qb
        short = sub_len < kw
        slope_d = slopes_ref[gi, h] * float(d)

        row = lax.broadcasted_iota(jnp.int32, (qb, kw), 0)
        col = lax.broadcasted_iota(jnp.int32, (qb, kw), 1)
        for kind, off in enumerate((0, -BAND_RADIUS, qb - kw)):
            arel = jnp.abs(col - row + off)
            bias = jnp.where(arel <= BAND_RADIUS, -slope_d * arel.astype(F32), NEG)
            if short:
                key = col + off
                bias = jnp.where(key < 0, NEG, jnp.where(key < sub_len, bias, NEG))
            bias_ref[kind] = bias

        va_ref[:, :HEAD_DIM] = v_ref[...]

        def block(idx, gi=gi, d=d, q_ref=q_ref, k_ref=k_ref, sub_len=sub_len, kw=kw, nb=nb, short=short):
            qstart = pl.multiple_of(idx * qb, qb)
            if short:
                res, q0, pos, last = idx, 0, idx, d * nb - 1
                kstart = jnp.clip(qstart - BAND_RADIUS, 0, seq - kw)
            else:
                res = idx // nb
                pos, last = idx - res * nb, nb - 1
                q0 = pos * qb
                kstart = res * sub_len + jnp.clip(q0 - BAND_RADIUS, 0, sub_len - kw)
            kstart = pl.multiple_of(kstart, BAND_RADIUS)
            kind = jnp.where(pos == 0, 0, jnp.where(pos == last, 2, 1))
            s = lax.dot_general(q_ref[pl.ds(qstart, qb), :], k_ref[pl.ds(kstart, kw), :],
                                (((1,), (1,)), ((), ())), preferred_element_type=F32)
            s = s * scale + bias_ref[kind]
            m = jnp.max(s, axis=-1, keepdims=True)
            p = jnp.exp(s - m).astype(BF16)
            na = jnp.dot(p, va_ref[pl.ds(kstart, kw), :], preferred_element_type=F32)
            m = jnp.broadcast_to(m, (qb, HEAD_DIM))
            if d > 1:
                rows = pl.ds(q0 * d + res, qb, stride=d)
                m_refs[gi][rows, :] = m
                l_refs[gi][rows, :] = na[:, HEAD_DIM:]
                n_refs[gi][rows, :] = na[:, :HEAD_DIM]
            else:
                rows = pl.ds(qstart, qb)
                ms = [m] + [m_refs[g][rows, :] for g in range(1, N_GROUPS)]
                mx = functools.reduce(jnp.maximum, ms)
                e = jnp.exp(m - mx)
                num = e * na[:, :HEAD_DIM]
                den = e * na[:, HEAD_DIM:]
                for g in range(1, N_GROUPS):
                    e = jnp.exp(ms[g] - mx)
                    num += e * n_refs[g][rows, :]
                    den += e * l_refs[g][rows, :]
                o_ref[rows, :] = (num / den).astype(o_ref.dtype)

        unroll = min(ATT_UNROLL, d * nb // 2)

        def blocks(it, carry, block=block, unroll=unroll):
            for u in range(unroll):
                block(it * unroll + u)
            return carry

        lax.fori_loop(0, d * nb // unroll, blocks, 0)


def _side_cast_specs(kind, w, steps):
    rows, cols = w.shape

    def step(b, h):
        return b * HEADS_PER_GROUP + h

    if kind == "gate_up":
        br, dff = rows // steps, cols // 2
        nchunk = pl.cdiv(dff, FFN_TF)
        in_spec = pl.BlockSpec((br, cols), lambda b, h: (step(b, h), 0))
        out_specs = [pl.BlockSpec((nchunk, br, FFN_TF), lambda b, h: (0, step(b, h), 0)) for _ in range(2)]
        out_shapes = [jax.ShapeDtypeStruct((nchunk, rows, FFN_TF), BF16) for _ in range(2)]
        return functools.partial(_cast_chunked_kernel, width=dff), in_spec, out_specs, out_shapes, br * cols
    bc = cols // steps
    in_spec = pl.BlockSpec((rows, bc), lambda b, h: (0, step(b, h)))
    out_specs = [pl.BlockSpec((rows, bc), lambda b, h: (0, step(b, h)))]
    out_shapes = [jax.ShapeDtypeStruct((rows, cols), BF16)]
    return _cast_kernel, in_spec, out_specs, out_shapes, rows * bc


def _attention(nat, qkv4, qkv16, slopes, *, batch, seq, col0, side=None):
    blk0 = col0 // HEAD_DIM
    qkv4 = qkv4.reshape(batch * seq, qkv4.shape[-1])
    qkv16 = qkv16.reshape(batch * seq, qkv16.shape[-1])

    def nat_map(b, h, *, which):
        return (b, blk0 + which * HEADS_PER_GROUP + h)

    def dil_map(b, h, *, which):
        return (b, which * HEADS_PER_GROUP + h)

    in_specs = [pl.BlockSpec(memory_space=pltpu.SMEM)]
    in_specs += [pl.BlockSpec((seq, HEAD_DIM), functools.partial(nat_map, which=w)) for w in range(3)]
    in_specs += 2 * [pl.BlockSpec((seq, HEAD_DIM), functools.partial(dil_map, which=w)) for w in range(3)]
    tile = seq * HEAD_DIM
    est = 10 * 2 * tile * 2 + 6 * tile * 4 + 2 * tile * 2 + 3 * ATT_QB * ATT_KW * 4
    out_specs = [pl.BlockSpec((seq, HEAD_DIM), lambda b, h: (b, h))]
    out_shapes = [jax.ShapeDtypeStruct((batch * seq, HEADS_PER_GROUP * HEAD_DIM), BF16)]
    operands = [slopes, nat, nat, nat, qkv4, qkv4, qkv4, qkv16, qkv16, qkv16]
    side_cast = None
    if side is not None:
        kind, w = side
        side_cast, side_in, side_outs, side_shapes, block_elems = _side_cast_specs(
            kind, w, batch * HEADS_PER_GROUP)
        in_specs.append(side_in)
        operands.append(w)
        out_specs += side_outs
        out_shapes += side_shapes
        est += 2 * block_elems * (4 + 2)
    state = [pltpu.VMEM((seq, HEAD_DIM), F32) for _ in range(3 * (N_GROUPS - 1))]
    outs = pl.pallas_call(
        functools.partial(_attn_kernel, seq=seq, side_cast=side_cast, n_side_out=len(out_specs) - 1),
        grid=(batch, HEADS_PER_GROUP),
        in_specs=in_specs,
        out_specs=out_specs,
        out_shape=out_shapes,
        scratch_shapes=state + [
            pltpu.VMEM((seq, 2 * HEAD_DIM), BF16),
            pltpu.VMEM((3, ATT_QB, ATT_KW), F32),
        ],
        compiler_params=pltpu.CompilerParams(
            dimension_semantics=("arbitrary", "arbitrary"), vmem_limit_bytes=_vmem_limit(est)),
        name="attention",
    )(*operands)
    return outs[0], tuple(outs[1:])


def _pool_kernel(u_ref, w_ref, sc_ref, o_ref, pad_ref, band_ref, pooled_ref, *, seq):
    grp = pl.program_id(1)
    halo, rows = POOL_HALO, POOL_ROWS
    ext = rows + 2 * halo
    zeros = jnp.zeros((halo, POOL_GROUP), BF16)
    pad_ref[pl.ds(0, halo), :] = zeros
    pad_ref[pl.ds(halo + seq, halo), :] = zeros
    pad_ref[pl.ds(halo, seq), :] = u_ref[...]

    def run(window):
        first = lax.broadcasted_iota(jnp.int32, (rows, ext), 0) + (halo - window // 2)
        col = lax.broadcasted_iota(jnp.int32, (rows, ext), 1)
        band_ref[...] = jnp.where(col < first, 0.0, jnp.where(col < first + window, 1.0, 0.0)).astype(BF16)

        def chunk(c):
            base = pl.multiple_of(c * rows, rows)
            sums = jnp.dot(band_ref[...], pad_ref[pl.ds(base, ext), :], preferred_element_type=F32)
            tok = pad_ref[pl.ds(base + halo, rows), :].astype(F32)
            t = base + lax.broadcasted_iota(jnp.int32, (rows, 1), 0)
            lo = jnp.maximum(t - window // 2, 0)
            hi = jnp.minimum(t - window // 2 + window, seq)
            pooled_ref[pl.ds(base, rows), :] = (sums / (hi - lo).astype(F32) - tok).astype(BF16)

        def chunks(it, carry):
            for u in range(POOL_UNROLL):
                chunk(it * POOL_UNROLL + u)
            return carry

        lax.fori_loop(0, seq // rows // POOL_UNROLL, chunks, 0)

    for gi, window in enumerate(POOL_WINDOWS):
        pl.when(grp == gi)(functools.partial(run, window))

    y = jnp.dot(pooled_ref[...], w_ref[...], preferred_element_type=F32) * sc_ref[...]
    o_ref[...] = y.astype(o_ref.dtype)


def _pool(pg, w_pool, pool_scale, *, batch, seq, col0):
    blk0 = col0 // POOL_GROUP
    ngrp = len(POOL_WINDOWS)
    est = (2 * seq * POOL_GROUP * 2 * 2 + (seq + 2 * POOL_HALO) * POOL_GROUP * 4 + 2 * POOL_GROUP * POOL_GROUP * 2
           + seq * POOL_GROUP * (2 + 4))
    return pl.pallas_call(
        functools.partial(_pool_kernel, seq=seq),
        grid=(batch, ngrp),
        in_specs=[
            pl.BlockSpec((seq, POOL_GROUP), lambda b, g: (b, blk0 + g)),
            pl.BlockSpec((None, POOL_GROUP, POOL_GROUP), lambda b, g: (g, 0, 0)),
            pl.BlockSpec((1, POOL_GROUP), lambda b, g: (0, g)),
        ],
        out_specs=pl.BlockSpec((seq, POOL_GROUP), lambda b, g: (b, g)),
        out_shape=jax.ShapeDtypeStruct((batch * seq, D_POOL), BF16),
        scratch_shapes=[pltpu.VMEM((seq + 2 * POOL_HALO, POOL_GROUP), BF16),
                        pltpu.VMEM((POOL_ROWS, POOL_ROWS + 2 * POOL_HALO), BF16),
                        pltpu.VMEM((seq, POOL_GROUP), BF16)],
        compiler_params=pltpu.CompilerParams(
            dimension_semantics=("arbitrary", "arbitrary"), vmem_limit_bytes=_vmem_limit(est)),
        name="pool",
    )(pg, w_pool, pool_scale)


def _mem_attention(q, kv_ref):
    scale = XHEAD_DIM ** -0.5
    heads = []
    for hh in range(N_XHEADS):
        lo = hh * XHEAD_DIM
        k = kv_ref[:, lo:lo + XHEAD_DIM]
        v = kv_ref[:, D_XATTN + lo:D_XATTN + lo + XHEAD_DIM]
        s = lax.dot_general(q[:, lo:lo + XHEAD_DIM], k, (((1,), (1,)), ((), ())), preferred_element_type=F32) * scale
        m = jnp.max(s, axis=-1, keepdims=True)
        p = jnp.exp(s - m)
        l = jnp.sum(p, axis=-1, keepdims=True)
        heads.append((jnp.dot(p.astype(BF16), v, preferred_element_type=F32) / l).astype(BF16))
    return jnp.concatenate(heads, axis=1)


def _merge_kernel(gates_ref, ya_ref, yp_ref, qm_ref, kv_ref, x_ref, wa_ref, wp_ref, wm_ref, wo_ref, gain_ref, o_ref):
    d = x_ref.shape[1]
    for r in range(0, x_ref.shape[0], MERGE_ROWS):
        rows = pl.ds(r, MERGE_ROWS)
        merged = gates_ref[rows, 0:d].astype(F32) * jnp.dot(ya_ref[rows, :], wa_ref[...], preferred_element_type=F32)
        merged += gates_ref[rows, d:2 * d].astype(F32) * jnp.dot(yp_ref[rows, :], wp_ref[...],
                                                                   preferred_element_type=F32)
        y_mem = _mem_attention(qm_ref[rows, :], kv_ref)
        merged += gates_ref[rows, 2 * d:3 * d].astype(F32) * jnp.dot(y_mem, wm_ref[...],
                                                                       preferred_element_type=F32)
        z = jnp.dot(merged.astype(BF16), wo_ref[...], preferred_element_type=F32)
        o_ref[rows, :] = x_ref[rows, :] + _rms(z, gain_ref[...])


def _merge(gates, y_attn, y_pool, nat, kv, x, wa, wp, wm, wo, gain, *, seq, col_qmem):
    n, d = x.shape
    tm = MERGE_TM
    tiles_per_seq = seq // tm
    wbytes = (wa.size + wp.size + wm.size + wo.size) * 2

    def const(shape):
        return pl.BlockSpec(shape, lambda i: (0, 0), pipeline_mode=pl.Buffered(1))

    est = (2 * tm * (N_BRANCH * d + y_attn.shape[1] + y_pool.shape[1] + D_XATTN) * 2 + 2 * N_MEM * 2 * D_XATTN * 2
           + 4 * tm * d * 4 + wbytes + 4 * tm * d * 4)
    return pl.pallas_call(
        _merge_kernel,
        grid=(n // tm,),
        in_specs=[
            pl.BlockSpec((tm, N_BRANCH * d), lambda i: (i, 0)),
            pl.BlockSpec((tm, y_attn.shape[1]), lambda i: (i, 0)),
            pl.BlockSpec((tm, y_pool.shape[1]), lambda i: (i, 0)),
            pl.BlockSpec((tm, D_XATTN), lambda i: (i, col_qmem // D_XATTN)),
            pl.BlockSpec((N_MEM, 2 * D_XATTN), lambda i: (i // tiles_per_seq, 0)),
            pl.BlockSpec((tm, d), lambda i: (i, 0)),
            const(wa.shape), const(wp.shape), const(wm.shape), const(wo.shape),
            pl.BlockSpec((1, d), lambda i: (0, 0)),
        ],
        out_specs=pl.BlockSpec((tm, d), lambda i: (i, 0)),
        out_shape=jax.ShapeDtypeStruct((n, d), F32),
        compiler_params=pltpu.CompilerParams(
            dimension_semantics=("arbitrary",), vmem_limit_bytes=_vmem_limit(est)),
        name="merge",
    )(gates, y_attn, y_pool, nat, kv, x, wa, wp, wm, wo, gain)


def _cast_kernel(w_ref, *o_refs):
    lo = 0
    for o_ref in o_refs:
        width = o_ref.shape[1]
        o_ref[...] = w_ref[:, lo:lo + width].astype(o_ref.dtype)
        lo += width


def _cast_bf16(w, widths=None, target_bytes=CAST_BLOCK_BYTES):
    rows, cols = w.shape
    widths = (cols,) if widths is None else tuple(widths)
    br = rows
    while br * cols * 4 > target_bytes and br % 32 == 0:
        br //= 2
    est = 2 * br * cols * 4 + 2 * br * cols * 2
    outs = pl.pallas_call(
        _cast_kernel,
        grid=(rows // br,),
        in_specs=[pl.BlockSpec((br, cols), lambda i: (i, 0))],
        out_specs=[pl.BlockSpec((br, wd), lambda i: (i, 0)) for wd in widths],
        out_shape=[jax.ShapeDtypeStruct((rows, wd), BF16) for wd in widths],
        compiler_params=pltpu.CompilerParams(
            dimension_semantics=("arbitrary",), vmem_limit_bytes=_vmem_limit(est)),
        name="cast",
    )(w)
    return outs if len(widths) > 1 else outs[0]


def _cast_chunked_kernel(w_ref, *o_refs, width):
    for k, o_ref in enumerate(o_refs):
        nchunk, _, tf = o_ref.shape
        for c in range(nchunk):
            valid = min(tf, width - c * tf)
            lo = k * width + c * tf
            o_ref[c, :, :valid] = w_ref[:, lo:lo + valid].astype(o_ref.dtype)
            if valid < tf:
                o_ref[c, :, valid:] = jnp.zeros((o_ref.shape[1], tf - valid), o_ref.dtype)


def _cast_chunked(w, n_out, tf, target_bytes=CAST_BLOCK_BYTES):
    rows, cols = w.shape
    width = cols // n_out
    nchunk = pl.cdiv(width, tf)
    br = rows
    while br * cols * 4 > target_bytes and br % 32 == 0:
        br //= 2
    est = 2 * br * cols * 4 + 2 * n_out * nchunk * br * tf * 2
    return pl.pallas_call(
        functools.partial(_cast_chunked_kernel, width=width),
        grid=(rows // br,),
        in_specs=[pl.BlockSpec((br, cols), lambda i: (i, 0))],
        out_specs=[pl.BlockSpec((nchunk, br, tf), lambda i: (0, i, 0)) for _ in range(n_out)],
        out_shape=[jax.ShapeDtypeStruct((nchunk, rows, tf), BF16) for _ in range(n_out)],
        compiler_params=pltpu.CompilerParams(
            dimension_semantics=("arbitrary",), vmem_limit_bytes=_vmem_limit(est)),
        name="cast_chunked",
    )(w)


def _row(v):
    return v.reshape(1, -1).astype(F32)


def _mixer_half(x, kv, p, slopes, side):
    batch, seq, d = x.shape
    xf = x.reshape(batch * seq, d)
    x1 = _ffn(xf, p["ffn1_norm_pre"], p["ffn1_wg"], p["ffn1_wu"], p["ffn1_wd"], p["ffn1_norm_post"],
              p["ffn1_norm_post"], final=False)
    gates, nat, qkv4, qkv16 = _mixproj(x1, p["mix_norm_pre"], p["w_gate"], p["b_gate"], p["w_in"],
                                       batch=batch, seq=seq)
    col_pool = D_XATTN
    col_qkv0 = col_pool + D_POOL
    y_attn, cast = _attention(nat, qkv4, qkv16, slopes, batch=batch, seq=seq, col0=col_qkv0, side=side)
    y_pool = _pool(nat, p["w_pool"], p["pool_scale"], batch=batch, seq=seq, col0=col_pool)
    x2 = _merge(gates, y_attn, y_pool, nat, kv, x1, p["w_br_attn"], p["w_br_pool"], p["w_br_mem"], p["w_out"],
                p["mix_norm_post"], seq=seq, col_qmem=0)
    return x2, cast


def _prep_params(ffn1_norm_pre, ffn1_w_up, ffn1_w_down, ffn1_norm_post, mix_norm_pre, mem_norm, w_in, w_mem_kv,
                 w_pool, pool_scale, w_br_attn, w_br_pool, w_br_mem, w_gate, b_gate, w_out, mix_norm_post,
                 ffn2_norm_pre, ffn2_w_up, ffn2_w_down, ffn2_norm_post, final_norm):
    p = {}
    p["ffn1_wg"], p["ffn1_wu"] = _cast_chunked(ffn1_w_up, 2, FFN_TF)
    p["ffn1_wd"] = _cast_bf16(ffn1_w_down)
    p["ffn2_w_up"], p["ffn2_w_down"] = ffn2_w_up, ffn2_w_down
    p["w_gate_f32"], p["w_in_f32"] = w_gate, w_in
    p["b_gate"] = _row(b_gate)
    p["w_mem_kv"] = _cast_bf16(w_mem_kv)
    p["w_pool"] = w_pool.astype(BF16)
    p["pool_scale"] = _row(pool_scale)
    for name, w in (("w_br_attn", w_br_attn), ("w_br_pool", w_br_pool), ("w_br_mem", w_br_mem), ("w_out", w_out)):
        p[name] = _cast_bf16(w)
    for name, v in (("ffn1_norm_pre", ffn1_norm_pre), ("ffn1_norm_post", ffn1_norm_post),
                    ("mix_norm_pre", mix_norm_pre), ("mem_norm", mem_norm), ("mix_norm_post", mix_norm_post),
                    ("ffn2_norm_pre", ffn2_norm_pre), ("ffn2_norm_post", ffn2_norm_post),
                    ("final_norm", final_norm)):
        p[name] = _row(v)
    return p


def kernel(x_prompt, x_sample, mem_prompt, mem_sample, ffn1_norm_pre, ffn1_w_up, ffn1_w_down, ffn1_norm_post,
           mix_norm_pre, mem_norm, w_in, w_mem_kv, w_pool, pool_scale, w_br_attn, w_br_pool, w_br_mem, w_gate, b_gate,
           w_out, mix_norm_post, ffn2_norm_pre, ffn2_w_up, ffn2_w_down, ffn2_norm_post, final_norm):
    layer = [ffn1_norm_pre, ffn1_w_up, ffn1_w_down, ffn1_norm_post, mix_norm_pre, mem_norm, w_in, w_mem_kv, w_pool,
             pool_scale, w_br_attn, w_br_pool, w_br_mem, w_gate, b_gate, w_out, mix_norm_post, ffn2_norm_pre,
             ffn2_w_up, ffn2_w_down, ffn2_norm_post, final_norm]
    depth = ffn1_norm_pre.shape[0]
    slopes = jnp.asarray(_alibi_slopes())
    y_prompt, y_sample = x_prompt, x_sample
    for layer_idx in range(depth):
        p = _prep_params(*[w[layer_idx] for w in layer])
        d_model = x_prompt.shape[-1]
        kv_prompt, p["w_gate"] = _kvproj(mem_prompt.reshape(-1, d_model), p["mem_norm"], p["w_mem_kv"],
                                         p["w_gate_f32"])
        kv_sample, p["w_in"] = _kvproj(mem_sample.reshape(-1, d_model), p["mem_norm"], p["w_mem_kv"], p["w_in_f32"])
        x2_prompt, (wg2, wu2) = _mixer_half(y_prompt, kv_prompt, p, slopes, ("gate_up", p["ffn2_w_up"]))
        x2_sample, (wd2,) = _mixer_half(y_sample, kv_sample, p, slopes, ("plain", p["ffn2_w_down"]))
        y_prompt, y_sample = (
            _ffn(x2, p["ffn2_norm_pre"], wg2, wu2, wd2, p["ffn2_norm_post"], p["final_norm"],
                 final=True).reshape(x.shape)
            for x2, x in ((x2_prompt, y_prompt), (x2_sample, y_sample)))
    return (y_prompt, y_sample)
```

```python
import functools

import numpy as np
import jax
import jax.numpy as jnp
from jax import lax
from jax.experimental import pallas as pl
from jax.experimental.pallas import tpu as pltpu

F32 = jnp.float32
BF16 = jnp.bfloat16

D_MODEL = 2048
N_MEM = 256
HEAD_DIM = 128
DIL_PAIRS = ((128, 1), (512, 4), (2048, 16))
HEADS_PER_GROUP = 4
N_GROUPS = len(DIL_PAIRS)
N_DIL_HEADS = HEADS_PER_GROUP * N_GROUPS
D_ATTN = N_DIL_HEADS * HEAD_DIM
POOL_WINDOWS = (2, 4, 8, 16)
POOL_GROUP = 256
D_POOL = POOL_GROUP * len(POOL_WINDOWS)
N_XHEADS = 4
XHEAD_DIM = 256
D_XATTN = N_XHEADS * XHEAD_DIM
N_BRANCH = 3
EPS = 1e-6
NEG = -1e30
BAND_RADIUS = 64

V7X_VMEM_BYTES = 64 * 1024 * 1024
MIB = 1024 * 1024

FFN_TM = 1024
FFN_TF = 1024
PROJ_TM = 1024
PROJ_TN = 512
PROJ_TG = 1536
PROJ_ROWS = 256
DEINT_STRIDE = 4
ATT_QB = 128
ATT_KW = 256
ATT_UNROLL = 16
POOL_ROWS = 256
POOL_HALO = 64
POOL_UNROLL = 4
MEM_TM = 512
MERGE_TM = 512
MERGE_ROWS = 512
CAST_BLOCK_BYTES = 8 * MIB


def _vmem_limit(nbytes):
    return int(min(nbytes * 5 // 4 + 8 * MIB, V7X_VMEM_BYTES - 2 * MIB))


def _rms(x, gain):
    ms = jnp.mean(x * x, axis=-1, keepdims=True)
    return x * lax.rsqrt(ms + EPS) * gain


def _alibi_slopes():
    s = 2.0 ** (-8.0 * np.arange(1, N_DIL_HEADS + 1) / N_DIL_HEADS)
    return s.reshape(HEADS_PER_GROUP, N_GROUPS).T.astype(np.float32)


def _ffn_kernel(x_ref, gpre_ref, wg_ref, wu_ref, wd_ref, gpost_ref, gfin_ref, o_hbm, acc_ref, hs_ref, sem,
                *, tf, rem, final):
    i = pl.program_id(0)
    f = pl.program_id(1)
    n_tiles = pl.num_programs(0)
    last = pl.num_programs(1) - 1
    tm = acc_ref.shape[0]

    def writeback(tile):
        return pltpu.make_async_copy(acc_ref, o_hbm.at[pl.ds(pl.multiple_of(tile * tm, tm), tm), :], sem)

    @pl.when(f == 0)
    def _():
        hs_ref[...] = _rms(x_ref[...], gpre_ref[...]).astype(BF16)

    def step(valid, first):
        hs = hs_ref[...]
        g = jnp.dot(hs, wg_ref[:, :valid], preferred_element_type=F32)
        u = jnp.dot(hs, wu_ref[:, :valid], preferred_element_type=F32)
        a = (g * jax.nn.sigmoid(g) * u).astype(BF16)
        if first:
            @pl.when(i > 0)
            def _():
                writeback(i - 1).wait()

            acc_ref[...] = jnp.dot(a, wd_ref[:valid, :], preferred_element_type=F32)
        else:
            acc_ref[...] += jnp.dot(a, wd_ref[:valid, :], preferred_element_type=F32)

    pl.when(f == 0)(functools.partial(step, tf, True))
    pl.when((f > 0) & (f < last))(functools.partial(step, tf, False))
    pl.when(f == last)(functools.partial(step, rem, False))

    @pl.when(f == last)
    def _():
        y = x_ref[...] + 0.5 * _rms(acc_ref[...], gpost_ref[...])
        if final:
            y = _rms(y, gfin_ref[...])
        acc_ref[...] = y
        writeback(i).start(priority=1)

        @pl.when(i == n_tiles - 1)
        def _():
            writeback(i).wait()


def _ffn(x, gpre, wg, wu, wd, gpost, gfin, *, final):
    n, d = x.shape
    tm, tf = FFN_TM, FFN_TF
    dff = wd.shape[0]
    nf = pl.cdiv(dff, tf)
    rem = dff - (nf - 1) * tf
    assert nf >= 2
    est = 2 * tm * d * 4 + tm * d * 4 + tm * d * 2 + 2 * 3 * d * tf * 2 + 2 * tm * tf * 4
    vec = pl.BlockSpec((1, d), lambda i, f: (0, 0))
    return pl.pallas_call(
        functools.partial(_ffn_kernel, tf=tf, rem=rem, final=final),
        grid=(n // tm, nf),
        in_specs=[
            pl.BlockSpec((tm, d), lambda i, f: (i, 0)),
            vec,
            pl.BlockSpec((None, d, tf), lambda i, f: (f, 0, 0)),
            pl.BlockSpec((None, d, tf), lambda i, f: (f, 0, 0)),
            pl.BlockSpec((tf, d), lambda i, f: (f, 0)),
            vec,
            vec,
        ],
        out_specs=pl.BlockSpec(memory_space=pl.ANY),
        out_shape=jax.ShapeDtypeStruct((n, d), F32),
        scratch_shapes=[pltpu.VMEM((tm, d), F32), pltpu.VMEM((tm, d), BF16), pltpu.SemaphoreType.DMA(())],
        compiler_params=pltpu.CompilerParams(
            dimension_semantics=("arbitrary", "arbitrary"), vmem_limit_bytes=_vmem_limit(est)),
        name="ffn_final" if final else "ffn",
    )(x, gpre, wg, wu, wd, gpost, gfin)


def _kvproj_kernel(x_ref, g_ref, w_ref, o_ref, hs_ref):
    @pl.when(pl.program_id(1) == 0)
    def _():
        hs_ref[...] = _rms(x_ref[...], g_ref[...]).astype(BF16)

    o_ref[...] = jnp.dot(hs_ref[...], w_ref[...], preferred_element_type=F32).astype(o_ref.dtype)


def _kvproj(x, gain, w):
    n, d = x.shape
    ncol = w.shape[1]
    tm, tn = n, PROJ_TN
    est = 2 * tm * d * 4 + tm * d * 2 + 2 * d * tn * 2 + 2 * tm * tn * 2 + 2 * tm * tn * 4
    return pl.pallas_call(
        _kvproj_kernel,
        grid=(n // tm, ncol // tn),
        in_specs=[
            pl.BlockSpec((tm, d), lambda i, j: (i, 0)),
            pl.BlockSpec((1, d), lambda i, j: (0, 0)),
            pl.BlockSpec((d, tn), lambda i, j: (0, j)),
        ],
        out_specs=pl.BlockSpec((tm, tn), lambda i, j: (i, j)),
        out_shape=jax.ShapeDtypeStruct((n, ncol), BF16),
        scratch_shapes=[pltpu.VMEM((tm, d), BF16)],
        compiler_params=pltpu.CompilerParams(
            dimension_semantics=("arbitrary", "arbitrary"), vmem_limit_bytes=_vmem_limit(est)),
        name="kvproj",
    )(x, gain, w)


NAT_CHUNKS = (11, 12, 9, 10, 0, 3, 6)
N_NAT_STEPS = len(NAT_CHUNKS)
N_DIL_STEPS = 3
N_GATE_STEPS = N_BRANCH * D_MODEL // PROJ_TG
FIRST_D4 = N_NAT_STEPS
FIRST_D16 = FIRST_D4 + N_DIL_STEPS
FIRST_GATE = FIRST_D16 + N_DIL_STEPS
MIX_STEPS = FIRST_GATE + N_GATE_STEPS


def _lin_chunk(s):
    t = jnp.minimum(s, FIRST_GATE - 1)
    nat = jnp.where(t < 2, 11 + t, jnp.where(t < 4, 7 + t, 3 * (t - 4)))
    d4 = 3 * (t - FIRST_D4) + 1
    d16 = 3 * (t - FIRST_D16) + 2
    return jnp.where(t < FIRST_D4, nat, jnp.where(t < FIRST_D16, d4, d16))


def _mixproj_kernel(x_ref, g_ref, wg_ref, b_ref, wi_ref, og_ref, on_ref, o4_ref, o16_ref, hs_ref, stage_ref,
                    stage2_ref):
    s = pl.program_id(1)
    tm = x_ref.shape[0]
    row_chunks = [pl.ds(r, PROJ_ROWS) for r in range(0, tm, PROJ_ROWS)]

    @pl.when(s == 0)
    def _():
        hs_ref[...] = _rms(x_ref[...], g_ref[...]).astype(BF16)

    @pl.when(s < FIRST_D4)
    def _():
        for rows in row_chunks:
            on_ref[rows, :] = jnp.dot(hs_ref[rows, :], wi_ref[...], preferred_element_type=F32).astype(on_ref.dtype)

    def dilated(o_ref, d):
        per_res = PROJ_ROWS // d
        slabs = range(PROJ_TN // HEAD_DIM)
        for rc, rows in enumerate(row_chunks):
            acc = jnp.dot(hs_ref[rows, :], wi_ref[...], preferred_element_type=F32)
            for c in slabs:
                stage_ref[c] = acc[:, c * HEAD_DIM:(c + 1) * HEAD_DIM]
            if d == DEINT_STRIDE:
                for res in range(d):
                    for c in slabs:
                        o_ref[res, pl.ds(rc * per_res, per_res), c * HEAD_DIM:(c + 1) * HEAD_DIM] = (
                            stage_ref[c, pl.ds(res, per_res, stride=d), :].astype(o_ref.dtype))
            else:
                for c in slabs:
                    for r1 in range(DEINT_STRIDE):
                        stage2_ref[c, r1] = stage_ref[c, pl.ds(r1, PROJ_ROWS // DEINT_STRIDE, stride=DEINT_STRIDE), :]
                for r1 in range(DEINT_STRIDE):
                    for r2 in range(DEINT_STRIDE):
                        res = r1 + DEINT_STRIDE * r2
                        for c in slabs:
                            o_ref[res, pl.ds(rc * per_res, per_res), c * HEAD_DIM:(c + 1) * HEAD_DIM] = (
                                stage2_ref[c, r1, pl.ds(r2, per_res, stride=DEINT_STRIDE), :].astype(o_ref.dtype))

    pl.when((s >= FIRST_D4) & (s < FIRST_D16))(functools.partial(dilated, o4_ref, DIL_PAIRS[1][1]))
    pl.when((s >= FIRST_D16) & (s < FIRST_GATE))(functools.partial(dilated, o16_ref, DIL_PAIRS[2][1]))

    @pl.when(s >= FIRST_GATE)
    def _():
        for rows in row_chunks:
            z = jnp.dot(hs_ref[rows, :], wg_ref[...], preferred_element_type=F32) + b_ref[...]
            og_ref[rows, :] = (0.5 * jnp.tanh(0.5 * z) + 0.5).astype(og_ref.dtype)


def _mixproj(x, gain, w_gate, b_gate, w_in, *, batch, seq):
    n, d = x.shape
    tm, tn, tg = PROJ_TM, PROJ_TN, PROJ_TG
    tiles_per_seq = seq // tm
    d4, d16 = DIL_PAIRS[1][1], DIL_PAIRS[2][1]

    def gate_idx(s):
        return jnp.maximum(s - FIRST_GATE, 0)

    def dil_spec(dil, first):
        return pl.BlockSpec(
            (None, dil, tm // dil, tn),
            lambda i, s: (i // tiles_per_seq, 0, i % tiles_per_seq, jnp.clip(s - first, 0, N_DIL_STEPS - 1)))

    est = (2 * tm * d * 4 + tm * d * 2 + 2 * d * (tg + tn) * 2 + 2 * tm * (tg + 3 * tn) * 2
           + PROJ_ROWS * tn * 4 + 2 * PROJ_ROWS * tg * 4)
    return pl.pallas_call(
        _mixproj_kernel,
        grid=(n // tm, MIX_STEPS),
        in_specs=[
            pl.BlockSpec((tm, d), lambda i, s: (i, 0)),
            pl.BlockSpec((1, d), lambda i, s: (0, 0)),
            pl.BlockSpec((d, tg), lambda i, s: (0, gate_idx(s))),
            pl.BlockSpec((1, tg), lambda i, s: (0, gate_idx(s))),
            pl.BlockSpec((d, tn), lambda i, s: (0, _lin_chunk(s))),
        ],
        out_specs=[
            pl.BlockSpec((tm, tg), lambda i, s: (i, gate_idx(s))),
            pl.BlockSpec((tm, tn), lambda i, s: (i, jnp.minimum(s, N_NAT_STEPS - 1))),
            dil_spec(d4, FIRST_D4),
            dil_spec(d16, FIRST_D16),
        ],
        out_shape=[
            jax.ShapeDtypeStruct((n, N_BRANCH * d), BF16),
            jax.ShapeDtypeStruct((n, N_NAT_STEPS * tn), BF16),
            jax.ShapeDtypeStruct((batch, d4, seq // d4, N_DIL_STEPS * tn), BF16),
            jax.ShapeDtypeStruct((batch, d16, seq // d16, N_DIL_STEPS * tn), BF16),
        ],
        scratch_shapes=[
            pltpu.VMEM((tm, d), BF16),
            pltpu.VMEM((tn // HEAD_DIM, PROJ_ROWS, HEAD_DIM), F32),
            pltpu.VMEM((tn // HEAD_DIM, DEINT_STRIDE, PROJ_ROWS // DEINT_STRIDE, HEAD_DIM), F32),
        ],
        compiler_params=pltpu.CompilerParams(
            dimension_semantics=("arbitrary", "arbitrary"), vmem_limit_bytes=_vmem_limit(est)),
        name="mixproj",
    )(x, gain, w_gate, b_gate, w_in)


def _attn_kernel(*refs, seq, side_cast, n_side_out):
    slopes_ref, q0_ref, k0_ref, v0_ref, q1_ref, k1_ref, v1_ref, q2_ref, k2_ref, v2_ref = refs[:10]
    n_side_in = 0 if side_cast is None else 1
    o_ref = refs[10 + n_side_in]
    m1_ref, m2_ref, l1_ref, l2_ref, n1_ref, n2_ref, va_ref, bias_ref = refs[11 + n_side_in + n_side_out:]
    if side_cast is not None:
        side_cast(refs[10], *refs[11 + n_side_in:11 + n_side_in + n_side_out])
    h = pl.program_id(1)
    scale = HEAD_DIM ** -0.5
    qb = ATT_QB
    qkv_refs = ((q0_ref, k0_ref, v0_ref), (q1_ref, k1_ref, v1_ref), (q2_ref, k2_ref, v2_ref))
    m_refs = (None, m1_ref, m2_ref)
    l_refs = (None, l1_ref, l2_ref)
    n_refs = (None, n1_ref, n2_ref)

    va_ref[:, HEAD_DIM:] = jnp.ones((seq, HEAD_DIM), BF16)

    for gi in reversed(range(N_GROUPS)):
        d = DIL_PAIRS[gi][1]
        q_ref, k_ref, v_ref = qkv_refs[gi]
        sub_len = seq // d
        kw = ATT_KW
        nb = sub_len // qb
        short = sub_len < kw
        slope_d = slopes_ref[gi, h] * float(d)

        row = lax.broadcasted_iota(jnp.int32, (qb, kw), 0)
        col = lax.broadcasted_iota(jnp.int32, (qb, kw), 1)
        for kind, off in enumerate((0, -BAND_RADIUS, qb - kw)):
            arel = jnp.abs(col - row + off)
            bias = jnp.where(arel <= BAND_RADIUS, -slope_d * arel.astype(F32), NEG)
            if short:
                key = col + off
                bias = jnp.where(key < 0, NEG, jnp.where(key < sub_len, bias, NEG))
            bias_ref[kind] = bias

        va_ref[:, :HEAD_DIM] = v_ref[...]

        def block(idx, gi=gi, d=d, q_ref=q_ref, k_ref=k_ref, sub_len=sub_len, kw=kw, nb=nb, short=short):
            qstart = pl.multiple_of(idx * qb, qb)
            if short:
                res, q0, pos, last = idx, 0, idx, d * nb - 1
                kstart = jnp.clip(qstart - BAND_RADIUS, 0, seq - kw)
            else:
                res = idx // nb
                pos, last = idx - res * nb, nb - 1
                q0 = pos * qb
                kstart = res * sub_len + jnp.clip(q0 - BAND_RADIUS, 0, sub_len - kw)
            kstart = pl.multiple_of(kstart, BAND_RADIUS)
            kind = jnp.where(pos == 0, 0, jnp.where(pos == last, 2, 1))
            s = lax.dot_general(q_ref[pl.ds(qstart, qb), :], k_ref[pl.ds(kstart, kw), :],
                                (((1,), (1,)), ((), ())), preferred_element_type=F32)
            s = s * scale + bias_ref[kind]
            m = jnp.max(s, axis=-1, keepdims=True)
            p = jnp.exp(s - m).astype(BF16)
            na = jnp.dot(p, va_ref[pl.ds(kstart, kw), :], preferred_element_type=F32)
            m = jnp.broadcast_to(m, (qb, HEAD_DIM))
            if d > 1:
                rows = pl.ds(q0 * d + res, qb, stride=d)
                m_refs[gi][rows, :] = m
                l_refs[gi][rows, :] = na[:, HEAD_DIM:]
                n_refs[gi][rows, :] = na[:, :HEAD_DIM]
            else:
                rows = pl.ds(qstart, qb)
                ms = [m] + [m_refs[g][rows, :] for g in range(1, N_GROUPS)]
                mx = functools.reduce(jnp.maximum, ms)
                e = jnp.exp(m - mx)
                num = e * na[:, :HEAD_DIM]
                den = e * na[:, HEAD_DIM:]
                for g in range(1, N_GROUPS):
                    e = jnp.exp(ms[g] - mx)
                    num += e * n_refs[g][rows, :]
                    den += e * l_refs[g][rows, :]
                o_ref[rows, :] = (num / den).astype(o_ref.dtype)

        unroll = min(ATT_UNROLL, d * nb // 2)

        def blocks(it, carry, block=block, unroll=unroll):
            for u in range(unroll):
                block(it * unroll + u)
            return carry

        lax.fori_loop(0, d * nb // unroll, blocks, 0)


def _side_cast_specs(kind, w, steps):
    rows, cols = w.shape

    def step(b, h):
        return b * HEADS_PER_GROUP + h

    if kind == "gate_up":
        br, dff = rows // steps, cols // 2
        nchunk = pl.cdiv(dff, FFN_TF)
        in_spec = pl.BlockSpec((br, cols), lambda b, h: (step(b, h), 0))
        out_specs = [pl.BlockSpec((nchunk, br, FFN_TF), lambda b, h: (0, step(b, h), 0)) for _ in range(2)]
        out_shapes = [jax.ShapeDtypeStruct((nchunk, rows, FFN_TF), BF16) for _ in range(2)]
        return functools.partial(_cast_chunked_kernel, width=dff), in_spec, out_specs, out_shapes, br * cols
    bc = cols // steps
    in_spec = pl.BlockSpec((rows, bc), lambda b, h: (0, step(b, h)))
    out_specs = [pl.BlockSpec((rows, bc), lambda b, h: (0, step(b, h)))]
    out_shapes = [jax.ShapeDtypeStruct((rows, cols), BF16)]
    return _cast_kernel, in_spec, out_specs, out_shapes, rows * bc


def _attention(nat, qkv4, qkv16, slopes, *, batch, seq, col0, side=None):
    blk0 = col0 // HEAD_DIM
    qkv4 = qkv4.reshape(batch * seq, qkv4.shape[-1])
    qkv16 = qkv16.reshape(batch * seq, qkv16.shape[-1])

    def nat_map(b, h, *, which):
        return (b, blk0 + which * HEADS_PER_GROUP + h)

    def dil_map(b, h, *, which):
        return (b, which * HEADS_PER_GROUP + h)

    in_specs = [pl.BlockSpec(memory_space=pltpu.SMEM)]
    in_specs += [pl.BlockSpec((seq, HEAD_DIM), functools.partial(nat_map, which=w)) for w in range(3)]
    in_specs += 2 * [pl.BlockSpec((seq, HEAD_DIM), functools.partial(dil_map, which=w)) for w in range(3)]
    tile = seq * HEAD_DIM
    est = 10 * 2 * tile * 2 + 6 * tile * 4 + 2 * tile * 2 + 3 * ATT_QB * ATT_KW * 4
    out_specs = [pl.BlockSpec((seq, HEAD_DIM), lambda b, h: (b, h))]
    out_shapes = [jax.ShapeDtypeStruct((batch * seq, HEADS_PER_GROUP * HEAD_DIM), BF16)]
    operands = [slopes, nat, nat, nat, qkv4, qkv4, qkv4, qkv16, qkv16, qkv16]
    side_cast = None
    if side is not None:
        kind, w = side
        side_cast, side_in, side_outs, side_shapes, block_elems = _side_cast_specs(
            kind, w, batch * HEADS_PER_GROUP)
        in_specs.append(side_in)
        operands.append(w)
        out_specs += side_outs
        out_shapes += side_shapes
        est += 2 * block_elems * (4 + 2)
    state = [pltpu.VMEM((seq, HEAD_DIM), F32) for _ in range(3 * (N_GROUPS - 1))]
    outs = pl.pallas_call(
        functools.partial(_attn_kernel, seq=seq, side_cast=side_cast, n_side_out=len(out_specs) - 1),
        grid=(batch, HEADS_PER_GROUP),
        in_specs=in_specs,
        out_specs=out_specs,
        out_shape=out_shapes,
        scratch_shapes=state + [
            pltpu.VMEM((seq, 2 * HEAD_DIM), BF16),
            pltpu.VMEM((3, ATT_QB, ATT_KW), F32),
        ],
        compiler_params=pltpu.CompilerParams(
            dimension_semantics=("arbitrary", "arbitrary"), vmem_limit_bytes=_vmem_limit(est)),
        name="attention",
    )(*operands)
    return outs[0], tuple(outs[1:])


def _pool_kernel(u_ref, w_ref, sc_ref, o_ref, pad_ref, band_ref, pooled_ref, *, seq):
    grp = pl.program_id(1)
    halo, rows = POOL_HALO, POOL_ROWS
    ext = rows + 2 * halo
    zeros = jnp.zeros((halo, POOL_GROUP), BF16)
    pad_ref[pl.ds(0, halo), :] = zeros
    pad_ref[pl.ds(halo + seq, halo), :] = zeros
    pad_ref[pl.ds(halo, seq), :] = u_ref[...]

    def run(window):
        first = lax.broadcasted_iota(jnp.int32, (rows, ext), 0) + (halo - window // 2)
        col = lax.broadcasted_iota(jnp.int32, (rows, ext), 1)
        band_ref[...] = jnp.where(col < first, 0.0, jnp.where(col < first + window, 1.0, 0.0)).astype(BF16)

        def chunk(c):
            base = pl.multiple_of(c * rows, rows)
            sums = jnp.dot(band_ref[...], pad_ref[pl.ds(base, ext), :], preferred_element_type=F32)
            tok = pad_ref[pl.ds(base + halo, rows), :].astype(F32)
            t = base + lax.broadcasted_iota(jnp.int32, (rows, 1), 0)
            lo = jnp.maximum(t - window // 2, 0)
            hi = jnp.minimum(t - window // 2 + window, seq)
            pooled_ref[pl.ds(base, rows), :] = (sums / (hi - lo).astype(F32) - tok).astype(BF16)

        def chunks(it, carry):
            for u in range(POOL_UNROLL):
                chunk(it * POOL_UNROLL + u)
            return carry

        lax.fori_loop(0, seq // rows // POOL_UNROLL, chunks, 0)

    for gi, window in enumerate(POOL_WINDOWS):
        pl.when(grp == gi)(functools.partial(run, window))

    y = jnp.dot(pooled_ref[...], w_ref[...], preferred_element_type=F32) * sc_ref[...]
    o_ref[...] = y.astype(o_ref.dtype)


def _pool(pg, w_pool, pool_scale, *, batch, seq, col0):
    blk0 = col0 // POOL_GROUP
    ngrp = len(POOL_WINDOWS)
    est = (2 * seq * POOL_GROUP * 2 * 2 + (seq + 2 * POOL_HALO) * POOL_GROUP * 4 + 2 * POOL_GROUP * POOL_GROUP * 2
           + seq * POOL_GROUP * (2 + 4))
    return pl.pallas_call(
        functools.partial(_pool_kernel, seq=seq),
        grid=(batch, ngrp),
        in_specs=[
            pl.BlockSpec((seq, POOL_GROUP), lambda b, g: (b, blk0 + g)),
            pl.BlockSpec((None, POOL_GROUP, POOL_GROUP), lambda b, g: (g, 0, 0)),
            pl.BlockSpec((1, POOL_GROUP), lambda b, g: (0, g)),
        ],
        out_specs=pl.BlockSpec((seq, POOL_GROUP), lambda b, g: (b, g)),
        out_shape=jax.ShapeDtypeStruct((batch * seq, D_POOL), BF16),
        scratch_shapes=[pltpu.VMEM((seq + 2 * POOL_HALO, POOL_GROUP), BF16),
                        pltpu.VMEM((POOL_ROWS, POOL_ROWS + 2 * POOL_HALO), BF16),
                        pltpu.VMEM((seq, POOL_GROUP), BF16)],
        compiler_params=pltpu.CompilerParams(
            dimension_semantics=("arbitrary", "arbitrary"), vmem_limit_bytes=_vmem_limit(est)),
        name="pool",
    )(pg, w_pool, pool_scale)


def _memattn_kernel(q_ref, kv_ref, o_ref):
    scale = XHEAD_DIM ** -0.5
    for hh in range(N_XHEADS):
        lo = hh * XHEAD_DIM
        q = q_ref[:, lo:lo + XHEAD_DIM]
        k = kv_ref[:, lo:lo + XHEAD_DIM]
        v = kv_ref[:, D_XATTN + lo:D_XATTN + lo + XHEAD_DIM]
        s = lax.dot_general(q, k, (((1,), (1,)), ((), ())), preferred_element_type=F32) * scale
        m = jnp.max(s, axis=-1, keepdims=True)
        p = jnp.exp(s - m)
        l = jnp.sum(p, axis=-1, keepdims=True)
        y = jnp.dot(p.astype(BF16), v, preferred_element_type=F32) / l
        o_ref[:, lo:lo + XHEAD_DIM] = y.astype(o_ref.dtype)


def _memattn(pg, kv, *, seq, col0):
    n = pg.shape[0]
    tm = MEM_TM
    blk0 = col0 // D_XATTN
    tiles_per_seq = seq // tm
    est = 2 * tm * D_XATTN * 2 * 2 + 2 * N_MEM * 2 * D_XATTN * 2 + 4 * tm * N_MEM * 4
    return pl.pallas_call(
        _memattn_kernel,
        grid=(n // tm,),
        in_specs=[
            pl.BlockSpec((tm, D_XATTN), lambda i: (i, blk0)),
            pl.BlockSpec((N_MEM, 2 * D_XATTN), lambda i: (i // tiles_per_seq, 0)),
        ],
        out_specs=pl.BlockSpec((tm, D_XATTN), lambda i: (i, 0)),
        out_shape=jax.ShapeDtypeStruct((n, D_XATTN), BF16),
        compiler_params=pltpu.CompilerParams(
            dimension_semantics=("arbitrary",), vmem_limit_bytes=_vmem_limit(est)),
        name="memattn",
    )(pg, kv)


def _merge_kernel(gates_ref, ya_ref, yp_ref, ym_ref, x_ref, wa_ref, wp_ref, wm_ref, wo_ref, gain_ref, o_ref):
    d = x_ref.shape[1]
    for r in range(0, x_ref.shape[0], MERGE_ROWS):
        rows = pl.ds(r, MERGE_ROWS)
        merged = gates_ref[rows, 0:d].astype(F32) * jnp.dot(ya_ref[rows, :], wa_ref[...], preferred_element_type=F32)
        merged += gates_ref[rows, d:2 * d].astype(F32) * jnp.dot(yp_ref[rows, :], wp_ref[...],
                                                                   preferred_element_type=F32)
        merged += gates_ref[rows, 2 * d:3 * d].astype(F32) * jnp.dot(ym_ref[rows, :], wm_ref[...],
                                                                       preferred_element_type=F32)
        z = jnp.dot(merged.astype(BF16), wo_ref[...], preferred_element_type=F32)
        o_ref[rows, :] = x_ref[rows, :] + _rms(z, gain_ref[...])


def _merge(gates, y_attn, y_pool, y_mem, x, wa, wp, wm, wo, gain):
    n, d = x.shape
    tm = MERGE_TM
    wbytes = (wa.size + wp.size + wm.size + wo.size) * 2

    def const(shape):
        return pl.BlockSpec(shape, lambda i: (0, 0), pipeline_mode=pl.Buffered(1))

    est = (2 * tm * (N_BRANCH * d + y_attn.shape[1] + y_pool.shape[1] + y_mem.shape[1]) * 2
           + 4 * tm * d * 4 + wbytes + 4 * tm * d * 4)
    return pl.pallas_call(
        _merge_kernel,
        grid=(n // tm,),
        in_specs=[
            pl.BlockSpec((tm, N_BRANCH * d), lambda i: (i, 0)),
            pl.BlockSpec((tm, y_attn.shape[1]), lambda i: (i, 0)),
            pl.BlockSpec((tm, y_pool.shape[1]), lambda i: (i, 0)),
            pl.BlockSpec((tm, y_mem.shape[1]), lambda i: (i, 0)),
            pl.BlockSpec((tm, d), lambda i: (i, 0)),
            const(wa.shape), const(wp.shape), const(wm.shape), const(wo.shape),
            pl.BlockSpec((1, d), lambda i: (0, 0)),
        ],
        out_specs=pl.BlockSpec((tm, d), lambda i: (i, 0)),
        out_shape=jax.ShapeDtypeStruct((n, d), F32),
        compiler_params=pltpu.CompilerParams(
            dimension_semantics=("arbitrary",), vmem_limit_bytes=_vmem_limit(est)),
        name="merge",
    )(gates, y_attn, y_pool, y_mem, x, wa, wp, wm, wo, gain)


def _cast_kernel(w_ref, *o_refs):
    lo = 0
    for o_ref in o_refs:
        width = o_ref.shape[1]
        o_ref[...] = w_ref[:, lo:lo + width].astype(o_ref.dtype)
        lo += width


def _cast_bf16(w, widths=None, target_bytes=CAST_BLOCK_BYTES):
    rows, cols = w.shape
    widths = (cols,) if widths is None else tuple(widths)
    br = rows
    while br * cols * 4 > target_bytes and br % 32 == 0:
        br //= 2
    est = 2 * br * cols * 4 + 2 * br * cols * 2
    outs = pl.pallas_call(
        _cast_kernel,
        grid=(rows // br,),
        in_specs=[pl.BlockSpec((br, cols), lambda i: (i, 0))],
        out_specs=[pl.BlockSpec((br, wd), lambda i: (i, 0)) for wd in widths],
        out_shape=[jax.ShapeDtypeStruct((rows, wd), BF16) for wd in widths],
        compiler_params=pltpu.CompilerParams(
            dimension_semantics=("arbitrary",), vmem_limit_bytes=_vmem_limit(est)),
        name="cast",
    )(w)
    return outs if len(widths) > 1 else outs[0]


def _cast_chunked_kernel(w_ref, *o_refs, width):
    for k, o_ref in enumerate(o_refs):
        nchunk, _, tf = o_ref.shape
        for c in range(nchunk):
            valid = min(tf, width - c * tf)
            lo = k * width + c * tf
            o_ref[c, :, :valid] = w_ref[:, lo:lo + valid].astype(o_ref.dtype)
            if valid < tf:
                o_ref[c, :, valid:] = jnp.zeros((o_ref.shape[1], tf - valid), o_ref.dtype)


def _cast_chunked(w, n_out, tf, target_bytes=CAST_BLOCK_BYTES):
    rows, cols = w.shape
    width = cols // n_out
    nchunk = pl.cdiv(width, tf)
    br = rows
    while br * cols * 4 > target_bytes and br % 32 == 0:
        br //= 2
    est = 2 * br * cols * 4 + 2 * n_out * nchunk * br * tf * 2
    return pl.pallas_call(
        functools.partial(_cast_chunked_kernel, width=width),
        grid=(rows // br,),
        in_specs=[pl.BlockSpec((br, cols), lambda i: (i, 0))],
        out_specs=[pl.BlockSpec((nchunk, br, tf), lambda i: (0, i, 0)) for _ in range(n_out)],
        out_shape=[jax.ShapeDtypeStruct((nchunk, rows, tf), BF16) for _ in range(n_out)],
        compiler_params=pltpu.CompilerParams(
            dimension_semantics=("arbitrary",), vmem_limit_bytes=_vmem_limit(est)),
        name="cast_chunked",
    )(w)


def _row(v):
    return v.reshape(1, -1).astype(F32)


def _mixer_half(x, mem, p, slopes, side):
    batch, seq, d = x.shape
    xf = x.reshape(batch * seq, d)
    x1 = _ffn(xf, p["ffn1_norm_pre"], p["ffn1_wg"], p["ffn1_wu"], p["ffn1_wd"], p["ffn1_norm_post"],
              p["ffn1_norm_post"], final=False)
    gates, nat, qkv4, qkv16 = _mixproj(x1, p["mix_norm_pre"], p["w_gate"], p["b_gate"], p["w_in"],
                                       batch=batch, seq=seq)
    col_pool = D_XATTN
    col_qkv0 = col_pool + D_POOL
    kv = _kvproj(mem.reshape(batch * N_MEM, d), p["mem_norm"], p["w_mem_kv"])
    y_attn, cast = _attention(nat, qkv4, qkv16, slopes, batch=batch, seq=seq, col0=col_qkv0, side=side)
    y_pool = _pool(nat, p["w_pool"], p["pool_scale"], batch=batch, seq=seq, col0=col_pool)
    y_mem = _memattn(nat, kv, seq=seq, col0=0)
    x2 = _merge(gates, y_attn, y_pool, y_mem, x1, p["w_br_attn"], p["w_br_pool"], p["w_br_mem"], p["w_out"],
                p["mix_norm_post"])
    return x2, cast


def _prep_params(ffn1_norm_pre, ffn1_w_up, ffn1_w_down, ffn1_norm_post, mix_norm_pre, mem_norm, w_in, w_mem_kv,
                 w_pool, pool_scale, w_br_attn, w_br_pool, w_br_mem, w_gate, b_gate, w_out, mix_norm_post,
                 ffn2_norm_pre, ffn2_w_up, ffn2_w_down, ffn2_norm_post, final_norm):
    p = {}
    p["ffn1_wg"], p["ffn1_wu"] = _cast_chunked(ffn1_w_up, 2, FFN_TF)
    p["ffn1_wd"] = _cast_bf16(ffn1_w_down)
    p["ffn2_w_up"], p["ffn2_w_down"] = ffn2_w_up, ffn2_w_down
    p["w_gate"] = _cast_bf16(w_gate)
    p["w_in"] = _cast_bf16(w_in)
    p["b_gate"] = _row(b_gate)
    p["w_mem_kv"] = _cast_bf16(w_mem_kv)
    p["w_pool"] = w_pool.astype(BF16)
    p["pool_scale"] = _row(pool_scale)
    for name, w in (("w_br_attn", w_br_attn), ("w_br_pool", w_br_pool), ("w_br_mem", w_br_mem), ("w_out", w_out)):
        p[name] = _cast_bf16(w)
    for name, v in (("ffn1_norm_pre", ffn1_norm_pre), ("ffn1_norm_post", ffn1_norm_post),
                    ("mix_norm_pre", mix_norm_pre), ("mem_norm", mem_norm), ("mix_norm_post", mix_norm_post),
                    ("ffn2_norm_pre", ffn2_norm_pre), ("ffn2_norm_post", ffn2_norm_post),
                    ("final_norm", final_norm)):
        p[name] = _row(v)
    return p


def kernel(x_prompt, x_sample, mem_prompt, mem_sample, ffn1_norm_pre, ffn1_w_up, ffn1_w_down, ffn1_norm_post,
           mix_norm_pre, mem_norm, w_in, w_mem_kv, w_pool, pool_scale, w_br_attn, w_br_pool, w_br_mem, w_gate, b_gate,
           w_out, mix_norm_post, ffn2_norm_pre, ffn2_w_up, ffn2_w_down, ffn2_norm_post, final_norm):
    layer = [ffn1_norm_pre, ffn1_w_up, ffn1_w_down, ffn1_norm_post, mix_norm_pre, mem_norm, w_in, w_mem_kv, w_pool,
             pool_scale, w_br_attn, w_br_pool, w_br_mem, w_gate, b_gate, w_out, mix_norm_post, ffn2_norm_pre,
             ffn2_w_up, ffn2_w_down, ffn2_norm_post, final_norm]
    depth = ffn1_norm_pre.shape[0]
    slopes = jnp.asarray(_alibi_slopes())
    y_prompt, y_sample = x_prompt, x_sample
    for layer_idx in range(depth):
        p = _prep_params(*[w[layer_idx] for w in layer])
        x2_prompt, (wg2, wu2) = _mixer_half(y_prompt, mem_prompt, p, slopes, ("gate_up", p["ffn2_w_up"]))
        x2_sample, (wd2,) = _mixer_half(y_sample, mem_sample, p, slopes, ("plain", p["ffn2_w_down"]))
        y_prompt, y_sample = (
            _ffn(x2, p["ffn2_norm_pre"], wg2, wu2, wd2, p["ffn2_norm_post"], p["final_norm"],
                 final=True).reshape(x.shape)
            for x2, x in ((x2_prompt, y_prompt), (x2_sample, y_sample)))
    return (y_prompt, y_sample)
```

```python
import functools

import numpy as np
import jax
import jax.numpy as jnp
from jax import lax
from jax.experimental import pallas as pl
from jax.experimental.pallas import tpu as pltpu

F32 = jnp.float32
BF16 = jnp.bfloat16

D_MODEL = 2048
N_MEM = 256
HEAD_DIM = 128
DIL_PAIRS = ((128, 1), (512, 4), (2048, 16))
HEADS_PER_GROUP = 4
N_GROUPS = len(DIL_PAIRS)
N_DIL_HEADS = HEADS_PER_GROUP * N_GROUPS
D_ATTN = N_DIL_HEADS * HEAD_DIM
POOL_WINDOWS = (2, 4, 8, 16)
POOL_GROUP = 256
D_POOL = POOL_GROUP * len(POOL_WINDOWS)
N_XHEADS = 4
XHEAD_DIM = 256
D_XATTN = N_XHEADS * XHEAD_DIM
N_BRANCH = 3
EPS = 1e-6
NEG = -1e30
BAND_RADIUS = 64

V7X_VMEM_BYTES = 64 * 1024 * 1024
MIB = 1024 * 1024

FFN_TM = 1024
FFN_TF = 1024
PROJ_TM = 1024
PROJ_TN = 512
PROJ_TG = 1536
PROJ_ROWS = 256
DEINT_STRIDE = 4
ATT_QB = 128
ATT_KW = 256
ATT_UNROLL = 16
POOL_ROWS = 256
POOL_HALO = 64
POOL_UNROLL = 4
MEM_TM = 512
MERGE_TM = 512
MERGE_ROWS = 512
CAST_BLOCK_BYTES = 8 * MIB


def _vmem_limit(nbytes):
    return int(min(nbytes * 5 // 4 + 8 * MIB, V7X_VMEM_BYTES - 2 * MIB))


def _hbm_array(shape, dtype):
    return pltpu.HBM(shape, dtype)


def _rms(x, gain):
    ms = jnp.mean(x * x, axis=-1, keepdims=True)
    return x * lax.rsqrt(ms + EPS) * gain


def _alibi_slopes():
    s = 2.0 ** (-8.0 * np.arange(1, N_DIL_HEADS + 1) / N_DIL_HEADS)
    return s.reshape(HEADS_PER_GROUP, N_GROUPS).T.astype(np.float32)


def _ffn_kernel(x_ref, gpre_ref, wg_ref, wu_ref, wd_ref, gpost_ref, gfin_ref, o_hbm, acc_ref, hs_ref, sem,
                *, tf, rem, final):
    i = pl.program_id(0)
    f = pl.program_id(1)
    n_tiles = pl.num_programs(0)
    last = pl.num_programs(1) - 1
    tm = acc_ref.shape[0]

    def writeback(tile):
        return pltpu.make_async_copy(acc_ref, o_hbm.at[pl.ds(pl.multiple_of(tile * tm, tm), tm), :], sem)

    @pl.when(f == 0)
    def _():
        hs_ref[...] = _rms(x_ref[...], gpre_ref[...]).astype(BF16)

    def step(valid, first):
        hs = hs_ref[...]
        g = jnp.dot(hs, wg_ref[:, :valid], preferred_element_type=F32)
        u = jnp.dot(hs, wu_ref[:, :valid], preferred_element_type=F32)
        a = (g * jax.nn.sigmoid(g) * u).astype(BF16)
        if first:
            @pl.when(i > 0)
            def _():
                writeback(i - 1).wait()

            acc_ref[...] = jnp.dot(a, wd_ref[:valid, :], preferred_element_type=F32)
        else:
            acc_ref[...] += jnp.dot(a, wd_ref[:valid, :], preferred_element_type=F32)

    pl.when(f == 0)(functools.partial(step, tf, True))
    pl.when((f > 0) & (f < last))(functools.partial(step, tf, False))
    pl.when(f == last)(functools.partial(step, rem, False))

    @pl.when(f == last)
    def _():
        y = x_ref[...] + 0.5 * _rms(acc_ref[...], gpost_ref[...])
        if final:
            y = _rms(y, gfin_ref[...])
        acc_ref[...] = y
        writeback(i).start()

        @pl.when(i == n_tiles - 1)
        def _():
            writeback(i).wait()


def _ffn(x, gpre, wg, wu, wd, gpost, gfin, *, final):
    n, d = x.shape
    tm, tf = FFN_TM, FFN_TF
    dff = wd.shape[0]
    nf = pl.cdiv(dff, tf)
    rem = dff - (nf - 1) * tf
    assert nf >= 2
    est = 2 * tm * d * 4 + tm * d * 4 + tm * d * 2 + 2 * 3 * d * tf * 2 + 2 * tm * tf * 4
    vec = pl.BlockSpec((1, d), lambda i, f: (0, 0))
    return pl.pallas_call(
        functools.partial(_ffn_kernel, tf=tf, rem=rem, final=final),
        grid=(n // tm, nf),
        in_specs=[
            pl.BlockSpec((tm, d), lambda i, f: (i, 0)),
            vec,
            pl.BlockSpec((None, d, tf), lambda i, f: (f, 0, 0)),
            pl.BlockSpec((None, d, tf), lambda i, f: (f, 0, 0)),
            pl.BlockSpec((tf, d), lambda i, f: (f, 0)),
            vec,
            vec,
        ],
        out_specs=pl.BlockSpec(memory_space=pl.ANY),
        out_shape=_hbm_array((n, d), F32),
        scratch_shapes=[pltpu.VMEM((tm, d), F32), pltpu.VMEM((tm, d), BF16), pltpu.SemaphoreType.DMA(())],
        compiler_params=pltpu.CompilerParams(
            dimension_semantics=("arbitrary", "arbitrary"), vmem_limit_bytes=_vmem_limit(est)),
        name="ffn_final" if final else "ffn",
    )(x, gpre, wg, wu, wd, gpost, gfin)


def _kvproj_kernel(x_ref, g_ref, w_ref, o_ref, hs_ref):
    @pl.when(pl.program_id(1) == 0)
    def _():
        hs_ref[...] = _rms(x_ref[...], g_ref[...]).astype(BF16)

    o_ref[...] = jnp.dot(hs_ref[...], w_ref[...], preferred_element_type=F32).astype(o_ref.dtype)


def _kvproj(x, gain, w):
    n, d = x.shape
    ncol = w.shape[1]
    tm, tn = n, PROJ_TN
    est = 2 * tm * d * 4 + tm * d * 2 + 2 * d * tn * 2 + 2 * tm * tn * 2 + 2 * tm * tn * 4
    return pl.pallas_call(
        _kvproj_kernel,
        grid=(n // tm, ncol // tn),
        in_specs=[
            pl.BlockSpec((tm, d), lambda i, j: (i, 0)),
            pl.BlockSpec((1, d), lambda i, j: (0, 0)),
            pl.BlockSpec((d, tn), lambda i, j: (0, j)),
        ],
        out_specs=pl.BlockSpec((tm, tn), lambda i, j: (i, j)),
        out_shape=_hbm_array((n, ncol), BF16),
        scratch_shapes=[pltpu.VMEM((tm, d), BF16)],
        compiler_params=pltpu.CompilerParams(
            dimension_semantics=("arbitrary", "arbitrary"), vmem_limit_bytes=_vmem_limit(est)),
        name="kvproj",
    )(x, gain, w)


NAT_CHUNKS = (11, 12, 9, 10, 0, 3, 6)
N_NAT_STEPS = len(NAT_CHUNKS)
N_DIL_STEPS = 3
N_GATE_STEPS = N_BRANCH * D_MODEL // PROJ_TG
FIRST_D4 = N_NAT_STEPS
FIRST_D16 = FIRST_D4 + N_DIL_STEPS
FIRST_GATE = FIRST_D16 + N_DIL_STEPS
MIX_STEPS = FIRST_GATE + N_GATE_STEPS


def _lin_chunk(s):
    t = jnp.minimum(s, FIRST_GATE - 1)
    nat = jnp.where(t < 2, 11 + t, jnp.where(t < 4, 7 + t, 3 * (t - 4)))
    d4 = 3 * (t - FIRST_D4) + 1
    d16 = 3 * (t - FIRST_D16) + 2
    return jnp.where(t < FIRST_D4, nat, jnp.where(t < FIRST_D16, d4, d16))


def _mixproj_kernel(x_ref, g_ref, wg_ref, b_ref, wi_ref, og_ref, on_ref, o4_ref, o16_ref, hs_ref, stage_ref,
                    stage2_ref):
    s = pl.program_id(1)
    tm = x_ref.shape[0]
    row_chunks = [pl.ds(r, PROJ_ROWS) for r in range(0, tm, PROJ_ROWS)]

    @pl.when(s == 0)
    def _():
        hs_ref[...] = _rms(x_ref[...], g_ref[...]).astype(BF16)

    @pl.when(s < FIRST_D4)
    def _():
        for rows in row_chunks:
            on_ref[rows, :] = jnp.dot(hs_ref[rows, :], wi_ref[...], preferred_element_type=F32).astype(on_ref.dtype)

    def dilated(o_ref, d):
        per_res = PROJ_ROWS // d
        slabs = range(PROJ_TN // HEAD_DIM)
        for rc, rows in enumerate(row_chunks):
            acc = jnp.dot(hs_ref[rows, :], wi_ref[...], preferred_element_type=F32)
            for c in slabs:
                stage_ref[c] = acc[:, c * HEAD_DIM:(c + 1) * HEAD_DIM]
            if d == DEINT_STRIDE:
                for res in range(d):
                    for c in slabs:
                        o_ref[res, pl.ds(rc * per_res, per_res), c * HEAD_DIM:(c + 1) * HEAD_DIM] = (
                            stage_ref[c, pl.ds(res, per_res, stride=d), :].astype(o_ref.dtype))
            else:
                for c in slabs:
                    for r1 in range(DEINT_STRIDE):
                        stage2_ref[c, r1] = stage_ref[c, pl.ds(r1, PROJ_ROWS // DEINT_STRIDE, stride=DEINT_STRIDE), :]
                for r1 in range(DEINT_STRIDE):
                    for r2 in range(DEINT_STRIDE):
                        res = r1 + DEINT_STRIDE * r2
                        for c in slabs:
                            o_ref[res, pl.ds(rc * per_res, per_res), c * HEAD_DIM:(c + 1) * HEAD_DIM] = (
                                stage2_ref[c, r1, pl.ds(r2, per_res, stride=DEINT_STRIDE), :].astype(o_ref.dtype))

    pl.when((s >= FIRST_D4) & (s < FIRST_D16))(functools.partial(dilated, o4_ref, DIL_PAIRS[1][1]))
    pl.when((s >= FIRST_D16) & (s < FIRST_GATE))(functools.partial(dilated, o16_ref, DIL_PAIRS[2][1]))

    @pl.when(s >= FIRST_GATE)
    def _():
        for rows in row_chunks:
            z = jnp.dot(hs_ref[rows, :], wg_ref[...], preferred_element_type=F32) + b_ref[...]
            og_ref[rows, :] = (0.5 * jnp.tanh(0.5 * z) + 0.5).astype(og_ref.dtype)


def _mixproj(x, gain, w_gate, b_gate, w_in, *, batch, seq):
    n, d = x.shape
    tm, tn, tg = PROJ_TM, PROJ_TN, PROJ_TG
    tiles_per_seq = seq // tm
    d4, d16 = DIL_PAIRS[1][1], DIL_PAIRS[2][1]

    def gate_idx(s):
        return jnp.maximum(s - FIRST_GATE, 0)

    def dil_spec(dil, first):
        return pl.BlockSpec(
            (None, dil, tm // dil, tn),
            lambda i, s: (i // tiles_per_seq, 0, i % tiles_per_seq, jnp.clip(s - first, 0, N_DIL_STEPS - 1)))

    est = (2 * tm * d * 4 + tm * d * 2 + 2 * d * (tg + tn) * 2 + 2 * tm * (tg + 3 * tn) * 2
           + PROJ_ROWS * tn * 4 + 2 * PROJ_ROWS * tg * 4)
    return pl.pallas_call(
        _mixproj_kernel,
        grid=(n // tm, MIX_STEPS),
        in_specs=[
            pl.BlockSpec((tm, d), lambda i, s: (i, 0)),
            pl.BlockSpec((1, d), lambda i, s: (0, 0)),
            pl.BlockSpec((d, tg), lambda i, s: (0, gate_idx(s))),
            pl.BlockSpec((1, tg), lambda i, s: (0, gate_idx(s))),
            pl.BlockSpec((d, tn), lambda i, s: (0, _lin_chunk(s))),
        ],
        out_specs=[
            pl.BlockSpec((tm, tg), lambda i, s: (i, gate_idx(s))),
            pl.BlockSpec((tm, tn), lambda i, s: (i, jnp.minimum(s, N_NAT_STEPS - 1))),
            dil_spec(d4, FIRST_D4),
            dil_spec(d16, FIRST_D16),
        ],
        out_shape=[
            _hbm_array((n, N_BRANCH * d), BF16),
            _hbm_array((n, N_NAT_STEPS * tn), BF16),
            _hbm_array((batch, d4, seq // d4, N_DIL_STEPS * tn), BF16),
            _hbm_array((batch, d16, seq // d16, N_DIL_STEPS * tn), BF16),
        ],
        scratch_shapes=[
            pltpu.VMEM((tm, d), BF16),
            pltpu.VMEM((tn // HEAD_DIM, PROJ_ROWS, HEAD_DIM), F32),
            pltpu.VMEM((tn // HEAD_DIM, DEINT_STRIDE, PROJ_ROWS // DEINT_STRIDE, HEAD_DIM), F32),
        ],
        compiler_params=pltpu.CompilerParams(
            dimension_semantics=("arbitrary", "arbitrary"), vmem_limit_bytes=_vmem_limit(est)),
        name="mixproj",
    )(x, gain, w_gate, b_gate, w_in)


def _attn_kernel(*refs, seq, side_cast, n_side_out):
    slopes_ref, q0_ref, k0_ref, v0_ref, q1_ref, k1_ref, v1_ref, q2_ref, k2_ref, v2_ref = refs[:10]
    n_side_in = 0 if side_cast is None else 1
    o_ref = refs[10 + n_side_in]
    m1_ref, m2_ref, l1_ref, l2_ref, n1_ref, n2_ref, va_ref, bias_ref = refs[11 + n_side_in + n_side_out:]
    if side_cast is not None:
        side_cast(refs[10], *refs[11 + n_side_in:11 + n_side_in + n_side_out])
    h = pl.program_id(1)
    scale = HEAD_DIM ** -0.5
    qb = ATT_QB
    qkv_refs = ((q0_ref, k0_ref, v0_ref), (q1_ref, k1_ref, v1_ref), (q2_ref, k2_ref, v2_ref))
    m_refs = (None, m1_ref, m2_ref)
    l_refs = (None, l1_ref, l2_ref)
    n_refs = (None, n1_ref, n2_ref)

    va_ref[:, HEAD_DIM:] = jnp.ones((seq, HEAD_DIM), BF16)

    for gi in reversed(range(N_GROUPS)):
        d = DIL_PAIRS[gi][1]
        q_ref, k_ref, v_ref = qkv_refs[gi]
        sub_len = seq // d
        kw = ATT_KW
        nb = sub_len // qb
        short = sub_len < kw
        slope_d = slopes_ref[gi, h] * float(d)

        row = lax.broadcasted_iota(jnp.int32, (qb, kw), 0)
        col = lax.broadcasted_iota(jnp.int32, (qb, kw), 1)
        for kind, off in enumerate((0, -BAND_RADIUS, qb - kw)):
            arel = jnp.abs(col - row + off)
            bias = jnp.where(arel <= BAND_RADIUS, -slope_d * arel.astype(F32), NEG)
            if short:
                key = col + off
                bias = jnp.where(key < 0, NEG, jnp.where(key < sub_len, bias, NEG))
            bias_ref[kind] = bias

        va_ref[:, :HEAD_DIM] = v_ref[...]

        def block(idx, gi=gi, d=d, q_ref=q_ref, k_ref=k_ref, sub_len=sub_len, kw=kw, nb=nb, short=short):
            qstart = pl.multiple_of(idx * qb, qb)
            if short:
                res, q0, pos, last = idx, 0, idx, d * nb - 1
                kstart = jnp.clip(qstart - BAND_RADIUS, 0, seq - kw)
            else:
                res = idx // nb
                pos, last = idx - res * nb, nb - 1
                q0 = pos * qb
                kstart = res * sub_len + jnp.clip(q0 - BAND_RADIUS, 0, sub_len - kw)
            kstart = pl.multiple_of(kstart, BAND_RADIUS)
            kind = jnp.where(pos == 0, 0, jnp.where(pos == last, 2, 1))
            s = lax.dot_general(q_ref[pl.ds(qstart, qb), :], k_ref[pl.ds(kstart, kw), :],
                                (((1,), (1,)), ((), ())), preferred_element_type=F32)
            s = s * scale + bias_ref[kind]
            m = jnp.max(s, axis=-1, keepdims=True)
            p = jnp.exp(s - m).astype(BF16)
            na = jnp.dot(p, va_ref[pl.ds(kstart, kw), :], preferred_element_type=F32)
            m = jnp.broadcast_to(m, (qb, HEAD_DIM))
            if d > 1:
                rows = pl.ds(q0 * d + res, qb, stride=d)
                m_refs[gi][rows, :] = m
                l_refs[gi][rows, :] = na[:, HEAD_DIM:]
                n_refs[gi][rows, :] = na[:, :HEAD_DIM]
            else:
                rows = pl.ds(qstart, qb)
                ms = [m] + [m_refs[g][rows, :] for g in range(1, N_GROUPS)]
                mx = functools.reduce(jnp.maximum, ms)
                e = jnp.exp(m - mx)
                num = e * na[:, :HEAD_DIM]
                den = e * na[:, HEAD_DIM:]
                for g in range(1, N_GROUPS):
                    e = jnp.exp(ms[g] - mx)
                    num += e * n_refs[g][rows, :]
                    den += e * l_refs[g][rows, :]
                o_ref[rows, :] = (num / den).astype(o_ref.dtype)

        unroll = min(ATT_UNROLL, d * nb // 2)

        def blocks(it, carry, block=block, unroll=unroll):
            for u in range(unroll):
                block(it * unroll + u)
            return carry

        lax.fori_loop(0, d * nb // unroll, blocks, 0)


def _side_cast_specs(kind, w, steps):
    rows, cols = w.shape

    def step(b, h):
        return b * HEADS_PER_GROUP + h

    if kind == "gate_up":
        br, dff = rows // steps, cols // 2
        nchunk = pl.cdiv(dff, FFN_TF)
        in_spec = pl.BlockSpec((br, cols), lambda b, h: (step(b, h), 0))
        out_specs = [pl.BlockSpec((nchunk, br, FFN_TF), lambda b, h: (0, step(b, h), 0)) for _ in range(2)]
        out_shapes = [_hbm_array((nchunk, rows, FFN_TF), BF16) for _ in range(2)]
        return functools.partial(_cast_chunked_kernel, width=dff), in_spec, out_specs, out_shapes, br * cols
    bc = cols // steps
    in_spec = pl.BlockSpec((rows, bc), lambda b, h: (0, step(b, h)))
    out_specs = [pl.BlockSpec((rows, bc), lambda b, h: (0, step(b, h)))]
    out_shapes = [_hbm_array((rows, cols), BF16)]
    return _cast_kernel, in_spec, out_specs, out_shapes, rows * bc


def _attention(nat, qkv4, qkv16, slopes, *, batch, seq, col0, side=None):
    blk0 = col0 // HEAD_DIM
    qkv4 = qkv4.reshape(batch * seq, qkv4.shape[-1])
    qkv16 = qkv16.reshape(batch * seq, qkv16.shape[-1])

    def nat_map(b, h, *, which):
        return (b, blk0 + which * HEADS_PER_GROUP + h)

    def dil_map(b, h, *, which):
        return (b, which * HEADS_PER_GROUP + h)

    in_specs = [pl.BlockSpec(memory_space=pltpu.SMEM)]
    in_specs += [pl.BlockSpec((seq, HEAD_DIM), functools.partial(nat_map, which=w)) for w in range(3)]
    in_specs += 2 * [pl.BlockSpec((seq, HEAD_DIM), functools.partial(dil_map, which=w)) for w in range(3)]
    tile = seq * HEAD_DIM
    est = 10 * 2 * tile * 2 + 6 * tile * 4 + 2 * tile * 2 + 3 * ATT_QB * ATT_KW * 4
    out_specs = [pl.BlockSpec((seq, HEAD_DIM), lambda b, h: (b, h))]
    out_shapes = [_hbm_array((batch * seq, HEADS_PER_GROUP * HEAD_DIM), BF16)]
    operands = [slopes, nat, nat, nat, qkv4, qkv4, qkv4, qkv16, qkv16, qkv16]
    side_cast = None
    if side is not None:
        kind, w = side
        side_cast, side_in, side_outs, side_shapes, block_elems = _side_cast_specs(
            kind, w, batch * HEADS_PER_GROUP)
        in_specs.append(side_in)
        operands.append(w)
        out_specs += side_outs
        out_shapes += side_shapes
        est += 2 * block_elems * (4 + 2)
    state = [pltpu.VMEM((seq, HEAD_DIM), F32) for _ in range(3 * (N_GROUPS - 1))]
    outs = pl.pallas_call(
        functools.partial(_attn_kernel, seq=seq, side_cast=side_cast, n_side_out=len(out_specs) - 1),
        grid=(batch, HEADS_PER_GROUP),
        in_specs=in_specs,
        out_specs=out_specs,
        out_shape=out_shapes,
        scratch_shapes=state + [
            pltpu.VMEM((seq, 2 * HEAD_DIM), BF16),
            pltpu.VMEM((3, ATT_QB, ATT_KW), F32),
        ],
        compiler_params=pltpu.CompilerParams(
            dimension_semantics=("arbitrary", "arbitrary"), vmem_limit_bytes=_vmem_limit(est)),
        name="attention",
    )(*operands)
    return outs[0], tuple(outs[1:])


def _pool_kernel(u_ref, w_ref, sc_ref, o_ref, pad_ref, band_ref, pooled_ref, *, seq):
    grp = pl.program_id(1)
    halo, rows = POOL_HALO, POOL_ROWS
    ext = rows + 2 * halo
    zeros = jnp.zeros((halo, POOL_GROUP), BF16)
    pad_ref[pl.ds(0, halo), :] = zeros
    pad_ref[pl.ds(halo + seq, halo), :] = zeros
    pad_ref[pl.ds(halo, seq), :] = u_ref[...]

    def run(window):
        first = lax.broadcasted_iota(jnp.int32, (rows, ext), 0) + (halo - window // 2)
        col = lax.broadcasted_iota(jnp.int32, (rows, ext), 1)
        band_ref[...] = jnp.where(col < first, 0.0, jnp.where(col < first + window, 1.0, 0.0)).astype(BF16)

        def chunk(c):
            base = pl.multiple_of(c * rows, rows)
            sums = jnp.dot(band_ref[...], pad_ref[pl.ds(base, ext), :], preferred_element_type=F32)
            tok = pad_ref[pl.ds(base + halo, rows), :].astype(F32)
            t = base + lax.broadcasted_iota(jnp.int32, (rows, 1), 0)
            lo = jnp.maximum(t - window // 2, 0)
            hi = jnp.minimum(t - window // 2 + window, seq)
            pooled_ref[pl.ds(base, rows), :] = (sums / (hi - lo).astype(F32) - tok).astype(BF16)

        def chunks(it, carry):
            for u in range(POOL_UNROLL):
                chunk(it * POOL_UNROLL + u)
            return carry

        lax.fori_loop(0, seq // rows // POOL_UNROLL, chunks, 0)

    for gi, window in enumerate(POOL_WINDOWS):
        pl.when(grp == gi)(functools.partial(run, window))

    y = jnp.dot(pooled_ref[...], w_ref[...], preferred_element_type=F32) * sc_ref[...]
    o_ref[...] = y.astype(o_ref.dtype)


def _pool(pg, w_pool, pool_scale, *, batch, seq, col0):
    blk0 = col0 // POOL_GROUP
    ngrp = len(POOL_WINDOWS)
    est = (2 * seq * POOL_GROUP * 2 * 2 + (seq + 2 * POOL_HALO) * POOL_GROUP * 4 + 2 * POOL_GROUP * POOL_GROUP * 2
           + seq * POOL_GROUP * (2 + 4))
    return pl.pallas_call(
        functools.partial(_pool_kernel, seq=seq),
        grid=(batch, ngrp),
        in_specs=[
            pl.BlockSpec((seq, POOL_GROUP), lambda b, g: (b, blk0 + g)),
            pl.BlockSpec((None, POOL_GROUP, POOL_GROUP), lambda b, g: (g, 0, 0)),
            pl.BlockSpec((1, POOL_GROUP), lambda b, g: (0, g)),
        ],
        out_specs=pl.BlockSpec((seq, POOL_GROUP), lambda b, g: (b, g)),
        out_shape=_hbm_array((batch * seq, D_POOL), BF16),
        scratch_shapes=[pltpu.VMEM((seq + 2 * POOL_HALO, POOL_GROUP), BF16),
                        pltpu.VMEM((POOL_ROWS, POOL_ROWS + 2 * POOL_HALO), BF16),
                        pltpu.VMEM((seq, POOL_GROUP), BF16)],
        compiler_params=pltpu.CompilerParams(
            dimension_semantics=("arbitrary", "arbitrary"), vmem_limit_bytes=_vmem_limit(est)),
        name="pool",
    )(pg, w_pool, pool_scale)


def _memattn_kernel(q_ref, kv_ref, o_ref):
    scale = XHEAD_DIM ** -0.5
    for hh in range(N_XHEADS):
        lo = hh * XHEAD_DIM
        q = q_ref[:, lo:lo + XHEAD_DIM]
        k = kv_ref[:, lo:lo + XHEAD_DIM]
        v = kv_ref[:, D_XATTN + lo:D_XATTN + lo + XHEAD_DIM]
        s = lax.dot_general(q, k, (((1,), (1,)), ((), ())), preferred_element_type=F32) * scale
        m = jnp.max(s, axis=-1, keepdims=True)
        p = jnp.exp(s - m)
        l = jnp.sum(p, axis=-1, keepdims=True)
        y = jnp.dot(p.astype(BF16), v, preferred_element_type=F32) / l
        o_ref[:, lo:lo + XHEAD_DIM] = y.astype(o_ref.dtype)


def _memattn(pg, kv, *, seq, col0):
    n = pg.shape[0]
    tm = MEM_TM
    blk0 = col0 // D_XATTN
    tiles_per_seq = seq // tm
    est = 2 * tm * D_XATTN * 2 * 2 + 2 * N_MEM * 2 * D_XATTN * 2 + 4 * tm * N_MEM * 4
    return pl.pallas_call(
        _memattn_kernel,
        grid=(n // tm,),
        in_specs=[
            pl.BlockSpec((tm, D_XATTN), lambda i: (i, blk0)),
            pl.BlockSpec((N_MEM, 2 * D_XATTN), lambda i: (i // tiles_per_seq, 0)),
        ],
        out_specs=pl.BlockSpec((tm, D_XATTN), lambda i: (i, 0)),
        out_shape=_hbm_array((n, D_XATTN), BF16),
        compiler_params=pltpu.CompilerParams(
            dimension_semantics=("arbitrary",), vmem_limit_bytes=_vmem_limit(est)),
        name="memattn",
    )(pg, kv)


def _merge_kernel(gates_ref, ya_ref, yp_ref, ym_ref, x_ref, wa_ref, wp_ref, wm_ref, wo_ref, gain_ref, o_ref):
    d = x_ref.shape[1]
    for r in range(0, x_ref.shape[0], MERGE_ROWS):
        rows = pl.ds(r, MERGE_ROWS)
        merged = gates_ref[rows, 0:d].astype(F32) * jnp.dot(ya_ref[rows, :], wa_ref[...], preferred_element_type=F32)
        merged += gates_ref[rows, d:2 * d].astype(F32) * jnp.dot(yp_ref[rows, :], wp_ref[...],
                                                                   preferred_element_type=F32)
        merged += gates_ref[rows, 2 * d:3 * d].astype(F32) * jnp.dot(ym_ref[rows, :], wm_ref[...],
                                                                       preferred_element_type=F32)
        z = jnp.dot(merged.astype(BF16), wo_ref[...], preferred_element_type=F32)
        o_ref[rows, :] = x_ref[rows, :] + _rms(z, gain_ref[...])


def _merge(gates, y_attn, y_pool, y_mem, x, wa, wp, wm, wo, gain):
    n, d = x.shape
    tm = MERGE_TM
    wbytes = (wa.size + wp.size + wm.size + wo.size) * 2

    def const(shape):
        return pl.BlockSpec(shape, lambda i: (0, 0), pipeline_mode=pl.Buffered(1))

    est = (2 * tm * (N_BRANCH * d + y_attn.shape[1] + y_pool.shape[1] + y_mem.shape[1]) * 2
           + 4 * tm * d * 4 + wbytes + 4 * tm * d * 4)
    return pl.pallas_call(
        _merge_kernel,
        grid=(n // tm,),
        in_specs=[
            pl.BlockSpec((tm, N_BRANCH * d), lambda i: (i, 0)),
            pl.BlockSpec((tm, y_attn.shape[1]), lambda i: (i, 0)),
            pl.BlockSpec((tm, y_pool.shape[1]), lambda i: (i, 0)),
            pl.BlockSpec((tm, y_mem.shape[1]), lambda i: (i, 0)),
            pl.BlockSpec((tm, d), lambda i: (i, 0)),
            const(wa.shape), const(wp.shape), const(wm.shape), const(wo.shape),
            pl.BlockSpec((1, d), lambda i: (0, 0)),
        ],
        out_specs=pl.BlockSpec((tm, d), lambda i: (i, 0)),
        out_shape=_hbm_array((n, d), F32),
        compiler_params=pltpu.CompilerParams(
            dimension_semantics=("arbitrary",), vmem_limit_bytes=_vmem_limit(est)),
        name="merge",
    )(gates, y_attn, y_pool, y_mem, x, wa, wp, wm, wo, gain)


def _cast_kernel(w_ref, *o_refs):
    lo = 0
    for o_ref in o_refs:
        width = o_ref.shape[1]
        o_ref[...] = w_ref[:, lo:lo + width].astype(o_ref.dtype)
        lo += width


def _cast_bf16(w, widths=None, target_bytes=CAST_BLOCK_BYTES):
    rows, cols = w.shape
    widths = (cols,) if widths is None else tuple(widths)
    br = rows
    while br * cols * 4 > target_bytes and br % 32 == 0:
        br //= 2
    est = 2 * br * cols * 4 + 2 * br * cols * 2
    outs = pl.pallas_call(
        _cast_kernel,
        grid=(rows // br,),
        in_specs=[pl.BlockSpec((br, cols), lambda i: (i, 0))],
        out_specs=[pl.BlockSpec((br, wd), lambda i: (i, 0)) for wd in widths],
        out_shape=[_hbm_array((rows, wd), BF16) for wd in widths],
        compiler_params=pltpu.CompilerParams(
            dimension_semantics=("arbitrary",), vmem_limit_bytes=_vmem_limit(est)),
        name="cast",
    )(w)
    return outs if len(widths) > 1 else outs[0]


def _cast_chunked_kernel(w_ref, *o_refs, width):
    for k, o_ref in enumerate(o_refs):
        nchunk, _, tf = o_ref.shape
        for c in range(nchunk):
            valid = min(tf, width - c * tf)
            lo = k * width + c * tf
            o_ref[c, :, :valid] = w_ref[:, lo:lo + valid].astype(o_ref.dtype)
            if valid < tf:
                o_ref[c, :, valid:] = jnp.zeros((o_ref.shape[1], tf - valid), o_ref.dtype)


def _cast_chunked(w, n_out, tf, target_bytes=CAST_BLOCK_BYTES):
    rows, cols = w.shape
    width = cols // n_out
    nchunk = pl.cdiv(width, tf)
    br = rows
    while br * cols * 4 > target_bytes and br % 32 == 0:
        br //= 2
    est = 2 * br * cols * 4 + 2 * n_out * nchunk * br * tf * 2
    return pl.pallas_call(
        functools.partial(_cast_chunked_kernel, width=width),
        grid=(rows // br,),
        in_specs=[pl.BlockSpec((br, cols), lambda i: (i, 0))],
        out_specs=[pl.BlockSpec((nchunk, br, tf), lambda i: (0, i, 0)) for _ in range(n_out)],
        out_shape=[_hbm_array((nchunk, rows, tf), BF16) for _ in range(n_out)],
        compiler_params=pltpu.CompilerParams(
            dimension_semantics=("arbitrary",), vmem_limit_bytes=_vmem_limit(est)),
        name="cast_chunked",
    )(w)


def _row(v):
    return v.reshape(1, -1).astype(F32)


def _mixer_half(x, mem, p, slopes, side):
    batch, seq, d = x.shape
    xf = x.reshape(batch * seq, d)
    x1 = _ffn(xf, p["ffn1_norm_pre"], p["ffn1_wg"], p["ffn1_wu"], p["ffn1_wd"], p["ffn1_norm_post"],
              p["ffn1_norm_post"], final=False)
    gates, nat, qkv4, qkv16 = _mixproj(x1, p["mix_norm_pre"], p["w_gate"], p["b_gate"], p["w_in"],
                                       batch=batch, seq=seq)
    col_pool = D_XATTN
    col_qkv0 = col_pool + D_POOL
    kv = _kvproj(mem.reshape(batch * N_MEM, d), p["mem_norm"], p["w_mem_kv"])
    y_attn, cast = _attention(nat, qkv4, qkv16, slopes, batch=batch, seq=seq, col0=col_qkv0, side=side)
    y_pool = _pool(nat, p["w_pool"], p["pool_scale"], batch=batch, seq=seq, col0=col_pool)
    y_mem = _memattn(nat, kv, seq=seq, col0=0)
    x2 = _merge(gates, y_attn, y_pool, y_mem, x1, p["w_br_attn"], p["w_br_pool"], p["w_br_mem"], p["w_out"],
                p["mix_norm_post"])
    return x2, cast


def _prep_params(ffn1_norm_pre, ffn1_w_up, ffn1_w_down, ffn1_norm_post, mix_norm_pre, mem_norm, w_in, w_mem_kv,
                 w_pool, pool_scale, w_br_attn, w_br_pool, w_br_mem, w_gate, b_gate, w_out, mix_norm_post,
                 ffn2_norm_pre, ffn2_w_up, ffn2_w_down, ffn2_norm_post, final_norm):
    p = {}
    p["ffn1_wg"], p["ffn1_wu"] = _cast_chunked(ffn1_w_up, 2, FFN_TF)
    p["ffn1_wd"] = _cast_bf16(ffn1_w_down)
    p["ffn2_w_up"], p["ffn2_w_down"] = ffn2_w_up, ffn2_w_down
    p["w_gate"] = _cast_bf16(w_gate)
    p["w_in"] = _cast_bf16(w_in)
    p["b_gate"] = _row(b_gate)
    p["w_mem_kv"] = _cast_bf16(w_mem_kv)
    p["w_pool"] = w_pool.astype(BF16)
    p["pool_scale"] = _row(pool_scale)
    for name, w in (("w_br_attn", w_br_attn), ("w_br_pool", w_br_pool), ("w_br_mem", w_br_mem), ("w_out", w_out)):
        p[name] = _cast_bf16(w)
    for name, v in (("ffn1_norm_pre", ffn1_norm_pre), ("ffn1_norm_post", ffn1_norm_post),
                    ("mix_norm_pre", mix_norm_pre), ("mem_norm", mem_norm), ("mix_norm_post", mix_norm_post),
                    ("ffn2_norm_pre", ffn2_norm_pre), ("ffn2_norm_post", ffn2_norm_post),
                    ("final_norm", final_norm)):
        p[name] = _row(v)
    return p


def kernel(x_prompt, x_sample, mem_prompt, mem_sample, ffn1_norm_pre, ffn1_w_up, ffn1_w_down, ffn1_norm_post,
           mix_norm_pre, mem_norm, w_in, w_mem_kv, w_pool, pool_scale, w_br_attn, w_br_pool, w_br_mem, w_gate, b_gate,
           w_out, mix_norm_post, ffn2_norm_pre, ffn2_w_up, ffn2_w_down, ffn2_norm_post, final_norm):
    layer = [ffn1_norm_pre, ffn1_w_up, ffn1_w_down, ffn1_norm_post, mix_norm_pre, mem_norm, w_in, w_mem_kv, w_pool,
             pool_scale, w_br_attn, w_br_pool, w_br_mem, w_gate, b_gate, w_out, mix_norm_post, ffn2_norm_pre,
             ffn2_w_up, ffn2_w_down, ffn2_norm_post, final_norm]
    depth = ffn1_norm_pre.shape[0]
    slopes = jnp.asarray(_alibi_slopes())
    y_prompt, y_sample = x_prompt, x_sample
    for layer_idx in range(depth):
        p = _prep_params(*[w[layer_idx] for w in layer])
        x2_prompt, (wg2, wu2) = _mixer_half(y_prompt, mem_prompt, p, slopes, ("gate_up", p["ffn2_w_up"]))
        x2_sample, (wd2,) = _mixer_half(y_sample, mem_sample, p, slopes, ("plain", p["ffn2_w_down"]))
        y_prompt, y_sample = (
            _ffn(x2, p["ffn2_norm_pre"], wg2, wu2, wd2, p["ffn2_norm_post"], p["final_norm"],
                 final=True).reshape(x.shape)
            for x2, x in ((x2_prompt, y_prompt), (x2_sample, y_sample)))
    return (y_prompt, y_sample)
```

```python
import functools

import numpy as np
import jax
import jax.numpy as jnp
from jax import lax
from jax.experimental import pallas as pl
from jax.experimental.pallas import tpu as pltpu

F32 = jnp.float32
BF16 = jnp.bfloat16

D_MODEL = 2048
N_MEM = 256
HEAD_DIM = 128
DIL_PAIRS = ((128, 1), (512, 4), (2048, 16))
HEADS_PER_GROUP = 4
N_GROUPS = len(DIL_PAIRS)
N_DIL_HEADS = HEADS_PER_GROUP * N_GROUPS
D_ATTN = N_DIL_HEADS * HEAD_DIM
POOL_WINDOWS = (2, 4, 8, 16)
POOL_GROUP = 256
D_POOL = POOL_GROUP * len(POOL_WINDOWS)
N_XHEADS = 4
XHEAD_DIM = 256
D_XATTN = N_XHEADS * XHEAD_DIM
N_BRANCH = 3
EPS = 1e-6
NEG = -1e30
BAND_RADIUS = 64

V7X_VMEM_BYTES = 64 * 1024 * 1024
MIB = 1024 * 1024

FFN_TM = 1024
FFN_TF = 1024
PROJ_TM = 1024
PROJ_TN = 512
PROJ_TG = 1536
PROJ_ROWS = 256
DEINT_STRIDE = 4
ATT_QB = 128
ATT_KW = 256
ATT_UNROLL = 16
POOL_ROWS = 256
POOL_HALO = 64
POOL_UNROLL = 4
MEM_TM = 512
MERGE_TM = 512
MERGE_ROWS = 512
CAST_BLOCK_BYTES = 8 * MIB


def _vmem_limit(nbytes):
    return int(min(nbytes * 5 // 4 + 8 * MIB, V7X_VMEM_BYTES - 2 * MIB))


def _rms(x, gain):
    ms = jnp.mean(x * x, axis=-1, keepdims=True)
    return x * lax.rsqrt(ms + EPS) * gain


def _alibi_slopes():
    s = 2.0 ** (-8.0 * np.arange(1, N_DIL_HEADS + 1) / N_DIL_HEADS)
    return s.reshape(HEADS_PER_GROUP, N_GROUPS).T.astype(np.float32)


def _ffn_kernel(x_ref, gpre_ref, wg_ref, wu_ref, wd_ref, gpost_ref, gfin_ref, o_hbm, acc_ref, hs_ref, sem,
                *, tf, rem, final):
    i = pl.program_id(0)
    f = pl.program_id(1)
    n_tiles = pl.num_programs(0)
    last = pl.num_programs(1) - 1
    tm = acc_ref.shape[0]

    def writeback(tile):
        return pltpu.make_async_copy(acc_ref, o_hbm.at[pl.ds(pl.multiple_of(tile * tm, tm), tm), :], sem)

    @pl.when(f == 0)
    def _():
        hs_ref[...] = _rms(x_ref[...], gpre_ref[...]).astype(BF16)

    def step(valid, first):
        hs = hs_ref[...]
        g = jnp.dot(hs, wg_ref[:, :valid], preferred_element_type=F32)
        u = jnp.dot(hs, wu_ref[:, :valid], preferred_element_type=F32)
        a = (g * jax.nn.sigmoid(g) * u).astype(BF16)
        if first:
            @pl.when(i > 0)
            def _():
                writeback(i - 1).wait()

            acc_ref[...] = jnp.dot(a, wd_ref[:valid, :], preferred_element_type=F32)
        else:
            acc_ref[...] += jnp.dot(a, wd_ref[:valid, :], preferred_element_type=F32)

    pl.when(f == 0)(functools.partial(step, tf, True))
    pl.when((f > 0) & (f < last))(functools.partial(step, tf, False))
    pl.when(f == last)(functools.partial(step, rem, False))

    @pl.when(f == last)
    def _():
        y = x_ref[...] + 0.5 * _rms(acc_ref[...], gpost_ref[...])
        if final:
            y = _rms(y, gfin_ref[...])
        acc_ref[...] = y
        writeback(i).start()

        @pl.when(i == n_tiles - 1)
        def _():
            writeback(i).wait()


def _ffn(x, gpre, wg, wu, wd, gpost, gfin, *, final):
    n, d = x.shape
    tm, tf = FFN_TM, FFN_TF
    dff = wd.shape[0]
    nf = pl.cdiv(dff, tf)
    rem = dff - (nf - 1) * tf
    assert nf >= 2
    est = 2 * tm * d * 4 + tm * d * 4 + tm * d * 2 + 2 * 3 * d * tf * 2 + 2 * tm * tf * 4
    vec = pl.BlockSpec((1, d), lambda i, f: (0, 0))
    return pl.pallas_call(
        functools.partial(_ffn_kernel, tf=tf, rem=rem, final=final),
        grid=(n // tm, nf),
        in_specs=[
            pl.BlockSpec((tm, d), lambda i, f: (i, 0)),
            vec,
            pl.BlockSpec((None, d, tf), lambda i, f: (f, 0, 0)),
            pl.BlockSpec((None, d, tf), lambda i, f: (f, 0, 0)),
            pl.BlockSpec((tf, d), lambda i, f: (f, 0)),
            vec,
            vec,
        ],
        out_specs=pl.BlockSpec(memory_space=pl.ANY),
        out_shape=jax.ShapeDtypeStruct((n, d), F32),
        scratch_shapes=[pltpu.VMEM((tm, d), F32), pltpu.VMEM((tm, d), BF16), pltpu.SemaphoreType.DMA(())],
        compiler_params=pltpu.CompilerParams(
            dimension_semantics=("arbitrary", "arbitrary"), vmem_limit_bytes=_vmem_limit(est)),
        name="ffn_final" if final else "ffn",
    )(x, gpre, wg, wu, wd, gpost, gfin)


NAT_CHUNKS = (11, 12, 9, 10, 0, 3, 6)
N_NAT_STEPS = len(NAT_CHUNKS)
N_DIL_STEPS = 3
N_GATE_STEPS = N_BRANCH * D_MODEL // PROJ_TG
FIRST_D4 = N_NAT_STEPS
FIRST_D16 = FIRST_D4 + N_DIL_STEPS
FIRST_GATE = FIRST_D16 + N_DIL_STEPS
MIX_STEPS = FIRST_GATE + N_GATE_STEPS


def _lin_chunk(s):
    t = jnp.minimum(s, FIRST_GATE - 1)
    nat = jnp.where(t < 2, 11 + t, jnp.where(t < 4, 7 + t, 3 * (t - 4)))
    d4 = 3 * (t - FIRST_D4) + 1
    d16 = 3 * (t - FIRST_D16) + 2
    return jnp.where(t < FIRST_D4, nat, jnp.where(t < FIRST_D16, d4, d16))


def _mixproj_kernel(x_ref, g_ref, wg_ref, b_ref, wi_ref, og_ref, on_ref, o4_ref, o16_ref, hs_ref, stage_ref,
                    stage2_ref):
    s = pl.program_id(1)
    tm = x_ref.shape[0]
    row_chunks = [pl.ds(r, PROJ_ROWS) for r in range(0, tm, PROJ_ROWS)]

    @pl.when(s == 0)
    def _():
        hs_ref[...] = _rms(x_ref[...], g_ref[...]).astype(BF16)

    @pl.when(s < FIRST_D4)
    def _():
        for rows in row_chunks:
            on_ref[rows, :] = jnp.dot(hs_ref[rows, :], wi_ref[...], preferred_element_type=F32).astype(on_ref.dtype)

    def dilated(o_ref, d):
        per_res = PROJ_ROWS // d
        slabs = range(PROJ_TN // HEAD_DIM)
        for rc, rows in enumerate(row_chunks):
            acc = jnp.dot(hs_ref[rows, :], wi_ref[...], preferred_element_type=F32)
            for c in slabs:
                stage_ref[c] = acc[:, c * HEAD_DIM:(c + 1) * HEAD_DIM]
            if d == DEINT_STRIDE:
                for res in range(d):
                    for c in slabs:
                        o_ref[res, pl.ds(rc * per_res, per_res), c * HEAD_DIM:(c + 1) * HEAD_DIM] = (
                            stage_ref[c, pl.ds(res, per_res, stride=d), :].astype(o_ref.dtype))
            else:
                for c in slabs:
                    for r1 in range(DEINT_STRIDE):
                        stage2_ref[c, r1] = stage_ref[c, pl.ds(r1, PROJ_ROWS // DEINT_STRIDE, stride=DEINT_STRIDE), :]
                for r1 in range(DEINT_STRIDE):
                    for r2 in range(DEINT_STRIDE):
                        res = r1 + DEINT_STRIDE * r2
                        for c in slabs:
                            o_ref[res, pl.ds(rc * per_res, per_res), c * HEAD_DIM:(c + 1) * HEAD_DIM] = (
                                stage2_ref[c, r1, pl.ds(r2, per_res, stride=DEINT_STRIDE), :].astype(o_ref.dtype))

    pl.when((s >= FIRST_D4) & (s < FIRST_D16))(functools.partial(dilated, o4_ref, DIL_PAIRS[1][1]))
    pl.when((s >= FIRST_D16) & (s < FIRST_GATE))(functools.partial(dilated, o16_ref, DIL_PAIRS[2][1]))

    @pl.when(s >= FIRST_GATE)
    def _():
        for rows in row_chunks:
            z = jnp.dot(hs_ref[rows, :], wg_ref[...], preferred_element_type=F32) + b_ref[...]
            og_ref[rows, :] = (0.5 * jnp.tanh(0.5 * z) + 0.5).astype(og_ref.dtype)


def _mixproj(x, gain, w_gate, b_gate, w_in, *, batch, seq):
    n, d = x.shape
    tm, tn, tg = PROJ_TM, PROJ_TN, PROJ_TG
    tiles_per_seq = seq // tm
    d4, d16 = DIL_PAIRS[1][1], DIL_PAIRS[2][1]

    def gate_idx(s):
        return jnp.maximum(s - FIRST_GATE, 0)

    def dil_spec(dil, first):
        return pl.BlockSpec(
            (None, dil, tm // dil, tn),
            lambda i, s: (i // tiles_per_seq, 0, i % tiles_per_seq, jnp.clip(s - first, 0, N_DIL_STEPS - 1)))

    est = (2 * tm * d * 4 + tm * d * 2 + 2 * d * (tg + tn) * 2 + 2 * tm * (tg + 3 * tn) * 2
           + PROJ_ROWS * tn * 4 + 2 * PROJ_ROWS * tg * 4)
    return pl.pallas_call(
        _mixproj_kernel,
        grid=(n // tm, MIX_STEPS),
        in_specs=[
            pl.BlockSpec((tm, d), lambda i, s: (i, 0)),
            pl.BlockSpec((1, d), lambda i, s: (0, 0)),
            pl.BlockSpec((d, tg), lambda i, s: (0, gate_idx(s))),
            pl.BlockSpec((1, tg), lambda i, s: (0, gate_idx(s))),
            pl.BlockSpec((d, tn), lambda i, s: (0, _lin_chunk(s))),
        ],
        out_specs=[
            pl.BlockSpec((tm, tg), lambda i, s: (i, gate_idx(s))),
            pl.BlockSpec((tm, tn), lambda i, s: (i, jnp.minimum(s, N_NAT_STEPS - 1))),
            dil_spec(d4, FIRST_D4),
            dil_spec(d16, FIRST_D16),
        ],
        out_shape=[
            jax.ShapeDtypeStruct((n, N_BRANCH * d), BF16),
            jax.ShapeDtypeStruct((n, N_NAT_STEPS * tn), BF16),
            jax.ShapeDtypeStruct((batch, d4, seq // d4, N_DIL_STEPS * tn), BF16),
            jax.ShapeDtypeStruct((batch, d16, seq // d16, N_DIL_STEPS * tn), BF16),
        ],
        scratch_shapes=[
            pltpu.VMEM((tm, d), BF16),
            pltpu.VMEM((tn // HEAD_DIM, PROJ_ROWS, HEAD_DIM), F32),
            pltpu.VMEM((tn // HEAD_DIM, DEINT_STRIDE, PROJ_ROWS // DEINT_STRIDE, HEAD_DIM), F32),
        ],
        compiler_params=pltpu.CompilerParams(
            dimension_semantics=("arbitrary", "arbitrary"), vmem_limit_bytes=_vmem_limit(est)),
        name="mixproj",
    )(x, gain, w_gate, b_gate, w_in)


def _attn_kernel(*refs, seq, side_cast, n_side_out):
    slopes_ref, q0_ref, k0_ref, v0_ref, q1_ref, k1_ref, v1_ref, q2_ref, k2_ref, v2_ref = refs[:10]
    n_side_in = 0 if side_cast is None else 1
    o_ref = refs[10 + n_side_in]
    m1_ref, m2_ref, l1_ref, l2_ref, n1_ref, n2_ref, va_ref, bias_ref = refs[11 + n_side_in + n_side_out:]
    if side_cast is not None:
        side_cast(refs[10], *refs[11 + n_side_in:11 + n_side_in + n_side_out])
    h = pl.program_id(1)
    scale = HEAD_DIM ** -0.5
    qb = ATT_QB
    qkv_refs = ((q0_ref, k0_ref, v0_ref), (q1_ref, k1_ref, v1_ref), (q2_ref, k2_ref, v2_ref))
    m_refs = (None, m1_ref, m2_ref)
    l_refs = (None, l1_ref, l2_ref)
    n_refs = (None, n1_ref, n2_ref)

    va_ref[:, HEAD_DIM:] = jnp.ones((seq, HEAD_DIM), BF16)

    for gi in reversed(range(N_GROUPS)):
        d = DIL_PAIRS[gi][1]
        q_ref, k_ref, v_ref = qkv_refs[gi]
        sub_len = seq // d
        kw = ATT_KW
        nb = sub_len // qb
        short = sub_len < kw
        slope_d = slopes_ref[gi, h] * float(d)

        row = lax.broadcasted_iota(jnp.int32, (qb, kw), 0)
        col = lax.broadcasted_iota(jnp.int32, (qb, kw), 1)
        for kind, off in enumerate((0, -BAND_RADIUS, qb - kw)):
            arel = jnp.abs(col - row + off)
            bias = jnp.where(arel <= BAND_RADIUS, -slope_d * arel.astype(F32), NEG)
            if short:
                key = col + off
                bias = jnp.where(key < 0, NEG, jnp.where(key < sub_len, bias, NEG))
            bias_ref[kind] = bias

        va_ref[:, :HEAD_DIM] = v_ref[...]

        def block(idx, gi=gi, d=d, q_ref=q_ref, k_ref=k_ref, sub_len=sub_len, kw=kw, nb=nb, short=short):
            qstart = pl.multiple_of(idx * qb, qb)
            if short:
                res, q0, pos, last = idx, 0, idx, d * nb - 1
                kstart = jnp.clip(qstart - BAND_RADIUS, 0, seq - kw)
            else:
                res = idx // nb
                pos, last = idx - res * nb, nb - 1
                q0 = pos * qb
                kstart = res * sub_len + jnp.clip(q0 - BAND_RADIUS, 0, sub_len - kw)
            kstart = pl.multiple_of(kstart, BAND_RADIUS)
            kind = jnp.where(pos == 0, 0, jnp.where(pos == last, 2, 1))
            s = lax.dot_general(q_ref[pl.ds(qstart, qb), :], k_ref[pl.ds(kstart, kw), :],
                                (((1,), (1,)), ((), ())), preferred_element_type=F32)
            s = s * scale + bias_ref[kind]
            m = jnp.max(s, axis=-1, keepdims=True)
            p = jnp.exp(s - m).astype(BF16)
            na = jnp.dot(p, va_ref[pl.ds(kstart, kw), :], preferred_element_type=F32)
            m = jnp.broadcast_to(m, (qb, HEAD_DIM))
            if d > 1:
                rows = pl.ds(q0 * d + res, qb, stride=d)
                m_refs[gi][rows, :] = m
                l_refs[gi][rows, :] = na[:, HEAD_DIM:]
                n_refs[gi][rows, :] = na[:, :HEAD_DIM]
            else:
                rows = pl.ds(qstart, qb)
                ms = [m] + [m_refs[g][rows, :] for g in range(1, N_GROUPS)]
                mx = functools.reduce(jnp.maximum, ms)
                e = jnp.exp(m - mx)
                num = e * na[:, :HEAD_DIM]
                den = e * na[:, HEAD_DIM:]
                for g in range(1, N_GROUPS):
                    e = jnp.exp(ms[g] - mx)
                    num += e * n_refs[g][rows, :]
                    den += e * l_refs[g][rows, :]
                o_ref[rows, :] = (num / den).astype(o_ref.dtype)

        unroll = min(ATT_UNROLL, d * nb // 2)

        def blocks(it, carry, block=block, unroll=unroll):
            for u in range(unroll):
                block(it * unroll + u)
            return carry

        lax.fori_loop(0, d * nb // unroll, blocks, 0)


def _side_cast_specs(kind, w, steps):
    rows, cols = w.shape

    def step(b, h):
        return b * HEADS_PER_GROUP + h

    if kind == "gate_up":
        br, dff = rows // steps, cols // 2
        nchunk = pl.cdiv(dff, FFN_TF)
        in_spec = pl.BlockSpec((br, cols), lambda b, h: (step(b, h), 0))
        out_specs = [pl.BlockSpec((nchunk, br, FFN_TF), lambda b, h: (0, step(b, h), 0)) for _ in range(2)]
        out_shapes = [jax.ShapeDtypeStruct((nchunk, rows, FFN_TF), BF16) for _ in range(2)]
        return functools.partial(_cast_chunked_kernel, width=dff), in_spec, out_specs, out_shapes, br * cols
    bc = cols // steps
    in_spec = pl.BlockSpec((rows, bc), lambda b, h: (0, step(b, h)))
    out_specs = [pl.BlockSpec((rows, bc), lambda b, h: (0, step(b, h)))]
    out_shapes = [jax.ShapeDtypeStruct((rows, cols), BF16)]
    return _cast_kernel, in_spec, out_specs, out_shapes, rows * bc


def _attention(nat, qkv4, qkv16, slopes, *, batch, seq, col0, side=None):
    blk0 = col0 // HEAD_DIM
    qkv4 = qkv4.reshape(batch * seq, qkv4.shape[-1])
    qkv16 = qkv16.reshape(batch * seq, qkv16.shape[-1])

    def nat_map(b, h, *, which):
        return (b, blk0 + which * HEADS_PER_GROUP + h)

    def dil_map(b, h, *, which):
        return (b, which * HEADS_PER_GROUP + h)

    in_specs = [pl.BlockSpec(memory_space=pltpu.SMEM)]
    in_specs += [pl.BlockSpec((seq, HEAD_DIM), functools.partial(nat_map, which=w)) for w in range(3)]
    in_specs += 2 * [pl.BlockSpec((seq, HEAD_DIM), functools.partial(dil_map, which=w)) for w in range(3)]
    tile = seq * HEAD_DIM
    est = 10 * 2 * tile * 2 + 6 * tile * 4 + 2 * tile * 2 + 3 * ATT_QB * ATT_KW * 4
    out_specs = [pl.BlockSpec((seq, HEAD_DIM), lambda b, h: (b, h))]
    out_shapes = [jax.ShapeDtypeStruct((batch * seq, HEADS_PER_GROUP * HEAD_DIM), BF16)]
    operands = [slopes, nat, nat, nat, qkv4, qkv4, qkv4, qkv16, qkv16, qkv16]
    side_cast = None
    if side is not None:
        kind, w = side
        side_cast, side_in, side_outs, side_shapes, block_elems = _side_cast_specs(
            kind, w, batch * HEADS_PER_GROUP)
        in_specs.append(side_in)
        operands.append(w)
        out_specs += side_outs
        out_shapes += side_shapes
        est += 2 * block_elems * (4 + 2)
    state = [pltpu.VMEM((seq, HEAD_DIM), F32) for _ in range(3 * (N_GROUPS - 1))]
    outs = pl.pallas_call(
        functools.partial(_attn_kernel, seq=seq, side_cast=side_cast, n_side_out=len(out_specs) - 1),
        grid=(batch, HEADS_PER_GROUP),
        in_specs=in_specs,
        out_specs=out_specs,
        out_shape=out_shapes,
        scratch_shapes=state + [
            pltpu.VMEM((seq, 2 * HEAD_DIM), BF16),
            pltpu.VMEM((3, ATT_QB, ATT_KW), F32),
        ],
        compiler_params=pltpu.CompilerParams(
            dimension_semantics=("arbitrary", "arbitrary"), vmem_limit_bytes=_vmem_limit(est)),
        name="attention",
    )(*operands)
    return outs[0], tuple(outs[1:])


def _pool_kernel(u_ref, w_ref, sc_ref, o_ref, pad_ref, band_ref, pooled_ref, *, seq):
    grp = pl.program_id(1)
    halo, rows = POOL_HALO, POOL_ROWS
    ext = rows + 2 * halo
    zeros = jnp.zeros((halo, POOL_GROUP), BF16)
    pad_ref[pl.ds(0, halo), :] = zeros
    pad_ref[pl.ds(halo + seq, halo), :] = zeros
    pad_ref[pl.ds(halo, seq), :] = u_ref[...]

    def run(window):
        first = lax.broadcasted_iota(jnp.int32, (rows, ext), 0) + (halo - window // 2)
        col = lax.broadcasted_iota(jnp.int32, (rows, ext), 1)
        band_ref[...] = jnp.where(col < first, 0.0, jnp.where(col < first + window, 1.0, 0.0)).astype(BF16)

        def chunk(c):
            base = pl.multiple_of(c * rows, rows)
            sums = jnp.dot(band_ref[...], pad_ref[pl.ds(base, ext), :], preferred_element_type=F32)
            tok = pad_ref[pl.ds(base + halo, rows), :].astype(F32)
            t = base + lax.broadcasted_iota(jnp.int32, (rows, 1), 0)
            lo = jnp.maximum(t - window // 2, 0)
            hi = jnp.minimum(t - window // 2 + window, seq)
            pooled_ref[pl.ds(base, rows), :] = (sums / (hi - lo).astype(F32) - tok).astype(BF16)

        def chunks(it, carry):
            for u in range(POOL_UNROLL):
                chunk(it * POOL_UNROLL + u)
            return carry

        lax.fori_loop(0, seq // rows // POOL_UNROLL, chunks, 0)

    for gi, window in enumerate(POOL_WINDOWS):
        pl.when(grp == gi)(functools.partial(run, window))

    y = jnp.dot(pooled_ref[...], w_ref[...], preferred_element_type=F32) * sc_ref[...]
    o_ref[...] = y.astype(o_ref.dtype)


def _pool(pg, w_pool, pool_scale, *, batch, seq, col0):
    blk0 = col0 // POOL_GROUP
    ngrp = len(POOL_WINDOWS)
    est = (2 * seq * POOL_GROUP * 2 * 2 + (seq + 2 * POOL_HALO) * POOL_GROUP * 4 + 2 * POOL_GROUP * POOL_GROUP * 2
           + seq * POOL_GROUP * (2 + 4))
    return pl.pallas_call(
        functools.partial(_pool_kernel, seq=seq),
        grid=(batch, ngrp),
        in_specs=[
            pl.BlockSpec((seq, POOL_GROUP), lambda b, g: (b, blk0 + g)),
            pl.BlockSpec((None, POOL_GROUP, POOL_GROUP), lambda b, g: (g, 0, 0)),
            pl.BlockSpec((1, POOL_GROUP), lambda b, g: (0, g)),
        ],
        out_specs=pl.BlockSpec((seq, POOL_GROUP), lambda b, g: (b, g)),
        out_shape=jax.ShapeDtypeStruct((batch * seq, D_POOL), BF16),
        scratch_shapes=[pltpu.VMEM((seq + 2 * POOL_HALO, POOL_GROUP), BF16),
                        pltpu.VMEM((POOL_ROWS, POOL_ROWS + 2 * POOL_HALO), BF16),
                        pltpu.VMEM((seq, POOL_GROUP), BF16)],
        compiler_params=pltpu.CompilerParams(
            dimension_semantics=("arbitrary", "arbitrary"), vmem_limit_bytes=_vmem_limit(est)),
        name="pool",
    )(pg, w_pool, pool_scale)


def _memattn_kernel(q_ref, mem_ref, g_ref, w_ref, o_ref, kv_ref, *, tiles_per_seq):
    @pl.when(pl.program_id(0) % tiles_per_seq == 0)
    def _():
        hm = _rms(mem_ref[...], g_ref[...]).astype(BF16)
        kv_ref[...] = jnp.dot(hm, w_ref[...], preferred_element_type=F32).astype(kv_ref.dtype)

    scale = XHEAD_DIM ** -0.5
    for hh in range(N_XHEADS):
        lo = hh * XHEAD_DIM
        q = q_ref[:, lo:lo + XHEAD_DIM]
        k = kv_ref[:, lo:lo + XHEAD_DIM]
        v = kv_ref[:, D_XATTN + lo:D_XATTN + lo + XHEAD_DIM]
        s = lax.dot_general(q, k, (((1,), (1,)), ((), ())), preferred_element_type=F32) * scale
        m = jnp.max(s, axis=-1, keepdims=True)
        p = jnp.exp(s - m)
        l = jnp.sum(p, axis=-1, keepdims=True)
        y = jnp.dot(p.astype(BF16), v, preferred_element_type=F32) / l
        o_ref[:, lo:lo + XHEAD_DIM] = y.astype(o_ref.dtype)


def _memattn(pg, mem, gain, w_kv, *, seq, col0):
    n = pg.shape[0]
    d = mem.shape[1]
    tm = MEM_TM
    blk0 = col0 // D_XATTN
    tiles_per_seq = seq // tm
    est = (2 * tm * D_XATTN * 2 * 2 + 2 * N_MEM * d * 4 + d * 2 * D_XATTN * 2 + N_MEM * 2 * D_XATTN * (2 + 4)
           + N_MEM * d * 4 + 4 * tm * N_MEM * 4)
    return pl.pallas_call(
        functools.partial(_memattn_kernel, tiles_per_seq=tiles_per_seq),
        grid=(n // tm,),
        in_specs=[
            pl.BlockSpec((tm, D_XATTN), lambda i: (i, blk0)),
            pl.BlockSpec((N_MEM, d), lambda i: (i // tiles_per_seq, 0)),
            pl.BlockSpec((1, d), lambda i: (0, 0)),
            pl.BlockSpec((d, 2 * D_XATTN), lambda i: (0, 0), pipeline_mode=pl.Buffered(1)),
        ],
        out_specs=pl.BlockSpec((tm, D_XATTN), lambda i: (i, 0)),
        out_shape=jax.ShapeDtypeStruct((n, D_XATTN), BF16),
        scratch_shapes=[pltpu.VMEM((N_MEM, 2 * D_XATTN), BF16)],
        compiler_params=pltpu.CompilerParams(
            dimension_semantics=("arbitrary",), vmem_limit_bytes=_vmem_limit(est)),
        name="memattn",
    )(pg, mem, gain, w_kv)


def _merge_kernel(gates_ref, ya_ref, yp_ref, ym_ref, x_ref, wa_ref, wp_ref, wm_ref, wo_ref, gain_ref, o_ref):
    d = x_ref.shape[1]
    for r in range(0, x_ref.shape[0], MERGE_ROWS):
        rows = pl.ds(r, MERGE_ROWS)
        merged = gates_ref[rows, 0:d].astype(F32) * jnp.dot(ya_ref[rows, :], wa_ref[...], preferred_element_type=F32)
        merged += gates_ref[rows, d:2 * d].astype(F32) * jnp.dot(yp_ref[rows, :], wp_ref[...],
                                                                   preferred_element_type=F32)
        merged += gates_ref[rows, 2 * d:3 * d].astype(F32) * jnp.dot(ym_ref[rows, :], wm_ref[...],
                                                                       preferred_element_type=F32)
        z = jnp.dot(merged.astype(BF16), wo_ref[...], preferred_element_type=F32)
        o_ref[rows, :] = x_ref[rows, :] + _rms(z, gain_ref[...])


def _merge(gates, y_attn, y_pool, y_mem, x, wa, wp, wm, wo, gain):
    n, d = x.shape
    tm = MERGE_TM
    wbytes = (wa.size + wp.size + wm.size + wo.size) * 2

    def const(shape):
        return pl.BlockSpec(shape, lambda i: (0, 0), pipeline_mode=pl.Buffered(1))

    est = (2 * tm * (N_BRANCH * d + y_attn.shape[1] + y_pool.shape[1] + y_mem.shape[1]) * 2
           + 4 * tm * d * 4 + wbytes + 4 * tm * d * 4)
    return pl.pallas_call(
        _merge_kernel,
        grid=(n // tm,),
        in_specs=[
            pl.BlockSpec((tm, N_BRANCH * d), lambda i: (i, 0)),
            pl.BlockSpec((tm, y_attn.shape[1]), lambda i: (i, 0)),
            pl.BlockSpec((tm, y_pool.shape[1]), lambda i: (i, 0)),
            pl.BlockSpec((tm, y_mem.shape[1]), lambda i: (i, 0)),
            pl.BlockSpec((tm, d), lambda i: (i, 0)),
            const(wa.shape), const(wp.shape), const(wm.shape), const(wo.shape),
            pl.BlockSpec((1, d), lambda i: (0, 0)),
        ],
        out_specs=pl.BlockSpec((tm, d), lambda i: (i, 0)),
        out_shape=jax.ShapeDtypeStruct((n, d), F32),
        compiler_params=pltpu.CompilerParams(
            dimension_semantics=("arbitrary",), vmem_limit_bytes=_vmem_limit(est)),
        name="merge",
    )(gates, y_attn, y_pool, y_mem, x, wa, wp, wm, wo, gain)


def _cast_kernel(w_ref, *o_refs):
    lo = 0
    for o_ref in o_refs:
        width = o_ref.shape[1]
        o_ref[...] = w_ref[:, lo:lo + width].astype(o_ref.dtype)
        lo += width


def _cast_bf16(w, widths=None, target_bytes=CAST_BLOCK_BYTES):
    rows, cols = w.shape
    widths = (cols,) if widths is None else tuple(widths)
    br = rows
    while br * cols * 4 > target_bytes and br % 32 == 0:
        br //= 2
    est = 2 * br * cols * 4 + 2 * br * cols * 2
    outs = pl.pallas_call(
        _cast_kernel,
        grid=(rows // br,),
        in_specs=[pl.BlockSpec((br, cols), lambda i: (i, 0))],
        out_specs=[pl.BlockSpec((br, wd), lambda i: (i, 0)) for wd in widths],
        out_shape=[jax.ShapeDtypeStruct((rows, wd), BF16) for wd in widths],
        compiler_params=pltpu.CompilerParams(
            dimension_semantics=("arbitrary",), vmem_limit_bytes=_vmem_limit(est)),
        name="cast",
    )(w)
    return outs if len(widths) > 1 else outs[0]


def _cast_chunked_kernel(w_ref, *o_refs, width):
    for k, o_ref in enumerate(o_refs):
        nchunk, _, tf = o_ref.shape
        for c in range(nchunk):
            valid = min(tf, width - c * tf)
            lo = k * width + c * tf
            o_ref[c, :, :valid] = w_ref[:, lo:lo + valid].astype(o_ref.dtype)
            if valid < tf:
                o_ref[c, :, valid:] = jnp.zeros((o_ref.shape[1], tf - valid), o_ref.dtype)


def _cast_chunked(w, n_out, tf, target_bytes=CAST_BLOCK_BYTES):
    rows, cols = w.shape
    width = cols // n_out
    nchunk = pl.cdiv(width, tf)
    br = rows
    while br * cols * 4 > target_bytes and br % 32 == 0:
        br //= 2
    est = 2 * br * cols * 4 + 2 * n_out * nchunk * br * tf * 2
    return pl.pallas_call(
        functools.partial(_cast_chunked_kernel, width=width),
        grid=(rows // br,),
        in_specs=[pl.BlockSpec((br, cols), lambda i: (i, 0))],
        out_specs=[pl.BlockSpec((nchunk, br, tf), lambda i: (0, i, 0)) for _ in range(n_out)],
        out_shape=[jax.ShapeDtypeStruct((nchunk, rows, tf), BF16) for _ in range(n_out)],
        compiler_params=pltpu.CompilerParams(
            dimension_semantics=("arbitrary",), vmem_limit_bytes=_vmem_limit(est)),
        name="cast_chunked",
    )(w)


def _row(v):
    return v.reshape(1, -1).astype(F32)


def _mixer_half(x, mem, p, slopes, side):
    batch, seq, d = x.shape
    xf = x.reshape(batch * seq, d)
    x1 = _ffn(xf, p["ffn1_norm_pre"], p["ffn1_wg"], p["ffn1_wu"], p["ffn1_wd"], p["ffn1_norm_post"],
              p["ffn1_norm_post"], final=False)
    gates, nat, qkv4, qkv16 = _mixproj(x1, p["mix_norm_pre"], p["w_gate"], p["b_gate"], p["w_in"],
                                       batch=batch, seq=seq)
    col_pool = D_XATTN
    col_qkv0 = col_pool + D_POOL
    y_attn, cast = _attention(nat, qkv4, qkv16, slopes, batch=batch, seq=seq, col0=col_qkv0, side=side)
    y_pool = _pool(nat, p["w_pool"], p["pool_scale"], batch=batch, seq=seq, col0=col_pool)
    y_mem = _memattn(nat, mem.reshape(batch * N_MEM, d), p["mem_norm"], p["w_mem_kv"], seq=seq, col0=0)
    x2 = _merge(gates, y_attn, y_pool, y_mem, x1, p["w_br_attn"], p["w_br_pool"], p["w_br_mem"], p["w_out"],
                p["mix_norm_post"])
    return x2, cast


def _prep_params(ffn1_norm_pre, ffn1_w_up, ffn1_w_down, ffn1_norm_post, mix_norm_pre, mem_norm, w_in, w_mem_kv,
                 w_pool, pool_scale, w_br_attn, w_br_pool, w_br_mem, w_gate, b_gate, w_out, mix_norm_post,
                 ffn2_norm_pre, ffn2_w_up, ffn2_w_down, ffn2_norm_post, final_norm):
    p = {}
    p["ffn1_wg"], p["ffn1_wu"] = _cast_chunked(ffn1_w_up, 2, FFN_TF)
    p["ffn1_wd"] = _cast_bf16(ffn1_w_down)
    p["ffn2_w_up"], p["ffn2_w_down"] = ffn2_w_up, ffn2_w_down
    p["w_gate"] = _cast_bf16(w_gate)
    p["w_in"] = _cast_bf16(w_in)
    p["b_gate"] = _row(b_gate)
    p["w_mem_kv"] = _cast_bf16(w_mem_kv)
    p["w_pool"] = w_pool.astype(BF16)
    p["pool_scale"] = _row(pool_scale)
    for name, w in (("w_br_attn", w_br_attn), ("w_br_pool", w_br_pool), ("w_br_mem", w_br_mem), ("w_out", w_out)):
        p[name] = _cast_bf16(w)
    for name, v in (("ffn1_norm_pre", ffn1_norm_pre), ("ffn1_norm_post", ffn1_norm_post),
                    ("mix_norm_pre", mix_norm_pre), ("mem_norm", mem_norm), ("mix_norm_post", mix_norm_post),
                    ("ffn2_norm_pre", ffn2_norm_pre), ("ffn2_norm_post", ffn2_norm_post),
                    ("final_norm", final_norm)):
        p[name] = _row(v)
    return p


def kernel(x_prompt, x_sample, mem_prompt, mem_sample, ffn1_norm_pre, ffn1_w_up, ffn1_w_down, ffn1_norm_post,
           mix_norm_pre, mem_norm, w_in, w_mem_kv, w_pool, pool_scale, w_br_attn, w_br_pool, w_br_mem, w_gate, b_gate,
           w_out, mix_norm_post, ffn2_norm_pre, ffn2_w_up, ffn2_w_down, ffn2_norm_post, final_norm):
    layer = [ffn1_norm_pre, ffn1_w_up, ffn1_w_down, ffn1_norm_post, mix_norm_pre, mem_norm, w_in, w_mem_kv, w_pool,
             pool_scale, w_br_attn, w_br_pool, w_br_mem, w_gate, b_gate, w_out, mix_norm_post, ffn2_norm_pre,
             ffn2_w_up, ffn2_w_down, ffn2_norm_post, final_norm]
    depth = ffn1_norm_pre.shape[0]
    slopes = jnp.asarray(_alibi_slopes())
    y_prompt, y_sample = x_prompt, x_sample
    for layer_idx in range(depth):
        p = _prep_params(*[w[layer_idx] for w in layer])
        x2_prompt, (wg2, wu2) = _mixer_half(y_prompt, mem_prompt, p, slopes, ("gate_up", p["ffn2_w_up"]))
        x2_sample, (wd2,) = _mixer_half(y_sample, mem_sample, p, slopes, ("plain", p["ffn2_w_down"]))
        y_prompt, y_sample = (
            _ffn(x2, p["ffn2_norm_pre"], wg2, wu2, wd2, p["ffn2_norm_post"], p["final_norm"],
                 final=True).reshape(x.shape)
            for x2, x in ((x2_prompt, y_prompt), (x2_sample, y_sample)))
    return (y_prompt, y_sample)
```

```python
import functools

import numpy as np
import jax
import jax.numpy as jnp
from jax import lax
from jax.experimental import pallas as pl
from jax.experimental.pallas import tpu as pltpu

F32 = jnp.float32
BF16 = jnp.bfloat16

D_MODEL = 2048
N_MEM = 256
HEAD_DIM = 128
DIL_PAIRS = ((128, 1), (512, 4), (2048, 16))
HEADS_PER_GROUP = 4
N_GROUPS = len(DIL_PAIRS)
N_DIL_HEADS = HEADS_PER_GROUP * N_GROUPS
D_ATTN = N_DIL_HEADS * HEAD_DIM
POOL_WINDOWS = (2, 4, 8, 16)
POOL_GROUP = 256
D_POOL = POOL_GROUP * len(POOL_WINDOWS)
N_XHEADS = 4
XHEAD_DIM = 256
D_XATTN = N_XHEADS * XHEAD_DIM
N_BRANCH = 3
EPS = 1e-6
NEG = -1e30
BAND_RADIUS = 64

V7X_VMEM_BYTES = 64 * 1024 * 1024
MIB = 1024 * 1024

FFN_TM = 1024
FFN_TF = 1024
PROJ_TM = 1024
PROJ_TN = 512
PROJ_TG = 2048
PROJ_ROWS = 256
DEINT_STRIDE = 4
ATT_QB = 128
ATT_KW = 256
ATT_UNROLL = 16
POOL_ROWS = 256
POOL_HALO = 64
POOL_UNROLL = 4
MEM_TM = 512
MERGE_TM = 512
MERGE_ROWS = 512
CAST_BLOCK_BYTES = 8 * MIB


def _vmem_limit(nbytes):
    return int(min(nbytes * 5 // 4 + 8 * MIB, V7X_VMEM_BYTES - 2 * MIB))


def _rms(x, gain):
    ms = jnp.mean(x * x, axis=-1, keepdims=True)
    return x * lax.rsqrt(ms + EPS) * gain


def _alibi_slopes():
    s = 2.0 ** (-8.0 * np.arange(1, N_DIL_HEADS + 1) / N_DIL_HEADS)
    return s.reshape(HEADS_PER_GROUP, N_GROUPS).T.astype(np.float32)


def _ffn_kernel(x_ref, gpre_ref, wg_ref, wu_ref, wd_ref, gpost_ref, gfin_ref, o_hbm, acc_ref, hs_ref, sem,
                *, tf, rem, final):
    i = pl.program_id(0)
    f = pl.program_id(1)
    n_tiles = pl.num_programs(0)
    last = pl.num_programs(1) - 1
    tm = acc_ref.shape[0]

    def writeback(tile):
        return pltpu.make_async_copy(acc_ref, o_hbm.at[pl.ds(pl.multiple_of(tile * tm, tm), tm), :], sem)

    @pl.when(f == 0)
    def _():
        hs_ref[...] = _rms(x_ref[...], gpre_ref[...]).astype(BF16)

    def step(valid, first):
        hs = hs_ref[...]
        g = jnp.dot(hs, wg_ref[:, :valid], preferred_element_type=F32)
        u = jnp.dot(hs, wu_ref[:, :valid], preferred_element_type=F32)
        a = (g * jax.nn.sigmoid(g) * u).astype(BF16)
        if first:
            @pl.when(i > 0)
            def _():
                writeback(i - 1).wait()

            acc_ref[...] = jnp.dot(a, wd_ref[:valid, :], preferred_element_type=F32)
        else:
            acc_ref[...] += jnp.dot(a, wd_ref[:valid, :], preferred_element_type=F32)

    pl.when(f == 0)(functools.partial(step, tf, True))
    pl.when((f > 0) & (f < last))(functools.partial(step, tf, False))
    pl.when(f == last)(functools.partial(step, rem, False))

    @pl.when(f == last)
    def _():
        y = x_ref[...] + 0.5 * _rms(acc_ref[...], gpost_ref[...])
        if final:
            y = _rms(y, gfin_ref[...])
        acc_ref[...] = y
        writeback(i).start()

        @pl.when(i == n_tiles - 1)
        def _():
            writeback(i).wait()


def _ffn(x, gpre, wg, wu, wd, gpost, gfin, *, final):
    n, d = x.shape
    tm, tf = FFN_TM, FFN_TF
    dff = wd.shape[0]
    nf = pl.cdiv(dff, tf)
    rem = dff - (nf - 1) * tf
    assert nf >= 2
    est = 2 * tm * d * 4 + tm * d * 4 + tm * d * 2 + 2 * 3 * d * tf * 2 + 2 * tm * tf * 4
    vec = pl.BlockSpec((1, d), lambda i, f: (0, 0))
    return pl.pallas_call(
        functools.partial(_ffn_kernel, tf=tf, rem=rem, final=final),
        grid=(n // tm, nf),
        in_specs=[
            pl.BlockSpec((tm, d), lambda i, f: (i, 0)),
            vec,
            pl.BlockSpec((None, d, tf), lambda i, f: (f, 0, 0)),
            pl.BlockSpec((None, d, tf), lambda i, f: (f, 0, 0)),
            pl.BlockSpec((tf, d), lambda i, f: (f, 0)),
            vec,
            vec,
        ],
        out_specs=pl.BlockSpec(memory_space=pl.ANY),
        out_shape=jax.ShapeDtypeStruct((n, d), F32),
        scratch_shapes=[pltpu.VMEM((tm, d), F32), pltpu.VMEM((tm, d), BF16), pltpu.SemaphoreType.DMA(())],
        compiler_params=pltpu.CompilerParams(
            dimension_semantics=("arbitrary", "arbitrary"), vmem_limit_bytes=_vmem_limit(est)),
        name="ffn_final" if final else "ffn",
    )(x, gpre, wg, wu, wd, gpost, gfin)


def _kvproj_kernel(x_ref, g_ref, w_ref, o_ref, hs_ref):
    @pl.when(pl.program_id(1) == 0)
    def _():
        hs_ref[...] = _rms(x_ref[...], g_ref[...]).astype(BF16)

    o_ref[...] = jnp.dot(hs_ref[...], w_ref[...], preferred_element_type=F32).astype(o_ref.dtype)


def _kvproj(x, gain, w):
    n, d = x.shape
    ncol = w.shape[1]
    tm, tn = n, PROJ_TN
    est = 2 * tm * d * 4 + tm * d * 2 + 2 * d * tn * 2 + 2 * tm * tn * 2 + 2 * tm * tn * 4
    return pl.pallas_call(
        _kvproj_kernel,
        grid=(n // tm, ncol // tn),
        in_specs=[
            pl.BlockSpec((tm, d), lambda i, j: (i, 0)),
            pl.BlockSpec((1, d), lambda i, j: (0, 0)),
            pl.BlockSpec((d, tn), lambda i, j: (0, j)),
        ],
        out_specs=pl.BlockSpec((tm, tn), lambda i, j: (i, j)),
        out_shape=jax.ShapeDtypeStruct((n, ncol), BF16),
        scratch_shapes=[pltpu.VMEM((tm, d), BF16)],
        compiler_params=pltpu.CompilerParams(
            dimension_semantics=("arbitrary", "arbitrary"), vmem_limit_bytes=_vmem_limit(est)),
        name="kvproj",
    )(x, gain, w)


NAT_CHUNKS = (11, 12, 9, 10, 0, 3, 6)
N_NAT_STEPS = len(NAT_CHUNKS)
N_DIL_STEPS = 3
N_GATE_STEPS = N_BRANCH * D_MODEL // PROJ_TG
FIRST_D4 = N_NAT_STEPS
FIRST_D16 = FIRST_D4 + N_DIL_STEPS
FIRST_GATE = FIRST_D16 + N_DIL_STEPS
MIX_STEPS = FIRST_GATE + N_GATE_STEPS


def _lin_chunk(s):
    t = jnp.minimum(s, FIRST_GATE - 1)
    nat = jnp.where(t < 2, 11 + t, jnp.where(t < 4, 7 + t, 3 * (t - 4)))
    d4 = 3 * (t - FIRST_D4) + 1
    d16 = 3 * (t - FIRST_D16) + 2
    return jnp.where(t < FIRST_D4, nat, jnp.where(t < FIRST_D16, d4, d16))


def _mixproj_kernel(x_ref, g_ref, wg_ref, b_ref, wi_ref, og_ref, on_ref, o4_ref, o16_ref, hs_ref, stage_ref,
                    stage2_ref):
    s = pl.program_id(1)
    tm = x_ref.shape[0]
    row_chunks = [pl.ds(r, PROJ_ROWS) for r in range(0, tm, PROJ_ROWS)]

    @pl.when(s == 0)
    def _():
        hs_ref[...] = _rms(x_ref[...], g_ref[...]).astype(BF16)

    @pl.when(s < FIRST_D4)
    def _():
        for rows in row_chunks:
            on_ref[rows, :] = jnp.dot(hs_ref[rows, :], wi_ref[...], preferred_element_type=F32).astype(on_ref.dtype)

    def dilated(o_ref, d):
        per_res = PROJ_ROWS // d
        slabs = range(PROJ_TN // HEAD_DIM)
        for rc, rows in enumerate(row_chunks):
            acc = jnp.dot(hs_ref[rows, :], wi_ref[...], preferred_element_type=F32)
            for c in slabs:
                stage_ref[c] = acc[:, c * HEAD_DIM:(c + 1) * HEAD_DIM]
            if d == DEINT_STRIDE:
                for res in range(d):
                    for c in slabs:
                        o_ref[res, pl.ds(rc * per_res, per_res), c * HEAD_DIM:(c + 1) * HEAD_DIM] = (
                            stage_ref[c, pl.ds(res, per_res, stride=d), :].astype(o_ref.dtype))
            else:
                for c in slabs:
                    for r1 in range(DEINT_STRIDE):
                        stage2_ref[c, r1] = stage_ref[c, pl.ds(r1, PROJ_ROWS // DEINT_STRIDE, stride=DEINT_STRIDE), :]
                for r1 in range(DEINT_STRIDE):
                    for r2 in range(DEINT_STRIDE):
                        res = r1 + DEINT_STRIDE * r2
                        for c in slabs:
                            o_ref[res, pl.ds(rc * per_res, per_res), c * HEAD_DIM:(c + 1) * HEAD_DIM] = (
                                stage2_ref[c, r1, pl.ds(r2, per_res, stride=DEINT_STRIDE), :].astype(o_ref.dtype))

    pl.when((s >= FIRST_D4) & (s < FIRST_D16))(functools.partial(dilated, o4_ref, DIL_PAIRS[1][1]))
    pl.when((s >= FIRST_D16) & (s < FIRST_GATE))(functools.partial(dilated, o16_ref, DIL_PAIRS[2][1]))

    @pl.when(s >= FIRST_GATE)
    def _():
        for rows in row_chunks:
            z = jnp.dot(hs_ref[rows, :], wg_ref[...], preferred_element_type=F32) + b_ref[...]
            og_ref[rows, :] = (0.5 * jnp.tanh(0.5 * z) + 0.5).astype(og_ref.dtype)


def _mixproj(x, gain, w_gate, b_gate, w_in, *, batch, seq):
    n, d = x.shape
    tm, tn, tg = PROJ_TM, PROJ_TN, PROJ_TG
    tiles_per_seq = seq // tm
    d4, d16 = DIL_PAIRS[1][1], DIL_PAIRS[2][1]

    def gate_idx(s):
        return jnp.maximum(s - FIRST_GATE, 0)

    def dil_spec(dil, first):
        return pl.BlockSpec(
            (None, dil, tm // dil, tn),
            lambda i, s: (i // tiles_per_seq, 0, i % tiles_per_seq, jnp.clip(s - first, 0, N_DIL_STEPS - 1)))

    est = (2 * tm * d * 4 + tm * d * 2 + 2 * d * (tg + tn) * 2 + 2 * tm * (tg + 3 * tn) * 2
           + PROJ_ROWS * tn * 4 + 2 * PROJ_ROWS * tg * 4)
    return pl.pallas_call(
        _mixproj_kernel,
        grid=(n // tm, MIX_STEPS),
        in_specs=[
            pl.BlockSpec((tm, d), lambda i, s: (i, 0)),
            pl.BlockSpec((1, d), lambda i, s: (0, 0)),
            pl.BlockSpec((d, tg), lambda i, s: (0, gate_idx(s))),
            pl.BlockSpec((1, tg), lambda i, s: (0, gate_idx(s))),
            pl.BlockSpec((d, tn), lambda i, s: (0, _lin_chunk(s))),
        ],
        out_specs=[
            pl.BlockSpec((tm, tg), lambda i, s: (i, gate_idx(s))),
            pl.BlockSpec((tm, tn), lambda i, s: (i, jnp.minimum(s, N_NAT_STEPS - 1))),
            dil_spec(d4, FIRST_D4),
            dil_spec(d16, FIRST_D16),
        ],
        out_shape=[
            jax.ShapeDtypeStruct((n, N_BRANCH * d), BF16),
            jax.ShapeDtypeStruct((n, N_NAT_STEPS * tn), BF16),
            jax.ShapeDtypeStruct((batch, d4, seq // d4, N_DIL_STEPS * tn), BF16),
            jax.ShapeDtypeStruct((batch, d16, seq // d16, N_DIL_STEPS * tn), BF16),
        ],
        scratch_shapes=[
            pltpu.VMEM((tm, d), BF16),
            pltpu.VMEM((tn // HEAD_DIM, PROJ_ROWS, HEAD_DIM), F32),
            pltpu.VMEM((tn // HEAD_DIM, DEINT_STRIDE, PROJ_ROWS // DEINT_STRIDE, HEAD_DIM), F32),
        ],
        compiler_params=pltpu.CompilerParams(
            dimension_semantics=("arbitrary", "arbitrary"), vmem_limit_bytes=_vmem_limit(est)),
        name="mixproj",
    )(x, gain, w_gate, b_gate, w_in)


def _attn_kernel(*refs, seq, side_casts):
    slopes_ref, q0_ref, k0_ref, v0_ref, q1_ref, k1_ref, v1_ref, q2_ref, k2_ref, v2_ref = refs[:10]
    n_in = 10 + len(side_casts)
    o_ref = refs[n_in]
    lo = n_in + 1
    for k, (cast, n_out) in enumerate(side_casts):
        cast(refs[10 + k], *refs[lo:lo + n_out])
        lo += n_out
    m1_ref, m2_ref, l1_ref, l2_ref, n1_ref, n2_ref, va_ref, bias_ref = refs[lo:]
    h = pl.program_id(1)
    scale = HEAD_DIM ** -0.5
    qb = ATT_QB
    qkv_refs = ((q0_ref, k0_ref, v0_ref), (q1_ref, k1_ref, v1_ref), (q2_ref, k2_ref, v2_ref))
    m_refs = (None, m1_ref, m2_ref)
    l_refs = (None, l1_ref, l2_ref)
    n_refs = (None, n1_ref, n2_ref)

    va_ref[:, HEAD_DIM:] = jnp.ones((seq, HEAD_DIM), BF16)

    for gi in reversed(range(N_GROUPS)):
        d = DIL_PAIRS[gi][1]
        q_ref, k_ref, v_ref = qkv_refs[gi]
        sub_len = seq // d
        kw = ATT_KW
        nb = sub_len // qb
        short = sub_len < kw
        slope_d = slopes_ref[gi, h] * float(d)

        row = lax.broadcasted_iota(jnp.int32, (qb, kw), 0)
        col = lax.broadcasted_iota(jnp.int32, (qb, kw), 1)
        for kind, off in enumerate((0, -BAND_RADIUS, qb - kw)):
            arel = jnp.abs(col - row + off)
            bias = jnp.where(arel <= BAND_RADIUS, -slope_d * arel.astype(F32), NEG)
            if short:
                key = col + off
                bias = jnp.where(key < 0, NEG, jnp.where(key < sub_len, bias, NEG))
            bias_ref[kind] = bias

        va_ref[:, :HEAD_DIM] = v_ref[...]

        def block(idx, gi=gi, d=d, q_ref=q_ref, k_ref=k_ref, sub_len=sub_len, kw=kw, nb=nb, short=short):
            qstart = pl.multiple_of(idx * qb, qb)
            if short:
                res, q0, pos, last = idx, 0, idx, d * nb - 1
                kstart = jnp.clip(qstart - BAND_RADIUS, 0, seq - kw)
            else:
                res = idx // nb
                pos, last = idx - res * nb, nb - 1
                q0 = pos * qb
                kstart = res * sub_len + jnp.clip(q0 - BAND_RADIUS, 0, sub_len - kw)
            kstart = pl.multiple_of(kstart, BAND_RADIUS)
            kind = jnp.where(pos == 0, 0, jnp.where(pos == last, 2, 1))
            s = lax.dot_general(q_ref[pl.ds(qstart, qb), :], k_ref[pl.ds(kstart, kw), :],
                                (((1,), (1,)), ((), ())), preferred_element_type=F32)
            s = s * scale + bias_ref[kind]
            m = jnp.max(s, axis=-1, keepdims=True)
            p = jnp.exp(s - m).astype(BF16)
            na = jnp.dot(p, va_ref[pl.ds(kstart, kw), :], preferred_element_type=F32)
            m = jnp.broadcast_to(m, (qb, HEAD_DIM))
            if d > 1:
                rows = pl.ds(q0 * d + res, qb, stride=d)
                m_refs[gi][rows, :] = m
                l_refs[gi][rows, :] = na[:, HEAD_DIM:]
                n_refs[gi][rows, :] = na[:, :HEAD_DIM]
            else:
                rows = pl.ds(qstart, qb)
                ms = [m] + [m_refs[g][rows, :] for g in range(1, N_GROUPS)]
                mx = functools.reduce(jnp.maximum, ms)
                e = jnp.exp(m - mx)
                num = e * na[:, :HEAD_DIM]
                den = e * na[:, HEAD_DIM:]
                for g in range(1, N_GROUPS):
                    e = jnp.exp(ms[g] - mx)
                    num += e * n_refs[g][rows, :]
                    den += e * l_refs[g][rows, :]
                o_ref[rows, :] = (num / den).astype(o_ref.dtype)

        unroll = min(ATT_UNROLL, d * nb // 2)

        def blocks(it, carry, block=block, unroll=unroll):
            for u in range(unroll):
                block(it * unroll + u)
            return carry

        lax.fori_loop(0, d * nb // unroll, blocks, 0)


def _side_cast_specs(kind, w, steps):
    rows, cols = w.shape

    def step(b, h):
        return b * HEADS_PER_GROUP + h

    if kind == "gate_up":
        br, dff = rows // steps, cols // 2
        nchunk = pl.cdiv(dff, FFN_TF)
        in_spec = pl.BlockSpec((br, cols), lambda b, h: (step(b, h), 0))
        out_specs = [pl.BlockSpec((nchunk, br, FFN_TF), lambda b, h: (0, step(b, h), 0)) for _ in range(2)]
        out_shapes = [jax.ShapeDtypeStruct((nchunk, rows, FFN_TF), BF16) for _ in range(2)]
        return functools.partial(_cast_chunked_kernel, width=dff), in_spec, out_specs, out_shapes, br * cols
    bc = cols // steps
    in_spec = pl.BlockSpec((rows, bc), lambda b, h: (0, step(b, h)))
    out_specs = [pl.BlockSpec((rows, bc), lambda b, h: (0, step(b, h)))]
    out_shapes = [jax.ShapeDtypeStruct((rows, cols), BF16)]
    return _cast_kernel, in_spec, out_specs, out_shapes, rows * bc


def _attention(nat, qkv4, qkv16, slopes, *, batch, seq, col0, sides=()):
    blk0 = col0 // HEAD_DIM
    qkv4 = qkv4.reshape(batch * seq, qkv4.shape[-1])
    qkv16 = qkv16.reshape(batch * seq, qkv16.shape[-1])

    def nat_map(b, h, *, which):
        return (b, blk0 + which * HEADS_PER_GROUP + h)

    def dil_map(b, h, *, which):
        return (b, which * HEADS_PER_GROUP + h)

    in_specs = [pl.BlockSpec(memory_space=pltpu.SMEM)]
    in_specs += [pl.BlockSpec((seq, HEAD_DIM), functools.partial(nat_map, which=w)) for w in range(3)]
    in_specs += 2 * [pl.BlockSpec((seq, HEAD_DIM), functools.partial(dil_map, which=w)) for w in range(3)]
    tile = seq * HEAD_DIM
    est = 10 * 2 * tile * 2 + 6 * tile * 4 + 2 * tile * 2 + 3 * ATT_QB * ATT_KW * 4
    out_specs = [pl.BlockSpec((seq, HEAD_DIM), lambda b, h: (b, h))]
    out_shapes = [jax.ShapeDtypeStruct((batch * seq, HEADS_PER_GROUP * HEAD_DIM), BF16)]
    operands = [slopes, nat, nat, nat, qkv4, qkv4, qkv4, qkv16, qkv16, qkv16]
    side_casts = []
    for kind, w in sides:
        cast, side_in, side_outs, side_shapes, block_elems = _side_cast_specs(kind, w, batch * HEADS_PER_GROUP)
        side_casts.append((cast, len(side_outs)))
        in_specs.append(side_in)
        operands.append(w)
        out_specs += side_outs
        out_shapes += side_shapes
        est += 2 * block_elems * (4 + 2)
    state = [pltpu.VMEM((seq, HEAD_DIM), F32) for _ in range(3 * (N_GROUPS - 1))]
    outs = pl.pallas_call(
        functools.partial(_attn_kernel, seq=seq, side_casts=tuple(side_casts)),
        grid=(batch, HEADS_PER_GROUP),
        in_specs=in_specs,
        out_specs=out_specs,
        out_shape=out_shapes,
        scratch_shapes=state + [
            pltpu.VMEM((seq, 2 * HEAD_DIM), BF16),
            pltpu.VMEM((3, ATT_QB, ATT_KW), F32),
        ],
        compiler_params=pltpu.CompilerParams(
            dimension_semantics=("arbitrary", "arbitrary"), vmem_limit_bytes=_vmem_limit(est)),
        name="attention",
    )(*operands)
    return outs[0], tuple(outs[1:])


def _pool_kernel(u_ref, w_ref, sc_ref, o_ref, pad_ref, band_ref, pooled_ref, *, seq):
    grp = pl.program_id(1)
    halo, rows = POOL_HALO, POOL_ROWS
    ext = rows + 2 * halo
    zeros = jnp.zeros((halo, POOL_GROUP), BF16)
    pad_ref[pl.ds(0, halo), :] = zeros
    pad_ref[pl.ds(halo + seq, halo), :] = zeros
    pad_ref[pl.ds(halo, seq), :] = u_ref[...]

    def run(window):
        first = lax.broadcasted_iota(jnp.int32, (rows, ext), 0) + (halo - window // 2)
        col = lax.broadcasted_iota(jnp.int32, (rows, ext), 1)
        band_ref[...] = jnp.where(col < first, 0.0, jnp.where(col < first + window, 1.0, 0.0)).astype(BF16)

        def chunk(c):
            base = pl.multiple_of(c * rows, rows)
            sums = jnp.dot(band_ref[...], pad_ref[pl.ds(base, ext), :], preferred_element_type=F32)
            tok = pad_ref[pl.ds(base + halo, rows), :].astype(F32)
            t = base + lax.broadcasted_iota(jnp.int32, (rows, 1), 0)
            lo = jnp.maximum(t - window // 2, 0)
            hi = jnp.minimum(t - window // 2 + window, seq)
            pooled_ref[pl.ds(base, rows), :] = (sums / (hi - lo).astype(F32) - tok).astype(BF16)

        def chunks(it, carry):
            for u in range(POOL_UNROLL):
                chunk(it * POOL_UNROLL + u)
            return carry

        lax.fori_loop(0, seq // rows // POOL_UNROLL, chunks, 0)

    for gi, window in enumerate(POOL_WINDOWS):
        pl.when(grp == gi)(functools.partial(run, window))

    y = jnp.dot(pooled_ref[...], w_ref[...], preferred_element_type=F32) * sc_ref[...]
    o_ref[...] = y.astype(o_ref.dtype)


def _pool(pg, w_pool, pool_scale, *, batch, seq, col0):
    blk0 = col0 // POOL_GROUP
    ngrp = len(POOL_WINDOWS)
    est = (2 * seq * POOL_GROUP * 2 * 2 + (seq + 2 * POOL_HALO) * POOL_GROUP * 4 + 2 * POOL_GROUP * POOL_GROUP * 2
           + seq * POOL_GROUP * (2 + 4))
    return pl.pallas_call(
        functools.partial(_pool_kernel, seq=seq),
        grid=(batch, ngrp),
        in_specs=[
            pl.BlockSpec((seq, POOL_GROUP), lambda b, g: (b, blk0 + g)),
            pl.BlockSpec((None, POOL_GROUP, POOL_GROUP), lambda b, g: (g, 0, 0)),
            pl.BlockSpec((1, POOL_GROUP), lambda b, g: (0, g)),
        ],
        out_specs=pl.BlockSpec((seq, POOL_GROUP), lambda b, g: (b, g)),
        out_shape=jax.ShapeDtypeStruct((batch * seq, D_POOL), BF16),
        scratch_shapes=[pltpu.VMEM((seq + 2 * POOL_HALO, POOL_GROUP), BF16),
                        pltpu.VMEM((POOL_ROWS, POOL_ROWS + 2 * POOL_HALO), BF16),
                        pltpu.VMEM((seq, POOL_GROUP), BF16)],
        compiler_params=pltpu.CompilerParams(
            dimension_semantics=("arbitrary", "arbitrary"), vmem_limit_bytes=_vmem_limit(est)),
        name="pool",
    )(pg, w_pool, pool_scale)


def _memattn_kernel(q_ref, kv_ref, o_ref):
    scale = XHEAD_DIM ** -0.5
    for hh in range(N_XHEADS):
        lo = hh * XHEAD_DIM
        q = q_ref[:, lo:lo + XHEAD_DIM]
        k = kv_ref[:, lo:lo + XHEAD_DIM]
        v = kv_ref[:, D_XATTN + lo:D_XATTN + lo + XHEAD_DIM]
        s = lax.dot_general(q, k, (((1,), (1,)), ((), ())), preferred_element_type=F32) * scale
        m = jnp.max(s, axis=-1, keepdims=True)
        p = jnp.exp(s - m)
        l = jnp.sum(p, axis=-1, keepdims=True)
        y = jnp.dot(p.astype(BF16), v, preferred_element_type=F32) / l
        o_ref[:, lo:lo + XHEAD_DIM] = y.astype(o_ref.dtype)


def _memattn(pg, kv, *, seq, col0):
    n = pg.shape[0]
    tm = MEM_TM
    blk0 = col0 // D_XATTN
    tiles_per_seq = seq // tm
    est = 2 * tm * D_XATTN * 2 * 2 + 2 * N_MEM * 2 * D_XATTN * 2 + 4 * tm * N_MEM * 4
    return pl.pallas_call(
        _memattn_kernel,
        grid=(n // tm,),
        in_specs=[
            pl.BlockSpec((tm, D_XATTN), lambda i: (i, blk0)),
            pl.BlockSpec((N_MEM, 2 * D_XATTN), lambda i: (i // tiles_per_seq, 0)),
        ],
        out_specs=pl.BlockSpec((tm, D_XATTN), lambda i: (i, 0)),
        out_shape=jax.ShapeDtypeStruct((n, D_XATTN), BF16),
        compiler_params=pltpu.CompilerParams(
            dimension_semantics=("arbitrary",), vmem_limit_bytes=_vmem_limit(est)),
        name="memattn",
    )(pg, kv)


def _merge_kernel(gates_ref, ya_ref, yp_ref, ym_ref, x_ref, wa_ref, wp_ref, wm_ref, wo_ref, gain_ref, o_ref):
    d = x_ref.shape[1]
    for r in range(0, x_ref.shape[0], MERGE_ROWS):
        rows = pl.ds(r, MERGE_ROWS)
        merged = gates_ref[rows, 0:d].astype(F32) * jnp.dot(ya_ref[rows, :], wa_ref[...], preferred_element_type=F32)
        merged += gates_ref[rows, d:2 * d].astype(F32) * jnp.dot(yp_ref[rows, :], wp_ref[...],
                                                                   preferred_element_type=F32)
        merged += gates_ref[rows, 2 * d:3 * d].astype(F32) * jnp.dot(ym_ref[rows, :], wm_ref[...],
                                                                       preferred_element_type=F32)
        z = jnp.dot(merged.astype(BF16), wo_ref[...], preferred_element_type=F32)
        o_ref[rows, :] = x_ref[rows, :] + _rms(z, gain_ref[...])


def _merge(gates, y_attn, y_pool, y_mem, x, wa, wp, wm, wo, gain):
    n, d = x.shape
    tm = MERGE_TM
    wbytes = (wa.size + wp.size + wm.size + wo.size) * 2

    def const(shape):
        return pl.BlockSpec(shape, lambda i: (0, 0), pipeline_mode=pl.Buffered(1))

    est = (2 * tm * (N_BRANCH * d + y_attn.shape[1] + y_pool.shape[1] + y_mem.shape[1]) * 2
           + 4 * tm * d * 4 + wbytes + 4 * tm * d * 4)
    return pl.pallas_call(
        _merge_kernel,
        grid=(n // tm,),
        in_specs=[
            pl.BlockSpec((tm, N_BRANCH * d), lambda i: (i, 0)),
            pl.BlockSpec((tm, y_attn.shape[1]), lambda i: (i, 0)),
            pl.BlockSpec((tm, y_pool.shape[1]), lambda i: (i, 0)),
            pl.BlockSpec((tm, y_mem.shape[1]), lambda i: (i, 0)),
            pl.BlockSpec((tm, d), lambda i: (i, 0)),
            const(wa.shape), const(wp.shape), const(wm.shape), const(wo.shape),
            pl.BlockSpec((1, d), lambda i: (0, 0)),
        ],
        out_specs=pl.BlockSpec((tm, d), lambda i: (i, 0)),
        out_shape=jax.ShapeDtypeStruct((n, d), F32),
        compiler_params=pltpu.CompilerParams(
            dimension_semantics=("arbitrary",), vmem_limit_bytes=_vmem_limit(est)),
        name="merge",
    )(gates, y_attn, y_pool, y_mem, x, wa, wp, wm, wo, gain)


def _cast_kernel(w_ref, *o_refs):
    lo = 0
    for o_ref in o_refs:
        width = o_ref.shape[1]
        o_ref[...] = w_ref[:, lo:lo + width].astype(o_ref.dtype)
        lo += width


def _cast_bf16(w, widths=None, target_bytes=CAST_BLOCK_BYTES):
    rows, cols = w.shape
    widths = (cols,) if widths is None else tuple(widths)
    br = rows
    while br * cols * 4 > target_bytes and br % 32 == 0:
        br //= 2
    est = 2 * br * cols * 4 + 2 * br * cols * 2
    outs = pl.pallas_call(
        _cast_kernel,
        grid=(rows // br,),
        in_specs=[pl.BlockSpec((br, cols), lambda i: (i, 0))],
        out_specs=[pl.BlockSpec((br, wd), lambda i: (i, 0)) for wd in widths],
        out_shape=[jax.ShapeDtypeStruct((rows, wd), BF16) for wd in widths],
        compiler_params=pltpu.CompilerParams(
            dimension_semantics=("arbitrary",), vmem_limit_bytes=_vmem_limit(est)),
        name="cast",
    )(w)
    return outs if len(widths) > 1 else outs[0]


def _cast_chunked_kernel(w_ref, *o_refs, width):
    for k, o_ref in enumerate(o_refs):
        nchunk, _, tf = o_ref.shape
        for c in range(nchunk):
            valid = min(tf, width - c * tf)
            lo = k * width + c * tf
            o_ref[c, :, :valid] = w_ref[:, lo:lo + valid].astype(o_ref.dtype)
            if valid < tf:
                o_ref[c, :, valid:] = jnp.zeros((o_ref.shape[1], tf - valid), o_ref.dtype)


def _cast_chunked(w, n_out, tf, target_bytes=CAST_BLOCK_BYTES):
    rows, cols = w.shape
    width = cols // n_out
    nchunk = pl.cdiv(width, tf)
    br = rows
    while br * cols * 4 > target_bytes and br % 32 == 0:
        br //= 2
    est = 2 * br * cols * 4 + 2 * n_out * nchunk * br * tf * 2
    return pl.pallas_call(
        functools.partial(_cast_chunked_kernel, width=width),
        grid=(rows // br,),
        in_specs=[pl.BlockSpec((br, cols), lambda i: (i, 0))],
        out_specs=[pl.BlockSpec((nchunk, br, tf), lambda i: (0, i, 0)) for _ in range(n_out)],
        out_shape=[jax.ShapeDtypeStruct((nchunk, rows, tf), BF16) for _ in range(n_out)],
        compiler_params=pltpu.CompilerParams(
            dimension_semantics=("arbitrary",), vmem_limit_bytes=_vmem_limit(est)),
        name="cast_chunked",
    )(w)


def _row(v):
    return v.reshape(1, -1).astype(F32)


LATE_WEIGHTS = ("w_mem_kv", "w_br_attn", "w_br_pool", "w_br_mem", "w_out")


def _mixer_half(x, mem, p, slopes, side, late=None):
    batch, seq, d = x.shape
    xf = x.reshape(batch * seq, d)
    x1 = _ffn(xf, p["ffn1_norm_pre"], p["ffn1_wg"], p["ffn1_wu"], p["ffn1_wd"], p["ffn1_norm_post"],
              p["ffn1_norm_post"], final=False)
    gates, nat, qkv4, qkv16 = _mixproj(x1, p["mix_norm_pre"], p["w_gate"], p["b_gate"], p["w_in"],
                                       batch=batch, seq=seq)
    col_pool = D_XATTN
    col_qkv0 = col_pool + D_POOL
    sides = [side] + ([("plain", p[name]) for name in LATE_WEIGHTS] if late is None else [])
    y_attn, cast = _attention(nat, qkv4, qkv16, slopes, batch=batch, seq=seq, col0=col_qkv0, sides=sides)
    if late is None:
        n_late = len(LATE_WEIGHTS)
        cast, late = cast[:-n_late], dict(zip(LATE_WEIGHTS, cast[-n_late:]))
    kv = _kvproj(mem.reshape(batch * N_MEM, d), p["mem_norm"], late["w_mem_kv"])
    y_pool = _pool(nat, p["w_pool"], p["pool_scale"], batch=batch, seq=seq, col0=col_pool)
    y_mem = _memattn(nat, kv, seq=seq, col0=0)
    x2 = _merge(gates, y_attn, y_pool, y_mem, x1, late["w_br_attn"], late["w_br_pool"], late["w_br_mem"],
                late["w_out"], p["mix_norm_post"])
    return x2, cast, late


def _prep_params(ffn1_norm_pre, ffn1_w_up, ffn1_w_down, ffn1_norm_post, mix_norm_pre, mem_norm, w_in, w_mem_kv,
                 w_pool, pool_scale, w_br_attn, w_br_pool, w_br_mem, w_gate, b_gate, w_out, mix_norm_post,
                 ffn2_norm_pre, ffn2_w_up, ffn2_w_down, ffn2_norm_post, final_norm):
    p = {}
    p["ffn1_wg"], p["ffn1_wu"] = _cast_chunked(ffn1_w_up, 2, FFN_TF)
    p["ffn1_wd"] = _cast_bf16(ffn1_w_down)
    p["ffn2_w_up"], p["ffn2_w_down"] = ffn2_w_up, ffn2_w_down
    p["w_gate"] = _cast_bf16(w_gate)
    p["w_in"] = _cast_bf16(w_in)
    p["b_gate"] = _row(b_gate)
    p["w_pool"] = w_pool.astype(BF16)
    p["pool_scale"] = _row(pool_scale)
    p.update(w_mem_kv=w_mem_kv, w_br_attn=w_br_attn, w_br_pool=w_br_pool, w_br_mem=w_br_mem, w_out=w_out)
    for name, v in (("ffn1_norm_pre", ffn1_norm_pre), ("ffn1_norm_post", ffn1_norm_post),
                    ("mix_norm_pre", mix_norm_pre), ("mem_norm", mem_norm), ("mix_norm_post", mix_norm_post),
                    ("ffn2_norm_pre", ffn2_norm_pre), ("ffn2_norm_post", ffn2_norm_post),
                    ("final_norm", final_norm)):
        p[name] = _row(v)
    return p


def kernel(x_prompt, x_sample, mem_prompt, mem_sample, ffn1_norm_pre, ffn1_w_up, ffn1_w_down, ffn1_norm_post,
           mix_norm_pre, mem_norm, w_in, w_mem_kv, w_pool, pool_scale, w_br_attn, w_br_pool, w_br_mem, w_gate, b_gate,
           w_out, mix_norm_post, ffn2_norm_pre, ffn2_w_up, ffn2_w_down, ffn2_norm_post, final_norm):
    layer = [ffn1_norm_pre, ffn1_w_up, ffn1_w_down, ffn1_norm_post, mix_norm_pre, mem_norm, w_in, w_mem_kv, w_pool,
             pool_scale, w_br_attn, w_br_pool, w_br_mem, w_gate, b_gate, w_out, mix_norm_post, ffn2_norm_pre,
             ffn2_w_up, ffn2_w_down, ffn2_norm_post, final_norm]
    depth = ffn1_norm_pre.shape[0]
    slopes = jnp.asarray(_alibi_slopes())
    y_prompt, y_sample = x_prompt, x_sample
    for layer_idx in range(depth):
        p = _prep_params(*[w[layer_idx] for w in layer])
        x2_sample, (wd2,), late = _mixer_half(y_sample, mem_sample, p, slopes, ("plain", p["ffn2_w_down"]))
        x2_prompt, (wg2, wu2), _ = _mixer_half(y_prompt, mem_prompt, p, slopes, ("gate_up", p["ffn2_w_up"]), late)
        y_prompt, y_sample = (
            _ffn(x2, p["ffn2_norm_pre"], wg2, wu2, wd2, p["ffn2_norm_post"], p["final_norm"],
                 final=True).reshape(x.shape)
            for x2, x in ((x2_prompt, y_prompt), (x2_sample, y_sample)))
    return (y_prompt, y_sample)
```

```python
import functools

import numpy as np
import jax
import jax.numpy as jnp
from jax import lax
from jax.experimental import pallas as pl
from jax.experimental.pallas import tpu as pltpu

F32 = jnp.float32
BF16 = jnp.bfloat16

D_MODEL = 2048
N_MEM = 256
HEAD_DIM = 128
DIL_PAIRS = ((128, 1), (512, 4), (2048, 16))
HEADS_PER_GROUP = 4
N_GROUPS = len(DIL_PAIRS)
N_DIL_HEADS = HEADS_PER_GROUP * N_GROUPS
D_ATTN = N_DIL_HEADS * HEAD_DIM
POOL_WINDOWS = (2, 4, 8, 16)
POOL_GROUP = 256
D_POOL = POOL_GROUP * len(POOL_WINDOWS)
N_XHEADS = 4
XHEAD_DIM = 256
D_XATTN = N_XHEADS * XHEAD_DIM
N_BRANCH = 3
EPS = 1e-6
NEG = -1e30
BAND_RADIUS = 64

V7X_VMEM_BYTES = 64 * 1024 * 1024
MIB = 1024 * 1024

FFN_TM = 1024
FFN_TF = 1024
PROJ_TM = 1024
PROJ_TN = 512
PROJ_TG = 1024
PROJ_ROWS = 256
DEINT_STRIDE = 4
ATT_QB = 128
ATT_KW = 256
ATT_UNROLL = 16
POOL_ROWS = 256
POOL_HALO = 64
POOL_UNROLL = 4
MEM_TM = 512
MERGE_TM = 512
MERGE_ROWS = 512
CAST_BLOCK_BYTES = 8 * MIB


def _vmem_limit(nbytes):
    return int(min(nbytes * 5 // 4 + 8 * MIB, V7X_VMEM_BYTES - 2 * MIB))


def _rms(x, gain):
    ms = jnp.mean(x * x, axis=-1, keepdims=True)
    return x * lax.rsqrt(ms + EPS) * gain


def _alibi_slopes():
    s = 2.0 ** (-8.0 * np.arange(1, N_DIL_HEADS + 1) / N_DIL_HEADS)
    return s.reshape(HEADS_PER_GROUP, N_GROUPS).T.astype(np.float32)


def _ffn_kernel(x_ref, gpre_ref, wg_ref, wu_ref, wd_ref, gpost_ref, gfin_ref, o_hbm, acc_ref, hs_ref, sem,
                *, tf, rem, final):
    i = pl.program_id(0)
    f = pl.program_id(1)
    n_tiles = pl.num_programs(0)
    last = pl.num_programs(1) - 1
    tm = acc_ref.shape[0]

    def writeback(tile):
        return pltpu.make_async_copy(acc_ref, o_hbm.at[pl.ds(pl.multiple_of(tile * tm, tm), tm), :], sem)

    @pl.when(f == 0)
    def _():
        hs_ref[...] = _rms(x_ref[...], gpre_ref[...]).astype(BF16)

    def step(valid, first):
        hs = hs_ref[...]
        g = jnp.dot(hs, wg_ref[:, :valid], preferred_element_type=F32)
        u = jnp.dot(hs, wu_ref[:, :valid], preferred_element_type=F32)
        a = (g * jax.nn.sigmoid(g) * u).astype(BF16)
        if first:
            @pl.when(i > 0)
            def _():
                writeback(i - 1).wait()

            acc_ref[...] = jnp.dot(a, wd_ref[:valid, :], preferred_element_type=F32)
        else:
            acc_ref[...] += jnp.dot(a, wd_ref[:valid, :], preferred_element_type=F32)

    pl.when(f == 0)(functools.partial(step, tf, True))
    pl.when((f > 0) & (f < last))(functools.partial(step, tf, False))
    pl.when(f == last)(functools.partial(step, rem, False))

    @pl.when(f == last)
    def _():
        y = x_ref[...] + 0.5 * _rms(acc_ref[...], gpost_ref[...])
        if final:
            y = _rms(y, gfin_ref[...])
        acc_ref[...] = y
        writeback(i).start()

        @pl.when(i == n_tiles - 1)
        def _():
            writeback(i).wait()


def _ffn(x, gpre, wg, wu, wd, gpost, gfin, *, final):
    n, d = x.shape
    tm, tf = FFN_TM, FFN_TF
    dff = wd.shape[0]
    nf = pl.cdiv(dff, tf)
    rem = dff - (nf - 1) * tf
    assert nf >= 2
    est = 2 * tm * d * 4 + tm * d * 4 + tm * d * 2 + 2 * 3 * d * tf * 2 + 2 * tm * tf * 4
    vec = pl.BlockSpec((1, d), lambda i, f: (0, 0))
    return pl.pallas_call(
        functools.partial(_ffn_kernel, tf=tf, rem=rem, final=final),
        grid=(n // tm, nf),
        in_specs=[
            pl.BlockSpec((tm, d), lambda i, f: (i, 0)),
            vec,
            pl.BlockSpec((None, d, tf), lambda i, f: (f, 0, 0)),
            pl.BlockSpec((None, d, tf), lambda i, f: (f, 0, 0)),
            pl.BlockSpec((tf, d), lambda i, f: (f, 0)),
            vec,
            vec,
        ],
        out_specs=pl.BlockSpec(memory_space=pl.ANY),
        out_shape=jax.ShapeDtypeStruct((n, d), F32),
        scratch_shapes=[pltpu.VMEM((tm, d), F32), pltpu.VMEM((tm, d), BF16), pltpu.SemaphoreType.DMA(())],
        compiler_params=pltpu.CompilerParams(
            dimension_semantics=("arbitrary", "arbitrary"), vmem_limit_bytes=_vmem_limit(est)),
        name="ffn_final" if final else "ffn",
    )(x, gpre, wg, wu, wd, gpost, gfin)


def _kvproj_kernel(x_ref, g_ref, w_ref, o_ref, hs_ref):
    @pl.when(pl.program_id(1) == 0)
    def _():
        hs_ref[...] = _rms(x_ref[...], g_ref[...]).astype(BF16)

    o_ref[...] = jnp.dot(hs_ref[...], w_ref[...], preferred_element_type=F32).astype(o_ref.dtype)


def _kvproj(x, gain, w):
    n, d = x.shape
    ncol = w.shape[1]
    tm, tn = n, PROJ_TN
    est = 2 * tm * d * 4 + tm * d * 2 + 2 * d * tn * 2 + 2 * tm * tn * 2 + 2 * tm * tn * 4
    return pl.pallas_call(
        _kvproj_kernel,
        grid=(n // tm, ncol // tn),
        in_specs=[
            pl.BlockSpec((tm, d), lambda i, j: (i, 0)),
            pl.BlockSpec((1, d), lambda i, j: (0, 0)),
            pl.BlockSpec((d, tn), lambda i, j: (0, j)),
        ],
        out_specs=pl.BlockSpec((tm, tn), lambda i, j: (i, j)),
        out_shape=jax.ShapeDtypeStruct((n, ncol), BF16),
        scratch_shapes=[pltpu.VMEM((tm, d), BF16)],
        compiler_params=pltpu.CompilerParams(
            dimension_semantics=("arbitrary", "arbitrary"), vmem_limit_bytes=_vmem_limit(est)),
        name="kvproj",
    )(x, gain, w)


NAT_CHUNKS = (11, 12, 9, 10, 0, 3, 6)
N_NAT_STEPS = len(NAT_CHUNKS)
N_DIL_STEPS = 3
N_GATE_STEPS = N_BRANCH * D_MODEL // PROJ_TG
FIRST_D4 = N_NAT_STEPS
FIRST_D16 = FIRST_D4 + N_DIL_STEPS
FIRST_GATE = FIRST_D16 + N_DIL_STEPS
MIX_STEPS = FIRST_GATE + N_GATE_STEPS


def _lin_chunk(s):
    t = jnp.minimum(s, FIRST_GATE - 1)
    nat = jnp.where(t < 2, 11 + t, jnp.where(t < 4, 7 + t, 3 * (t - 4)))
    d4 = 3 * (t - FIRST_D4) + 1
    d16 = 3 * (t - FIRST_D16) + 2
    return jnp.where(t < FIRST_D4, nat, jnp.where(t < FIRST_D16, d4, d16))


def _mixproj_kernel(x_ref, g_ref, wg_ref, b_ref, wi_ref, og_ref, on_ref, o4_ref, o16_ref, hs_ref, stage_ref,
                    stage2_ref):
    s = pl.program_id(1)
    tm = x_ref.shape[0]
    row_chunks = [pl.ds(r, PROJ_ROWS) for r in range(0, tm, PROJ_ROWS)]

    @pl.when(s == 0)
    def _():
        hs_ref[...] = _rms(x_ref[...], g_ref[...]).astype(BF16)

    @pl.when(s < FIRST_D4)
    def _():
        for rows in row_chunks:
            on_ref[rows, :] = jnp.dot(hs_ref[rows, :], wi_ref[...], preferred_element_type=F32).astype(on_ref.dtype)

    def dilated(o_ref, d):
        per_res = PROJ_ROWS // d
        slabs = range(PROJ_TN // HEAD_DIM)
        for rc, rows in enumerate(row_chunks):
            acc = jnp.dot(hs_ref[rows, :], wi_ref[...], preferred_element_type=F32)
            for c in slabs:
                stage_ref[c] = acc[:, c * HEAD_DIM:(c + 1) * HEAD_DIM]
            if d == DEINT_STRIDE:
                for res in range(d):
                    for c in slabs:
                        o_ref[res, pl.ds(rc * per_res, per_res), c * HEAD_DIM:(c + 1) * HEAD_DIM] = (
                            stage_ref[c, pl.ds(res, per_res, stride=d), :].astype(o_ref.dtype))
            else:
                for c in slabs:
                    for r1 in range(DEINT_STRIDE):
                        stage2_ref[c, r1] = stage_ref[c, pl.ds(r1, PROJ_ROWS // DEINT_STRIDE, stride=DEINT_STRIDE), :]
                for r1 in range(DEINT_STRIDE):
                    for r2 in range(DEINT_STRIDE):
                        res = r1 + DEINT_STRIDE * r2
                        for c in slabs:
                            o_ref[res, pl.ds(rc * per_res, per_res), c * HEAD_DIM:(c + 1) * HEAD_DIM] = (
                                stage2_ref[c, r1, pl.ds(r2, per_res, stride=DEINT_STRIDE), :].astype(o_ref.dtype))

    pl.when((s >= FIRST_D4) & (s < FIRST_D16))(functools.partial(dilated, o4_ref, DIL_PAIRS[1][1]))
    pl.when((s >= FIRST_D16) & (s < FIRST_GATE))(functools.partial(dilated, o16_ref, DIL_PAIRS[2][1]))

    @pl.when(s >= FIRST_GATE)
    def _():
        for rows in row_chunks:
            z = jnp.dot(hs_ref[rows, :], wg_ref[...], preferred_element_type=F32) + b_ref[...]
            og_ref[rows, :] = (0.5 * jnp.tanh(0.5 * z) + 0.5).astype(og_ref.dtype)


def _mixproj(x, gain, w_gate, b_gate, w_in, *, batch, seq):
    n, d = x.shape
    tm, tn, tg = PROJ_TM, PROJ_TN, PROJ_TG
    tiles_per_seq = seq // tm
    d4, d16 = DIL_PAIRS[1][1], DIL_PAIRS[2][1]

    def gate_idx(s):
        return jnp.maximum(s - FIRST_GATE, 0)

    def dil_spec(dil, first):
        return pl.BlockSpec(
            (None, dil, tm // dil, tn),
            lambda i, s: (i // tiles_per_seq, 0, i % tiles_per_seq, jnp.clip(s - first, 0, N_DIL_STEPS - 1)))

    est = (2 * tm * d * 4 + tm * d * 2 + 2 * d * (tg + tn) * 2 + 2 * tm * (tg + 3 * tn) * 2
           + PROJ_ROWS * tn * 4 + 2 * PROJ_ROWS * tg * 4)
    return pl.pallas_call(
        _mixproj_kernel,
        grid=(n // tm, MIX_STEPS),
        in_specs=[
            pl.BlockSpec((tm, d), lambda i, s: (i, 0)),
            pl.BlockSpec((1, d), lambda i, s: (0, 0)),
            pl.BlockSpec((d, tg), lambda i, s: (0, gate_idx(s))),
            pl.BlockSpec((1, tg), lambda i, s: (0, gate_idx(s))),
            pl.BlockSpec((d, tn), lambda i, s: (0, _lin_chunk(s))),
        ],
        out_specs=[
            pl.BlockSpec((tm, tg), lambda i, s: (i, gate_idx(s))),
            pl.BlockSpec((tm, tn), lambda i, s: (i, jnp.minimum(s, N_NAT_STEPS - 1))),
            dil_spec(d4, FIRST_D4),
            dil_spec(d16, FIRST_D16),
        ],
        out_shape=[
            jax.ShapeDtypeStruct((n, N_BRANCH * d), BF16),
            jax.ShapeDtypeStruct((n, N_NAT_STEPS * tn), BF16),
            jax.ShapeDtypeStruct((batch, d4, seq // d4, N_DIL_STEPS * tn), BF16),
            jax.ShapeDtypeStruct((batch, d16, seq // d16, N_DIL_STEPS * tn), BF16),
        ],
        scratch_shapes=[
            pltpu.VMEM((tm, d), BF16),
            pltpu.VMEM((tn // HEAD_DIM, PROJ_ROWS, HEAD_DIM), F32),
            pltpu.VMEM((tn // HEAD_DIM, DEINT_STRIDE, PROJ_ROWS // DEINT_STRIDE, HEAD_DIM), F32),
        ],
        compiler_params=pltpu.CompilerParams(
            dimension_semantics=("arbitrary", "arbitrary"), vmem_limit_bytes=_vmem_limit(est)),
        name="mixproj",
    )(x, gain, w_gate, b_gate, w_in)


def _attn_kernel(*refs, seq, side_casts):
    slopes_ref, q0_ref, k0_ref, v0_ref, q1_ref, k1_ref, v1_ref, q2_ref, k2_ref, v2_ref = refs[:10]
    n_in = 10 + len(side_casts)
    o_ref = refs[n_in]
    lo = n_in + 1
    for k, (cast, n_out) in enumerate(side_casts):
        cast(refs[10 + k], *refs[lo:lo + n_out])
        lo += n_out
    m1_ref, m2_ref, l1_ref, l2_ref, n1_ref, n2_ref, va_ref, bias_ref = refs[lo:]
    h = pl.program_id(1)
    scale = HEAD_DIM ** -0.5
    qb = ATT_QB
    qkv_refs = ((q0_ref, k0_ref, v0_ref), (q1_ref, k1_ref, v1_ref), (q2_ref, k2_ref, v2_ref))
    m_refs = (None, m1_ref, m2_ref)
    l_refs = (None, l1_ref, l2_ref)
    n_refs = (None, n1_ref, n2_ref)

    va_ref[:, HEAD_DIM:] = jnp.ones((seq, HEAD_DIM), BF16)

    for gi in reversed(range(N_GROUPS)):
        d = DIL_PAIRS[gi][1]
        q_ref, k_ref, v_ref = qkv_refs[gi]
        sub_len = seq // d
        kw = ATT_KW
        nb = sub_len // qb
        short = sub_len < kw
        slope_d = slopes_ref[gi, h] * float(d)

        row = lax.broadcasted_iota(jnp.int32, (qb, kw), 0)
        col = lax.broadcasted_iota(jnp.int32, (qb, kw), 1)
        for kind, off in enumerate((0, -BAND_RADIUS, qb - kw)):
            arel = jnp.abs(col - row + off)
            bias = jnp.where(arel <= BAND_RADIUS, -slope_d * arel.astype(F32), NEG)
            if short:
                key = col + off
                bias = jnp.where(key < 0, NEG, jnp.where(key < sub_len, bias, NEG))
            bias_ref[kind] = bias

        va_ref[:, :HEAD_DIM] = v_ref[...]

        def block(idx, gi=gi, d=d, q_ref=q_ref, k_ref=k_ref, sub_len=sub_len, kw=kw, nb=nb, short=short):
            qstart = pl.multiple_of(idx * qb, qb)
            if short:
                res, q0, pos, last = idx, 0, idx, d * nb - 1
                kstart = jnp.clip(qstart - BAND_RADIUS, 0, seq - kw)
            else:
                res = idx // nb
                pos, last = idx - res * nb, nb - 1
                q0 = pos * qb
                kstart = res * sub_len + jnp.clip(q0 - BAND_RADIUS, 0, sub_len - kw)
            kstart = pl.multiple_of(kstart, BAND_RADIUS)
            kind = jnp.where(pos == 0, 0, jnp.where(pos == last, 2, 1))
            s = lax.dot_general(q_ref[pl.ds(qstart, qb), :], k_ref[pl.ds(kstart, kw), :],
                                (((1,), (1,)), ((), ())), preferred_element_type=F32)
            s = s * scale + bias_ref[kind]
            m = jnp.max(s, axis=-1, keepdims=True)
            p = jnp.exp(s - m).astype(BF16)
            na = jnp.dot(p, va_ref[pl.ds(kstart, kw), :], preferred_element_type=F32)
            m = jnp.broadcast_to(m, (qb, HEAD_DIM))
            if d > 1:
                rows = pl.ds(q0 * d + res, qb, stride=d)
                m_refs[gi][rows, :] = m
                l_refs[gi][rows, :] = na[:, HEAD_DIM:]
                n_refs[gi][rows, :] = na[:, :HEAD_DIM]
            else:
                rows = pl.ds(qstart, qb)
                ms = [m] + [m_refs[g][rows, :] for g in range(1, N_GROUPS)]
                mx = functools.reduce(jnp.maximum, ms)
                e = jnp.exp(m - mx)
                num = e * na[:, :HEAD_DIM]
                den = e * na[:, HEAD_DIM:]
                for g in range(1, N_GROUPS):
                    e = jnp.exp(ms[g] - mx)
                    num += e * n_refs[g][rows, :]
                    den += e * l_refs[g][rows, :]
                o_ref[rows, :] = (num / den).astype(o_ref.dtype)

        unroll = min(ATT_UNROLL, d * nb // 2)

        def blocks(it, carry, block=block, unroll=unroll):
            for u in range(unroll):
                block(it * unroll + u)
            return carry

        lax.fori_loop(0, d * nb // unroll, blocks, 0)


def _side_cast_specs(kind, w, steps):
    rows, cols = w.shape

    def step(b, h):
        return b * HEADS_PER_GROUP + h

    if kind == "gate_up":
        br, dff = rows // steps, cols // 2
        nchunk = pl.cdiv(dff, FFN_TF)
        in_spec = pl.BlockSpec((br, cols), lambda b, h: (step(b, h), 0))
        out_specs = [pl.BlockSpec((nchunk, br, FFN_TF), lambda b, h: (0, step(b, h), 0)) for _ in range(2)]
        out_shapes = [jax.ShapeDtypeStruct((nchunk, rows, FFN_TF), BF16) for _ in range(2)]
        return functools.partial(_cast_chunked_kernel, width=dff), in_spec, out_specs, out_shapes, br * cols
    bc = cols // steps
    in_spec = pl.BlockSpec((rows, bc), lambda b, h: (0, step(b, h)))
    out_specs = [pl.BlockSpec((rows, bc), lambda b, h: (0, step(b, h)))]
    out_shapes = [jax.ShapeDtypeStruct((rows, cols), BF16)]
    return _cast_kernel, in_spec, out_specs, out_shapes, rows * bc


def _attention(nat, qkv4, qkv16, slopes, *, batch, seq, col0, sides=()):
    blk0 = col0 // HEAD_DIM
    qkv4 = qkv4.reshape(batch * seq, qkv4.shape[-1])
    qkv16 = qkv16.reshape(batch * seq, qkv16.shape[-1])

    def nat_map(b, h, *, which):
        return (b, blk0 + which * HEADS_PER_GROUP + h)

    def dil_map(b, h, *, which):
        return (b, which * HEADS_PER_GROUP + h)

    in_specs = [pl.BlockSpec(memory_space=pltpu.SMEM)]
    in_specs += [pl.BlockSpec((seq, HEAD_DIM), functools.partial(nat_map, which=w)) for w in range(3)]
    in_specs += 2 * [pl.BlockSpec((seq, HEAD_DIM), functools.partial(dil_map, which=w)) for w in range(3)]
    tile = seq * HEAD_DIM
    est = 10 * 2 * tile * 2 + 6 * tile * 4 + 2 * tile * 2 + 3 * ATT_QB * ATT_KW * 4
    out_specs = [pl.BlockSpec((seq, HEAD_DIM), lambda b, h: (b, h))]
    out_shapes = [jax.ShapeDtypeStruct((batch * seq, HEADS_PER_GROUP * HEAD_DIM), BF16)]
    operands = [slopes, nat, nat, nat, qkv4, qkv4, qkv4, qkv16, qkv16, qkv16]
    side_casts = []
    for kind, w in sides:
        cast, side_in, side_outs, side_shapes, block_elems = _side_cast_specs(kind, w, batch * HEADS_PER_GROUP)
        side_casts.append((cast, len(side_outs)))
        in_specs.append(side_in)
        operands.append(w)
        out_specs += side_outs
        out_shapes += side_shapes
        est += 2 * block_elems * (4 + 2)
    state = [pltpu.VMEM((seq, HEAD_DIM), F32) for _ in range(3 * (N_GROUPS - 1))]
    outs = pl.pallas_call(
        functools.partial(_attn_kernel, seq=seq, side_casts=tuple(side_casts)),
        grid=(batch, HEADS_PER_GROUP),
        in_specs=in_specs,
        out_specs=out_specs,
        out_shape=out_shapes,
        scratch_shapes=state + [
            pltpu.VMEM((seq, 2 * HEAD_DIM), BF16),
            pltpu.VMEM((3, ATT_QB, ATT_KW), F32),
        ],
        compiler_params=pltpu.CompilerParams(
            dimension_semantics=("arbitrary", "arbitrary"), vmem_limit_bytes=_vmem_limit(est)),
        name="attention",
    )(*operands)
    return outs[0], tuple(outs[1:])


def _pool_kernel(u_ref, w_ref, sc_ref, o_ref, pad_ref, band_ref, pooled_ref, *, seq):
    grp = pl.program_id(1)
    halo, rows = POOL_HALO, POOL_ROWS
    ext = rows + 2 * halo
    zeros = jnp.zeros((halo, POOL_GROUP), BF16)
    pad_ref[pl.ds(0, halo), :] = zeros
    pad_ref[pl.ds(halo + seq, halo), :] = zeros
    pad_ref[pl.ds(halo, seq), :] = u_ref[...]

    def run(window):
        first = lax.broadcasted_iota(jnp.int32, (rows, ext), 0) + (halo - window // 2)
        col = lax.broadcasted_iota(jnp.int32, (rows, ext), 1)
        band_ref[...] = jnp.where(col < first, 0.0, jnp.where(col < first + window, 1.0, 0.0)).astype(BF16)

        def chunk(c):
            base = pl.multiple_of(c * rows, rows)
            sums = jnp.dot(band_ref[...], pad_ref[pl.ds(base, ext), :], preferred_element_type=F32)
            tok = pad_ref[pl.ds(base + halo, rows), :].astype(F32)
            t = base + lax.broadcasted_iota(jnp.int32, (rows, 1), 0)
            lo = jnp.maximum(t - window // 2, 0)
            hi = jnp.minimum(t - window // 2 + window, seq)
            pooled_ref[pl.ds(base, rows), :] = (sums / (hi - lo).astype(F32) - tok).astype(BF16)

        def chunks(it, carry):
            for u in range(POOL_UNROLL):
                chunk(it * POOL_UNROLL + u)
            return carry

        lax.fori_loop(0, seq // rows // POOL_UNROLL, chunks, 0)

    for gi, window in enumerate(POOL_WINDOWS):
        pl.when(grp == gi)(functools.partial(run, window))

    y = jnp.dot(pooled_ref[...], w_ref[...], preferred_element_type=F32) * sc_ref[...]
    o_ref[...] = y.astype(o_ref.dtype)


def _pool(pg, w_pool, pool_scale, *, batch, seq, col0):
    blk0 = col0 // POOL_GROUP
    ngrp = len(POOL_WINDOWS)
    est = (2 * seq * POOL_GROUP * 2 * 2 + (seq + 2 * POOL_HALO) * POOL_GROUP * 4 + 2 * POOL_GROUP * POOL_GROUP * 2
           + seq * POOL_GROUP * (2 + 4))
    return pl.pallas_call(
        functools.partial(_pool_kernel, seq=seq),
        grid=(batch, ngrp),
        in_specs=[
            pl.BlockSpec((seq, POOL_GROUP), lambda b, g: (b, blk0 + g)),
            pl.BlockSpec((None, POOL_GROUP, POOL_GROUP), lambda b, g: (g, 0, 0)),
            pl.BlockSpec((1, POOL_GROUP), lambda b, g: (0, g)),
        ],
        out_specs=pl.BlockSpec((seq, POOL_GROUP), lambda b, g: (b, g)),
        out_shape=jax.ShapeDtypeStruct((batch * seq, D_POOL), BF16),
        scratch_shapes=[pltpu.VMEM((seq + 2 * POOL_HALO, POOL_GROUP), BF16),
                        pltpu.VMEM((POOL_ROWS, POOL_ROWS + 2 * POOL_HALO), BF16),
                        pltpu.VMEM((seq, POOL_GROUP), BF16)],
        compiler_params=pltpu.CompilerParams(
            dimension_semantics=("arbitrary", "arbitrary"), vmem_limit_bytes=_vmem_limit(est)),
        name="pool",
    )(pg, w_pool, pool_scale)


def _memattn_kernel(q_ref, kv_ref, o_ref):
    scale = XHEAD_DIM ** -0.5
    for hh in range(N_XHEADS):
        lo = hh * XHEAD_DIM
        q = q_ref[:, lo:lo + XHEAD_DIM]
        k = kv_ref[:, lo:lo + XHEAD_DIM]
        v = kv_ref[:, D_XATTN + lo:D_XATTN + lo + XHEAD_DIM]
        s = lax.dot_general(q, k, (((1,), (1,)), ((), ())), preferred_element_type=F32) * scale
        m = jnp.max(s, axis=-1, keepdims=True)
        p = jnp.exp(s - m)
        l = jnp.sum(p, axis=-1, keepdims=True)
        y = jnp.dot(p.astype(BF16), v, preferred_element_type=F32) / l
        o_ref[:, lo:lo + XHEAD_DIM] = y.astype(o_ref.dtype)


def _memattn(pg, kv, *, seq, col0):
    n = pg.shape[0]
    tm = MEM_TM
    blk0 = col0 // D_XATTN
    tiles_per_seq = seq // tm
    est = 2 * tm * D_XATTN * 2 * 2 + 2 * N_MEM * 2 * D_XATTN * 2 + 4 * tm * N_MEM * 4
    return pl.pallas_call(
        _memattn_kernel,
        grid=(n // tm,),
        in_specs=[
            pl.BlockSpec((tm, D_XATTN), lambda i: (i, blk0)),
            pl.BlockSpec((N_MEM, 2 * D_XATTN), lambda i: (i // tiles_per_seq, 0)),
        ],
        out_specs=pl.BlockSpec((tm, D_XATTN), lambda i: (i, 0)),
        out_shape=jax.ShapeDtypeStruct((n, D_XATTN), BF16),
        compiler_params=pltpu.CompilerParams(
            dimension_semantics=("arbitrary",), vmem_limit_bytes=_vmem_limit(est)),
        name="memattn",
    )(pg, kv)


def _merge_kernel(gates_ref, ya_ref, yp_ref, ym_ref, x_ref, wa_ref, wp_ref, wm_ref, wo_ref, gain_ref, o_ref):
    d = x_ref.shape[1]
    for r in range(0, x_ref.shape[0], MERGE_ROWS):
        rows = pl.ds(r, MERGE_ROWS)
        merged = gates_ref[rows, 0:d].astype(F32) * jnp.dot(ya_ref[rows, :], wa_ref[...], preferred_element_type=F32)
        merged += gates_ref[rows, d:2 * d].astype(F32) * jnp.dot(yp_ref[rows, :], wp_ref[...],
                                                                   preferred_element_type=F32)
        merged += gates_ref[rows, 2 * d:3 * d].astype(F32) * jnp.dot(ym_ref[rows, :], wm_ref[...],
                                                                       preferred_element_type=F32)
        z = jnp.dot(merged.astype(BF16), wo_ref[...], preferred_element_type=F32)
        o_ref[rows, :] = x_ref[rows, :] + _rms(z, gain_ref[...])


def _merge(gates, y_attn, y_pool, y_mem, x, wa, wp, wm, wo, gain):
    n, d = x.shape
    tm = MERGE_TM
    wbytes = (wa.size + wp.size + wm.size + wo.size) * 2

    def const(shape):
        return pl.BlockSpec(shape, lambda i: (0, 0), pipeline_mode=pl.Buffered(1))

    est = (2 * tm * (N_BRANCH * d + y_attn.shape[1] + y_pool.shape[1] + y_mem.shape[1]) * 2
           + 4 * tm * d * 4 + wbytes + 4 * tm * d * 4)
    return pl.pallas_call(
        _merge_kernel,
        grid=(n // tm,),
        in_specs=[
            pl.BlockSpec((tm, N_BRANCH * d), lambda i: (i, 0)),
            pl.BlockSpec((tm, y_attn.shape[1]), lambda i: (i, 0)),
            pl.BlockSpec((tm, y_pool.shape[1]), lambda i: (i, 0)),
            pl.BlockSpec((tm, y_mem.shape[1]), lambda i: (i, 0)),
            pl.BlockSpec((tm, d), lambda i: (i, 0)),
            const(wa.shape), const(wp.shape), const(wm.shape), const(wo.shape),
            pl.BlockSpec((1, d), lambda i: (0, 0)),
        ],
        out_specs=pl.BlockSpec((tm, d), lambda i: (i, 0)),
        out_shape=jax.ShapeDtypeStruct((n, d), F32),
        compiler_params=pltpu.CompilerParams(
            dimension_semantics=("arbitrary",), vmem_limit_bytes=_vmem_limit(est)),
        name="merge",
    )(gates, y_attn, y_pool, y_mem, x, wa, wp, wm, wo, gain)


def _cast_kernel(w_ref, *o_refs):
    lo = 0
    for o_ref in o_refs:
        width = o_ref.shape[1]
        o_ref[...] = w_ref[:, lo:lo + width].astype(o_ref.dtype)
        lo += width


def _cast_bf16(w, widths=None, target_bytes=CAST_BLOCK_BYTES):
    rows, cols = w.shape
    widths = (cols,) if widths is None else tuple(widths)
    br = rows
    while br * cols * 4 > target_bytes and br % 32 == 0:
        br //= 2
    est = 2 * br * cols * 4 + 2 * br * cols * 2
    outs = pl.pallas_call(
        _cast_kernel,
        grid=(rows // br,),
        in_specs=[pl.BlockSpec((br, cols), lambda i: (i, 0))],
        out_specs=[pl.BlockSpec((br, wd), lambda i: (i, 0)) for wd in widths],
        out_shape=[jax.ShapeDtypeStruct((rows, wd), BF16) for wd in widths],
        compiler_params=pltpu.CompilerParams(
            dimension_semantics=("arbitrary",), vmem_limit_bytes=_vmem_limit(est)),
        name="cast",
    )(w)
    return outs if len(widths) > 1 else outs[0]


def _cast_chunked_kernel(w_ref, *o_refs, width):
    for k, o_ref in enumerate(o_refs):
        nchunk, _, tf = o_ref.shape
        for c in range(nchunk):
            valid = min(tf, width - c * tf)
            lo = k * width + c * tf
            o_ref[c, :, :valid] = w_ref[:, lo:lo + valid].astype(o_ref.dtype)
            if valid < tf:
                o_ref[c, :, valid:] = jnp.zeros((o_ref.shape[1], tf - valid), o_ref.dtype)


def _cast_chunked(w, n_out, tf, target_bytes=CAST_BLOCK_BYTES):
    rows, cols = w.shape
    width = cols // n_out
    nchunk = pl.cdiv(width, tf)
    br = rows
    while br * cols * 4 > target_bytes and br % 32 == 0:
        br //= 2
    est = 2 * br * cols * 4 + 2 * n_out * nchunk * br * tf * 2
    return pl.pallas_call(
        functools.partial(_cast_chunked_kernel, width=width),
        grid=(rows // br,),
        in_specs=[pl.BlockSpec((br, cols), lambda i: (i, 0))],
        out_specs=[pl.BlockSpec((nchunk, br, tf), lambda i: (0, i, 0)) for _ in range(n_out)],
        out_shape=[jax.ShapeDtypeStruct((nchunk, rows, tf), BF16) for _ in range(n_out)],
        compiler_params=pltpu.CompilerParams(
            dimension_semantics=("arbitrary",), vmem_limit_bytes=_vmem_limit(est)),
        name="cast_chunked",
    )(w)


def _row(v):
    return v.reshape(1, -1).astype(F32)


LATE_WEIGHTS = ("w_mem_kv", "w_br_attn", "w_br_pool", "w_br_mem", "w_out")


def _mixer_half(x, mem, p, slopes, side, late=None):
    batch, seq, d = x.shape
    xf = x.reshape(batch * seq, d)
    x1 = _ffn(xf, p["ffn1_norm_pre"], p["ffn1_wg"], p["ffn1_wu"], p["ffn1_wd"], p["ffn1_norm_post"],
              p["ffn1_norm_post"], final=False)
    gates, nat, qkv4, qkv16 = _mixproj(x1, p["mix_norm_pre"], p["w_gate"], p["b_gate"], p["w_in"],
                                       batch=batch, seq=seq)
    col_pool = D_XATTN
    col_qkv0 = col_pool + D_POOL
    sides = [side] + ([("plain", p[name]) for name in LATE_WEIGHTS] if late is None else [])
    y_attn, cast = _attention(nat, qkv4, qkv16, slopes, batch=batch, seq=seq, col0=col_qkv0, sides=sides)
    if late is None:
        n_late = len(LATE_WEIGHTS)
        cast, late = cast[:-n_late], dict(zip(LATE_WEIGHTS, cast[-n_late:]))
    kv = _kvproj(mem.reshape(batch * N_MEM, d), p["mem_norm"], late["w_mem_kv"])
    y_pool = _pool(nat, p["w_pool"], p["pool_scale"], batch=batch, seq=seq, col0=col_pool)
    y_mem = _memattn(nat, kv, seq=seq, col0=0)
    x2 = _merge(gates, y_attn, y_pool, y_mem, x1, late["w_br_attn"], late["w_br_pool"], late["w_br_mem"],
                late["w_out"], p["mix_norm_post"])
    return x2, cast, late


def _prep_params(ffn1_norm_pre, ffn1_w_up, ffn1_w_down, ffn1_norm_post, mix_norm_pre, mem_norm, w_in, w_mem_kv,
                 w_pool, pool_scale, w_br_attn, w_br_pool, w_br_mem, w_gate, b_gate, w_out, mix_norm_post,
                 ffn2_norm_pre, ffn2_w_up, ffn2_w_down, ffn2_norm_post, final_norm):
    p = {}
    p["ffn1_wg"], p["ffn1_wu"] = _cast_chunked(ffn1_w_up, 2, FFN_TF)
    p["ffn1_wd"] = _cast_bf16(ffn1_w_down)
    p["ffn2_w_up"], p["ffn2_w_down"] = ffn2_w_up, ffn2_w_down
    p["w_gate"] = _cast_bf16(w_gate)
    p["w_in"] = _cast_bf16(w_in)
    p["b_gate"] = _row(b_gate)
    p["w_pool"] = w_pool.astype(BF16)
    p["pool_scale"] = _row(pool_scale)
    p.update(w_mem_kv=w_mem_kv, w_br_attn=w_br_attn, w_br_pool=w_br_pool, w_br_mem=w_br_mem, w_out=w_out)
    for name, v in (("ffn1_norm_pre", ffn1_norm_pre), ("ffn1_norm_post", ffn1_norm_post),
                    ("mix_norm_pre", mix_norm_pre), ("mem_norm", mem_norm), ("mix_norm_post", mix_norm_post),
                    ("ffn2_norm_pre", ffn2_norm_pre), ("ffn2_norm_post", ffn2_norm_post),
                    ("final_norm", final_norm)):
        p[name] = _row(v)
    return p


def kernel(x_prompt, x_sample, mem_prompt, mem_sample, ffn1_norm_pre, ffn1_w_up, ffn1_w_down, ffn1_norm_post,
           mix_norm_pre, mem_norm, w_in, w_mem_kv, w_pool, pool_scale, w_br_attn, w_br_pool, w_br_mem, w_gate, b_gate,
           w_out, mix_norm_post, ffn2_norm_pre, ffn2_w_up, ffn2_w_down, ffn2_norm_post, final_norm):
    layer = [ffn1_norm_pre, ffn1_w_up, ffn1_w_down, ffn1_norm_post, mix_norm_pre, mem_norm, w_in, w_mem_kv, w_pool,
             pool_scale, w_br_attn, w_br_pool, w_br_mem, w_gate, b_gate, w_out, mix_norm_post, ffn2_norm_pre,
             ffn2_w_up, ffn2_w_down, ffn2_norm_post, final_norm]
    depth = ffn1_norm_pre.shape[0]
    slopes = jnp.asarray(_alibi_slopes())
    y_prompt, y_sample = x_prompt, x_sample
    for layer_idx in range(depth):
        p = _prep_params(*[w[layer_idx] for w in layer])
        x2_sample, (wd2,), late = _mixer_half(y_sample, mem_sample, p, slopes, ("plain", p["ffn2_w_down"]))
        x2_prompt, (wg2, wu2), _ = _mixer_half(y_prompt, mem_prompt, p, slopes, ("gate_up", p["ffn2_w_up"]), late)
        y_prompt, y_sample = (
            _ffn(x2, p["ffn2_norm_pre"], wg2, wu2, wd2, p["ffn2_norm_post"], p["final_norm"],
                 final=True).reshape(x.shape)
            for x2, x in ((x2_prompt, y_prompt), (x2_sample, y_sample)))
    return (y_prompt, y_sample)
```
